```python
import jax, jax.numpy as jnp
from jax import lax
import numpy as np


D_MODEL = 1024
BATCH = 16
SEQ = 2048
DEPTH = 2

GRID_W = 64
CTX_LEN = 256
F32 = jnp.float32
NORM_EPS = 1e-6
ROPE_THETA = 10000.0
D_FF = 2816
N_MOD = 9
Q_BLOCK = 128

DN_HEADS = 4
DN_DK = 128
DN_DV = 128
DN_CONV = 5
DN_CHUNK = 64
DN_W = DN_HEADS * DN_DV
RW_HEADS = 8
RW_HS = 64
RW_W = RW_HEADS * RW_HS
RW_W_LORA = 64
RW_A_LORA = 64
RW_G_LORA = 160
RW_GN_EPS = 64e-5
RW_SLAB = 3 * RW_W + 2 * RW_W_LORA + 2 * RW_A_LORA + RW_G_LORA
GQ_HEADS = 8
GQ_KV_HEADS = 2
GQ_HD = 64
ML_HEADS = 8
ML_Q_LORA = 384
ML_KV_LORA = 256
ML_NOPE = 64
ML_ROPE = 32
ML_V = 64

MIX_W = DN_W + RW_W
EV_SIZES = (3 * DN_W, DN_W, 2 * DN_HEADS, 2 * DN_HEADS, RW_SLAB)
EV_IN = sum(EV_SIZES)
OD_SIZES = (GQ_HEADS * GQ_HD, GQ_KV_HEADS * GQ_HD, GQ_KV_HEADS * GQ_HD, ML_Q_LORA, ML_KV_LORA, ML_ROPE)
OD_IN = sum(OD_SIZES)
N_EVEN = (DEPTH + 1) // 2
N_ODD = DEPTH // 2

kernel_name = 'hybrid_deltanet_rwkv7_gqa_mla_macaron_dit'


def split_cols(z, sizes):
    return jnp.split(z, np.cumsum(sizes)[:-1].tolist(), axis=-1)


def rms_norm(x, gain):
    xf = x.astype(F32)
    y = xf * lax.rsqrt(jnp.mean(xf * xf, axis=-1, keepdims=True) + NORM_EPS)
    return (y * gain.astype(F32)).astype(x.dtype)


def l2_normalize(x):
    xf = x.astype(F32)
    return (xf * lax.rsqrt(jnp.sum(xf * xf, axis=-1, keepdims=True) + 1e-6)).astype(x.dtype)


def modulate(h, gain, shift, scale):
    return rms_norm(h, gain) * (1 + scale) + shift


def swiglu(x, w1, w3, w2):
    return (jax.nn.silu(x @ w1) * (x @ w3)) @ w2


def seg_flip(z, n_ctx):
    return jnp.concatenate([jnp.flip(z[:, :n_ctx], 1), jnp.flip(z[:, n_ctx:], 1)], axis=1)


def centred_dwconv(z, w):
    k = w.shape[0]
    return lax.conv_general_dilated(z, w[:, None, :].astype(z.dtype), window_strides=(1,),
                                    padding=[(k // 2, k // 2)], dimension_numbers=('NWC', 'WIO', 'NWC'),
                                    feature_group_count=z.shape[-1])


def centred_shift(z):
    zp = jnp.pad(z, ((0, 0), (1, 1), (0, 0)))
    return 0.5 * (zp[:, :-2] + zp[:, 2:])


def axial_rope_tables(row, col, rot_dim):
    axis_dim = rot_dim // 2
    inv = ROPE_THETA ** (-jnp.arange(0, axis_dim, 2, dtype=F32) / axis_dim)
    ang = jnp.concatenate([row.astype(F32)[:, None] * inv, col.astype(F32)[:, None] * inv], axis=-1)
    return jnp.cos(ang), jnp.sin(ang)


def apply_axial_rope(x, cos, sin):
    b, t, h, d = x.shape
    xa = x.reshape(b, t, h, 2, 2, d // 4)
    c = cos.reshape(t, 1, 2, d // 4).astype(x.dtype)
    s = sin.reshape(t, 1, 2, d // 4).astype(x.dtype)
    x1, x2 = xa[..., 0, :], xa[..., 1, :]
    return jnp.stack([x1 * c - x2 * s, x1 * s + x2 * c], axis=-2).reshape(b, t, h, d)


def blocked_attention(q, k, v):
    b, tq, hkv, g, dk = q.shape
    dv = v.shape[-1]
    nb = tq // Q_BLOCK
    scale = dk ** -0.5
    qb = jnp.moveaxis(q.reshape(b, nb, Q_BLOCK, hkv, g, dk), 1, 0)

    def one_block(qblk):
        s = jnp.einsum('bqhgd,bkhd->bhgqk', qblk, k).astype(F32) * scale
        p = jax.nn.softmax(s, axis=-1).astype(v.dtype)
        return jnp.einsum('bhgqk,bkhe->bqhge', p, v)

    o = lax.map(one_block, qb)
    return jnp.moveaxis(o, 0, 1).reshape(b, tq, hkv * g * dv)


def chunk_gated_delta(q, k, v, beta, g):
    in_dtype = v.dtype
    b, t, h, dk = q.shape
    dv = v.shape[-1]
    n = t // DN_CHUNK

    def to_chunks(z):
        z = z.astype(F32).reshape((b, n, DN_CHUNK, h) + z.shape[3:])
        return jnp.moveaxis(z, 3, 1)

    q = to_chunks(q) * dk ** -0.5
    k, v, beta, g = to_chunks(k), to_chunks(v), to_chunks(beta), to_chunks(g)
    gc = jnp.cumsum(g, axis=-1)
    idx = jnp.arange(DN_CHUNK)
    incl = idx[:, None] >= idx[None, :]
    strict = idx[:, None] > idx[None, :]
    decay = jnp.exp(jnp.where(incl, gc[..., :, None] - gc[..., None, :], -jnp.inf))
    kb = k * beta[..., None]
    lower = jnp.where(strict, jnp.einsum('bhnid,bhnjd->bhnij', kb, k) * decay, 0.0)
    eye = jnp.eye(DN_CHUNK, dtype=F32)
    rhs = jnp.concatenate([v * beta[..., None], kb * jnp.exp(gc)[..., None]], axis=-1)
    sol = lax.linalg.triangular_solve(eye + lower, rhs, left_side=True, lower=True, unit_diagonal=True)
    u, w = sol[..., :dv], sol[..., dv:]
    attn = jnp.einsum('bhnid,bhnjd->bhnij', q, k) * decay
    q_dec = q * jnp.exp(gc)[..., None]
    k_dec = k * jnp.exp(gc[..., -1:] - gc)[..., None]
    g_last = jnp.exp(gc[..., -1])

    def step(s, xs):
        u_n, w_n, attn_n, qd_n, kd_n, gl_n = xs
        v_new = u_n - jnp.einsum('bhcd,bhde->bhce', w_n, s)
        o = jnp.einsum('bhcd,bhde->bhce', qd_n, s) + jnp.einsum('bhij,bhje->bhie', attn_n, v_new)
        s = s * gl_n[..., None, None] + jnp.einsum('bhcd,bhce->bhde', kd_n, v_new)
        return s, o

    xs = tuple(jnp.moveaxis(z, 2, 0) for z in (u, w, attn, q_dec, k_dec, g_last))
    _, o = lax.scan(step, jnp.zeros((b, h, dk, dv), F32), xs)
    o = jnp.moveaxis(jnp.moveaxis(o, 0, 2), 1, 3).reshape(b, t, h, dv)
    return o.astype(in_dtype)


def rwkv7_scan(r, w, k, v, kk, a):
    in_dtype = v.dtype
    b, t, h, n = r.shape
    xs = tuple(jnp.moveaxis(z.astype(F32), 1, 0) for z in (r, w, k, v, kk, a))

    def step(s, x):
        r_t, w_t, k_t, v_t, kk_t, a_t = x
        sk = jnp.einsum('bhvk,bhk->bhv', s, kk_t)
        s = s * w_t[:, :, None, :] - sk[..., None] * (kk_t * a_t)[:, :, None, :] + v_t[..., None] * k_t[:, :, None, :]
        return s, jnp.einsum('bhvk,bhk->bhv', s, r_t)

    _, y = lax.scan(step, jnp.zeros((b, h, n, n), F32), xs)
    return jnp.moveaxis(y, 0, 1).astype(in_dtype)


def head_group_norm(y, w, bias):
    yf = y.astype(F32)
    mu = jnp.mean(yf, axis=-1, keepdims=True)
    var = jnp.mean(jnp.square(yf - mu), axis=-1, keepdims=True)
    yn = ((yf - mu) * lax.rsqrt(var + RW_GN_EPS)).reshape(y.shape[:-2] + (-1,))
    return (yn * w.astype(F32) + bias.astype(F32)).astype(y.dtype)


def rw_heads(z):
    return z.reshape(z.shape[:-1] + (RW_HEADS, RW_HS))


def recurrent_mixer(xc, xl, w_in, dn_conv, dn_a_log, dn_dt_bias, dn_norm, rw_mu, rw_w0, rw_w2,
                    rw_a0, rw_a2, rw_g2, rw_kk, rw_ka, rw_rk, rw_gn_w, rw_gn_b):
    b, n_ctx, _ = xc.shape

    def local_ops(p):
        qkv, gate, a_in, b_in, slab = split_cols(p, EV_SIZES)
        qkv = jax.nn.silu(centred_dwconv(qkv, dn_conv))
        slab = slab + rw_mu * (centred_shift(slab) - slab)
        return jnp.concatenate([qkv, gate, a_in, b_in, slab], axis=-1)

    p = jnp.concatenate([local_ops(xc @ w_in), local_ops(xl @ w_in)], axis=1)
    t = p.shape[1]
    qkv, gate, a_in, b_in, slab = split_cols(p, EV_SIZES)

    def fold(fwd, bwd):
        return jnp.concatenate([fwd, seg_flip(bwd, n_ctx)], axis=0)

    def unfold(z):
        return z[:b] + seg_flip(z[b:], n_ctx)

    q, k, v = [z.reshape(b, t, DN_HEADS, DN_DK) for z in jnp.split(qkv, 3, axis=-1)]
    q, k = l2_normalize(q), l2_normalize(k)
    beta = jax.nn.sigmoid(b_in).reshape(b, t, 2, DN_HEADS)
    g = -jnp.exp(dn_a_log.astype(F32)) * jax.nn.softplus(a_in.astype(F32).reshape(b, t, 2, DN_HEADS) + dn_dt_bias.astype(F32))
    o = unfold(chunk_gated_delta(fold(q, q), fold(k, k), fold(v, v),
                                 fold(beta[:, :, 0], beta[:, :, 1]), fold(g[:, :, 0], g[:, :, 1])))
    o_dn = (rms_norm(o, dn_norm) * jax.nn.silu(gate.reshape(b, t, DN_HEADS, DN_DV))).reshape(b, t, DN_W)

    r, k7, v7, wd, ad, gd = split_cols(slab, (RW_W, RW_W, RW_W, 2 * RW_W_LORA, 2 * RW_A_LORA, RW_G_LORA))
    w_logit = jnp.einsum('btdl,dlc->btdc', jnp.tanh(wd.astype(F32).reshape(b, t, 2, RW_W_LORA)),
                         rw_w2.astype(F32)) + rw_w0.astype(F32)
    decay = jnp.exp(-jnp.exp(-jax.nn.softplus(-w_logit) - 0.5))
    a = jax.nn.sigmoid(jnp.einsum('btdl,dlc->btdc', ad.reshape(b, t, 2, RW_A_LORA), rw_a2) + rw_a0)
    gate7 = jax.nn.sigmoid(gd) @ rw_g2
    kk = l2_normalize(rw_heads(k7 * rw_kk))
    kd = rw_heads(k7[:, :, None, :] * (1 + (a - 1) * rw_ka))
    rh, vh = rw_heads(r), rw_heads(v7)
    y = rwkv7_scan(fold(rh, rh), fold(rw_heads(decay[:, :, 0]), rw_heads(decay[:, :, 1])),
                   fold(kd[:, :, 0], kd[:, :, 1]), fold(vh, vh), fold(kk, kk),
                   fold(rw_heads(a[:, :, 0]), rw_heads(a[:, :, 1])))
    y = head_group_norm(unfold(y), rw_gn_w, rw_gn_b)
    bonus = (jnp.einsum('bthn,btdhn->bth', rh * rw_rk, kd)[..., None] * vh).reshape(b, t, RW_W)
    o_rw = (y + bonus) * gate7

    out = jnp.concatenate([o_dn, o_rw], axis=-1)
    return out[:, :n_ctx], out[:, n_ctx:]


def attention_mixer(xc, xl, w_in, gq_q_norm, gq_k_norm, ml_q_norm, ml_w_uq, ml_kv_norm, ml_w_ukv,
                    rope_gq, rope_ml, need_ctx):
    def prep(p, rope_a, rope_b):
        b, t, _ = p.shape
        q, k, v, cq, ckv, kr = split_cols(p, OD_SIZES)
        q = rms_norm(q.reshape(b, t, GQ_HEADS, GQ_HD), gq_q_norm)
        k = rms_norm(k.reshape(b, t, GQ_KV_HEADS, GQ_HD), gq_k_norm)
        v = v.reshape(b, t, GQ_KV_HEADS, GQ_HD)
        qm = (rms_norm(cq, ml_q_norm) @ ml_w_uq).reshape(b, t, ML_HEADS, ML_NOPE + ML_ROPE)
        kvm = (rms_norm(ckv, ml_kv_norm) @ ml_w_ukv).reshape(b, t, ML_HEADS, ML_NOPE + ML_V)
        q_nope, q_pe = qm[..., :ML_NOPE], qm[..., ML_NOPE:]
        k_nope, vm = kvm[..., :ML_NOPE], kvm[..., ML_NOPE:]
        k_pe = kr.reshape(b, t, 1, ML_ROPE)
        if rope_a is not None:
            q, k = apply_axial_rope(q, *rope_a), apply_axial_rope(k, *rope_a)
            q_pe, k_pe = apply_axial_rope(q_pe, *rope_b), apply_axial_rope(k_pe, *rope_b)
        qm = jnp.concatenate([q_nope, q_pe], axis=-1)[:, :, :, None, :]
        km = jnp.concatenate([k_nope, jnp.broadcast_to(k_pe, (b, t, ML_HEADS, ML_ROPE))], axis=-1)
        qg = q.reshape(b, t, GQ_KV_HEADS, GQ_HEADS // GQ_KV_HEADS, GQ_HD)
        return qg, k, v, qm, km, vm

    qg_c, kg_c, vg_c, qm_c, km_c, vm_c = prep(xc @ w_in, None, None)
    qg_l, kg_l, vg_l, qm_l, km_l, vm_l = prep(xl @ w_in, rope_gq, rope_ml)
    kg, vg = jnp.concatenate([kg_c, kg_l], axis=1), jnp.concatenate([vg_c, vg_l], axis=1)
    km, vm = jnp.concatenate([km_c, km_l], axis=1), jnp.concatenate([vm_c, vm_l], axis=1)
    ol = jnp.concatenate([blocked_attention(qg_l, kg, vg), blocked_attention(qm_l, km, vm)], axis=-1)
    oc = None
    if need_ctx:
        oc = jnp.concatenate([blocked_attention(qg_c, kg_c, vg_c), blocked_attention(qm_c, km_c, vm_c)], axis=-1)
    return oc, ol


def setup_inputs(seed: int = 0) -> dict:
    key = jax.random.key(seed)
    keys = list(jax.random.split(key, 64))

    def nrm(shape, scale):
        return jax.random.normal(keys.pop(), shape, jnp.float32) * scale

    def unif(shape, lo, hi):
        return jax.random.uniform(keys.pop(), shape, jnp.float32, lo, hi)

    def gain(shape):
        return 1.0 + nrm(shape, 0.02)

    D, F, L, E, O = D_MODEL, D_FF, DEPTH, N_EVEN, N_ODD
    dt = jnp.exp(unif((E, 2, DN_HEADS), float(np.log(1e-3)), float(np.log(1e-1))))
    return {
        'x': nrm((BATCH, SEQ, D), 1.0),
        'c': nrm((BATCH, D), 1.0),
        'ctx': nrm((BATCH, CTX_LEN, D), 1.0),
        'c_ctx': nrm((D,), 1.0),
        'mod_w': nrm((L, D, N_MOD * D), 0.5 * D ** -0.5),
        'mod_b': nrm((L, N_MOD * D), 0.02),
        'norm_ffn1': gain((L, D)),
        'norm_mix': gain((L, D)),
        'norm_ffn2': gain((L, D)),
        'ffn1_w1': nrm((L, D, F), D ** -0.5),
        'ffn1_w3': nrm((L, D, F), D ** -0.5),
        'ffn1_w2': nrm((L, F, D), F ** -0.5),
        'ffn2_w1': nrm((L, D, F), D ** -0.5),
        'ffn2_w3': nrm((L, D, F), D ** -0.5),
        'ffn2_w2': nrm((L, F, D), F ** -0.5),
        'ev_w_in': nrm((E, D, EV_IN), D ** -0.5),
        'ev_w_out': nrm((E, MIX_W, D), MIX_W ** -0.5),
        'dn_conv': nrm((E, DN_CONV, 3 * DN_W), DN_CONV ** -0.5),
        'dn_a_log': jnp.log(unif((E, 2, DN_HEADS), 1.0, 16.0)),
        'dn_dt_bias': dt + jnp.log(-jnp.expm1(-dt)),
        'dn_norm': gain((E, DN_DV)),
        'rw_mu': unif((E, RW_SLAB), 0.0, 1.0),
        'rw_w0': unif((E, 2, RW_W), -6.0, -1.0),
        'rw_w2': nrm((E, 2, RW_W_LORA, RW_W), 0.1),
        'rw_a0': nrm((E, 2, RW_W), 0.1),
        'rw_a2': nrm((E, 2, RW_A_LORA, RW_W), 0.1),
        'rw_g2': nrm((E, RW_G_LORA, RW_W), RW_G_LORA ** -0.5),
        'rw_kk': 0.85 + nrm((E, RW_W), 0.02),
        'rw_ka': 1.0 + nrm((E, RW_W), 0.02),
        'rw_rk': nrm((E, RW_HEADS, RW_HS), 0.1),
        'rw_gn_w': gain((E, RW_W)),
        'rw_gn_b': nrm((E, RW_W), 0.02),
        'od_w_in': nrm((O, D, OD_IN), D ** -0.5),
        'od_w_out': nrm((O, MIX_W, D), MIX_W ** -0.5),
        'gq_q_norm': gain((O, GQ_HD)),
        'gq_k_norm': gain((O, GQ_HD)),
        'ml_q_norm': gain((O, ML_Q_LORA)),
        'ml_w_uq': nrm((O, ML_Q_LORA, ML_HEADS * (ML_NOPE + ML_ROPE)), ML_Q_LORA ** -0.5),
        'ml_kv_norm': gain((O, ML_KV_LORA)),
        'ml_w_ukv': nrm((O, ML_KV_LORA, ML_HEADS * (ML_NOPE + ML_V)), ML_KV_LORA ** -0.5),
        'final_norm': gain((D,)),
    }


def reference(x, c, ctx, c_ctx, mod_w, mod_b, norm_ffn1, norm_mix, norm_ffn2,
              ffn1_w1, ffn1_w3, ffn1_w2, ffn2_w1, ffn2_w3, ffn2_w2,
              ev_w_in, ev_w_out, dn_conv, dn_a_log, dn_dt_bias, dn_norm,
              rw_mu, rw_w0, rw_w2, rw_a0, rw_a2, rw_g2, rw_kk, rw_ka, rw_rk, rw_gn_w, rw_gn_b,
              od_w_in, od_w_out, gq_q_norm, gq_k_norm, ml_q_norm, ml_w_uq, ml_kv_norm, ml_w_ukv,
              final_norm):
    b, n_lat, _ = x.shape
    rows = n_lat // GRID_W
    row = jnp.repeat(jnp.arange(rows), GRID_W)
    col = jnp.tile(jnp.arange(GRID_W), rows)
    rope_gq = axial_rope_tables(row, col, GQ_HD)
    rope_ml = axial_rope_tables(row, col, ML_ROPE)
    sc, scc = jax.nn.silu(c), jax.nn.silu(c_ctx)
    hl, hc = x, ctx
    for i in range(DEPTH):
        last = i == DEPTH - 1
        mod_l = (sc @ mod_w[i] + mod_b[i]).reshape(b, 1, N_MOD, D_MODEL)
        mod_c = (scc @ mod_w[i] + mod_b[i]).reshape(N_MOD, D_MODEL)
        ml = [mod_l[:, :, j] for j in range(N_MOD)]
        mc = [mod_c[j] for j in range(N_MOD)]
        f1 = (ffn1_w1[i], ffn1_w3[i], ffn1_w2[i])
        hl = hl + 0.5 * ml[2] * swiglu(modulate(hl, norm_ffn1[i], ml[0], ml[1]), *f1)
        hc = hc + 0.5 * mc[2] * swiglu(modulate(hc, norm_ffn1[i], mc[0], mc[1]), *f1)
        xl = modulate(hl, norm_mix[i], ml[3], ml[4])
        xc = modulate(hc, norm_mix[i], mc[3], mc[4])
        j = i // 2
        if i % 2 == 0:
            oc, ol = recurrent_mixer(xc, xl, ev_w_in[j], dn_conv[j], dn_a_log[j], dn_dt_bias[j], dn_norm[j],
                                     rw_mu[j], rw_w0[j], rw_w2[j], rw_a0[j], rw_a2[j], rw_g2[j],
                                     rw_kk[j], rw_ka[j], rw_rk[j], rw_gn_w[j], rw_gn_b[j])
            w_out = ev_w_out[j]
        else:
            oc, ol = attention_mixer(xc, xl, od_w_in[j], gq_q_norm[j], gq_k_norm[j], ml_q_norm[j],
                                     ml_w_uq[j], ml_kv_norm[j], ml_w_ukv[j], rope_gq, rope_ml,
                                     need_ctx=not last)
            w_out = od_w_out[j]
        hl = hl + ml[5] * (ol @ w_out)
        f2 = (ffn2_w1[i], ffn2_w3[i], ffn2_w2[i])
        hl = hl + 0.5 * ml[8] * swiglu(modulate(hl, norm_ffn2[i], ml[6], ml[7]), *f2)
        if not last:
            hc = hc + mc[5] * (oc @ w_out)
            hc = hc + 0.5 * mc[8] * swiglu(modulate(hc, norm_ffn2[i], mc[6], mc[7]), *f2)
    return rms_norm(hl, final_norm)
```

```python
import functools

import jax
import jax.numpy as jnp
import numpy as np
from jax import lax
from jax.experimental import pallas as pl
from jax.experimental.pallas import tpu as pltpu

F32 = jnp.float32
BF16 = jnp.bfloat16

NORM_EPS = 1e-6
ROPE_THETA = 10000.0
GRID_W = 64
N_MOD = 9

DN_HEADS = 4
DN_DK = 128
DN_CONV = 5
RW_HEADS = 8
RW_HS = 64
RW_W = RW_HEADS * RW_HS
RW_W_LORA = 64
RW_A_LORA = 64
RW_G_LORA = 160
RW_GN_EPS = 64e-5
GQ_HEADS = 8
GQ_KV_HEADS = 2
GQ_HD = 64
ML_HEADS = 8
ML_NOPE = 64
ML_ROPE = 32
ML_V = 64

TILE = 256
CHUNK = 64
HALO = 8
LANES = 128
VMEM_LIMIT = 56 * 1024 * 1024


def _cparams(*sem):
    return pltpu.CompilerParams(dimension_semantics=sem, vmem_limit_bytes=VMEM_LIMIT)


def _resident(shape):
    nd = len(shape)
    return pl.BlockSpec(shape, lambda *_: (0,) * nd, pipeline_mode=pl.Buffered(1))


def _mm(a, b):
    return jnp.dot(a.astype(BF16), b.astype(BF16), preferred_element_type=F32)


def _mm_nt(a, b):
    return lax.dot_general(a.astype(BF16), b.astype(BF16), (((1,), (1,)), ((), ())),
                           preferred_element_type=F32)


def _mm_tn(a, b):
    return lax.dot_general(a.astype(BF16), b.astype(BF16), (((0,), (0,)), ((), ())),
                           preferred_element_type=F32)


def _split2(x):
    hi = x.astype(BF16)
    lo = (x - hi.astype(F32)).astype(BF16)
    return hi, lo


def _split3(x):
    hi = x.astype(BF16)
    r = x - hi.astype(F32)
    mid = r.astype(BF16)
    lo = (r - mid.astype(F32)).astype(BF16)
    return hi, mid, lo


def _dot(a, b):
    return jnp.dot(a, b, preferred_element_type=F32)


def _mm3s(asp, bsp):
    (ah, al), (bh, bl) = asp, bsp
    return _dot(ah, bh) + (_dot(ah, bl) + _dot(al, bh))


def _mm3(a, b):
    return _mm3s(_split2(a), _split2(b))


def _mm_exact_lhs(a01, b):
    a = a01.astype(BF16)
    hi, mid, lo = _split3(b)
    return _dot(a, hi) + (_dot(a, mid) + _dot(a, lo))


def _rms_rows(x):
    return x * lax.rsqrt(jnp.mean(x * x, axis=-1, keepdims=True) + NORM_EPS)


def _modulate(x, gain, shift, scale):
    return (_rms_rows(x) * gain) * (1.0 + scale) + shift


def _silu(x):
    return x * jax.nn.sigmoid(x)


def _softplus(x):
    return jnp.maximum(x, 0.0) + jnp.log1p(jnp.exp(-jnp.abs(x)))


def _mod_kernel(c_ref, w_ref, b_ref, o_ref):
    s = _silu(c_ref[...])
    o_ref[0] = _mm3(s, w_ref[0]) + b_ref[0]


def _mod_call(cc, mod_w, mod_b):
    n_layers, d, n = mod_w.shape
    r = cc.shape[0]
    tn = n // 8
    return pl.pallas_call(
        _mod_kernel,
        out_shape=jax.ShapeDtypeStruct((n_layers, r, n), F32),
        grid=(n_layers, n // tn),
        in_specs=[pl.BlockSpec((r, d), lambda l, j: (0, 0)),
                  pl.BlockSpec((1, d, tn), lambda l, j: (l, 0, j)),
                  pl.BlockSpec((1, 1, tn), lambda l, j: (l, 0, j))],
        out_specs=pl.BlockSpec((1, r, tn), lambda l, j: (l, 0, j)),
        compiler_params=_cparams("parallel", "parallel"),
        name="adaln_mod",
    )(cc, mod_w, mod_b.reshape(n_layers, 1, n))


def _all_mod(c, c_ctx, mod_w, mod_b):
    n_batch, d = c.shape
    rows = -(-(n_batch + 1) // 8) * 8
    cc = jnp.zeros((rows, d), F32).at[:n_batch].set(c).at[n_batch].set(c_ctx)
    return _mod_call(cc, mod_w, mod_b).reshape(mod_w.shape[0], rows, N_MOD, d)


def _mod_row_map(n_batch, ctx_tiles, t_off):
    def index_map(b, t):
        return (jnp.where(t + t_off < ctx_tiles, n_batch, b), 0, 0)
    return index_map


def _ffn_kernel(h_ref, mod_ref, gain_ref, w1_ref, w3_ref, w2_ref, *rest, j0, final):
    o_ref = rest[-1]
    x = h_ref[0]
    shift, scale, gate = mod_ref[0, j0:j0 + 1], mod_ref[0, j0 + 1:j0 + 2], mod_ref[0, j0 + 2:j0 + 3]
    xn = _modulate(x, gain_ref[...], shift, scale).astype(BF16)
    a = _dot(xn, w1_ref[...])
    b = _dot(xn, w3_ref[...])
    f = _dot((_silu(a) * b).astype(BF16), w2_ref[...])
    y = x + (0.5 * gate) * f
    if final:
        y = _rms_rows(y) * rest[0][...]
    o_ref[0] = y


def _ffn_call(h, mod, gain, w1, w3, w2, *, j0, ctx_tiles, t_off=0, final_gain=None):
    n_batch, t_len, d = h.shape
    f = w1.shape[1]
    nt = t_len // TILE - t_off
    final = final_gain is not None
    in_specs = [pl.BlockSpec((1, TILE, d), lambda b, t: (b, t + t_off, 0)),
                pl.BlockSpec((1, N_MOD, d), _mod_row_map(n_batch, ctx_tiles, t_off)),
                _resident((1, d)), _resident((d, f)), _resident((d, f)), _resident((f, d))]
    args = [h, mod, gain.reshape(1, d), w1, w3, w2]
    if final:
        in_specs.append(_resident((1, d)))
        args.append(final_gain.reshape(1, d))
    return pl.pallas_call(
        functools.partial(_ffn_kernel, j0=j0, final=final),
        out_shape=jax.ShapeDtypeStruct((n_batch, nt * TILE, d), F32),
        grid=(n_batch, nt),
        in_specs=in_specs,
        out_specs=pl.BlockSpec((1, TILE, d), lambda b, t: (b, t, 0)),
        compiler_params=_cparams("parallel", "parallel"),
        name="macaron_ffn",
    )(*args)


def _halo_specs(d, ctx_tiles, n_tiles):
    per = TILE // HALO
    last = n_tiles * per - 1
    return [pl.BlockSpec((1, HALO, d), lambda b, t: (b, jnp.maximum(t * per - 1, 0), 0)),
            pl.BlockSpec((1, TILE, d), lambda b, t: (b, t, 0)),
            pl.BlockSpec((1, HALO, d), lambda b, t: (b, jnp.minimum((t + 1) * per, last), 0))]


def _project_with_halo(prev_ref, cur_ref, next_ref, mod_ref, gain_ref, w_ref, pe_ref, *, ctx_tiles, n_tiles):
    t = pl.program_id(1)
    xe = jnp.concatenate([prev_ref[0], cur_ref[0], next_ref[0]], axis=0)
    xn = _modulate(xe, gain_ref[...], mod_ref[0, 3:4], mod_ref[0, 4:5]).astype(BF16)
    p = _dot(xn, w_ref[...])
    row = lax.broadcasted_iota(jnp.int32, (TILE + 2 * HALO, 1), 0)
    prev_ok = jnp.logical_and(t > 0, t != ctx_tiles)
    next_ok = jnp.logical_and(t + 1 < n_tiles, t + 1 != ctx_tiles)
    keep = jnp.logical_and(jnp.logical_or(row >= HALO, prev_ok),
                           jnp.logical_or(row < HALO + TILE, next_ok))
    pe_ref[...] = jnp.where(keep, p, 0.0)


def _dn_prep_kernel(prev_ref, cur_ref, next_ref, mod_ref, gain_ref, w_ref, conv_ref, alog_ref, dtb_ref,
                    q_ref, k_ref, v_ref, small_ref, gate_ref, pe_ref, *, ctx_tiles, n_tiles):
    _project_with_halo(prev_ref, cur_ref, next_ref, mod_ref, gain_ref, w_ref, pe_ref,
                       ctx_tiles=ctx_tiles, n_tiles=n_tiles)
    nqkv = 3 * DN_HEADS * DN_DK
    half = DN_CONV // 2
    acc = None
    for j in range(DN_CONV):
        term = conv_ref[j:j + 1, :] * pe_ref[pl.ds(HALO - half + j, TILE), 0:nqkv]
        acc = term if acc is None else acc + term
    qkv = _silu(acc)
    w = DN_HEADS * DN_DK
    for idx, ref in ((0, q_ref), (1, k_ref)):
        for h in range(DN_HEADS):
            seg = qkv[:, idx * w + h * DN_DK: idx * w + (h + 1) * DN_DK]
            ref[0, :, h * DN_DK:(h + 1) * DN_DK] = seg * lax.rsqrt(jnp.sum(seg * seg, axis=-1, keepdims=True) + 1e-6)
    v_ref[0] = qkv[:, 2 * w:3 * w]
    gate_ref[0] = pe_ref[pl.ds(HALO, TILE), nqkv:nqkv + w]
    ab = pe_ref[pl.ds(HALO, TILE), nqkv + w:nqkv + w + LANES]
    g = -jnp.exp(alog_ref[...]) * _softplus(ab + dtb_ref[...])
    lane = lax.broadcasted_iota(jnp.int32, ab.shape, 1)
    nh2 = 2 * DN_HEADS
    small_ref[0] = jnp.where(lane < nh2, g, jnp.where(lane < 2 * nh2, jax.nn.sigmoid(ab), 0.0))


def _dn_prep_call(h, mod, gain, w, conv, alog, dtb, *, ctx_tiles):
    n_batch, t_len, d = h.shape
    nt = t_len // TILE
    wd = DN_HEADS * DN_DK
    out = lambda n, dt=F32: jax.ShapeDtypeStruct((n_batch, t_len, n), dt)
    ospec = lambda n: pl.BlockSpec((1, TILE, n), lambda b, t: (b, t, 0))
    return pl.pallas_call(
        functools.partial(_dn_prep_kernel, ctx_tiles=ctx_tiles, n_tiles=nt),
        out_shape=[out(wd), out(wd), out(wd), out(LANES), out(wd)],
        grid=(n_batch, nt),
        in_specs=_halo_specs(d, ctx_tiles, nt) + [
            pl.BlockSpec((1, N_MOD, d), _mod_row_map(n_batch, ctx_tiles, 0)),
            _resident((1, d)), _resident(w.shape), _resident(conv.shape),
            _resident((1, LANES)), _resident((1, LANES))],
        out_specs=[ospec(wd), ospec(wd), ospec(wd), ospec(LANES), ospec(wd)],
        scratch_shapes=[pltpu.VMEM((TILE + 2 * HALO, w.shape[1]), F32)],
        compiler_params=_cparams("parallel", "parallel"),
        name="deltanet_prep",
    )(h, h, h, mod, gain.reshape(1, d), w, conv, alog, dtb)


def _head_sums(x, ones_ref):
    hi, mid, lo = _split3(x)
    ones = ones_ref[...]
    return _dot(hi, ones) + (_dot(mid, ones) + _dot(lo, ones))


def _rw_prep_kernel(prev_ref, cur_ref, next_ref, mod_ref, gain_ref, w_ref, mu_ref, w2_ref, w0_ref, a2_ref,
                    a0_ref, g2_ref, kkw_ref, kaw_ref, rk_ref, ones_ref,
                    r_ref, v_ref, kk_ref, lw_ref, kd_ref, kka_ref, gate_ref, bonus_ref, pe_ref,
                    *, ctx_tiles, n_tiles):
    _project_with_halo(prev_ref, cur_ref, next_ref, mod_ref, gain_ref, w_ref, pe_ref,
                       ctx_tiles=ctx_tiles, n_tiles=n_tiles)
    z = pe_ref[pl.ds(HALO, TILE), :]
    zs = 0.5 * (pe_ref[pl.ds(HALO - 1, TILE), :] + pe_ref[pl.ds(HALO + 1, TILE), :])
    s = z + mu_ref[...] * (zs - z)
    r, k7, v7 = s[:, 0:RW_W], s[:, RW_W:2 * RW_W], s[:, 2 * RW_W:3 * RW_W]
    o = 3 * RW_W
    wd, ad, gd = s[:, o:o + LANES], s[:, o + LANES:o + 2 * LANES], s[:, o + 2 * LANES:o + 4 * LANES]
    w_logit = _mm3(jnp.tanh(wd), w2_ref[...]) + w0_ref[...]
    lw = -jnp.exp(-_softplus(-w_logit) - 0.5)
    a = jax.nn.sigmoid(_mm3(ad, a2_ref[...]) + a0_ref[...])
    gate_ref[0] = _mm3(jax.nn.sigmoid(gd), g2_ref[...])
    kx = k7 * kkw_ref[...]
    kk = kx * lax.rsqrt(_head_sums(kx * kx, ones_ref) + 1e-6)
    r_ref[0], v_ref[0], kk_ref[0], lw_ref[0] = r, v7, kk, lw
    kd_sum = None
    for d in range(2):
        a_d = a[:, d * RW_W:(d + 1) * RW_W]
        kd = k7 * (1.0 + (a_d - 1.0) * kaw_ref[...])
        kd_ref[0, :, d * RW_W:(d + 1) * RW_W] = kd
        kka_ref[0, :, d * RW_W:(d + 1) * RW_W] = kk * a_d
        kd_sum = kd if kd_sum is None else kd_sum + kd
    bonus_ref[0] = _head_sums((r * rk_ref[...]) * kd_sum, ones_ref) * v7


def _rw_prep_call(h, mod, gain, w, mu, w2, w0, a2, a0, g2, kkw, kaw, rk, ones, *, ctx_tiles):
    n_batch, t_len, d = h.shape
    nt = t_len // TILE
    out = lambda n: jax.ShapeDtypeStruct((n_batch, t_len, n), F32)
    ospec = lambda n: pl.BlockSpec((1, TILE, n), lambda b, t: (b, t, 0))
    widths = [RW_W, RW_W, RW_W, 2 * RW_W, 2 * RW_W, 2 * RW_W, RW_W, RW_W]
    params = [gain.reshape(1, d), w, mu, w2, w0, a2, a0, g2, kkw, kaw, rk, ones]
    return pl.pallas_call(
        functools.partial(_rw_prep_kernel, ctx_tiles=ctx_tiles, n_tiles=nt),
        out_shape=[out(n) for n in widths],
        grid=(n_batch, nt),
        in_specs=_halo_specs(d, ctx_tiles, nt) + [pl.BlockSpec((1, N_MOD, d), _mod_row_map(n_batch, ctx_tiles, 0))]
        + [_resident(p.shape) for p in params],
        out_specs=[ospec(n) for n in widths],
        scratch_shapes=[pltpu.VMEM((TILE + 2 * HALO, w.shape[1]), F32)],
        compiler_params=_cparams("parallel", "parallel"),
        name="rwkv_prep",
    )(h, h, h, mod, *params)


def _chunk_masks(direction):
    i = lax.broadcasted_iota(jnp.int32, (CHUNK, CHUNK), 0)
    j = lax.broadcasted_iota(jnp.int32, (CHUNK, CHUNK), 1)
    return (i >= j, i > j) if direction == 0 else (i <= j, i < j)


def _last_row(x, direction):
    return x[CHUNK - 1:CHUNK] if direction == 0 else x[0:1]


def _neumann_inverse(n):
    i = lax.broadcasted_iota(jnp.int32, (CHUNK, CHUNK), 0)
    j = lax.broadcasted_iota(jnp.int32, (CHUNK, CHUNK), 1)
    acc = jnp.where(i == j, 1.0, 0.0) + n
    psp = _split2(n)
    span = 2
    while span < CHUNK:
        p = _mm3s(psp, psp)
        psp = _split2(p)
        acc = acc + _mm3s(_split2(acc), psp)
        span *= 2
    return acc


def _rev_chunk(n, ctx_chunks, n_chunks):
    return jnp.where(n < ctx_chunks, ctx_chunks - 1 - n, n_chunks - 1 + ctx_chunks - n)


def _dn_chunk_kernel(q_ref, k_ref, v_ref, small_ref, u_ref, w_ref, qd_ref, kd_ref, attn_ref, gl_ref):
    sm = small_ref[0]
    q, k, v = q_ref[0] * (DN_DK ** -0.5), k_ref[0], v_ref[0]
    nh2 = 2 * DN_HEADS
    for d in range(2):
        incl, strict = _chunk_masks(d)
        gc = _mm_exact_lhs(jnp.where(incl, 1.0, 0.0), sm)
        gc_t = gc.T
        gtot = _last_row(gc, d)
        gl_ref[0, d, 0] = jnp.exp(gtot)
        for h in range(DN_HEADS):
            c = DN_HEADS * d + h
            sl = slice(h * DN_DK, (h + 1) * DN_DK)
            gcc, gcr, gt, beta = gc[:, c:c + 1], gc_t[c:c + 1, :], gtot[:, c:c + 1], sm[:, nh2 + c:nh2 + c + 1]
            decay = jnp.exp(jnp.where(incl, gcc - gcr, -1e30))
            kh, qh, vh = k[:, sl], q[:, sl], v[:, sl]
            lower = jnp.where(strict, (beta * _mm_nt(kh, kh)) * decay, 0.0)
            t_inv = _neumann_inverse(-lower)
            egc = jnp.exp(gcc)
            kb = kh * beta
            sol = _mm3(t_inv, jnp.concatenate([vh * beta, kb * egc], axis=1))
            u_ref[0, d, :, sl] = sol[:, :DN_DK]
            w_ref[0, d, :, sl] = sol[:, DN_DK:].astype(BF16)
            qd_ref[0, d, :, sl] = (qh * egc).astype(BF16)
            kd_ref[0, d, :, sl] = (kh * jnp.exp(gt - gcc)).astype(BF16)
            attn_ref[0, d, :, h * CHUNK:(h + 1) * CHUNK] = (_mm_nt(qh, kh) * decay).astype(BF16)


def _dn_chunk_call(q, k, v, small):
    n_batch, t_len, wd = q.shape
    nc = t_len // CHUNK
    ispec = lambda n: pl.BlockSpec((1, CHUNK, n), lambda b, c: (b, c, 0))
    ospec = lambda n: pl.BlockSpec((1, 2, CHUNK, n), lambda b, c: (b, 0, c, 0))
    shp = lambda n, dt: jax.ShapeDtypeStruct((n_batch, 2, t_len, n), dt)
    return pl.pallas_call(
        _dn_chunk_kernel,
        out_shape=[shp(wd, F32), shp(wd, BF16), shp(wd, BF16), shp(wd, BF16), shp(DN_HEADS * CHUNK, BF16),
                   jax.ShapeDtypeStruct((n_batch, 2, nc, 1, LANES), F32)],
        grid=(n_batch, nc),
        in_specs=[ispec(wd), ispec(wd), ispec(wd), ispec(LANES)],
        out_specs=[ospec(wd), ospec(wd), ospec(wd), ospec(wd), ospec(DN_HEADS * CHUNK),
                   pl.BlockSpec((1, 2, 1, 1, LANES), lambda b, c: (b, 0, c, 0, 0))],
        compiler_params=_cparams("parallel", "parallel"),
        name="deltanet_chunk_prep",
    )(q, k, v, small)


def _dn_scan_kernel(*refs):
    ins, (of_ref, ob_ref, s_ref) = refs[:12], refs[12:]

    @pl.when(pl.program_id(1) == 0)
    def _():
        s_ref[...] = jnp.zeros_like(s_ref)

    for d, o_ref in ((0, of_ref), (1, ob_ref)):
        u_ref, w_ref, qd_ref, kd_ref, attn_ref, gl_ref = ins[6 * d:6 * d + 6]
        for h in range(DN_HEADS):
            sl = slice(h * DN_DK, (h + 1) * DN_DK)
            s = s_ref[d, h]
            sb = s.astype(BF16)
            v_new = u_ref[0, 0, :, sl] - _dot(w_ref[0, 0, :, sl], sb)
            vb = v_new.astype(BF16)
            o_ref[0, :, sl] = _dot(qd_ref[0, 0, :, sl], sb) + _dot(attn_ref[0, 0, :, h * CHUNK:(h + 1) * CHUNK], vb)
            c = DN_HEADS * d + h
            s_ref[d, h] = s * gl_ref[0, 0, 0, :, c:c + 1] + _mm_tn(kd_ref[0, 0, :, sl], vb)


def _dn_scan_call(u, w, qd, kd, attn, gl, *, ctx_chunks):
    n_batch, _, t_len, wd = u.shape
    nc = t_len // CHUNK
    in_specs, args = [], []
    for d in range(2):
        chunk = (lambda n: n) if d == 0 else functools.partial(_rev_chunk, ctx_chunks=ctx_chunks, n_chunks=nc)
        for arr in (u, w, qd, kd, attn):
            in_specs.append(pl.BlockSpec((1, 1, CHUNK, arr.shape[-1]),
                                         lambda b, n, d=d, chunk=chunk: (b, d, chunk(n), 0)))
        in_specs.append(pl.BlockSpec((1, 1, 1, 1, LANES), lambda b, n, d=d, chunk=chunk: (b, d, chunk(n), 0, 0)))
        args += [u, w, qd, kd, attn, gl]
    return pl.pallas_call(
        _dn_scan_kernel,
        out_shape=[jax.ShapeDtypeStruct((n_batch, t_len, wd), F32)] * 2,
        grid=(n_batch, nc),
        in_specs=in_specs,
        out_specs=[pl.BlockSpec((1, CHUNK, wd), lambda b, n: (b, n, 0)),
                   pl.BlockSpec((1, CHUNK, wd), lambda b, n: (b, _rev_chunk(n, ctx_chunks, nc), 0))],
        scratch_shapes=[pltpu.VMEM((2, DN_HEADS, DN_DK, DN_DK), F32)],
        compiler_params=_cparams("parallel", "arbitrary"),
        name="deltanet_scan",
    )(*args)


def _rw_chunk_kernel(r_ref, v_ref, kk_ref, lw_ref, kd_ref, kka_ref,
                     ut_ref, wt_ref, rt_ref, arb_ref, bt_ref, y0_ref, sadd_ref, pc_ref):
    r, v, kk = r_ref[0], v_ref[0], kk_ref[0]
    hs = RW_HS
    for d in range(2):
        dsl = slice(d * RW_W, (d + 1) * RW_W)
        lw, kd, kka = lw_ref[0, :, dsl], kd_ref[0, :, dsl], kka_ref[0, :, dsl]
        incl, strict = _chunk_masks(d)
        cl = _mm_exact_lhs(jnp.where(incl, 1.0, 0.0), lw)
        tot = _last_row(cl, d)
        p_inv, p_tail = jnp.exp(-cl), jnp.exp(tot - cl)
        at = -kk * jnp.exp(cl - lw)
        rt = r * jnp.exp(cl)
        kh, bh, kt, bt = kd * p_inv, kka * p_inv, kd * p_tail, kka * p_tail
        pc_ref[0, d, 0] = jnp.exp(tot)
        rt_ref[0, d] = rt.astype(BF16)
        bt_ref[0, d] = bt.astype(BF16)
        for h in range(RW_HEADS):
            sl = slice(h * hs, (h + 1) * hs)
            vh = v[:, sl]
            aa = _mm_nt(jnp.concatenate([at[:, sl], rt[:, sl]], axis=0), jnp.concatenate([bh[:, sl], kh[:, sl]], axis=0))
            a_ab = jnp.where(strict, aa[:CHUNK, :CHUNK], 0.0)
            a_ak = jnp.where(strict, aa[:CHUNK, CHUNK:], 0.0)
            a_rb = jnp.where(incl, aa[CHUNK:, :CHUNK], 0.0)
            a_rk = jnp.where(incl, aa[CHUNK:, CHUNK:], 0.0)
            t_inv = _neumann_inverse(a_ab)
            sol = _mm3(t_inv, jnp.concatenate([at[:, sl], _mm(a_ak, vh)], axis=1))
            wt_ref[0, d, :, sl] = sol[:, :hs].astype(BF16)
            ut_ref[0, d, :, sl] = sol[:, hs:]
            arb_ref[0, d, :, sl] = a_rb.astype(BF16)
            y0_ref[0, d, :, sl] = _mm(a_rk, vh)
            sadd_ref[0, d, 0, :, sl] = _mm_tn(vh, kt[:, sl])


def _rw_chunk_call(r, v, kk, lw, kd, kka):
    n_batch, t_len, wd = r.shape
    nc = t_len // CHUNK
    ispec = lambda n: pl.BlockSpec((1, CHUNK, n), lambda b, c: (b, c, 0))
    ospec = pl.BlockSpec((1, 2, CHUNK, wd), lambda b, c: (b, 0, c, 0))
    shp = lambda dt: jax.ShapeDtypeStruct((n_batch, 2, t_len, wd), dt)
    return pl.pallas_call(
        _rw_chunk_kernel,
        out_shape=[shp(F32), shp(BF16), shp(BF16), shp(BF16), shp(BF16), shp(F32),
                   jax.ShapeDtypeStruct((n_batch, 2, nc, RW_HS, wd), F32),
                   jax.ShapeDtypeStruct((n_batch, 2, nc, 1, wd), F32)],
        grid=(n_batch, nc),
        in_specs=[ispec(wd), ispec(wd), ispec(wd), ispec(2 * wd), ispec(2 * wd), ispec(2 * wd)],
        out_specs=[ospec] * 6 + [pl.BlockSpec((1, 2, 1, RW_HS, wd), lambda b, c: (b, 0, c, 0, 0)),
                                 pl.BlockSpec((1, 2, 1, 1, wd), lambda b, c: (b, 0, c, 0, 0))],
        compiler_params=_cparams("parallel", "parallel"),
        name="rwkv_chunk_prep",
    )(r, v, kk, lw, kd, kka)


def _rw_scan_kernel(*refs):
    ins, (yf_ref, yb_ref, s_ref) = refs[:16], refs[16:]

    @pl.when(pl.program_id(1) == 0)
    def _():
        s_ref[...] = jnp.zeros_like(s_ref)

    for d, y_ref in ((0, yf_ref), (1, yb_ref)):
        ut_ref, wt_ref, rt_ref, arb_ref, bt_ref, y0_ref, sadd_ref, pc_ref = ins[8 * d:8 * d + 8]
        for h in range(RW_HEADS):
            sl = slice(h * RW_HS, (h + 1) * RW_HS)
            s = s_ref[d, h]
            sb = s.astype(BF16)
            u = ut_ref[0, 0, :, sl] + _mm_nt(wt_ref[0, 0, :, sl], sb)
            ub = u.astype(BF16)
            y_ref[0, :, sl] = y0_ref[0, 0, :, sl] + _mm_nt(rt_ref[0, 0, :, sl], sb) + _dot(arb_ref[0, 0, :, sl], ub)
            s_ref[d, h] = s * pc_ref[0, 0, 0, :, sl] + sadd_ref[0, 0, 0, :, sl] + _mm_tn(ub, bt_ref[0, 0, :, sl])


def _rw_scan_call(ut, wt, rt, arb, bt, y0, sadd, pc, *, ctx_chunks):
    n_batch, _, t_len, wd = ut.shape
    nc = t_len // CHUNK
    in_specs, args = [], []
    for d in range(2):
        chunk = (lambda n: n) if d == 0 else functools.partial(_rev_chunk, ctx_chunks=ctx_chunks, n_chunks=nc)
        for arr in (ut, wt, rt, arb, bt, y0):
            in_specs.append(pl.BlockSpec((1, 1, CHUNK, wd), lambda b, n, d=d, chunk=chunk: (b, d, chunk(n), 0)))
        in_specs.append(pl.BlockSpec((1, 1, 1, RW_HS, wd), lambda b, n, d=d, chunk=chunk: (b, d, chunk(n), 0, 0)))
        in_specs.append(pl.BlockSpec((1, 1, 1, 1, wd), lambda b, n, d=d, chunk=chunk: (b, d, chunk(n), 0, 0)))
        args += [ut, wt, rt, arb, bt, y0, sadd, pc]
    return pl.pallas_call(
        _rw_scan_kernel,
        out_shape=[jax.ShapeDtypeStruct((n_batch, t_len, wd), F32)] * 2,
        grid=(n_batch, nc),
        in_specs=in_specs,
        out_specs=[pl.BlockSpec((1, CHUNK, wd), lambda b, n: (b, n, 0)),
                   pl.BlockSpec((1, CHUNK, wd), lambda b, n: (b, _rev_chunk(n, ctx_chunks, nc), 0))],
        scratch_shapes=[pltpu.VMEM((2, RW_HEADS, RW_HS, RW_HS), F32)],
        compiler_params=_cparams("parallel", "arbitrary"),
        name="rwkv_scan",
    )(*args)


def _ev_out_kernel(h_ref, mod_ref, of_ref, ob_ref, dgate_ref, yf_ref, yb_ref, gate7_ref, bonus_ref,
                   dnorm_ref, gnw_ref, gnb_ref, ones_ref, wout_ref, o_ref):
    o = of_ref[0] + ob_ref[0]
    dgate = dgate_ref[0]
    parts = []
    for h in range(DN_HEADS):
        sl = slice(h * DN_DK, (h + 1) * DN_DK)
        parts.append(_rms_rows(o[:, sl]) * dnorm_ref[:, sl] * _silu(dgate[:, sl]))
    o_dn = jnp.concatenate(parts, axis=1)
    y = yf_ref[0] + yb_ref[0]
    inv_n = 1.0 / RW_HS
    mu = _head_sums(y, ones_ref) * inv_n
    yc = y - mu
    var = _head_sums(yc * yc, ones_ref) * inv_n
    yn = yc * lax.rsqrt(var + RW_GN_EPS) * gnw_ref[...] + gnb_ref[...]
    o_rw = (yn + bonus_ref[0]) * gate7_ref[0]
    wd = DN_HEADS * DN_DK
    proj = _dot(o_dn.astype(BF16), wout_ref[0:wd, :]) + _dot(o_rw.astype(BF16), wout_ref[wd:, :])
    o_ref[0] = h_ref[0] + mod_ref[0, 5:6] * proj


def _ev_out_call(h, mod, o_f, o_b, dgate, y_f, y_b, gate7, bonus, dnorm, gnw, gnb, ones, w_out, *, ctx_tiles):
    n_batch, t_len, d = h.shape
    nt = t_len // TILE
    tile = lambda n: pl.BlockSpec((1, TILE, n), lambda b, t: (b, t, 0))
    params = [dnorm.reshape(1, -1), gnw.reshape(1, -1), gnb.reshape(1, -1), ones, w_out]
    streams = [o_f, o_b, dgate, y_f, y_b, gate7, bonus]
    return pl.pallas_call(
        _ev_out_kernel,
        out_shape=jax.ShapeDtypeStruct(h.shape, F32),
        grid=(n_batch, nt),
        in_specs=[tile(d), pl.BlockSpec((1, N_MOD, d), _mod_row_map(n_batch, ctx_tiles, 0))]
        + [tile(s.shape[-1]) for s in streams] + [_resident(p.shape) for p in params],
        out_specs=tile(d),
        compiler_params=_cparams("parallel", "parallel"),
        name="even_mix_out",
    )(h, mod, *streams, *params)


def _rope_layout(width, blocks):
    perm = np.zeros((width, width), np.float32)
    angle = np.full((width,), -1, np.int64)
    for start, rot in blocks:
        q = rot // 4
        for blk in range(2):
            for idx in range(q):
                l1 = start + blk * 2 * q + idx
                l2 = l1 + q
                perm[l2, l1], perm[l1, l2] = -1.0, 1.0
                angle[l1] = angle[l2] = blk * q + idx
    return perm, angle


def _rope_tables(n_ctx, n_lat, rot, angle):
    rows = n_lat // GRID_W
    row = jnp.repeat(jnp.arange(rows), GRID_W).astype(F32)
    col = jnp.tile(jnp.arange(GRID_W), rows).astype(F32)
    axis_dim = rot // 2
    inv = ROPE_THETA ** (-jnp.arange(0, axis_dim, 2, dtype=F32) / axis_dim)
    ang = jnp.concatenate([row[:, None] * inv, col[:, None] * inv], axis=-1)
    on = jnp.asarray(angle >= 0)
    idx = np.maximum(angle, 0)
    cos = jnp.where(on, jnp.cos(ang)[:, idx], 1.0)
    sin = jnp.where(on, jnp.sin(ang)[:, idx], 0.0)
    width = angle.shape[0]
    return (jnp.concatenate([jnp.ones((n_ctx, width), F32), cos], axis=0),
            jnp.concatenate([jnp.zeros((n_ctx, width), F32), sin], axis=0))


def _rope(x, perm_ref, cos_ref, sin_ref):
    hi, mid, lo = _split3(x)
    p = perm_ref[...]
    return x * cos_ref[...] + (_dot(hi, p) + (_dot(mid, p) + _dot(lo, p))) * sin_ref[...]


def _od_prep_kernel(h_ref, mod_ref, gain_ref, w_ref, qn_ref, kn_ref, mqn_ref, wuq_ref, mkvn_ref, wukv_ref,
                    ones_q_ref, ones_k_ref, pq_ref, pk_ref, pm_ref, pr_ref,
                    cq_ref, sq_ref, ck_ref, sk_ref, cm_ref, sm_ref, cr_ref, sr_ref,
                    qg_ref, qm_ref, kgt_ref, vg_ref, kmt_ref, vm_ref):
    xn = _modulate(h_ref[0], gain_ref[...], mod_ref[0, 3:4], mod_ref[0, 4:5]).astype(BF16)
    p = _dot(xn, w_ref[...])
    nq, nkv = GQ_HEADS * GQ_HD, GQ_KV_HEADS * GQ_HD
    o = 0
    q, o = p[:, o:o + nq], o + nq
    k, o = p[:, o:o + nkv], o + nkv
    v, o = p[:, o:o + nkv], o + nkv
    n_cq, n_ckv = mqn_ref.shape[1], mkvn_ref.shape[1]
    cq, o = p[:, o:o + n_cq], o + n_cq
    ckv, o = p[:, o:o + n_ckv], o + n_ckv
    kr = p[:, o:o + LANES]
    inv_hd = 1.0 / GQ_HD
    q = q * lax.rsqrt(_head_sums(q * q, ones_q_ref) * inv_hd + NORM_EPS) * qn_ref[...]
    k = k * lax.rsqrt(_head_sums(k * k, ones_k_ref) * inv_hd + NORM_EPS) * kn_ref[...]
    qm = _dot((_rms_rows(cq) * mqn_ref[...]).astype(BF16), wuq_ref[...])
    kvm = _dot((_rms_rows(ckv) * mkvn_ref[...]).astype(BF16), wukv_ref[...])
    q = _rope(q, pq_ref, cq_ref, sq_ref) * (GQ_HD ** -0.5)
    k = _rope(k, pk_ref, ck_ref, sk_ref)
    qm = _rope(qm, pm_ref, cm_ref, sm_ref) * ((ML_NOPE + ML_ROPE) ** -0.5)
    kr = _rope(kr, pr_ref, cr_ref, sr_ref)
    qg_ref[0] = q.astype(BF16)
    qm_ref[0] = qm.astype(BF16)
    vg_ref[0] = v.astype(BF16)
    n_nope = ML_HEADS * ML_NOPE
    vm_ref[0] = kvm[:, n_nope:].astype(BF16)
    kgt_ref[0] = k.T.astype(BF16)
    knt = kvm[:, :n_nope].T.astype(BF16)
    krt = kr.T[:ML_ROPE].astype(BF16)
    dk = ML_NOPE + ML_ROPE
    for h in range(ML_HEADS):
        kmt_ref[0, h * dk:h * dk + ML_NOPE, :] = knt[h * ML_NOPE:(h + 1) * ML_NOPE]
        kmt_ref[0, h * dk + ML_NOPE:(h + 1) * dk, :] = krt


def _od_prep_call(h, mod, gain, params, tables, *, ctx_tiles):
    n_batch, t_len, d = h.shape
    nt = t_len // TILE
    nq, nkv = GQ_HEADS * GQ_HD, GQ_KV_HEADS * GQ_HD
    dk = ML_NOPE + ML_ROPE
    tile = lambda n: pl.BlockSpec((1, TILE, n), lambda b, t: (b, t, 0))
    tile_t = lambda n: pl.BlockSpec((1, n, TILE), lambda b, t: (b, 0, t))
    tab = lambda a: pl.BlockSpec((TILE, a.shape[1]), lambda b, t: (t, 0))
    shp = lambda *s: jax.ShapeDtypeStruct((n_batch,) + s, BF16)
    return pl.pallas_call(
        _od_prep_kernel,
        out_shape=[shp(t_len, nq), shp(t_len, ML_HEADS * dk), shp(nkv, t_len), shp(t_len, nkv),
                   shp(ML_HEADS * dk, t_len), shp(t_len, ML_HEADS * ML_V)],
        grid=(n_batch, nt),
        in_specs=[tile(d), pl.BlockSpec((1, N_MOD, d), _mod_row_map(n_batch, ctx_tiles, 0)), _resident((1, d))]
        + [_resident(p.shape) for p in params] + [tab(a) for a in tables],
        out_specs=[tile(nq), tile(ML_HEADS * dk), tile_t(nkv), tile(nkv), tile_t(ML_HEADS * dk), tile(ML_HEADS * ML_V)],
        compiler_params=_cparams("parallel", "parallel"),
        name="attn_prep",
    )(h, mod, gain.reshape(1, d), *params, *tables)


def _odd_layer_weights(od_w_in, gq_q_norm, gq_k_norm, ml_q_norm, ml_w_uq, ml_kv_norm, ml_w_ukv):
    d = od_w_in.shape[0]
    w = jnp.concatenate([od_w_in, jnp.zeros((d, LANES - ML_ROPE), F32)], axis=1)
    ukv = ml_w_ukv.reshape(ml_w_ukv.shape[0], ML_HEADS, ML_NOPE + ML_V)
    ukv = jnp.concatenate([ukv[:, :, :ML_NOPE].reshape(-1, ML_HEADS * ML_NOPE),
                           ukv[:, :, ML_NOPE:].reshape(-1, ML_HEADS * ML_V)], axis=1)
    dk = ML_NOPE + ML_ROPE
    pq, aq = _rope_layout(GQ_HEADS * GQ_HD, [(h * GQ_HD, GQ_HD) for h in range(GQ_HEADS)])
    pk, ak = _rope_layout(GQ_KV_HEADS * GQ_HD, [(h * GQ_HD, GQ_HD) for h in range(GQ_KV_HEADS)])
    pm, am = _rope_layout(ML_HEADS * dk, [(h * dk + ML_NOPE, ML_ROPE) for h in range(ML_HEADS)])
    pr, ar = _rope_layout(LANES, [(0, ML_ROPE)])
    params = [w.astype(BF16), jnp.tile(gq_q_norm, GQ_HEADS).reshape(1, -1), jnp.tile(gq_k_norm, GQ_KV_HEADS).reshape(1, -1),
              ml_q_norm.reshape(1, -1), ml_w_uq.astype(BF16), ml_kv_norm.reshape(1, -1), ukv.astype(BF16),
              _block_ones(GQ_HEADS * GQ_HD, GQ_HD), _block_ones(GQ_KV_HEADS * GQ_HD, GQ_HD)]
    params += [jnp.asarray(p, BF16) for p in (pq, pk, pm, pr)]
    return params, ((GQ_HD, aq), (GQ_HD, ak), (ML_ROPE, am), (ML_ROPE, ar))


def _softmax_pv(s, v):
    m = jnp.max(s, axis=-1, keepdims=True)
    p = jnp.exp(s - m)
    return _dot(p.astype(BF16), v) / jnp.sum(p, axis=-1, keepdims=True)


def _attn_kernel(h_ref, mod_ref, qg_ref, qm_ref, kgt_ref, vg_ref, kmt_ref, vm_ref, wout_ref, o_ref):
    group = GQ_HEADS // GQ_KV_HEADS
    parts = []
    for h in range(GQ_HEADS):
        g = h // group
        s = _dot(qg_ref[0, :, h * GQ_HD:(h + 1) * GQ_HD], kgt_ref[0, g * GQ_HD:(g + 1) * GQ_HD, :])
        parts.append(_softmax_pv(s, vg_ref[0])[:, g * GQ_HD:(g + 1) * GQ_HD])
    dk = ML_NOPE + ML_ROPE
    for h in range(ML_HEADS):
        s = _dot(qm_ref[0, :, h * dk:(h + 1) * dk], kmt_ref[0, h * dk:(h + 1) * dk, :])
        pair = (h * ML_V) // LANES * LANES
        off = h * ML_V - pair
        parts.append(_softmax_pv(s, vm_ref[0, :, pair:pair + LANES])[:, off:off + ML_V])
    ol = jnp.concatenate(parts, axis=1).astype(BF16)
    o_ref[0] = h_ref[0] + mod_ref[0, 5:6] * _dot(ol, wout_ref[...])


def _attn_call(h, mod, qg, qm, kgt, vg, kmt, vm, w_out, *, ctx_tiles):
    n_batch, t_len, d = h.shape
    nt = t_len // TILE - ctx_tiles
    qtile = lambda n: pl.BlockSpec((1, TILE, n), lambda b, t: (b, t + ctx_tiles, 0))
    whole = lambda a: pl.BlockSpec((1,) + a.shape[1:], lambda b, t: (b, 0, 0))
    return pl.pallas_call(
        _attn_kernel,
        out_shape=jax.ShapeDtypeStruct((n_batch, nt * TILE, d), F32),
        grid=(n_batch, nt),
        in_specs=[qtile(d), pl.BlockSpec((1, N_MOD, d), lambda b, t: (b, 0, 0)), qtile(qg.shape[-1]), qtile(qm.shape[-1]),
                  whole(kgt), whole(vg), whole(kmt), whole(vm), _resident(w_out.shape)],
        out_specs=pl.BlockSpec((1, TILE, d), lambda b, t: (b, t, 0)),
        compiler_params=_cparams("parallel", "parallel"),
        name="attention_out",
    )(h, mod, qg, qm, kgt, vg, kmt, vm, w_out)


def _block_ones(n, blk):
    i = np.arange(n) // blk
    return jnp.asarray(i[:, None] == i[None, :], BF16)


def _even_layer_weights(ev_w_in, dn_conv, dn_a_log, dn_dt_bias, rw_mu, rw_w0, rw_w2, rw_a0, rw_a2, rw_g2,
                        rw_kk, rw_ka, rw_rk):
    d = ev_w_in.shape[0]
    n_dn = 4 * DN_HEADS * DN_DK
    nh2 = 2 * DN_HEADS
    slab0 = n_dn + 2 * nh2
    zeros = lambda n: jnp.zeros((d, n), F32)
    w_dn = jnp.concatenate([ev_w_in[:, :n_dn], ev_w_in[:, n_dn:slab0], zeros(LANES - 2 * nh2)], axis=1)
    slab = ev_w_in[:, slab0:]
    o = 3 * RW_W
    lora = 2 * RW_W_LORA
    gpad = 2 * LANES - RW_G_LORA
    w_rw = jnp.concatenate([slab[:, :o + 2 * lora + RW_G_LORA], zeros(gpad)], axis=1)
    mu = jnp.concatenate([rw_mu, jnp.zeros((gpad,), F32)]).reshape(1, -1)
    pad_lanes = lambda v: jnp.zeros((1, LANES), F32).at[0, :v.size].set(v.reshape(-1))

    def dir_blocks(m):
        z = jnp.zeros_like(m[0])
        return jnp.concatenate([jnp.concatenate([m[0], z], axis=1), jnp.concatenate([z, m[1]], axis=1)], axis=0)

    g2 = jnp.concatenate([rw_g2, jnp.zeros((gpad, RW_W), F32)], axis=0)
    return dict(
        w_dn=w_dn.astype(BF16), conv=dn_conv, alog=pad_lanes(dn_a_log), dtb=pad_lanes(dn_dt_bias),
        w_rw=w_rw.astype(BF16), mu=mu, w2=dir_blocks(rw_w2), w0=rw_w0.reshape(1, -1), a2=dir_blocks(rw_a2),
        a0=rw_a0.reshape(1, -1), g2=g2, kkw=rw_kk.reshape(1, -1), kaw=rw_ka.reshape(1, -1),
        rk=rw_rk.reshape(1, -1), ones=_block_ones(RW_W, RW_HS))


def kernel(x, c, ctx, c_ctx, mod_w, mod_b, norm_ffn1, norm_mix, norm_ffn2, ffn1_w1, ffn1_w3, ffn1_w2, ffn2_w1, ffn2_w3, ffn2_w2, ev_w_in, ev_w_out, dn_conv, dn_a_log, dn_dt_bias, dn_norm, rw_mu, rw_w0, rw_w2, rw_a0, rw_a2, rw_g2, rw_kk, rw_ka, rw_rk, rw_gn_w, rw_gn_b, od_w_in, od_w_out, gq_q_norm, gq_k_norm, ml_q_norm, ml_w_uq, ml_kv_norm, ml_w_ukv, final_norm):
    n_batch, n_lat, d = x.shape
    n_ctx = ctx.shape[1]
    depth = mod_w.shape[0]
    assert n_ctx % TILE == 0 and n_lat % TILE == 0 and n_lat % GRID_W == 0
    assert depth % 2 == 0 and depth // 2 == od_w_in.shape[0] == 1, "supported stack: [recurrent, attention]"
    ctx_tiles, ctx_chunks = n_ctx // TILE, n_ctx // CHUNK
    bf = lambda a: a.astype(BF16)

    mod = _all_mod(c, c_ctx, mod_w, mod_b)
    h = jnp.concatenate([ctx, x], axis=1)
    for i in range(depth):
        j = i // 2
        last = i == depth - 1
        h = _ffn_call(h, mod[i], norm_ffn1[i], bf(ffn1_w1[i]), bf(ffn1_w3[i]), bf(ffn1_w2[i]), j0=0, ctx_tiles=ctx_tiles)
        if i % 2 == 0:
            w = _even_layer_weights(ev_w_in[j], dn_conv[j], dn_a_log[j], dn_dt_bias[j], rw_mu[j], rw_w0[j], rw_w2[j],
                                    rw_a0[j], rw_a2[j], rw_g2[j], rw_kk[j], rw_ka[j], rw_rk[j])
            q, k, v, small, dgate = _dn_prep_call(h, mod[i], norm_mix[i], w["w_dn"], w["conv"], w["alog"], w["dtb"],
                                                  ctx_tiles=ctx_tiles)
            o_f, o_b = _dn_scan_call(*_dn_chunk_call(q, k, v, small), ctx_chunks=ctx_chunks)
            r, v7, kk, lw, kd, kka, gate7, bonus = _rw_prep_call(
                h, mod[i], norm_mix[i], w["w_rw"], w["mu"], w["w2"], w["w0"], w["a2"], w["a0"], w["g2"], w["kkw"],
                w["kaw"], w["rk"], w["ones"], ctx_tiles=ctx_tiles)
            y_f, y_b = _rw_scan_call(*_rw_chunk_call(r, v7, kk, lw, kd, kka), ctx_chunks=ctx_chunks)
            h = _ev_out_call(h, mod[i], o_f, o_b, dgate, y_f, y_b, gate7, bonus, jnp.tile(dn_norm[j], DN_HEADS),
                             rw_gn_w[j], rw_gn_b[j], w["ones"], bf(ev_w_out[j]), ctx_tiles=ctx_tiles)
            h = _ffn_call(h, mod[i], norm_ffn2[i], bf(ffn2_w1[i]), bf(ffn2_w3[i]), bf(ffn2_w2[i]), j0=6,
                          ctx_tiles=ctx_tiles)
        else:
            params, layouts = _odd_layer_weights(od_w_in[j], gq_q_norm[j], gq_k_norm[j], ml_q_norm[j], ml_w_uq[j],
                                                 ml_kv_norm[j], ml_w_ukv[j])
            tables = [t for rot, angle in layouts for t in _rope_tables(n_ctx, n_lat, rot, angle)]
            qg, qm, kgt, vg, kmt, vm = _od_prep_call(h, mod[i], norm_mix[i], params, tables, ctx_tiles=ctx_tiles)
            hl = _attn_call(h, mod[i], qg, qm, kgt, vg, kmt, vm, bf(od_w_out[j]), ctx_tiles=ctx_tiles)
            assert last
            h = _ffn_call(hl, mod[i], norm_ffn2[i], bf(ffn2_w1[i]), bf(ffn2_w3[i]), bf(ffn2_w2[i]), j0=6, ctx_tiles=0,
                          final_gain=final_norm)
    return h
```

```python
import functools

import jax
import jax.numpy as jnp
import numpy as np
from jax import lax
from jax.experimental import pallas as pl
from jax.experimental.pallas import tpu as pltpu

F32 = jnp.float32
BF16 = jnp.bfloat16

NORM_EPS = 1e-6
ROPE_THETA = 10000.0
GRID_W = 64
N_MOD = 9

DN_HEADS = 4
DN_DK = 128
DN_CONV = 5
RW_HEADS = 8
RW_HS = 64
RW_W = RW_HEADS * RW_HS
RW_W_LORA = 64
RW_A_LORA = 64
RW_G_LORA = 160
RW_GN_EPS = 64e-5
GQ_HEADS = 8
GQ_KV_HEADS = 2
GQ_HD = 64
ML_HEADS = 8
ML_NOPE = 64
ML_ROPE = 32
ML_V = 64

TILE = 256
CHUNK = 64
HALO = 8
LANES = 128
VMEM_LIMIT = 56 * 1024 * 1024
SCAN_BATCH = 2
PRECISE_SPAN = 4


def _cparams(*sem):
    return pltpu.CompilerParams(dimension_semantics=sem, vmem_limit_bytes=VMEM_LIMIT)


def _resident(shape):
    nd = len(shape)
    return pl.BlockSpec(shape, lambda *_: (0,) * nd, pipeline_mode=pl.Buffered(1))


def _mm(a, b):
    return jnp.dot(a.astype(BF16), b.astype(BF16), preferred_element_type=F32)


def _mm_nt(a, b):
    return lax.dot_general(a.astype(BF16), b.astype(BF16), (((1,), (1,)), ((), ())),
                           preferred_element_type=F32)


def _mm_tn(a, b):
    return lax.dot_general(a.astype(BF16), b.astype(BF16), (((0,), (0,)), ((), ())),
                           preferred_element_type=F32)


def _split2(x):
    hi = x.astype(BF16)
    lo = (x - hi.astype(F32)).astype(BF16)
    return hi, lo


def _split3(x):
    hi = x.astype(BF16)
    r = x - hi.astype(F32)
    mid = r.astype(BF16)
    lo = (r - mid.astype(F32)).astype(BF16)
    return hi, mid, lo


def _dot(a, b):
    return jnp.dot(a, b, preferred_element_type=F32)


def _mm3s(asp, bsp):
    (ah, al), (bh, bl) = asp, bsp
    return _dot(ah, bh) + (_dot(ah, bl) + _dot(al, bh))


def _mm3(a, b):
    return _mm3s(_split2(a), _split2(b))


def _mm_exact_lhs(a01, b):
    a = a01.astype(BF16)
    hi, mid, lo = _split3(b)
    return _dot(a, hi) + (_dot(a, mid) + _dot(a, lo))


def _rms_rows(x):
    return x * lax.rsqrt(jnp.mean(x * x, axis=-1, keepdims=True) + NORM_EPS)


def _modulate(x, gain, shift, scale):
    return (_rms_rows(x) * gain) * (1.0 + scale) + shift


def _silu(x):
    return x * jax.nn.sigmoid(x)


def _softplus(x):
    return jnp.maximum(x, 0.0) + jnp.log1p(jnp.exp(-jnp.abs(x)))


def _mod_kernel(c_ref, w_ref, b_ref, o_ref):
    s = _silu(c_ref[...])
    o_ref[0] = _mm3(s, w_ref[0]) + b_ref[0]


def _mod_call(cc, mod_w, mod_b):
    n_layers, d, n = mod_w.shape
    r = cc.shape[0]
    tn = n // 8
    return pl.pallas_call(
        _mod_kernel,
        out_shape=jax.ShapeDtypeStruct((n_layers, r, n), F32),
        grid=(n_layers, n // tn),
        in_specs=[pl.BlockSpec((r, d), lambda l, j: (0, 0)),
                  pl.BlockSpec((1, d, tn), lambda l, j: (l, 0, j)),
                  pl.BlockSpec((1, 1, tn), lambda l, j: (l, 0, j))],
        out_specs=pl.BlockSpec((1, r, tn), lambda l, j: (l, 0, j)),
        compiler_params=_cparams("parallel", "parallel"),
        name="adaln_mod",
    )(cc, mod_w, mod_b.reshape(n_layers, 1, n))


def _all_mod(c, c_ctx, mod_w, mod_b):
    n_batch, d = c.shape
    rows = -(-(n_batch + 1) // 8) * 8
    cc = jnp.zeros((rows, d), F32).at[:n_batch].set(c).at[n_batch].set(c_ctx)
    return _mod_call(cc, mod_w, mod_b).reshape(mod_w.shape[0], rows, N_MOD, d)


def _mod_row_map(n_batch, ctx_tiles, t_off):
    def index_map(b, t):
        return (jnp.where(t + t_off < ctx_tiles, n_batch, b), 0, 0)
    return index_map


def _ffn_kernel(h_ref, mod_ref, gain_ref, w1_ref, w3_ref, w2_ref, *rest, j0, final):
    o_ref = rest[-1]
    x = h_ref[0]
    shift, scale, gate = mod_ref[0, j0:j0 + 1], mod_ref[0, j0 + 1:j0 + 2], mod_ref[0, j0 + 2:j0 + 3]
    xn = _modulate(x, gain_ref[...], shift, scale).astype(BF16)
    a = _dot(xn, w1_ref[...])
    b = _dot(xn, w3_ref[...])
    f = _dot((_silu(a) * b).astype(BF16), w2_ref[...])
    y = x + (0.5 * gate) * f
    if final:
        y = _rms_rows(y) * rest[0][...]
    o_ref[0] = y


def _ffn_call(h, mod, gain, w1, w3, w2, *, j0, ctx_tiles, t_off=0, final_gain=None):
    n_batch, t_len, d = h.shape
    f = w1.shape[1]
    nt = t_len // TILE - t_off
    final = final_gain is not None
    in_specs = [pl.BlockSpec((1, TILE, d), lambda b, t: (b, t + t_off, 0)),
                pl.BlockSpec((1, N_MOD, d), _mod_row_map(n_batch, ctx_tiles, t_off)),
                _resident((1, d)), _resident((d, f)), _resident((d, f)), _resident((f, d))]
    args = [h, mod, gain.reshape(1, d), w1, w3, w2]
    if final:
        in_specs.append(_resident((1, d)))
        args.append(final_gain.reshape(1, d))
    return pl.pallas_call(
        functools.partial(_ffn_kernel, j0=j0, final=final),
        out_shape=jax.ShapeDtypeStruct((n_batch, nt * TILE, d), F32),
        grid=(n_batch, nt),
        in_specs=in_specs,
        out_specs=pl.BlockSpec((1, TILE, d), lambda b, t: (b, t, 0)),
        compiler_params=_cparams("parallel", "parallel"),
        name="macaron_ffn",
    )(*args)


def _halo_specs(d, ctx_tiles, n_tiles):
    per = TILE // HALO
    last = n_tiles * per - 1
    return [pl.BlockSpec((1, HALO, d), lambda b, t: (b, jnp.maximum(t * per - 1, 0), 0)),
            pl.BlockSpec((1, TILE, d), lambda b, t: (b, t, 0)),
            pl.BlockSpec((1, HALO, d), lambda b, t: (b, jnp.minimum((t + 1) * per, last), 0))]


def _project_with_halo(prev_ref, cur_ref, next_ref, mod_ref, gain_ref, w_ref, pe_ref, *, ctx_tiles, n_tiles):
    t = pl.program_id(1)
    xe = jnp.concatenate([prev_ref[0], cur_ref[0], next_ref[0]], axis=0)
    xn = _modulate(xe, gain_ref[...], mod_ref[0, 3:4], mod_ref[0, 4:5]).astype(BF16)
    p = _dot(xn, w_ref[...])
    row = lax.broadcasted_iota(jnp.int32, (TILE + 2 * HALO, 1), 0)
    prev_ok = jnp.logical_and(t > 0, t != ctx_tiles)
    next_ok = jnp.logical_and(t + 1 < n_tiles, t + 1 != ctx_tiles)
    keep = jnp.logical_and(jnp.logical_or(row >= HALO, prev_ok),
                           jnp.logical_or(row < HALO + TILE, next_ok))
    pe_ref[...] = jnp.where(keep, p, 0.0)


def _dn_prep_kernel(prev_ref, cur_ref, next_ref, mod_ref, gain_ref, w_ref, conv_ref, alog_ref, dtb_ref,
                    q_ref, k_ref, v_ref, small_ref, gate_ref, pe_ref, *, ctx_tiles, n_tiles):
    _project_with_halo(prev_ref, cur_ref, next_ref, mod_ref, gain_ref, w_ref, pe_ref,
                       ctx_tiles=ctx_tiles, n_tiles=n_tiles)
    nqkv = 3 * DN_HEADS * DN_DK
    half = DN_CONV // 2
    acc = None
    for j in range(DN_CONV):
        term = conv_ref[j:j + 1, :] * pe_ref[pl.ds(HALO - half + j, TILE), 0:nqkv]
        acc = term if acc is None else acc + term
    qkv = _silu(acc)
    w = DN_HEADS * DN_DK
    for idx, ref in ((0, q_ref), (1, k_ref)):
        for h in range(DN_HEADS):
            seg = qkv[:, idx * w + h * DN_DK: idx * w + (h + 1) * DN_DK]
            ref[0, :, h * DN_DK:(h + 1) * DN_DK] = seg * lax.rsqrt(jnp.sum(seg * seg, axis=-1, keepdims=True) + 1e-6)
    v_ref[0] = qkv[:, 2 * w:3 * w]
    gate_ref[0] = pe_ref[pl.ds(HALO, TILE), nqkv:nqkv + w]
    ab = pe_ref[pl.ds(HALO, TILE), nqkv + w:nqkv + w + LANES]
    g = -jnp.exp(alog_ref[...]) * _softplus(ab + dtb_ref[...])
    lane = lax.broadcasted_iota(jnp.int32, ab.shape, 1)
    nh2 = 2 * DN_HEADS
    small_ref[0] = jnp.where(lane < nh2, g, jnp.where(lane < 2 * nh2, jax.nn.sigmoid(ab), 0.0))


def _dn_prep_call(h, mod, gain, w, conv, alog, dtb, *, ctx_tiles):
    n_batch, t_len, d = h.shape
    nt = t_len // TILE
    wd = DN_HEADS * DN_DK
    out = lambda n, dt=F32: jax.ShapeDtypeStruct((n_batch, t_len, n), dt)
    ospec = lambda n: pl.BlockSpec((1, TILE, n), lambda b, t: (b, t, 0))
    return pl.pallas_call(
        functools.partial(_dn_prep_kernel, ctx_tiles=ctx_tiles, n_tiles=nt),
        out_shape=[out(wd), out(wd), out(wd), out(LANES), out(wd)],
        grid=(n_batch, nt),
        in_specs=_halo_specs(d, ctx_tiles, nt) + [
            pl.BlockSpec((1, N_MOD, d), _mod_row_map(n_batch, ctx_tiles, 0)),
            _resident((1, d)), _resident(w.shape), _resident(conv.shape),
            _resident((1, LANES)), _resident((1, LANES))],
        out_specs=[ospec(wd), ospec(wd), ospec(wd), ospec(LANES), ospec(wd)],
        scratch_shapes=[pltpu.VMEM((TILE + 2 * HALO, w.shape[1]), F32)],
        compiler_params=_cparams("parallel", "parallel"),
        name="deltanet_prep",
    )(h, h, h, mod, gain.reshape(1, d), w, conv, alog, dtb)


def _head_sums(x, ones_ref):
    hi, mid, lo = _split3(x)
    ones = ones_ref[...]
    return _dot(hi, ones) + (_dot(mid, ones) + _dot(lo, ones))


def _rw_prep_kernel(prev_ref, cur_ref, next_ref, mod_ref, gain_ref, w_ref, mu_ref, w2_ref, w0_ref, a2_ref,
                    a0_ref, g2_ref, kkw_ref, kaw_ref, rk_ref, ones_ref,
                    r_ref, v_ref, kk_ref, lw_ref, kd_ref, kka_ref, gate_ref, bonus_ref, pe_ref,
                    *, ctx_tiles, n_tiles):
    _project_with_halo(prev_ref, cur_ref, next_ref, mod_ref, gain_ref, w_ref, pe_ref,
                       ctx_tiles=ctx_tiles, n_tiles=n_tiles)
    z = pe_ref[pl.ds(HALO, TILE), :]
    zs = 0.5 * (pe_ref[pl.ds(HALO - 1, TILE), :] + pe_ref[pl.ds(HALO + 1, TILE), :])
    s = z + mu_ref[...] * (zs - z)
    r, k7, v7 = s[:, 0:RW_W], s[:, RW_W:2 * RW_W], s[:, 2 * RW_W:3 * RW_W]
    o = 3 * RW_W
    wd, ad, gd = s[:, o:o + LANES], s[:, o + LANES:o + 2 * LANES], s[:, o + 2 * LANES:o + 4 * LANES]
    w_logit = _mm3(jnp.tanh(wd), w2_ref[...]) + w0_ref[...]
    lw = -jnp.exp(-_softplus(-w_logit) - 0.5)
    a = jax.nn.sigmoid(_mm3(ad, a2_ref[...]) + a0_ref[...])
    gate_ref[0] = _mm3(jax.nn.sigmoid(gd), g2_ref[...])
    kx = k7 * kkw_ref[...]
    kk = kx * lax.rsqrt(_head_sums(kx * kx, ones_ref) + 1e-6)
    r_ref[0], v_ref[0], kk_ref[0], lw_ref[0] = r, v7, kk, lw
    kd_sum = None
    for d in range(2):
        a_d = a[:, d * RW_W:(d + 1) * RW_W]
        kd = k7 * (1.0 + (a_d - 1.0) * kaw_ref[...])
        kd_ref[0, :, d * RW_W:(d + 1) * RW_W] = kd
        kka_ref[0, :, d * RW_W:(d + 1) * RW_W] = kk * a_d
        kd_sum = kd if kd_sum is None else kd_sum + kd
    bonus_ref[0] = _head_sums((r * rk_ref[...]) * kd_sum, ones_ref) * v7


def _rw_prep_call(h, mod, gain, w, mu, w2, w0, a2, a0, g2, kkw, kaw, rk, ones, *, ctx_tiles):
    n_batch, t_len, d = h.shape
    nt = t_len // TILE
    out = lambda n: jax.ShapeDtypeStruct((n_batch, t_len, n), F32)
    ospec = lambda n: pl.BlockSpec((1, TILE, n), lambda b, t: (b, t, 0))
    widths = [RW_W, RW_W, RW_W, 2 * RW_W, 2 * RW_W, 2 * RW_W, RW_W, RW_W]
    params = [gain.reshape(1, d), w, mu, w2, w0, a2, a0, g2, kkw, kaw, rk, ones]
    return pl.pallas_call(
        functools.partial(_rw_prep_kernel, ctx_tiles=ctx_tiles, n_tiles=nt),
        out_shape=[out(n) for n in widths],
        grid=(n_batch, nt),
        in_specs=_halo_specs(d, ctx_tiles, nt) + [pl.BlockSpec((1, N_MOD, d), _mod_row_map(n_batch, ctx_tiles, 0))]
        + [_resident(p.shape) for p in params],
        out_specs=[ospec(n) for n in widths],
        scratch_shapes=[pltpu.VMEM((TILE + 2 * HALO, w.shape[1]), F32)],
        compiler_params=_cparams("parallel", "parallel"),
        name="rwkv_prep",
    )(h, h, h, mod, *params)


def _chunk_masks(direction, width=CHUNK):
    i = lax.broadcasted_iota(jnp.int32, (CHUNK, width), 0)
    j = lax.broadcasted_iota(jnp.int32, (CHUNK, width), 1) % CHUNK
    return (i >= j, i > j) if direction == 0 else (i <= j, i < j)


def _last_row(x, direction):
    return x[CHUNK - 1:CHUNK] if direction == 0 else x[0:1]


def _bd(x):
    shape = (2 * CHUNK, x.shape[1])
    r = lax.broadcasted_iota(jnp.int32, shape, 0)
    c = lax.broadcasted_iota(jnp.int32, shape, 1)
    return jnp.where((r < CHUNK) == (c < x.shape[1] // 2), jnp.concatenate([x, x], axis=0), 0.0)


def _neumann_inverse_pairs(ns):
    i = lax.broadcasted_iota(jnp.int32, (CHUNK, 2 * CHUNK), 0)
    j = lax.broadcasted_iota(jnp.int32, (CHUNK, 2 * CHUNK), 1) % CHUNK
    eye = jnp.where(i == j, 1.0, 0.0)
    rs = list(ns)
    lhs = [_split2(n) for n in ns]
    rhs = [_split2(_bd(n)) for n in ns]
    span = 2
    while span < CHUNK:
        mul = _mm3s if span <= PRECISE_SPAN else (lambda a, b: _dot(a[0], b[0]))
        ps = [mul(a, b) for a, b in zip(lhs, rhs)]
        rhs = [_split2(_bd(p)) for p in ps]
        rs = [r + p + mul(_split2(r), b) for r, p, b in zip(rs, ps, rhs)]
        span *= 2
        if span < CHUNK:
            lhs = [_split2(p) for p in ps]
    return [eye + r for r in rs]


def _rev_chunk(n, ctx_chunks, n_chunks):
    return jnp.where(n < ctx_chunks, ctx_chunks - 1 - n, n_chunks - 1 + ctx_chunks - n)


def _dn_chunk_kernel(q_ref, k_ref, v_ref, small_ref, u_ref, w_ref, qd_ref, kd_ref, attn_ref, gl_ref):
    sm = small_ref[0]
    q, k, v = q_ref[0] * (DN_DK ** -0.5), k_ref[0], v_ref[0]
    nh2, n_pairs, pw, dk = 2 * DN_HEADS, DN_HEADS // 2, 2 * DN_DK, DN_DK
    first_c = lax.broadcasted_iota(jnp.int32, (CHUNK, 2 * CHUNK), 1) < CHUNK
    first_f = lax.broadcasted_iota(jnp.int32, (CHUNK, pw), 1) < dk

    def cols(x, c, first):
        return jnp.where(first[:x.shape[0]], x[:, c:c + 1], x[:, c + 1:c + 2])

    grams = [_mm_nt(jnp.concatenate([k[:, j * pw:(j + 1) * pw], q[:, j * pw:(j + 1) * pw]], axis=0),
                    _bd(k[:, j * pw:(j + 1) * pw])) for j in range(n_pairs)]
    work = []
    for d in range(2):
        incl, strict = _chunk_masks(d, 2 * CHUNK)
        gc = _mm_exact_lhs(jnp.where(_chunk_masks(d)[0], 1.0, 0.0), sm)
        gc_t = gc.T
        gtot = _last_row(gc, d)
        gl_ref[0, d, 0] = jnp.exp(gtot)
        for j in range(n_pairs):
            c = DN_HEADS * d + 2 * j
            gcr = jnp.concatenate([gc_t[c:c + 1, :], gc_t[c + 1:c + 2, :]], axis=1)
            decay = jnp.exp(jnp.where(incl, cols(gc, c, first_c) - gcr, -1e30))
            lower = jnp.where(strict, (cols(sm, nh2 + c, first_c) * grams[j][:CHUNK]) * decay, 0.0)
            work.append((d, j, c, gc, gtot, decay, -lower))
    t_invs = _neumann_inverse_pairs([item[-1] for item in work])
    for (d, j, c, gc, gtot, decay, _), t_inv in zip(work, t_invs):
        sl = slice(j * pw, (j + 1) * pw)
        beta, gcc, gt = cols(sm, nh2 + c, first_f), cols(gc, c, first_f), cols(gtot, c, first_f)
        egc = jnp.exp(gcc)
        kp, qp = k[:, sl], q[:, sl]
        vb, ke = v[:, sl] * beta, (kp * beta) * egc
        rhs = jnp.concatenate([vb[:, :dk], ke[:, :dk], vb[:, dk:], ke[:, dk:]], axis=1)
        sol = _mm3(t_inv, _bd(rhs))
        u_ref[0, d, :, sl] = jnp.concatenate([sol[:, :dk], sol[:, 2 * dk:3 * dk]], axis=1)
        w_ref[0, d, :, sl] = jnp.concatenate([sol[:, dk:2 * dk], sol[:, 3 * dk:]], axis=1).astype(BF16)
        qd_ref[0, d, :, sl] = (qp * egc).astype(BF16)
        kd_ref[0, d, :, sl] = (kp * jnp.exp(gt - gcc)).astype(BF16)
        attn_ref[0, d, :, 2 * j * CHUNK:2 * (j + 1) * CHUNK] = (grams[j][CHUNK:] * decay).astype(BF16)


def _dn_chunk_call(q, k, v, small):
    n_batch, t_len, wd = q.shape
    nc = t_len // CHUNK
    ispec = lambda n: pl.BlockSpec((1, CHUNK, n), lambda b, c: (b, c, 0))
    ospec = lambda n: pl.BlockSpec((1, 2, CHUNK, n), lambda b, c: (b, 0, c, 0))
    shp = lambda n, dt: jax.ShapeDtypeStruct((n_batch, 2, t_len, n), dt)
    return pl.pallas_call(
        _dn_chunk_kernel,
        out_shape=[shp(wd, F32), shp(wd, BF16), shp(wd, BF16), shp(wd, BF16), shp(DN_HEADS * CHUNK, BF16),
                   jax.ShapeDtypeStruct((n_batch, 2, nc, 1, LANES), F32)],
        grid=(n_batch, nc),
        in_specs=[ispec(wd), ispec(wd), ispec(wd), ispec(LANES)],
        out_specs=[ospec(wd), ospec(wd), ospec(wd), ospec(wd), ospec(DN_HEADS * CHUNK),
                   pl.BlockSpec((1, 2, 1, 1, LANES), lambda b, c: (b, 0, c, 0, 0))],
        compiler_params=_cparams("parallel", "parallel"),
        name="deltanet_chunk_prep",
    )(q, k, v, small)


def _dn_scan_kernel(*refs):
    ins, (of_ref, ob_ref, s_ref) = refs[:12], refs[12:]

    @pl.when(pl.program_id(1) == 0)
    def _():
        s_ref[...] = jnp.zeros_like(s_ref)

    for b, (d, o_ref) in [(b, x) for b in range(SCAN_BATCH) for x in ((0, of_ref), (1, ob_ref))]:
        u_ref, w_ref, qd_ref, kd_ref, attn_ref, gl_ref = ins[6 * d:6 * d + 6]
        for j in range(DN_HEADS // 2):
            heads = (2 * j, 2 * j + 1)
            sls = [slice(h * DN_DK, (h + 1) * DN_DK) for h in heads]
            ss = [s_ref[b, d, h] for h in heads]
            sbs = [s.astype(BF16) for s in ss]
            v_new = [u_ref[b, 0, :, sl] - _dot(w_ref[b, 0, :, sl], sb) for sl, sb in zip(sls, sbs)]
            inter = _dot(qd_ref[b, 0, :, sls[0]], sbs[0]), _dot(qd_ref[b, 0, :, sls[1]], sbs[1])
            intra = _dot(attn_ref[b, 0, :, 2 * j * CHUNK:2 * (j + 1) * CHUNK],
                         _bd(jnp.concatenate(v_new, axis=1)).astype(BF16))
            o_ref[b, :, sls[0].start:sls[1].stop] = jnp.concatenate(inter, axis=1) + intra
            for h, sl, s, vn in zip(heads, sls, ss, v_new):
                c = DN_HEADS * d + h
                s_ref[b, d, h] = s * gl_ref[b, 0, 0, :, c:c + 1] + _mm_tn(kd_ref[b, 0, :, sl], vn)


def _dn_scan_call(u, w, qd, kd, attn, gl, *, ctx_chunks):
    n_batch, _, t_len, wd = u.shape
    nc = t_len // CHUNK
    in_specs, args = [], []
    for d in range(2):
        chunk = (lambda n: n) if d == 0 else functools.partial(_rev_chunk, ctx_chunks=ctx_chunks, n_chunks=nc)
        for arr in (u, w, qd, kd, attn):
            in_specs.append(pl.BlockSpec((SCAN_BATCH, 1, CHUNK, arr.shape[-1]),
                                         lambda b, n, d=d, chunk=chunk: (b, d, chunk(n), 0)))
        in_specs.append(pl.BlockSpec((SCAN_BATCH, 1, 1, 1, LANES),
                                     lambda b, n, d=d, chunk=chunk: (b, d, chunk(n), 0, 0)))
        args += [u, w, qd, kd, attn, gl]
    return pl.pallas_call(
        _dn_scan_kernel,
        out_shape=[jax.ShapeDtypeStruct((n_batch, t_len, wd), F32)] * 2,
        grid=(n_batch // SCAN_BATCH, nc),
        in_specs=in_specs,
        out_specs=[pl.BlockSpec((SCAN_BATCH, CHUNK, wd), lambda b, n: (b, n, 0)),
                   pl.BlockSpec((SCAN_BATCH, CHUNK, wd), lambda b, n: (b, _rev_chunk(n, ctx_chunks, nc), 0))],
        scratch_shapes=[pltpu.VMEM((SCAN_BATCH, 2, DN_HEADS, DN_DK, DN_DK), F32)],
        compiler_params=_cparams("parallel", "arbitrary"),
        name="deltanet_scan",
    )(*args)


def _rw_chunk_kernel(r_ref, v_ref, kk_ref, lw_ref, kd_ref, kka_ref,
                     ut_ref, wt_ref, rt_ref, arb_ref, bt_ref, kt_ref, y0_ref, pc_ref):
    r, v, kk = r_ref[0], v_ref[0], kk_ref[0]
    pw = 2 * RW_HS
    work, n_list = [], []
    for d in range(2):
        dsl = slice(d * RW_W, (d + 1) * RW_W)
        lw, kd, kka = lw_ref[0, :, dsl], kd_ref[0, :, dsl], kka_ref[0, :, dsl]
        incl, strict = _chunk_masks(d, 2 * CHUNK)
        cl = _mm_exact_lhs(jnp.where(_chunk_masks(d)[0], 1.0, 0.0), lw)
        tot = _last_row(cl, d)
        p_inv, p_tail = jnp.exp(-cl), jnp.exp(tot - cl)
        at = -kk * jnp.exp(cl - lw)
        rt = r * jnp.exp(cl)
        kh, bh = kd * p_inv, kka * p_inv
        pc_ref[0, d, 0] = jnp.exp(tot)
        rt_ref[0, d] = rt.astype(BF16)
        bt_ref[0, d] = (kka * p_tail).astype(BF16)
        kt_ref[0, d] = (kd * p_tail).astype(BF16)
        for j in range(RW_HEADS // 2):
            sl = slice(j * pw, (j + 1) * pw)
            aa = _mm_nt(jnp.concatenate([at[:, sl], rt[:, sl]], axis=0),
                        jnp.concatenate([_bd(bh[:, sl]), _bd(kh[:, sl])], axis=0))
            n_list.append(jnp.where(strict, aa[:CHUNK, :pw], 0.0))
            work.append((d, sl, at[:, sl], jnp.where(strict, aa[:CHUNK, pw:], 0.0),
                         jnp.where(incl, aa[CHUNK:, :pw], 0.0), jnp.where(incl, aa[CHUNK:, pw:], 0.0)))
    t_invs = _neumann_inverse_pairs(n_list)
    for (d, sl, at_p, a_ak, a_rb, a_rk), t_inv in zip(work, t_invs):
        v_bd = _bd(v[:, sl]).astype(BF16)
        sol = _mm3(t_inv, jnp.concatenate([_bd(at_p), _bd(_dot(a_ak.astype(BF16), v_bd))], axis=1))
        wt_ref[0, d, :, sl] = sol[:, :pw].astype(BF16)
        ut_ref[0, d, :, sl] = sol[:, pw:]
        arb_ref[0, d, :, sl] = a_rb.astype(BF16)
        y0_ref[0, d, :, sl] = _dot(a_rk.astype(BF16), v_bd)


def _rw_chunk_call(r, v, kk, lw, kd, kka):
    n_batch, t_len, wd = r.shape
    nc = t_len // CHUNK
    ispec = lambda n: pl.BlockSpec((1, CHUNK, n), lambda b, c: (b, c, 0))
    ospec = pl.BlockSpec((1, 2, CHUNK, wd), lambda b, c: (b, 0, c, 0))
    shp = lambda dt: jax.ShapeDtypeStruct((n_batch, 2, t_len, wd), dt)
    return pl.pallas_call(
        _rw_chunk_kernel,
        out_shape=[shp(F32), shp(BF16), shp(BF16), shp(BF16), shp(BF16), shp(BF16), shp(F32),
                   jax.ShapeDtypeStruct((n_batch, 2, nc, 1, wd), F32)],
        grid=(n_batch, nc),
        in_specs=[ispec(wd), ispec(wd), ispec(wd), ispec(2 * wd), ispec(2 * wd), ispec(2 * wd)],
        out_specs=[ospec] * 7 + [pl.BlockSpec((1, 2, 1, 1, wd), lambda b, c: (b, 0, c, 0, 0))],
        compiler_params=_cparams("parallel", "parallel"),
        name="rwkv_chunk_prep",
    )(r, v, kk, lw, kd, kka)


def _rw_scan_kernel(*refs):
    ins, (yf_ref, yb_ref, s_ref) = refs[:18], refs[18:]

    @pl.when(pl.program_id(1) == 0)
    def _():
        s_ref[...] = jnp.zeros_like(s_ref)

    pw = 2 * RW_HS
    r_i = lax.broadcasted_iota(jnp.int32, (pw, pw), 0)
    c_i = lax.broadcasted_iota(jnp.int32, (pw, pw), 1)
    same_head = (r_i < RW_HS) == (c_i < RW_HS)
    for b, (d, y_ref) in [(b, x) for b in range(SCAN_BATCH) for x in ((0, yf_ref), (1, yb_ref))]:
        ut_ref, wt_ref, rt_ref, arb_ref, bt_ref, kt_ref, y0_ref, pc_ref, v_ref = ins[9 * d:9 * d + 9]
        for j in range(RW_HEADS // 2):
            sl = slice(j * pw, (j + 1) * pw)
            s = s_ref[b, d, j]
            sb = s.astype(BF16)
            u = ut_ref[b, 0, :, sl] + _mm_nt(wt_ref[b, 0, :, sl], sb)
            y_ref[b, :, sl] = (y0_ref[b, 0, :, sl] + _mm_nt(rt_ref[b, 0, :, sl], sb)
                               + _dot(arb_ref[b, 0, :, sl], _bd(u).astype(BF16)))
            grow = _mm_tn(jnp.concatenate([v_ref[b, :, sl], u], axis=0),
                          jnp.concatenate([kt_ref[b, 0, :, sl], bt_ref[b, 0, :, sl]], axis=0))
            s_ref[b, d, j] = s * pc_ref[b, 0, 0, :, sl] + jnp.where(same_head, grow, 0.0)


def _rw_scan_call(ut, wt, rt, arb, bt, kt, y0, pc, v, *, ctx_chunks):
    n_batch, _, t_len, wd = ut.shape
    nc = t_len // CHUNK
    in_specs, args = [], []
    for d in range(2):
        chunk = (lambda n: n) if d == 0 else functools.partial(_rev_chunk, ctx_chunks=ctx_chunks, n_chunks=nc)
        for arr in (ut, wt, rt, arb, bt, kt, y0):
            in_specs.append(pl.BlockSpec((SCAN_BATCH, 1, CHUNK, wd),
                                         lambda b, n, d=d, chunk=chunk: (b, d, chunk(n), 0)))
        in_specs.append(pl.BlockSpec((SCAN_BATCH, 1, 1, 1, wd), lambda b, n, d=d, chunk=chunk: (b, d, chunk(n), 0, 0)))
        in_specs.append(pl.BlockSpec((SCAN_BATCH, CHUNK, wd), lambda b, n, chunk=chunk: (b, chunk(n), 0)))
        args += [ut, wt, rt, arb, bt, kt, y0, pc, v]
    return pl.pallas_call(
        _rw_scan_kernel,
        out_shape=[jax.ShapeDtypeStruct((n_batch, t_len, wd), F32)] * 2,
        grid=(n_batch // SCAN_BATCH, nc),
        in_specs=in_specs,
        out_specs=[pl.BlockSpec((SCAN_BATCH, CHUNK, wd), lambda b, n: (b, n, 0)),
                   pl.BlockSpec((SCAN_BATCH, CHUNK, wd), lambda b, n: (b, _rev_chunk(n, ctx_chunks, nc), 0))],
        scratch_shapes=[pltpu.VMEM((SCAN_BATCH, 2, RW_HEADS // 2, 2 * RW_HS, 2 * RW_HS), F32)],
        compiler_params=_cparams("parallel", "arbitrary"),
        name="rwkv_scan",
    )(*args)


def _ev_out_kernel(h_ref, mod_ref, of_ref, ob_ref, dgate_ref, yf_ref, yb_ref, gate7_ref, bonus_ref,
                   dnorm_ref, gnw_ref, gnb_ref, ones_ref, wout_ref, o_ref):
    o = of_ref[0] + ob_ref[0]
    dgate = dgate_ref[0]
    parts = []
    for h in range(DN_HEADS):
        sl = slice(h * DN_DK, (h + 1) * DN_DK)
        parts.append(_rms_rows(o[:, sl]) * dnorm_ref[:, sl] * _silu(dgate[:, sl]))
    o_dn = jnp.concatenate(parts, axis=1)
    y = yf_ref[0] + yb_ref[0]
    inv_n = 1.0 / RW_HS
    mu = _head_sums(y, ones_ref) * inv_n
    yc = y - mu
    var = _head_sums(yc * yc, ones_ref) * inv_n
    yn = yc * lax.rsqrt(var + RW_GN_EPS) * gnw_ref[...] + gnb_ref[...]
    o_rw = (yn + bonus_ref[0]) * gate7_ref[0]
    wd = DN_HEADS * DN_DK
    proj = _dot(o_dn.astype(BF16), wout_ref[0:wd, :]) + _dot(o_rw.astype(BF16), wout_ref[wd:, :])
    o_ref[0] = h_ref[0] + mod_ref[0, 5:6] * proj


def _ev_out_call(h, mod, o_f, o_b, dgate, y_f, y_b, gate7, bonus, dnorm, gnw, gnb, ones, w_out, *, ctx_tiles):
    n_batch, t_len, d = h.shape
    nt = t_len // TILE
    tile = lambda n: pl.BlockSpec((1, TILE, n), lambda b, t: (b, t, 0))
    params = [dnorm.reshape(1, -1), gnw.reshape(1, -1), gnb.reshape(1, -1), ones, w_out]
    streams = [o_f, o_b, dgate, y_f, y_b, gate7, bonus]
    return pl.pallas_call(
        _ev_out_kernel,
        out_shape=jax.ShapeDtypeStruct(h.shape, F32),
        grid=(n_batch, nt),
        in_specs=[tile(d), pl.BlockSpec((1, N_MOD, d), _mod_row_map(n_batch, ctx_tiles, 0))]
        + [tile(s.shape[-1]) for s in streams] + [_resident(p.shape) for p in params],
        out_specs=tile(d),
        compiler_params=_cparams("parallel", "parallel"),
        name="even_mix_out",
    )(h, mod, *streams, *params)


def _rope_layout(width, blocks):
    perm = np.zeros((width, width), np.float32)
    angle = np.full((width,), -1, np.int64)
    for start, rot in blocks:
        q = rot // 4
        for blk in range(2):
            for idx in range(q):
                l1 = start + blk * 2 * q + idx
                l2 = l1 + q
                perm[l2, l1], perm[l1, l2] = -1.0, 1.0
                angle[l1] = angle[l2] = blk * q + idx
    return perm, angle


def _rope_tables(n_ctx, n_lat, rot, angle):
    rows = n_lat // GRID_W
    row = jnp.repeat(jnp.arange(rows), GRID_W).astype(F32)
    col = jnp.tile(jnp.arange(GRID_W), rows).astype(F32)
    axis_dim = rot // 2
    inv = ROPE_THETA ** (-jnp.arange(0, axis_dim, 2, dtype=F32) / axis_dim)
    ang = jnp.concatenate([row[:, None] * inv, col[:, None] * inv], axis=-1)
    on = jnp.asarray(angle >= 0)
    idx = np.maximum(angle, 0)
    cos = jnp.where(on, jnp.cos(ang)[:, idx], 1.0)
    sin = jnp.where(on, jnp.sin(ang)[:, idx], 0.0)
    width = angle.shape[0]
    return (jnp.concatenate([jnp.ones((n_ctx, width), F32), cos], axis=0),
            jnp.concatenate([jnp.zeros((n_ctx, width), F32), sin], axis=0))


def _rope(x, perm_ref, cos_ref, sin_ref):
    hi, mid, lo = _split3(x)
    p = perm_ref[...]
    return x * cos_ref[...] + (_dot(hi, p) + (_dot(mid, p) + _dot(lo, p))) * sin_ref[...]


def _od_prep_kernel(h_ref, mod_ref, gain_ref, w_ref, qn_ref, kn_ref, mqn_ref, wuq_ref, mkvn_ref, wukv_ref,
                    ones_q_ref, ones_k_ref, pq_ref, pk_ref, pm_ref, pr_ref,
                    cq_ref, sq_ref, ck_ref, sk_ref, cm_ref, sm_ref, cr_ref, sr_ref,
                    qg_ref, qm_ref, kgt_ref, vg_ref, kmt_ref, vm_ref):
    xn = _modulate(h_ref[0], gain_ref[...], mod_ref[0, 3:4], mod_ref[0, 4:5]).astype(BF16)
    p = _dot(xn, w_ref[...])
    nq, nkv = GQ_HEADS * GQ_HD, GQ_KV_HEADS * GQ_HD
    o = 0
    q, o = p[:, o:o + nq], o + nq
    k, o = p[:, o:o + nkv], o + nkv
    v, o = p[:, o:o + nkv], o + nkv
    n_cq, n_ckv = mqn_ref.shape[1], mkvn_ref.shape[1]
    cq, o = p[:, o:o + n_cq], o + n_cq
    ckv, o = p[:, o:o + n_ckv], o + n_ckv
    kr = p[:, o:o + LANES]
    inv_hd = 1.0 / GQ_HD
    q = q * lax.rsqrt(_head_sums(q * q, ones_q_ref) * inv_hd + NORM_EPS) * qn_ref[...]
    k = k * lax.rsqrt(_head_sums(k * k, ones_k_ref) * inv_hd + NORM_EPS) * kn_ref[...]
    qm = _dot((_rms_rows(cq) * mqn_ref[...]).astype(BF16), wuq_ref[...])
    kvm = _dot((_rms_rows(ckv) * mkvn_ref[...]).astype(BF16), wukv_ref[...])
    q = _rope(q, pq_ref, cq_ref, sq_ref) * (GQ_HD ** -0.5)
    k = _rope(k, pk_ref, ck_ref, sk_ref)
    qm = _rope(qm, pm_ref, cm_ref, sm_ref) * ((ML_NOPE + ML_ROPE) ** -0.5)
    kr = _rope(kr, pr_ref, cr_ref, sr_ref)
    qg_ref[0] = q.astype(BF16)
    qm_ref[0] = qm.astype(BF16)
    vg_ref[0] = v.astype(BF16)
    n_nope = ML_HEADS * ML_NOPE
    vm_ref[0] = kvm[:, n_nope:].astype(BF16)
    kgt_ref[0] = k.T.astype(BF16)
    knt = kvm[:, :n_nope].T.astype(BF16)
    krt = kr.T[:ML_ROPE].astype(BF16)
    dk = ML_NOPE + ML_ROPE
    for h in range(ML_HEADS):
        kmt_ref[0, h * dk:h * dk + ML_NOPE, :] = knt[h * ML_NOPE:(h + 1) * ML_NOPE]
        kmt_ref[0, h * dk + ML_NOPE:(h + 1) * dk, :] = krt


def _od_prep_call(h, mod, gain, params, tables, *, ctx_tiles):
    n_batch, t_len, d = h.shape
    nt = t_len // TILE
    nq, nkv = GQ_HEADS * GQ_HD, GQ_KV_HEADS * GQ_HD
    dk = ML_NOPE + ML_ROPE
    tile = lambda n: pl.BlockSpec((1, TILE, n), lambda b, t: (b, t, 0))
    tile_t = lambda n: pl.BlockSpec((1, n, TILE), lambda b, t: (b, 0, t))
    tab = lambda a: pl.BlockSpec((TILE, a.shape[1]), lambda b, t: (t, 0))
    shp = lambda *s: jax.ShapeDtypeStruct((n_batch,) + s, BF16)
    return pl.pallas_call(
        _od_prep_kernel,
        out_shape=[shp(t_len, nq), shp(t_len, ML_HEADS * dk), shp(nkv, t_len), shp(t_len, nkv),
                   shp(ML_HEADS * dk, t_len), shp(t_len, ML_HEADS * ML_V)],
        grid=(n_batch, nt),
        in_specs=[tile(d), pl.BlockSpec((1, N_MOD, d), _mod_row_map(n_batch, ctx_tiles, 0)), _resident((1, d))]
        + [_resident(p.shape) for p in params] + [tab(a) for a in tables],
        out_specs=[tile(nq), tile(ML_HEADS * dk), tile_t(nkv), tile(nkv), tile_t(ML_HEADS * dk), tile(ML_HEADS * ML_V)],
        compiler_params=_cparams("parallel", "parallel"),
        name="attn_prep",
    )(h, mod, gain.reshape(1, d), *params, *tables)


def _odd_layer_weights(od_w_in, gq_q_norm, gq_k_norm, ml_q_norm, ml_w_uq, ml_kv_norm, ml_w_ukv):
    d = od_w_in.shape[0]
    w = jnp.concatenate([od_w_in, jnp.zeros((d, LANES - ML_ROPE), F32)], axis=1)
    ukv = ml_w_ukv.reshape(ml_w_ukv.shape[0], ML_HEADS, ML_NOPE + ML_V)
    ukv = jnp.concatenate([ukv[:, :, :ML_NOPE].reshape(-1, ML_HEADS * ML_NOPE),
                           ukv[:, :, ML_NOPE:].reshape(-1, ML_HEADS * ML_V)], axis=1)
    dk = ML_NOPE + ML_ROPE
    pq, aq = _rope_layout(GQ_HEADS * GQ_HD, [(h * GQ_HD, GQ_HD) for h in range(GQ_HEADS)])
    pk, ak = _rope_layout(GQ_KV_HEADS * GQ_HD, [(h * GQ_HD, GQ_HD) for h in range(GQ_KV_HEADS)])
    pm, am = _rope_layout(ML_HEADS * dk, [(h * dk + ML_NOPE, ML_ROPE) for h in range(ML_HEADS)])
    pr, ar = _rope_layout(LANES, [(0, ML_ROPE)])
    params = [w.astype(BF16), jnp.tile(gq_q_norm, GQ_HEADS).reshape(1, -1), jnp.tile(gq_k_norm, GQ_KV_HEADS).reshape(1, -1),
              ml_q_norm.reshape(1, -1), ml_w_uq.astype(BF16), ml_kv_norm.reshape(1, -1), ukv.astype(BF16),
              _block_ones(GQ_HEADS * GQ_HD, GQ_HD), _block_ones(GQ_KV_HEADS * GQ_HD, GQ_HD)]
    params += [jnp.asarray(p, BF16) for p in (pq, pk, pm, pr)]
    return params, ((GQ_HD, aq), (GQ_HD, ak), (ML_ROPE, am), (ML_ROPE, ar))


def _softmax_pv(s, v):
    m = jnp.max(s, axis=-1, keepdims=True)
    p = jnp.exp(s - m)
    return _dot(p.astype(BF16), v) / jnp.sum(p, axis=-1, keepdims=True)


def _attn_kernel(h_ref, mod_ref, qg_ref, qm_ref, kgt_ref, vg_ref, kmt_ref, vm_ref, wout_ref, o_ref):
    group = GQ_HEADS // GQ_KV_HEADS
    parts = []
    for h in range(GQ_HEADS):
        g = h // group
        s = _dot(qg_ref[0, :, h * GQ_HD:(h + 1) * GQ_HD], kgt_ref[0, g * GQ_HD:(g + 1) * GQ_HD, :])
        parts.append(_softmax_pv(s, vg_ref[0])[:, g * GQ_HD:(g + 1) * GQ_HD])
    dk = ML_NOPE + ML_ROPE
    for h in range(ML_HEADS):
        s = _dot(qm_ref[0, :, h * dk:(h + 1) * dk], kmt_ref[0, h * dk:(h + 1) * dk, :])
        pair = (h * ML_V) // LANES * LANES
        off = h * ML_V - pair
        parts.append(_softmax_pv(s, vm_ref[0, :, pair:pair + LANES])[:, off:off + ML_V])
    ol = jnp.concatenate(parts, axis=1).astype(BF16)
    o_ref[0] = h_ref[0] + mod_ref[0, 5:6] * _dot(ol, wout_ref[...])


def _attn_call(h, mod, qg, qm, kgt, vg, kmt, vm, w_out, *, ctx_tiles):
    n_batch, t_len, d = h.shape
    nt = t_len // TILE - ctx_tiles
    qtile = lambda n: pl.BlockSpec((1, TILE, n), lambda b, t: (b, t + ctx_tiles, 0))
    whole = lambda a: pl.BlockSpec((1,) + a.shape[1:], lambda b, t: (b, 0, 0))
    return pl.pallas_call(
        _attn_kernel,
        out_shape=jax.ShapeDtypeStruct((n_batch, nt * TILE, d), F32),
        grid=(n_batch, nt),
        in_specs=[qtile(d), pl.BlockSpec((1, N_MOD, d), lambda b, t: (b, 0, 0)), qtile(qg.shape[-1]), qtile(qm.shape[-1]),
                  whole(kgt), whole(vg), whole(kmt), whole(vm), _resident(w_out.shape)],
        out_specs=pl.BlockSpec((1, TILE, d), lambda b, t: (b, t, 0)),
        compiler_params=_cparams("parallel", "parallel"),
        name="attention_out",
    )(h, mod, qg, qm, kgt, vg, kmt, vm, w_out)


def _block_ones(n, blk):
    i = np.arange(n) // blk
    return jnp.asarray(i[:, None] == i[None, :], BF16)


def _even_layer_weights(ev_w_in, dn_conv, dn_a_log, dn_dt_bias, rw_mu, rw_w0, rw_w2, rw_a0, rw_a2, rw_g2,
                        rw_kk, rw_ka, rw_rk):
    d = ev_w_in.shape[0]
    n_dn = 4 * DN_HEADS * DN_DK
    nh2 = 2 * DN_HEADS
    slab0 = n_dn + 2 * nh2
    zeros = lambda n: jnp.zeros((d, n), F32)
    w_dn = jnp.concatenate([ev_w_in[:, :n_dn], ev_w_in[:, n_dn:slab0], zeros(LANES - 2 * nh2)], axis=1)
    slab = ev_w_in[:, slab0:]
    o = 3 * RW_W
    lora = 2 * RW_W_LORA
    gpad = 2 * LANES - RW_G_LORA
    w_rw = jnp.concatenate([slab[:, :o + 2 * lora + RW_G_LORA], zeros(gpad)], axis=1)
    mu = jnp.concatenate([rw_mu, jnp.zeros((gpad,), F32)]).reshape(1, -1)
    pad_lanes = lambda v: jnp.zeros((1, LANES), F32).at[0, :v.size].set(v.reshape(-1))

    def dir_blocks(m):
        z = jnp.zeros_like(m[0])
        return jnp.concatenate([jnp.concatenate([m[0], z], axis=1), jnp.concatenate([z, m[1]], axis=1)], axis=0)

    g2 = jnp.concatenate([rw_g2, jnp.zeros((gpad, RW_W), F32)], axis=0)
    return dict(
        w_dn=w_dn.astype(BF16), conv=dn_conv, alog=pad_lanes(dn_a_log), dtb=pad_lanes(dn_dt_bias),
        w_rw=w_rw.astype(BF16), mu=mu, w2=dir_blocks(rw_w2), w0=rw_w0.reshape(1, -1), a2=dir_blocks(rw_a2),
        a0=rw_a0.reshape(1, -1), g2=g2, kkw=rw_kk.reshape(1, -1), kaw=rw_ka.reshape(1, -1),
        rk=rw_rk.reshape(1, -1), ones=_block_ones(RW_W, RW_HS))


def kernel(x, c, ctx, c_ctx, mod_w, mod_b, norm_ffn1, norm_mix, norm_ffn2, ffn1_w1, ffn1_w3, ffn1_w2, ffn2_w1, ffn2_w3, ffn2_w2, ev_w_in, ev_w_out, dn_conv, dn_a_log, dn_dt_bias, dn_norm, rw_mu, rw_w0, rw_w2, rw_a0, rw_a2, rw_g2, rw_kk, rw_ka, rw_rk, rw_gn_w, rw_gn_b, od_w_in, od_w_out, gq_q_norm, gq_k_norm, ml_q_norm, ml_w_uq, ml_kv_norm, ml_w_ukv, final_norm):
    n_batch, n_lat, d = x.shape
    n_ctx = ctx.shape[1]
    depth = mod_w.shape[0]
    assert n_ctx % TILE == 0 and n_lat % TILE == 0 and n_lat % GRID_W == 0 and n_batch % SCAN_BATCH == 0
    assert depth % 2 == 0 and depth // 2 == od_w_in.shape[0] == 1, "supported stack: [recurrent, attention]"
    ctx_tiles, ctx_chunks = n_ctx // TILE, n_ctx // CHUNK
    bf = lambda a: a.astype(BF16)

    mod = _all_mod(c, c_ctx, mod_w, mod_b)
    h = jnp.concatenate([ctx, x], axis=1)
    for i in range(depth):
        j = i // 2
        last = i == depth - 1
        h = _ffn_call(h, mod[i], norm_ffn1[i], bf(ffn1_w1[i]), bf(ffn1_w3[i]), bf(ffn1_w2[i]), j0=0, ctx_tiles=ctx_tiles)
        if i % 2 == 0:
            w = _even_layer_weights(ev_w_in[j], dn_conv[j], dn_a_log[j], dn_dt_bias[j], rw_mu[j], rw_w0[j], rw_w2[j],
                                    rw_a0[j], rw_a2[j], rw_g2[j], rw_kk[j], rw_ka[j], rw_rk[j])
            q, k, v, small, dgate = _dn_prep_call(h, mod[i], norm_mix[i], w["w_dn"], w["conv"], w["alog"], w["dtb"],
                                                  ctx_tiles=ctx_tiles)
            o_f, o_b = _dn_scan_call(*_dn_chunk_call(q, k, v, small), ctx_chunks=ctx_chunks)
            r, v7, kk, lw, kd, kka, gate7, bonus = _rw_prep_call(
                h, mod[i], norm_mix[i], w["w_rw"], w["mu"], w["w2"], w["w0"], w["a2"], w["a0"], w["g2"], w["kkw"],
                w["kaw"], w["rk"], w["ones"], ctx_tiles=ctx_tiles)
            y_f, y_b = _rw_scan_call(*_rw_chunk_call(r, v7, kk, lw, kd, kka), v7, ctx_chunks=ctx_chunks)
            h = _ev_out_call(h, mod[i], o_f, o_b, dgate, y_f, y_b, gate7, bonus, jnp.tile(dn_norm[j], DN_HEADS),
                             rw_gn_w[j], rw_gn_b[j], w["ones"], bf(ev_w_out[j]), ctx_tiles=ctx_tiles)
            h = _ffn_call(h, mod[i], norm_ffn2[i], bf(ffn2_w1[i]), bf(ffn2_w3[i]), bf(ffn2_w2[i]), j0=6,
                          ctx_tiles=ctx_tiles)
        else:
            params, layouts = _odd_layer_weights(od_w_in[j], gq_q_norm[j], gq_k_norm[j], ml_q_norm[j], ml_w_uq[j],
                                                 ml_kv_norm[j], ml_w_ukv[j])
            tables = [t for rot, angle in layouts for t in _rope_tables(n_ctx, n_lat, rot, angle)]
            qg, qm, kgt, vg, kmt, vm = _od_prep_call(h, mod[i], norm_mix[i], params, tables, ctx_tiles=ctx_tiles)
            hl = _attn_call(h, mod[i], qg, qm, kgt, vg, kmt, vm, bf(od_w_out[j]), ctx_tiles=ctx_tiles)
            assert last
            h = _ffn_call(hl, mod[i], norm_ffn2[i], bf(ffn2_w1[i]), bf(ffn2_w3[i]), bf(ffn2_w2[i]), j0=6, ctx_tiles=0,
                          final_gain=final_norm)
    return h
```

```python
import functools

import jax
import jax.numpy as jnp
import numpy as np
from jax import lax
from jax.experimental import pallas as pl
from jax.experimental.pallas import tpu as pltpu

F32 = jnp.float32
BF16 = jnp.bfloat16

NORM_EPS = 1e-6
ROPE_THETA = 10000.0
GRID_W = 64
N_MOD = 9

DN_HEADS = 4
DN_DK = 128
DN_CONV = 5
RW_HEADS = 8
RW_HS = 64
RW_W = RW_HEADS * RW_HS
RW_W_LORA = 64
RW_A_LORA = 64
RW_G_LORA = 160
RW_GN_EPS = 64e-5
GQ_HEADS = 8
GQ_KV_HEADS = 2
GQ_HD = 64
ML_HEADS = 8
ML_NOPE = 64
ML_ROPE = 32
ML_V = 64

TILE = 256
CHUNK = 64
HALO = 8
LANES = 128
VMEM_LIMIT = 56 * 1024 * 1024
SCAN_BATCH = 4
PRECISE_SPAN = 0
LOG2_E = 1.4426950408889634


def _cparams(*sem):
    return pltpu.CompilerParams(dimension_semantics=sem, vmem_limit_bytes=VMEM_LIMIT)


def _resident(shape):
    nd = len(shape)
    return pl.BlockSpec(shape, lambda *_: (0,) * nd, pipeline_mode=pl.Buffered(1))


def _mm(a, b):
    return jnp.dot(a.astype(BF16), b.astype(BF16), preferred_element_type=F32)


def _mm_nt(a, b):
    return lax.dot_general(a.astype(BF16), b.astype(BF16), (((1,), (1,)), ((), ())),
                           preferred_element_type=F32)


def _mm_tn(a, b):
    return lax.dot_general(a.astype(BF16), b.astype(BF16), (((0,), (0,)), ((), ())),
                           preferred_element_type=F32)


def _split2(x):
    hi = x.astype(BF16)
    lo = (x - hi.astype(F32)).astype(BF16)
    return hi, lo


def _split3(x):
    hi = x.astype(BF16)
    r = x - hi.astype(F32)
    mid = r.astype(BF16)
    lo = (r - mid.astype(F32)).astype(BF16)
    return hi, mid, lo


def _dot(a, b):
    return jnp.dot(a, b, preferred_element_type=F32)


def _mm3s(asp, bsp):
    (ah, al), (bh, bl) = asp, bsp
    return _dot(ah, bh) + (_dot(ah, bl) + _dot(al, bh))


def _mm3(a, b):
    return _mm3s(_split2(a), _split2(b))


def _mm2(a, b):
    ah, al = _split2(a)
    bb = b.astype(BF16)
    return _dot(ah, bb) + _dot(al, bb)


def _mm_exact_lhs(a01, b):
    a = a01.astype(BF16)
    hi, mid, lo = _split3(b)
    return _dot(a, hi) + (_dot(a, mid) + _dot(a, lo))


def _rms_rows(x):
    return x * lax.rsqrt(jnp.mean(x * x, axis=-1, keepdims=True) + NORM_EPS)


def _modulate(x, gain, shift, scale):
    return (_rms_rows(x) * gain) * (1.0 + scale) + shift


def _silu(x):
    return x * jax.nn.sigmoid(x)


def _softplus(x):
    return jnp.maximum(x, 0.0) + jnp.log1p(jnp.exp(-jnp.abs(x)))


def _mod_kernel(c_ref, w_ref, b_ref, o_ref):
    s = _silu(c_ref[...])
    o_ref[0] = _mm3(s, w_ref[0]) + b_ref[0]


def _mod_call(cc, mod_w, mod_b):
    n_layers, d, n = mod_w.shape
    r = cc.shape[0]
    tn = n // 8
    return pl.pallas_call(
        _mod_kernel,
        out_shape=jax.ShapeDtypeStruct((n_layers, r, n), F32),
        grid=(n_layers, n // tn),
        in_specs=[pl.BlockSpec((r, d), lambda l, j: (0, 0)),
                  pl.BlockSpec((1, d, tn), lambda l, j: (l, 0, j)),
                  pl.BlockSpec((1, 1, tn), lambda l, j: (l, 0, j))],
        out_specs=pl.BlockSpec((1, r, tn), lambda l, j: (l, 0, j)),
        compiler_params=_cparams("parallel", "parallel"),
        name="adaln_mod",
    )(cc, mod_w, mod_b.reshape(n_layers, 1, n))


def _all_mod(c, c_ctx, mod_w, mod_b):
    n_batch, d = c.shape
    rows = -(-(n_batch + 1) // 8) * 8
    cc = jnp.zeros((rows, d), F32).at[:n_batch].set(c).at[n_batch].set(c_ctx)
    return _mod_call(cc, mod_w, mod_b).reshape(mod_w.shape[0], rows, N_MOD, d)


def _mod_row_map(n_batch, ctx_tiles, t_off):
    def index_map(b, t):
        return (jnp.where(t + t_off < ctx_tiles, n_batch, b), 0, 0)
    return index_map


def _ffn_kernel(*refs, j0, final, split_tiles):
    if split_tiles:
        ctx_ref, lat_ref, *refs = refs
        x = jnp.where(pl.program_id(1) < split_tiles, ctx_ref[0], lat_ref[0])
    else:
        h_ref, *refs = refs
        x = h_ref[0]
    mod_ref, gain_ref, w1_ref, w3_ref, w2_ref, *rest = refs
    o_ref = rest[-1]
    shift, scale, gate = mod_ref[0, j0:j0 + 1], mod_ref[0, j0 + 1:j0 + 2], mod_ref[0, j0 + 2:j0 + 3]
    xn = _modulate(x, gain_ref[...], shift, scale).astype(BF16)
    a = _dot(xn, w1_ref[...])
    b = _dot(xn, w3_ref[...])
    f = _dot((_silu(a) * b).astype(BF16), w2_ref[...])
    y = x + (0.5 * gate) * f
    if final:
        y = _rms_rows(y) * rest[0][...]
    o_ref[0] = y


def _ffn_call(h, mod, gain, w1, w3, w2, *, j0, ctx_tiles, t_off=0, final_gain=None):
    split = isinstance(h, tuple)
    if split:
        ctx, lat = h
        n_batch, _, d = lat.shape
        t_len = ctx.shape[1] + lat.shape[1]
        last_ctx = ctx_tiles - 1
        streams = [ctx, lat]
        stream_specs = [pl.BlockSpec((1, TILE, d), lambda b, t: (b, jnp.minimum(t, last_ctx), 0)),
                        pl.BlockSpec((1, TILE, d), lambda b, t: (b, jnp.maximum(t - ctx_tiles, 0), 0))]
    else:
        n_batch, t_len, d = h.shape
        streams = [h]
        stream_specs = [pl.BlockSpec((1, TILE, d), lambda b, t: (b, t + t_off, 0))]
    f = w1.shape[1]
    nt = t_len // TILE - t_off
    final = final_gain is not None
    in_specs = stream_specs + [pl.BlockSpec((1, N_MOD, d), _mod_row_map(n_batch, ctx_tiles, t_off)),
                               _resident((1, d)), _resident((d, f)), _resident((d, f)), _resident((f, d))]
    args = streams + [mod, gain.reshape(1, d), w1, w3, w2]
    if final:
        in_specs.append(_resident((1, d)))
        args.append(final_gain.reshape(1, d))
    return pl.pallas_call(
        functools.partial(_ffn_kernel, j0=j0, final=final, split_tiles=ctx_tiles if split else 0),
        out_shape=jax.ShapeDtypeStruct((n_batch, nt * TILE, d), F32),
        grid=(n_batch, nt),
        in_specs=in_specs,
        out_specs=pl.BlockSpec((1, TILE, d), lambda b, t: (b, t, 0)),
        compiler_params=_cparams("parallel", "parallel"),
        name="macaron_ffn",
    )(*args)


def _halo_specs(d, ctx_tiles, n_tiles):
    per = TILE // HALO
    last = n_tiles * per - 1
    return [pl.BlockSpec((1, HALO, d), lambda b, t: (b, jnp.maximum(t * per - 1, 0), 0)),
            pl.BlockSpec((1, TILE, d), lambda b, t: (b, t, 0)),
            pl.BlockSpec((1, HALO, d), lambda b, t: (b, jnp.minimum((t + 1) * per, last), 0))]


def _project_with_halo(prev_ref, cur_ref, next_ref, mod_ref, gain_ref, w_ref, pe_ref, *, ctx_tiles, n_tiles):
    t = pl.program_id(1)
    xe = jnp.concatenate([prev_ref[0], cur_ref[0], next_ref[0]], axis=0)
    xn = _modulate(xe, gain_ref[...], mod_ref[0, 3:4], mod_ref[0, 4:5]).astype(BF16)
    p = _dot(xn, w_ref[...])
    row = lax.broadcasted_iota(jnp.int32, (TILE + 2 * HALO, 1), 0)
    prev_ok = jnp.logical_and(t > 0, t != ctx_tiles)
    next_ok = jnp.logical_and(t + 1 < n_tiles, t + 1 != ctx_tiles)
    keep = jnp.logical_and(jnp.logical_or(row >= HALO, prev_ok),
                           jnp.logical_or(row < HALO + TILE, next_ok))
    pe_ref[...] = jnp.where(keep, p, 0.0)


def _dn_prep_kernel(prev_ref, cur_ref, next_ref, mod_ref, gain_ref, w_ref, conv_ref, alog_ref, dtb_ref,
                    q_ref, k_ref, v_ref, small_ref, gate_ref, pe_ref, *, ctx_tiles, n_tiles):
    _project_with_halo(prev_ref, cur_ref, next_ref, mod_ref, gain_ref, w_ref, pe_ref,
                       ctx_tiles=ctx_tiles, n_tiles=n_tiles)
    nqkv = 3 * DN_HEADS * DN_DK
    half = DN_CONV // 2
    acc = None
    for j in range(DN_CONV):
        term = conv_ref[j:j + 1, :] * pe_ref[pl.ds(HALO - half + j, TILE), 0:nqkv]
        acc = term if acc is None else acc + term
    qkv = _silu(acc)
    w = DN_HEADS * DN_DK
    for idx, ref in ((0, q_ref), (1, k_ref)):
        for h in range(DN_HEADS):
            seg = qkv[:, idx * w + h * DN_DK: idx * w + (h + 1) * DN_DK]
            ref[0, :, h * DN_DK:(h + 1) * DN_DK] = seg * lax.rsqrt(jnp.sum(seg * seg, axis=-1, keepdims=True) + 1e-6)
    v_ref[0] = qkv[:, 2 * w:3 * w]
    gate_ref[0] = pe_ref[pl.ds(HALO, TILE), nqkv:nqkv + w]
    ab = pe_ref[pl.ds(HALO, TILE), nqkv + w:nqkv + w + LANES]
    g = -jnp.exp(alog_ref[...]) * _softplus(ab + dtb_ref[...])
    lane = lax.broadcasted_iota(jnp.int32, ab.shape, 1)
    nh2 = 2 * DN_HEADS
    small_ref[0] = jnp.where(lane < nh2, g, jnp.where(lane < 2 * nh2, jax.nn.sigmoid(ab), 0.0))


def _dn_prep_call(h, mod, gain, w, conv, alog, dtb, *, ctx_tiles):
    n_batch, t_len, d = h.shape
    nt = t_len // TILE
    wd = DN_HEADS * DN_DK
    out = lambda n, dt=F32: jax.ShapeDtypeStruct((n_batch, t_len, n), dt)
    ospec = lambda n: pl.BlockSpec((1, TILE, n), lambda b, t: (b, t, 0))
    return pl.pallas_call(
        functools.partial(_dn_prep_kernel, ctx_tiles=ctx_tiles, n_tiles=nt),
        out_shape=[out(wd), out(wd), out(wd), out(LANES), out(wd)],
        grid=(n_batch, nt),
        in_specs=_halo_specs(d, ctx_tiles, nt) + [
            pl.BlockSpec((1, N_MOD, d), _mod_row_map(n_batch, ctx_tiles, 0)),
            _resident((1, d)), _resident(w.shape), _resident(conv.shape),
            _resident((1, LANES)), _resident((1, LANES))],
        out_specs=[ospec(wd), ospec(wd), ospec(wd), ospec(LANES), ospec(wd)],
        scratch_shapes=[pltpu.VMEM((TILE + 2 * HALO, w.shape[1]), F32)],
        compiler_params=_cparams("parallel", "parallel"),
        name="deltanet_prep",
    )(h, h, h, mod, gain.reshape(1, d), w, conv, alog, dtb)


def _head_sums(x, ones_ref):
    ones = ones_ref[...]
    out = []
    for g in range(x.shape[1] // LANES):
        hi, mid, lo = _split3(x[:, g * LANES:(g + 1) * LANES])
        out.append(_dot(hi, ones) + (_dot(mid, ones) + _dot(lo, ones)))
    return out[0] if len(out) == 1 else jnp.concatenate(out, axis=1)


def _rw_prep_kernel(prev_ref, cur_ref, next_ref, mod_ref, gain_ref, w_ref, mu_ref, w2_ref, w0_ref, a2_ref,
                    a0_ref, g2_ref, kkw_ref, kaw_ref, rk_ref, ones_ref,
                    r_ref, v_ref, kk_ref, lw_ref, kd_ref, kka_ref, gate_ref, bonus_ref, pe_ref,
                    *, ctx_tiles, n_tiles):
    _project_with_halo(prev_ref, cur_ref, next_ref, mod_ref, gain_ref, w_ref, pe_ref,
                       ctx_tiles=ctx_tiles, n_tiles=n_tiles)
    z = pe_ref[pl.ds(HALO, TILE), :]
    zs = 0.5 * (pe_ref[pl.ds(HALO - 1, TILE), :] + pe_ref[pl.ds(HALO + 1, TILE), :])
    s = z + mu_ref[...] * (zs - z)
    r, k7, v7 = s[:, 0:RW_W], s[:, RW_W:2 * RW_W], s[:, 2 * RW_W:3 * RW_W]
    o = 3 * RW_W
    wd, ad, gd = s[:, o:o + LANES], s[:, o + LANES:o + 2 * LANES], s[:, o + 2 * LANES:o + 4 * LANES]
    w_logit = _mm3(jnp.tanh(wd), w2_ref[...]) + w0_ref[...]
    lw = -jnp.exp(-_softplus(-w_logit) - 0.5)
    a = jax.nn.sigmoid(_mm3(ad, a2_ref[...]) + a0_ref[...])
    gate_ref[0] = _mm3(jax.nn.sigmoid(gd), g2_ref[...])
    kx = k7 * kkw_ref[...]
    kk = kx * lax.rsqrt(_head_sums(kx * kx, ones_ref) + 1e-6)
    r_ref[0], v_ref[0], kk_ref[0], lw_ref[0] = r, v7, kk, lw
    kd_sum = None
    for d in range(2):
        a_d = a[:, d * RW_W:(d + 1) * RW_W]
        kd = k7 * (1.0 + (a_d - 1.0) * kaw_ref[...])
        kd_ref[0, :, d * RW_W:(d + 1) * RW_W] = kd
        kka_ref[0, :, d * RW_W:(d + 1) * RW_W] = kk * a_d
        kd_sum = kd if kd_sum is None else kd_sum + kd
    bonus_ref[0] = _head_sums((r * rk_ref[...]) * kd_sum, ones_ref) * v7


def _rw_prep_call(h, mod, gain, w, mu, w2, w0, a2, a0, g2, kkw, kaw, rk, ones, *, ctx_tiles):
    n_batch, t_len, d = h.shape
    nt = t_len // TILE
    out = lambda n: jax.ShapeDtypeStruct((n_batch, t_len, n), F32)
    ospec = lambda n: pl.BlockSpec((1, TILE, n), lambda b, t: (b, t, 0))
    widths = [RW_W, RW_W, RW_W, 2 * RW_W, 2 * RW_W, 2 * RW_W, RW_W, RW_W]
    params = [gain.reshape(1, d), w, mu, w2, w0, a2, a0, g2, kkw, kaw, rk, ones]
    return pl.pallas_call(
        functools.partial(_rw_prep_kernel, ctx_tiles=ctx_tiles, n_tiles=nt),
        out_shape=[out(n) for n in widths],
        grid=(n_batch, nt),
        in_specs=_halo_specs(d, ctx_tiles, nt) + [pl.BlockSpec((1, N_MOD, d), _mod_row_map(n_batch, ctx_tiles, 0))]
        + [_resident(p.shape) for p in params],
        out_specs=[ospec(n) for n in widths],
        scratch_shapes=[pltpu.VMEM((TILE + 2 * HALO, w.shape[1]), F32)],
        compiler_params=_cparams("parallel", "parallel"),
        name="rwkv_prep",
    )(h, h, h, mod, *params)


def _chunk_masks(direction, width=CHUNK):
    i = lax.broadcasted_iota(jnp.int32, (CHUNK, width), 0)
    j = lax.broadcasted_iota(jnp.int32, (CHUNK, width), 1) % CHUNK
    return (i >= j, i > j) if direction == 0 else (i <= j, i < j)


def _last_row(x, direction):
    return x[CHUNK - 1:CHUNK] if direction == 0 else x[0:1]


def _bd(x):
    shape = (2 * CHUNK, x.shape[1])
    r = lax.broadcasted_iota(jnp.int32, shape, 0)
    c = lax.broadcasted_iota(jnp.int32, shape, 1)
    return jnp.where((r < CHUNK) == (c < x.shape[1] // 2), jnp.concatenate([x, x], axis=0), 0.0)


def _neumann_inverse_pairs(ns):
    i = lax.broadcasted_iota(jnp.int32, (CHUNK, 2 * CHUNK), 0)
    j = lax.broadcasted_iota(jnp.int32, (CHUNK, 2 * CHUNK), 1) % CHUNK
    eye = jnp.where(i == j, 1.0, 0.0)
    rs = list(ns)
    lhs = [_split2(n) for n in ns]
    rhs = [_split2(_bd(n)) for n in ns]
    span = 2
    while span < CHUNK:
        mul = _mm3s if span <= PRECISE_SPAN else (lambda a, b: _dot(a[0], b[0]))
        ps = [mul(a, b) for a, b in zip(lhs, rhs)]
        rhs = [_split2(_bd(p)) for p in ps]
        rs = [r + p + mul(_split2(r), b) for r, p, b in zip(rs, ps, rhs)]
        span *= 2
        if span < CHUNK:
            lhs = [_split2(p) for p in ps]
    return [eye + r for r in rs]


def _rev_chunk(n, ctx_chunks, n_chunks):
    return jnp.where(n < ctx_chunks, ctx_chunks - 1 - n, n_chunks - 1 + ctx_chunks - n)


def _dn_chunk_kernel(q_ref, k_ref, v_ref, small_ref, u_ref, w_ref, qd_ref, kdt_ref, attn_ref, gl_ref):
    sm = small_ref[0]
    q, k, v = q_ref[0] * (DN_DK ** -0.5), k_ref[0], v_ref[0]
    nh2, n_pairs, pw, dk = 2 * DN_HEADS, DN_HEADS // 2, 2 * DN_DK, DN_DK
    first_c = lax.broadcasted_iota(jnp.int32, (CHUNK, 2 * CHUNK), 1) < CHUNK
    first_f = lax.broadcasted_iota(jnp.int32, (CHUNK, pw), 1) < dk

    def cols(x, c, first):
        return jnp.where(first[:x.shape[0]], x[:, c:c + 1], x[:, c + 1:c + 2])

    grams = [_mm_nt(jnp.concatenate([k[:, j * pw:(j + 1) * pw], q[:, j * pw:(j + 1) * pw]], axis=0),
                    _bd(k[:, j * pw:(j + 1) * pw])) for j in range(n_pairs)]
    work = []
    for d in range(2):
        incl, strict = _chunk_masks(d, 2 * CHUNK)
        gc = _mm_exact_lhs(jnp.where(_chunk_masks(d)[0], 1.0, 0.0), sm)
        gc_t = gc.T
        gtot = _last_row(gc, d)
        gl_ref[0, d, 0] = jnp.exp(gtot)
        for j in range(n_pairs):
            c = DN_HEADS * d + 2 * j
            gcr = jnp.concatenate([gc_t[c:c + 1, :], gc_t[c + 1:c + 2, :]], axis=1)
            decay = jnp.exp(jnp.where(incl, cols(gc, c, first_c) - gcr, -1e30))
            lower = jnp.where(strict, (cols(sm, nh2 + c, first_c) * grams[j][:CHUNK]) * decay, 0.0)
            work.append((d, j, c, gc, gtot, decay, -lower))
    t_invs = _neumann_inverse_pairs([item[-1] for item in work])
    for (d, j, c, gc, gtot, decay, _), t_inv in zip(work, t_invs):
        sl = slice(j * pw, (j + 1) * pw)
        beta, gcc, gt = cols(sm, nh2 + c, first_f), cols(gc, c, first_f), cols(gtot, c, first_f)
        egc = jnp.exp(gcc)
        kp, qp = k[:, sl], q[:, sl]
        vb, ke = v[:, sl] * beta, (kp * beta) * egc
        rhs = jnp.concatenate([vb[:, :dk], ke[:, :dk], vb[:, dk:], ke[:, dk:]], axis=1)
        sol = _mm2(t_inv, _bd(rhs))
        u_ref[0, d, :, sl] = jnp.concatenate([sol[:, :dk], sol[:, 2 * dk:3 * dk]], axis=1)
        w_ref[0, d, :, sl] = jnp.concatenate([sol[:, dk:2 * dk], sol[:, 3 * dk:]], axis=1).astype(BF16)
        qd_ref[0, d, :, sl] = (qp * egc).astype(BF16)
        k_tail = kp * jnp.exp(gt - gcc)
        kdt_ref[0, d, 0, j * dk:(j + 1) * dk, :] = jnp.concatenate([k_tail[:, :dk].T, k_tail[:, dk:].T], axis=1).astype(BF16)
        attn_ref[0, d, :, 2 * j * CHUNK:2 * (j + 1) * CHUNK] = (grams[j][CHUNK:] * decay).astype(BF16)


def _dn_chunk_call(q, k, v, small):
    n_batch, t_len, wd = q.shape
    nc = t_len // CHUNK
    ispec = lambda n: pl.BlockSpec((1, CHUNK, n), lambda b, c: (b, c, 0))
    ospec = lambda n: pl.BlockSpec((1, 2, CHUNK, n), lambda b, c: (b, 0, c, 0))
    shp = lambda n, dt: jax.ShapeDtypeStruct((n_batch, 2, t_len, n), dt)
    return pl.pallas_call(
        _dn_chunk_kernel,
        out_shape=[shp(wd, F32), shp(wd, BF16), shp(wd, BF16),
                   jax.ShapeDtypeStruct((n_batch, 2, nc, wd // 2, 2 * CHUNK), BF16), shp(DN_HEADS * CHUNK, BF16),
                   jax.ShapeDtypeStruct((n_batch, 2, nc, 1, LANES), F32)],
        grid=(n_batch, nc),
        in_specs=[ispec(wd), ispec(wd), ispec(wd), ispec(LANES)],
        out_specs=[ospec(wd), ospec(wd), ospec(wd),
                   pl.BlockSpec((1, 2, 1, wd // 2, 2 * CHUNK), lambda b, c: (b, 0, c, 0, 0)), ospec(DN_HEADS * CHUNK),
                   pl.BlockSpec((1, 2, 1, 1, LANES), lambda b, c: (b, 0, c, 0, 0))],
        compiler_params=_cparams("parallel", "parallel"),
        name="deltanet_chunk_prep",
    )(q, k, v, small)


def _dn_scan_kernel(*refs):
    ins, (of_ref, ob_ref, s_ref) = refs[:12], refs[12:]

    @pl.when(pl.program_id(1) == 0)
    def _():
        s_ref[...] = jnp.zeros_like(s_ref)

    dk = DN_DK
    chains = [(b, d, j) for b in range(SCAN_BATCH) for d in range(2) for j in range(DN_HEADS // 2)]
    outs = (of_ref, ob_ref)
    stage1 = []
    for b, d, j in chains:
        u_ref, w_ref, qd_ref = ins[6 * d:6 * d + 3]
        s = s_ref[b, d, j]
        sb = s.astype(BF16)
        halves = [(slice((2 * j + i) * dk, (2 * j + i + 1) * dk), slice(i * dk, (i + 1) * dk)) for i in range(2)]
        v_new = jnp.concatenate([u_ref[b, 0, :, sl] - _dot(w_ref[b, 0, :, sl], sb[:, hl]) for sl, hl in halves], axis=1)
        inter = jnp.concatenate([_dot(qd_ref[b, 0, :, sl], sb[:, hl]) for sl, hl in halves], axis=1)
        stage1.append((s, v_new, inter))
    for (b, d, j), (s, v_new, inter) in zip(chains, stage1):
        kdt_ref, attn_ref, gl_ref = ins[6 * d + 3:6 * d + 6]
        v_bd = _bd(v_new).astype(BF16)
        outs[d][b, :, 2 * j * dk:2 * (j + 1) * dk] = inter + _dot(attn_ref[b, 0, :, 2 * j * CHUNK:2 * (j + 1) * CHUNK], v_bd)
        c = DN_HEADS * d + 2 * j
        decayed = jnp.concatenate([s[:, i * dk:(i + 1) * dk] * gl_ref[b, 0, 0, :, c + i:c + i + 1] for i in range(2)], axis=1)
        s_ref[b, d, j] = decayed + _dot(kdt_ref[b, 0, 0, j * dk:(j + 1) * dk, :], v_bd)


def _dn_scan_call(u, w, qd, kdt, attn, gl, *, ctx_chunks):
    n_batch, _, t_len, wd = u.shape
    nc = t_len // CHUNK
    in_specs, args = [], []
    for d in range(2):
        chunk = (lambda n: n) if d == 0 else functools.partial(_rev_chunk, ctx_chunks=ctx_chunks, n_chunks=nc)
        per_token = lambda a: pl.BlockSpec((SCAN_BATCH, 1, CHUNK, a.shape[-1]),
                                           lambda b, n, d=d, chunk=chunk: (b, d, chunk(n), 0))
        per_chunk = lambda a: pl.BlockSpec((SCAN_BATCH, 1, 1) + a.shape[3:],
                                           lambda b, n, d=d, chunk=chunk: (b, d, chunk(n), 0, 0))
        in_specs += [per_token(u), per_token(w), per_token(qd), per_chunk(kdt), per_token(attn), per_chunk(gl)]
        args += [u, w, qd, kdt, attn, gl]
    return pl.pallas_call(
        _dn_scan_kernel,
        out_shape=[jax.ShapeDtypeStruct((n_batch, t_len, wd), F32)] * 2,
        grid=(n_batch // SCAN_BATCH, nc),
        in_specs=in_specs,
        out_specs=[pl.BlockSpec((SCAN_BATCH, CHUNK, wd), lambda b, n: (b, n, 0)),
                   pl.BlockSpec((SCAN_BATCH, CHUNK, wd), lambda b, n: (b, _rev_chunk(n, ctx_chunks, nc), 0))],
        scratch_shapes=[pltpu.VMEM((SCAN_BATCH, 2, DN_HEADS // 2, DN_DK, 2 * DN_DK), F32)],
        compiler_params=_cparams("parallel", "arbitrary"),
        name="deltanet_scan",
    )(*args)


def _rw_chunk_kernel(r_ref, v_ref, kk_ref, lw_ref, kd_ref, kka_ref,
                     ut_ref, wt_ref, rt_ref, arb_ref, kbt_ref, y0_ref, pct_ref):
    r, v, kk = r_ref[0], v_ref[0], kk_ref[0]
    pw = 2 * RW_HS
    work, n_list = [], []
    for d in range(2):
        dsl = slice(d * RW_W, (d + 1) * RW_W)
        lw, kd, kka = lw_ref[0, :, dsl], kd_ref[0, :, dsl], kka_ref[0, :, dsl]
        incl, strict = _chunk_masks(d, 2 * CHUNK)
        cl = _mm_exact_lhs(jnp.where(_chunk_masks(d)[0], 1.0, 0.0), lw)
        tot = _last_row(cl, d)
        p_inv, p_tail = jnp.exp(-cl), jnp.exp(tot - cl)
        at = -kk * jnp.exp(cl - lw)
        rt = r * jnp.exp(cl)
        kh, bh = kd * p_inv, kka * p_inv
        rt_ref[0, d] = rt.astype(BF16)
        pct_ref[0, d, 0] = jnp.broadcast_to(jnp.exp(tot), (8, RW_W)).T
        kbt_ref[0, d, 0] = jnp.concatenate([(kd * p_tail).T, (kka * p_tail).T], axis=1).astype(BF16)
        for j in range(RW_HEADS // 2):
            sl = slice(j * pw, (j + 1) * pw)
            aa = _mm_nt(jnp.concatenate([at[:, sl], rt[:, sl]], axis=0),
                        jnp.concatenate([_bd(bh[:, sl]), _bd(kh[:, sl])], axis=0))
            n_list.append(jnp.where(strict, aa[:CHUNK, :pw], 0.0))
            work.append((d, sl, at[:, sl], jnp.where(strict, aa[:CHUNK, pw:], 0.0),
                         jnp.where(incl, aa[CHUNK:, :pw], 0.0), jnp.where(incl, aa[CHUNK:, pw:], 0.0)))
    t_invs = _neumann_inverse_pairs(n_list)
    for (d, sl, at_p, a_ak, a_rb, a_rk), t_inv in zip(work, t_invs):
        v_bd = _bd(v[:, sl]).astype(BF16)
        sol = _mm2(t_inv, jnp.concatenate([_bd(at_p), _bd(_dot(a_ak.astype(BF16), v_bd))], axis=1))
        wt_ref[0, d, :, sl] = sol[:, :pw].astype(BF16)
        ut_ref[0, d, :, sl] = sol[:, pw:]
        arb_ref[0, d, :, sl] = a_rb.astype(BF16)
        y0_ref[0, d, :, sl] = _dot(a_rk.astype(BF16), v_bd)


def _rw_chunk_call(r, v, kk, lw, kd, kka):
    n_batch, t_len, wd = r.shape
    nc = t_len // CHUNK
    ispec = lambda n: pl.BlockSpec((1, CHUNK, n), lambda b, c: (b, c, 0))
    ospec = pl.BlockSpec((1, 2, CHUNK, wd), lambda b, c: (b, 0, c, 0))
    shp = lambda dt: jax.ShapeDtypeStruct((n_batch, 2, t_len, wd), dt)
    return pl.pallas_call(
        _rw_chunk_kernel,
        out_shape=[shp(F32), shp(BF16), shp(BF16), shp(BF16),
                   jax.ShapeDtypeStruct((n_batch, 2, nc, wd, 2 * CHUNK), BF16), shp(F32),
                   jax.ShapeDtypeStruct((n_batch, 2, nc, wd, 8), F32)],
        grid=(n_batch, nc),
        in_specs=[ispec(wd), ispec(wd), ispec(wd), ispec(2 * wd), ispec(2 * wd), ispec(2 * wd)],
        out_specs=[ospec] * 4 + [pl.BlockSpec((1, 2, 1, wd, 2 * CHUNK), lambda b, c: (b, 0, c, 0, 0)), ospec,
                                 pl.BlockSpec((1, 2, 1, wd, 8), lambda b, c: (b, 0, c, 0, 0))],
        compiler_params=_cparams("parallel", "parallel"),
        name="rwkv_chunk_prep",
    )(r, v, kk, lw, kd, kka)


def _rw_scan_kernel(*refs):
    ins, (yf_ref, yb_ref, s_ref) = refs[:16], refs[16:]

    @pl.when(pl.program_id(1) == 0)
    def _():
        s_ref[...] = jnp.zeros_like(s_ref)

    pw = 2 * RW_HS
    r_i = lax.broadcasted_iota(jnp.int32, (pw, pw), 0)
    c_i = lax.broadcasted_iota(jnp.int32, (pw, pw), 1)
    same_head = (r_i < RW_HS) == (c_i < RW_HS)
    chains = [(b, d, j) for b in range(SCAN_BATCH) for d in range(2) for j in range(RW_HEADS // 2)]
    outs = (yf_ref, yb_ref)
    stage1 = []
    for b, d, j in chains:
        ut_ref, wt_ref, rt_ref, _, _, y0_ref = ins[8 * d:8 * d + 6]
        sl = slice(j * pw, (j + 1) * pw)
        s = s_ref[b, d, j]
        sb = s.astype(BF16)
        u = ut_ref[b, 0, :, sl] + _dot(wt_ref[b, 0, :, sl], sb)
        stage1.append((s, u, y0_ref[b, 0, :, sl] + _dot(rt_ref[b, 0, :, sl], sb)))
    for (b, d, j), (s, u, y_inter) in zip(chains, stage1):
        arb_ref, kbt_ref, _, pct_ref, v_ref = ins[8 * d + 3:8 * d + 8]
        sl = slice(j * pw, (j + 1) * pw)
        outs[d][b, :, sl] = y_inter + _dot(arb_ref[b, 0, :, sl], _bd(u).astype(BF16))
        grow = _dot(kbt_ref[b, 0, 0, sl, :], jnp.concatenate([v_ref[b, :, sl], u], axis=0).astype(BF16))
        s_ref[b, d, j] = s * pct_ref[b, 0, 0, sl, 0:1] + jnp.where(same_head, grow, 0.0)


def _rw_scan_call(ut, wt, rt, arb, kbt, y0, pct, v, *, ctx_chunks):
    n_batch, _, t_len, wd = ut.shape
    nc = t_len // CHUNK
    in_specs, args = [], []
    for d in range(2):
        chunk = (lambda n: n) if d == 0 else functools.partial(_rev_chunk, ctx_chunks=ctx_chunks, n_chunks=nc)
        per_chunk = lambda a: pl.BlockSpec((SCAN_BATCH, 1, 1) + a.shape[3:],
                                           lambda b, n, d=d, chunk=chunk: (b, d, chunk(n), 0, 0))
        per_token = pl.BlockSpec((SCAN_BATCH, 1, CHUNK, wd), lambda b, n, d=d, chunk=chunk: (b, d, chunk(n), 0))
        in_specs += [per_token] * 4 + [per_chunk(kbt), per_token, per_chunk(pct),
                                       pl.BlockSpec((SCAN_BATCH, CHUNK, wd), lambda b, n, chunk=chunk: (b, chunk(n), 0))]
        args += [ut, wt, rt, arb, kbt, y0, pct, v]
    return pl.pallas_call(
        _rw_scan_kernel,
        out_shape=[jax.ShapeDtypeStruct((n_batch, t_len, wd), F32)] * 2,
        grid=(n_batch // SCAN_BATCH, nc),
        in_specs=in_specs,
        out_specs=[pl.BlockSpec((SCAN_BATCH, CHUNK, wd), lambda b, n: (b, n, 0)),
                   pl.BlockSpec((SCAN_BATCH, CHUNK, wd), lambda b, n: (b, _rev_chunk(n, ctx_chunks, nc), 0))],
        scratch_shapes=[pltpu.VMEM((SCAN_BATCH, 2, RW_HEADS // 2, 2 * RW_HS, 2 * RW_HS), F32)],
        compiler_params=_cparams("parallel", "arbitrary"),
        name="rwkv_scan",
    )(*args)


def _ev_out_kernel(h_ref, mod_ref, of_ref, ob_ref, dgate_ref, yf_ref, yb_ref, gate7_ref, bonus_ref,
                   dnorm_ref, gnw_ref, gnb_ref, ones_ref, wout_ref, o_ref):
    o = of_ref[0] + ob_ref[0]
    dgate = dgate_ref[0]
    parts = []
    for h in range(DN_HEADS):
        sl = slice(h * DN_DK, (h + 1) * DN_DK)
        parts.append(_rms_rows(o[:, sl]) * dnorm_ref[:, sl] * _silu(dgate[:, sl]))
    o_dn = jnp.concatenate(parts, axis=1)
    y = yf_ref[0] + yb_ref[0]
    inv_n = 1.0 / RW_HS
    mu = _head_sums(y, ones_ref) * inv_n
    yc = y - mu
    var = _head_sums(yc * yc, ones_ref) * inv_n
    yn = yc * lax.rsqrt(var + RW_GN_EPS) * gnw_ref[...] + gnb_ref[...]
    o_rw = (yn + bonus_ref[0]) * gate7_ref[0]
    wd = DN_HEADS * DN_DK
    proj = _dot(o_dn.astype(BF16), wout_ref[0:wd, :]) + _dot(o_rw.astype(BF16), wout_ref[wd:, :])
    o_ref[0] = h_ref[0] + mod_ref[0, 5:6] * proj


def _ev_out_call(h, mod, o_f, o_b, dgate, y_f, y_b, gate7, bonus, dnorm, gnw, gnb, ones, w_out, *, ctx_tiles):
    n_batch, t_len, d = h.shape
    nt = t_len // TILE
    tile = lambda n: pl.BlockSpec((1, TILE, n), lambda b, t: (b, t, 0))
    params = [dnorm.reshape(1, -1), gnw.reshape(1, -1), gnb.reshape(1, -1), ones, w_out]
    streams = [o_f, o_b, dgate, y_f, y_b, gate7, bonus]
    return pl.pallas_call(
        _ev_out_kernel,
        out_shape=jax.ShapeDtypeStruct(h.shape, F32),
        grid=(n_batch, nt),
        in_specs=[tile(d), pl.BlockSpec((1, N_MOD, d), _mod_row_map(n_batch, ctx_tiles, 0))]
        + [tile(s.shape[-1]) for s in streams] + [_resident(p.shape) for p in params],
        out_specs=tile(d),
        compiler_params=_cparams("parallel", "parallel"),
        name="even_mix_out",
    )(h, mod, *streams, *params)


def _rope_layout(width, rot, starts):
    angle = np.full((width,), -1, np.int64)
    first = np.zeros((1, width), np.float32)
    q = rot // 4
    for start in starts:
        for blk in range(2):
            for idx in range(q):
                l1 = start + blk * 2 * q + idx
                angle[l1] = angle[l1 + q] = blk * q + idx
                first[0, l1] = 1.0
    return angle, jnp.asarray(first)


def _rope_tables(n_ctx, n_lat, rot, angle, first):
    rows = n_lat // GRID_W
    row = jnp.repeat(jnp.arange(rows), GRID_W).astype(F32)
    col = jnp.tile(jnp.arange(GRID_W), rows).astype(F32)
    axis_dim = rot // 2
    inv = ROPE_THETA ** (-jnp.arange(0, axis_dim, 2, dtype=F32) / axis_dim)
    ang = jnp.concatenate([row[:, None] * inv, col[:, None] * inv], axis=-1)
    on = jnp.asarray(angle >= 0)
    idx = np.maximum(angle, 0)
    cos = jnp.where(on, jnp.cos(ang)[:, idx], 1.0)
    sin = jnp.where(on, jnp.sin(ang)[:, idx], 0.0) * (1.0 - 2.0 * first)
    width = angle.shape[0]
    return (jnp.concatenate([jnp.ones((n_ctx, width), F32), cos], axis=0),
            jnp.concatenate([jnp.zeros((n_ctx, width), F32), sin], axis=0))


def _rope(x, first_ref, cos_ref, sin_ref, quarter):
    width = x.shape[1]
    partner = jnp.where(first_ref[...] > 0.5, pltpu.roll(x, width - quarter, 1), pltpu.roll(x, quarter, 1))
    return x * cos_ref[...] + partner * sin_ref[...]


def _od_prep_kernel(h_ref, mod_ref, gain_ref, w_ref, qn_ref, kn_ref, mqn_ref, wuq_ref, mkvn_ref, wukv_ref,
                    ones_ref, fq_ref, fk_ref, fm_ref, fr_ref,
                    cq_ref, sq_ref, ck_ref, sk_ref, cm_ref, sm_ref, cr_ref, sr_ref,
                    qg_ref, qm_ref, kgt_ref, vg_ref, kmt_ref, vm_ref):
    xn = _modulate(h_ref[0], gain_ref[...], mod_ref[0, 3:4], mod_ref[0, 4:5]).astype(BF16)
    p = _dot(xn, w_ref[...])
    nq, nkv = GQ_HEADS * GQ_HD, GQ_KV_HEADS * GQ_HD
    o = 0
    q, o = p[:, o:o + nq], o + nq
    k, o = p[:, o:o + nkv], o + nkv
    v, o = p[:, o:o + nkv], o + nkv
    n_cq, n_ckv = mqn_ref.shape[1], mkvn_ref.shape[1]
    cq, o = p[:, o:o + n_cq], o + n_cq
    ckv, o = p[:, o:o + n_ckv], o + n_ckv
    kr = p[:, o:o + LANES]
    inv_hd = 1.0 / GQ_HD
    q = q * lax.rsqrt(_head_sums(q * q, ones_ref) * inv_hd + NORM_EPS) * qn_ref[...]
    k = k * lax.rsqrt(_head_sums(k * k, ones_ref) * inv_hd + NORM_EPS) * kn_ref[...]
    qm = _dot((_rms_rows(cq) * mqn_ref[...]).astype(BF16), wuq_ref[...])
    kvm = _dot((_rms_rows(ckv) * mkvn_ref[...]).astype(BF16), wukv_ref[...])
    q = _rope(q, fq_ref, cq_ref, sq_ref, GQ_HD // 4) * (GQ_HD ** -0.5 * LOG2_E)
    k = _rope(k, fk_ref, ck_ref, sk_ref, GQ_HD // 4)
    qm = _rope(qm, fm_ref, cm_ref, sm_ref, ML_ROPE // 4) * ((ML_NOPE + ML_ROPE) ** -0.5 * LOG2_E)
    kr = _rope(kr, fr_ref, cr_ref, sr_ref, ML_ROPE // 4)
    qg_ref[0] = q.astype(BF16)
    qm_ref[0] = qm.astype(BF16)
    vg_ref[0] = v.astype(BF16)
    n_nope = ML_HEADS * ML_NOPE
    vm_ref[0] = kvm[:, n_nope:].astype(BF16)
    kgt_ref[0] = k.T.astype(BF16)
    knt = kvm[:, :n_nope].T.astype(BF16)
    krt = kr.T[:ML_ROPE].astype(BF16)
    dk = ML_NOPE + ML_ROPE
    for h in range(ML_HEADS):
        kmt_ref[0, h * dk:h * dk + ML_NOPE, :] = knt[h * ML_NOPE:(h + 1) * ML_NOPE]
        kmt_ref[0, h * dk + ML_NOPE:(h + 1) * dk, :] = krt


def _od_prep_call(h, mod, gain, params, tables, *, ctx_tiles):
    n_batch, t_len, d = h.shape
    nt = t_len // TILE
    nq, nkv = GQ_HEADS * GQ_HD, GQ_KV_HEADS * GQ_HD
    dk = ML_NOPE + ML_ROPE
    tile = lambda n: pl.BlockSpec((1, TILE, n), lambda b, t: (b, t, 0))
    tile_t = lambda n: pl.BlockSpec((1, n, TILE), lambda b, t: (b, 0, t))
    tab = lambda a: pl.BlockSpec((TILE, a.shape[1]), lambda b, t: (t, 0))
    shp = lambda *s: jax.ShapeDtypeStruct((n_batch,) + s, BF16)
    return pl.pallas_call(
        _od_prep_kernel,
        out_shape=[shp(t_len, nq), shp(t_len, ML_HEADS * dk), shp(nkv, t_len), shp(t_len, nkv),
                   shp(ML_HEADS * dk, t_len), shp(t_len, ML_HEADS * ML_V)],
        grid=(n_batch, nt),
        in_specs=[tile(d), pl.BlockSpec((1, N_MOD, d), _mod_row_map(n_batch, ctx_tiles, 0)), _resident((1, d))]
        + [_resident(p.shape) for p in params] + [tab(a) for a in tables],
        out_specs=[tile(nq), tile(ML_HEADS * dk), tile_t(nkv), tile(nkv), tile_t(ML_HEADS * dk), tile(ML_HEADS * ML_V)],
        compiler_params=_cparams("parallel", "parallel"),
        name="attn_prep",
    )(h, mod, gain.reshape(1, d), *params, *tables)


def _odd_layer_weights(od_w_in, gq_q_norm, gq_k_norm, ml_q_norm, ml_w_uq, ml_kv_norm, ml_w_ukv):
    d = od_w_in.shape[0]
    w = jnp.concatenate([od_w_in, jnp.zeros((d, LANES - ML_ROPE), F32)], axis=1)
    ukv = ml_w_ukv.reshape(ml_w_ukv.shape[0], ML_HEADS, ML_NOPE + ML_V)
    ukv = jnp.concatenate([ukv[:, :, :ML_NOPE].reshape(-1, ML_HEADS * ML_NOPE),
                           ukv[:, :, ML_NOPE:].reshape(-1, ML_HEADS * ML_V)], axis=1)
    dk = ML_NOPE + ML_ROPE
    layouts = [(GQ_HD,) + _rope_layout(GQ_HEADS * GQ_HD, GQ_HD, [h * GQ_HD for h in range(GQ_HEADS)]),
               (GQ_HD,) + _rope_layout(GQ_KV_HEADS * GQ_HD, GQ_HD, [h * GQ_HD for h in range(GQ_KV_HEADS)]),
               (ML_ROPE,) + _rope_layout(ML_HEADS * dk, ML_ROPE, [h * dk + ML_NOPE for h in range(ML_HEADS)]),
               (ML_ROPE,) + _rope_layout(LANES, ML_ROPE, [0])]
    params = [w.astype(BF16), jnp.tile(gq_q_norm, GQ_HEADS).reshape(1, -1), jnp.tile(gq_k_norm, GQ_KV_HEADS).reshape(1, -1),
              ml_q_norm.reshape(1, -1), ml_w_uq.astype(BF16), ml_kv_norm.reshape(1, -1), ukv.astype(BF16),
              _block_ones(LANES, GQ_HD)] + [first for _, _, first in layouts]
    return params, layouts


def _softmax_pv(s, v):
    m = jnp.max(s, axis=-1, keepdims=True)
    p = jnp.exp2(s - m)
    return _dot(p.astype(BF16), v) / jnp.sum(p, axis=-1, keepdims=True)


def _attn_kernel(h_ref, mod_ref, qg_ref, qm_ref, kgt_ref, vg_ref, kmt_ref, vm_ref, wout_ref, o_ref):
    group = GQ_HEADS // GQ_KV_HEADS
    dk = ML_NOPE + ML_ROPE

    def logits(h):
        if h < GQ_HEADS:
            g = h // group
            return _dot(qg_ref[0, :, h * GQ_HD:(h + 1) * GQ_HD], kgt_ref[0, g * GQ_HD:(g + 1) * GQ_HD, :])
        h -= GQ_HEADS
        return _dot(qm_ref[0, :, h * dk:(h + 1) * dk], kmt_ref[0, h * dk:(h + 1) * dk, :])

    def values(h):
        if h < GQ_HEADS:
            return vg_ref[0], (h // group) * GQ_HD
        pair = ((h - GQ_HEADS) * ML_V) // LANES * LANES
        return vm_ref[0, :, pair:pair + LANES], (h - GQ_HEADS) * ML_V - pair

    n_heads = GQ_HEADS + ML_HEADS
    parts, s = [], logits(0)
    for h in range(n_heads):
        s_next = logits(h + 1) if h + 1 < n_heads else None
        v, off = values(h)
        parts.append(_softmax_pv(s, v)[:, off:off + ML_V])
        s = s_next
    ol = jnp.concatenate(parts, axis=1).astype(BF16)
    o_ref[0] = h_ref[0] + mod_ref[0, 5:6] * _dot(ol, wout_ref[...])


def _attn_call(h, mod, qg, qm, kgt, vg, kmt, vm, w_out, *, ctx_tiles):
    n_batch, t_len, d = h.shape
    nt = t_len // TILE - ctx_tiles
    qtile = lambda n: pl.BlockSpec((1, TILE, n), lambda b, t: (b, t + ctx_tiles, 0))
    whole = lambda a: pl.BlockSpec((1,) + a.shape[1:], lambda b, t: (b, 0, 0))
    return pl.pallas_call(
        _attn_kernel,
        out_shape=jax.ShapeDtypeStruct((n_batch, nt * TILE, d), F32),
        grid=(n_batch, nt),
        in_specs=[qtile(d), pl.BlockSpec((1, N_MOD, d), lambda b, t: (b, 0, 0)), qtile(qg.shape[-1]), qtile(qm.shape[-1]),
                  whole(kgt), whole(vg), whole(kmt), whole(vm), _resident(w_out.shape)],
        out_specs=pl.BlockSpec((1, TILE, d), lambda b, t: (b, t, 0)),
        compiler_params=_cparams("parallel", "parallel"),
        name="attention_out",
    )(h, mod, qg, qm, kgt, vg, kmt, vm, w_out)


def _block_ones(n, blk):
    i = np.arange(n) // blk
    return jnp.asarray(i[:, None] == i[None, :], BF16)


def _even_layer_weights(ev_w_in, dn_conv, dn_a_log, dn_dt_bias, rw_mu, rw_w0, rw_w2, rw_a0, rw_a2, rw_g2,
                        rw_kk, rw_ka, rw_rk):
    d = ev_w_in.shape[0]
    n_dn = 4 * DN_HEADS * DN_DK
    nh2 = 2 * DN_HEADS
    slab0 = n_dn + 2 * nh2
    zeros = lambda n: jnp.zeros((d, n), F32)
    w_dn = jnp.concatenate([ev_w_in[:, :n_dn], ev_w_in[:, n_dn:slab0], zeros(LANES - 2 * nh2)], axis=1)
    slab = ev_w_in[:, slab0:]
    o = 3 * RW_W
    lora = 2 * RW_W_LORA
    gpad = 2 * LANES - RW_G_LORA
    w_rw = jnp.concatenate([slab[:, :o + 2 * lora + RW_G_LORA], zeros(gpad)], axis=1)
    mu = jnp.concatenate([rw_mu, jnp.zeros((gpad,), F32)]).reshape(1, -1)
    pad_lanes = lambda v: jnp.zeros((1, LANES), F32).at[0, :v.size].set(v.reshape(-1))

    def dir_blocks(m):
        z = jnp.zeros_like(m[0])
        return jnp.concatenate([jnp.concatenate([m[0], z], axis=1), jnp.concatenate([z, m[1]], axis=1)], axis=0)

    g2 = jnp.concatenate([rw_g2, jnp.zeros((gpad, RW_W), F32)], axis=0)
    return dict(
        w_dn=w_dn.astype(BF16), conv=dn_conv, alog=pad_lanes(dn_a_log), dtb=pad_lanes(dn_dt_bias),
        w_rw=w_rw.astype(BF16), mu=mu, w2=dir_blocks(rw_w2), w0=rw_w0.reshape(1, -1), a2=dir_blocks(rw_a2),
        a0=rw_a0.reshape(1, -1), g2=g2, kkw=rw_kk.reshape(1, -1), kaw=rw_ka.reshape(1, -1),
        rk=rw_rk.reshape(1, -1), ones=_block_ones(LANES, RW_HS))


def kernel(x, c, ctx, c_ctx, mod_w, mod_b, norm_ffn1, norm_mix, norm_ffn2, ffn1_w1, ffn1_w3, ffn1_w2, ffn2_w1, ffn2_w3, ffn2_w2, ev_w_in, ev_w_out, dn_conv, dn_a_log, dn_dt_bias, dn_norm, rw_mu, rw_w0, rw_w2, rw_a0, rw_a2, rw_g2, rw_kk, rw_ka, rw_rk, rw_gn_w, rw_gn_b, od_w_in, od_w_out, gq_q_norm, gq_k_norm, ml_q_norm, ml_w_uq, ml_kv_norm, ml_w_ukv, final_norm):
    n_batch, n_lat, d = x.shape
    n_ctx = ctx.shape[1]
    depth = mod_w.shape[0]
    assert n_ctx % TILE == 0 and n_lat % TILE == 0 and n_lat % GRID_W == 0 and n_batch % SCAN_BATCH == 0
    assert depth % 2 == 0 and depth // 2 == od_w_in.shape[0] == 1, "supported stack: [recurrent, attention]"
    ctx_tiles, ctx_chunks = n_ctx // TILE, n_ctx // CHUNK
    bf = lambda a: a.astype(BF16)

    mod = _all_mod(c, c_ctx, mod_w, mod_b)
    h = (ctx, x)
    for i in range(depth):
        j = i // 2
        last = i == depth - 1
        h = _ffn_call(h, mod[i], norm_ffn1[i], bf(ffn1_w1[i]), bf(ffn1_w3[i]), bf(ffn1_w2[i]), j0=0, ctx_tiles=ctx_tiles)
        if i % 2 == 0:
            w = _even_layer_weights(ev_w_in[j], dn_conv[j], dn_a_log[j], dn_dt_bias[j], rw_mu[j], rw_w0[j], rw_w2[j],
                                    rw_a0[j], rw_a2[j], rw_g2[j], rw_kk[j], rw_ka[j], rw_rk[j])
            q, k, v, small, dgate = _dn_prep_call(h, mod[i], norm_mix[i], w["w_dn"], w["conv"], w["alog"], w["dtb"],
                                                  ctx_tiles=ctx_tiles)
            o_f, o_b = _dn_scan_call(*_dn_chunk_call(q, k, v, small), ctx_chunks=ctx_chunks)
            r, v7, kk, lw, kd, kka, gate7, bonus = _rw_prep_call(
                h, mod[i], norm_mix[i], w["w_rw"], w["mu"], w["w2"], w["w0"], w["a2"], w["a0"], w["g2"], w["kkw"],
                w["kaw"], w["rk"], w["ones"], ctx_tiles=ctx_tiles)
            y_f, y_b = _rw_scan_call(*_rw_chunk_call(r, v7, kk, lw, kd, kka), v7, ctx_chunks=ctx_chunks)
            h = _ev_out_call(h, mod[i], o_f, o_b, dgate, y_f, y_b, gate7, bonus, jnp.tile(dn_norm[j], DN_HEADS),
                             rw_gn_w[j], rw_gn_b[j], w["ones"], bf(ev_w_out[j]), ctx_tiles=ctx_tiles)
            h = _ffn_call(h, mod[i], norm_ffn2[i], bf(ffn2_w1[i]), bf(ffn2_w3[i]), bf(ffn2_w2[i]), j0=6,
                          ctx_tiles=ctx_tiles)
        else:
            params, layouts = _odd_layer_weights(od_w_in[j], gq_q_norm[j], gq_k_norm[j], ml_q_norm[j], ml_w_uq[j],
                                                 ml_kv_norm[j], ml_w_ukv[j])
            tables = [t for rot, angle, first in layouts for t in _rope_tables(n_ctx, n_lat, rot, angle, first)]
            qg, qm, kgt, vg, kmt, vm = _od_prep_call(h, mod[i], norm_mix[i], params, tables, ctx_tiles=ctx_tiles)
            hl = _attn_call(h, mod[i], qg, qm, kgt, vg, kmt, vm, bf(od_w_out[j]), ctx_tiles=ctx_tiles)
            assert last
            h = _ffn_call(hl, mod[i], norm_ffn2[i], bf(ffn2_w1[i]), bf(ffn2_w3[i]), bf(ffn2_w2[i]), j0=6, ctx_tiles=0,
                          final_gain=final_norm)
    return h
```

```python
import functools

import jax
import jax.numpy as jnp
import numpy as np
from jax import lax
from jax.experimental import pallas as pl
from jax.experimental.pallas import tpu as pltpu

F32 = jnp.float32
BF16 = jnp.bfloat16

NORM_EPS = 1e-6
ROPE_THETA = 10000.0
GRID_W = 64
N_MOD = 9

DN_HEADS = 4
DN_DK = 128
DN_CONV = 5
RW_HEADS = 8
RW_HS = 64
RW_W = RW_HEADS * RW_HS
RW_W_LORA = 64
RW_A_LORA = 64
RW_G_LORA = 160
RW_GN_EPS = 64e-5
GQ_HEADS = 8
GQ_KV_HEADS = 2
GQ_HD = 64
ML_HEADS = 8
ML_NOPE = 64
ML_ROPE = 32
ML_V = 64

TILE = 256
CHUNK = 64
HALO = 8
LANES = 128
VMEM_LIMIT = 56 * 1024 * 1024
SCAN_BATCH = 4
CHUNKS_PER_STEP = 4
PRECISE_SPAN = 0
LOG2_E = 1.4426950408889634


def _cparams(*sem):
    return pltpu.CompilerParams(dimension_semantics=sem, vmem_limit_bytes=VMEM_LIMIT)


def _resident(shape):
    nd = len(shape)
    return pl.BlockSpec(shape, lambda *_: (0,) * nd, pipeline_mode=pl.Buffered(1))


def _mm(a, b):
    return jnp.dot(a.astype(BF16), b.astype(BF16), preferred_element_type=F32)


def _mm_nt(a, b):
    return lax.dot_general(a.astype(BF16), b.astype(BF16), (((1,), (1,)), ((), ())),
                           preferred_element_type=F32)


def _mm_tn(a, b):
    return lax.dot_general(a.astype(BF16), b.astype(BF16), (((0,), (0,)), ((), ())),
                           preferred_element_type=F32)


def _split2(x):
    hi = x.astype(BF16)
    lo = (x - hi.astype(F32)).astype(BF16)
    return hi, lo


def _split3(x):
    hi = x.astype(BF16)
    r = x - hi.astype(F32)
    mid = r.astype(BF16)
    lo = (r - mid.astype(F32)).astype(BF16)
    return hi, mid, lo


def _dot(a, b):
    return jnp.dot(a, b, preferred_element_type=F32)


def _mm3s(asp, bsp):
    (ah, al), (bh, bl) = asp, bsp
    return _dot(ah, bh) + (_dot(ah, bl) + _dot(al, bh))


def _mm3(a, b):
    return _mm3s(_split2(a), _split2(b))


def _mm2(a, b):
    ah, al = _split2(a)
    bb = b.astype(BF16)
    return _dot(ah, bb) + _dot(al, bb)


def _mm_exact_lhs(a01, b):
    a = a01.astype(BF16)
    hi, mid, lo = _split3(b)
    return _dot(a, hi) + (_dot(a, mid) + _dot(a, lo))


def _rms_rows(x):
    return x * lax.rsqrt(jnp.mean(x * x, axis=-1, keepdims=True) + NORM_EPS)


def _modulate(x, gain, shift, scale):
    return (_rms_rows(x) * gain) * (1.0 + scale) + shift


def _silu(x):
    return x * jax.nn.sigmoid(x)


def _softplus(x):
    return jnp.maximum(x, 0.0) + jnp.log1p(jnp.exp(-jnp.abs(x)))


def _mod_kernel(c_ref, w_ref, b_ref, o_ref):
    s = _silu(c_ref[...])
    o_ref[0] = _mm3(s, w_ref[0]) + b_ref[0]


def _mod_call(cc, mod_w, mod_b):
    n_layers, d, n = mod_w.shape
    r = cc.shape[0]
    tn = n // 8
    return pl.pallas_call(
        _mod_kernel,
        out_shape=jax.ShapeDtypeStruct((n_layers, r, n), F32),
        grid=(n_layers, n // tn),
        in_specs=[pl.BlockSpec((r, d), lambda l, j: (0, 0)),
                  pl.BlockSpec((1, d, tn), lambda l, j: (l, 0, j)),
                  pl.BlockSpec((1, 1, tn), lambda l, j: (l, 0, j))],
        out_specs=pl.BlockSpec((1, r, tn), lambda l, j: (l, 0, j)),
        compiler_params=_cparams("parallel", "parallel"),
        name="adaln_mod",
    )(cc, mod_w, mod_b.reshape(n_layers, 1, n))


def _all_mod(c, c_ctx, mod_w, mod_b):
    n_batch, d = c.shape
    rows = -(-(n_batch + 1) // 8) * 8
    cc = jnp.zeros((rows, d), F32).at[:n_batch].set(c).at[n_batch].set(c_ctx)
    return _mod_call(cc, mod_w, mod_b).reshape(mod_w.shape[0], rows, N_MOD, d)


def _mod_row_map(n_batch, ctx_tiles, t_off):
    def index_map(b, t):
        return (jnp.where(t + t_off < ctx_tiles, n_batch, b), 0, 0)
    return index_map


def _ffn_kernel(*refs, j0, final, split_tiles):
    if split_tiles:
        ctx_ref, lat_ref, *refs = refs
        x = jnp.where(pl.program_id(1) < split_tiles, ctx_ref[0], lat_ref[0])
    else:
        h_ref, *refs = refs
        x = h_ref[0]
    mod_ref, gain_ref, w1_ref, w3_ref, w2_ref, *rest = refs
    o_ref = rest[-1]
    shift, scale, gate = mod_ref[0, j0:j0 + 1], mod_ref[0, j0 + 1:j0 + 2], mod_ref[0, j0 + 2:j0 + 3]
    xn = _modulate(x, gain_ref[...], shift, scale).astype(BF16)
    a = _dot(xn, w1_ref[...])
    b = _dot(xn, w3_ref[...])
    f = _dot((_silu(a) * b).astype(BF16), w2_ref[...])
    y = x + (0.5 * gate) * f
    if final:
        y = _rms_rows(y) * rest[0][...]
    o_ref[0] = y


def _ffn_call(h, mod, gain, w1, w3, w2, *, j0, ctx_tiles, t_off=0, final_gain=None):
    split = isinstance(h, tuple)
    if split:
        ctx, lat = h
        n_batch, _, d = lat.shape
        t_len = ctx.shape[1] + lat.shape[1]
        last_ctx = ctx_tiles - 1
        streams = [ctx, lat]
        stream_specs = [pl.BlockSpec((1, TILE, d), lambda b, t: (b, jnp.minimum(t, last_ctx), 0)),
                        pl.BlockSpec((1, TILE, d), lambda b, t: (b, jnp.maximum(t - ctx_tiles, 0), 0))]
    else:
        n_batch, t_len, d = h.shape
        streams = [h]
        stream_specs = [pl.BlockSpec((1, TILE, d), lambda b, t: (b, t + t_off, 0))]
    f = w1.shape[1]
    nt = t_len // TILE - t_off
    final = final_gain is not None
    in_specs = stream_specs + [pl.BlockSpec((1, N_MOD, d), _mod_row_map(n_batch, ctx_tiles, t_off)),
                               _resident((1, d)), _resident((d, f)), _resident((d, f)), _resident((f, d))]
    args = streams + [mod, gain.reshape(1, d), w1, w3, w2]
    if final:
        in_specs.append(_resident((1, d)))
        args.append(final_gain.reshape(1, d))
    return pl.pallas_call(
        functools.partial(_ffn_kernel, j0=j0, final=final, split_tiles=ctx_tiles if split else 0),
        out_shape=jax.ShapeDtypeStruct((n_batch, nt * TILE, d), F32),
        grid=(n_batch, nt),
        in_specs=in_specs,
        out_specs=pl.BlockSpec((1, TILE, d), lambda b, t: (b, t, 0)),
        compiler_params=_cparams("parallel", "parallel"),
        name="macaron_ffn",
    )(*args)


def _halo_specs(d, ctx_tiles, n_tiles):
    per = TILE // HALO
    last = n_tiles * per - 1
    return [pl.BlockSpec((1, HALO, d), lambda b, t: (b, jnp.maximum(t * per - 1, 0), 0)),
            pl.BlockSpec((1, TILE, d), lambda b, t: (b, t, 0)),
            pl.BlockSpec((1, HALO, d), lambda b, t: (b, jnp.minimum((t + 1) * per, last), 0))]


def _project_with_halo(prev_ref, cur_ref, next_ref, mod_ref, gain_ref, w_ref, pe_ref, *, ctx_tiles, n_tiles):
    t = pl.program_id(1)
    xe = jnp.concatenate([prev_ref[0], cur_ref[0], next_ref[0]], axis=0)
    xn = _modulate(xe, gain_ref[...], mod_ref[0, 3:4], mod_ref[0, 4:5]).astype(BF16)
    p = _dot(xn, w_ref[...])
    row = lax.broadcasted_iota(jnp.int32, (TILE + 2 * HALO, 1), 0)
    prev_ok = jnp.logical_and(t > 0, t != ctx_tiles)
    next_ok = jnp.logical_and(t + 1 < n_tiles, t + 1 != ctx_tiles)
    keep = jnp.logical_and(jnp.logical_or(row >= HALO, prev_ok),
                           jnp.logical_or(row < HALO + TILE, next_ok))
    pe_ref[...] = jnp.where(keep, p, 0.0)


def _dn_prep_kernel(prev_ref, cur_ref, next_ref, mod_ref, gain_ref, w_ref, conv_ref, alog_ref, dtb_ref,
                    q_ref, k_ref, v_ref, small_ref, gate_ref, pe_ref, *, ctx_tiles, n_tiles):
    _project_with_halo(prev_ref, cur_ref, next_ref, mod_ref, gain_ref, w_ref, pe_ref,
                       ctx_tiles=ctx_tiles, n_tiles=n_tiles)
    nqkv = 3 * DN_HEADS * DN_DK
    half = DN_CONV // 2
    acc = None
    for j in range(DN_CONV):
        term = conv_ref[j:j + 1, :] * pe_ref[pl.ds(HALO - half + j, TILE), 0:nqkv]
        acc = term if acc is None else acc + term
    qkv = _silu(acc)
    w = DN_HEADS * DN_DK
    for idx, ref in ((0, q_ref), (1, k_ref)):
        for h in range(DN_HEADS):
            seg = qkv[:, idx * w + h * DN_DK: idx * w + (h + 1) * DN_DK]
            ref[0, :, h * DN_DK:(h + 1) * DN_DK] = seg * lax.rsqrt(jnp.sum(seg * seg, axis=-1, keepdims=True) + 1e-6)
    v_ref[0] = qkv[:, 2 * w:3 * w]
    gate_ref[0] = pe_ref[pl.ds(HALO, TILE), nqkv:nqkv + w]
    ab = pe_ref[pl.ds(HALO, TILE), nqkv + w:nqkv + w + LANES]
    g = -jnp.exp(alog_ref[...]) * _softplus(ab + dtb_ref[...])
    lane = lax.broadcasted_iota(jnp.int32, ab.shape, 1)
    nh2 = 2 * DN_HEADS
    small_ref[0] = jnp.where(lane < nh2, g, jnp.where(lane < 2 * nh2, jax.nn.sigmoid(ab), 0.0))


def _dn_prep_call(h, mod, gain, w, conv, alog, dtb, *, ctx_tiles):
    n_batch, t_len, d = h.shape
    nt = t_len // TILE
    wd = DN_HEADS * DN_DK
    out = lambda n, dt=F32: jax.ShapeDtypeStruct((n_batch, t_len, n), dt)
    ospec = lambda n: pl.BlockSpec((1, TILE, n), lambda b, t: (b, t, 0))
    return pl.pallas_call(
        functools.partial(_dn_prep_kernel, ctx_tiles=ctx_tiles, n_tiles=nt),
        out_shape=[out(wd), out(wd), out(wd), out(LANES), out(wd)],
        grid=(n_batch, nt),
        in_specs=_halo_specs(d, ctx_tiles, nt) + [
            pl.BlockSpec((1, N_MOD, d), _mod_row_map(n_batch, ctx_tiles, 0)),
            _resident((1, d)), _resident(w.shape), _resident(conv.shape),
            _resident((1, LANES)), _resident((1, LANES))],
        out_specs=[ospec(wd), ospec(wd), ospec(wd), ospec(LANES), ospec(wd)],
        scratch_shapes=[pltpu.VMEM((TILE + 2 * HALO, w.shape[1]), F32)],
        compiler_params=_cparams("parallel", "parallel"),
        name="deltanet_prep",
    )(h, h, h, mod, gain.reshape(1, d), w, conv, alog, dtb)


def _head_sums(x, ones_ref):
    ones = ones_ref[...]
    out = []
    for g in range(x.shape[1] // LANES):
        hi, mid, lo = _split3(x[:, g * LANES:(g + 1) * LANES])
        out.append(_dot(hi, ones) + (_dot(mid, ones) + _dot(lo, ones)))
    return out[0] if len(out) == 1 else jnp.concatenate(out, axis=1)


def _rw_prep_kernel(prev_ref, cur_ref, next_ref, mod_ref, gain_ref, w_ref, mu_ref, w2_ref, w0_ref, a2_ref,
                    a0_ref, g2_ref, kkw_ref, kaw_ref, rk_ref, ones_ref,
                    r_ref, v_ref, kk_ref, lw_ref, kd_ref, kka_ref, gate_ref, bonus_ref, pe_ref,
                    *, ctx_tiles, n_tiles):
    _project_with_halo(prev_ref, cur_ref, next_ref, mod_ref, gain_ref, w_ref, pe_ref,
                       ctx_tiles=ctx_tiles, n_tiles=n_tiles)
    z = pe_ref[pl.ds(HALO, TILE), :]
    zs = 0.5 * (pe_ref[pl.ds(HALO - 1, TILE), :] + pe_ref[pl.ds(HALO + 1, TILE), :])
    s = z + mu_ref[...] * (zs - z)
    r, k7, v7 = s[:, 0:RW_W], s[:, RW_W:2 * RW_W], s[:, 2 * RW_W:3 * RW_W]
    o = 3 * RW_W
    wd, ad, gd = s[:, o:o + LANES], s[:, o + LANES:o + 2 * LANES], s[:, o + 2 * LANES:o + 4 * LANES]
    w_logit = _mm3(jnp.tanh(wd), w2_ref[...]) + w0_ref[...]
    lw = -float(np.exp(-0.5)) * jax.nn.sigmoid(w_logit)
    a = jax.nn.sigmoid(_mm3(ad, a2_ref[...]) + a0_ref[...])
    gate_ref[0] = _mm3(jax.nn.sigmoid(gd), g2_ref[...])
    kx = k7 * kkw_ref[...]
    kk = kx * lax.rsqrt(_head_sums(kx * kx, ones_ref) + 1e-6)
    r_ref[0], v_ref[0], kk_ref[0], lw_ref[0] = r, v7, kk, lw
    kd_sum = None
    for d in range(2):
        a_d = a[:, d * RW_W:(d + 1) * RW_W]
        kd = k7 * (1.0 + (a_d - 1.0) * kaw_ref[...])
        kd_ref[0, :, d * RW_W:(d + 1) * RW_W] = kd
        kka_ref[0, :, d * RW_W:(d + 1) * RW_W] = kk * a_d
        kd_sum = kd if kd_sum is None else kd_sum + kd
    bonus_ref[0] = _head_sums((r * rk_ref[...]) * kd_sum, ones_ref) * v7


def _rw_prep_call(h, mod, gain, w, mu, w2, w0, a2, a0, g2, kkw, kaw, rk, ones, *, ctx_tiles):
    n_batch, t_len, d = h.shape
    nt = t_len // TILE
    out = lambda n: jax.ShapeDtypeStruct((n_batch, t_len, n), F32)
    ospec = lambda n: pl.BlockSpec((1, TILE, n), lambda b, t: (b, t, 0))
    widths = [RW_W, RW_W, RW_W, 2 * RW_W, 2 * RW_W, 2 * RW_W, RW_W, RW_W]
    params = [gain.reshape(1, d), w, mu, w2, w0, a2, a0, g2, kkw, kaw, rk, ones]
    return pl.pallas_call(
        functools.partial(_rw_prep_kernel, ctx_tiles=ctx_tiles, n_tiles=nt),
        out_shape=[out(n) for n in widths],
        grid=(n_batch, nt),
        in_specs=_halo_specs(d, ctx_tiles, nt) + [pl.BlockSpec((1, N_MOD, d), _mod_row_map(n_batch, ctx_tiles, 0))]
        + [_resident(p.shape) for p in params],
        out_specs=[ospec(n) for n in widths],
        scratch_shapes=[pltpu.VMEM((TILE + 2 * HALO, w.shape[1]), F32)],
        compiler_params=_cparams("parallel", "parallel"),
        name="rwkv_prep",
    )(h, h, h, mod, *params)


def _chunk_masks(direction, width=CHUNK):
    i = lax.broadcasted_iota(jnp.int32, (CHUNK, width), 0)
    j = lax.broadcasted_iota(jnp.int32, (CHUNK, width), 1) % CHUNK
    return (i >= j, i > j) if direction == 0 else (i <= j, i < j)


def _last_row(x, direction):
    return x[CHUNK - 1:CHUNK] if direction == 0 else x[0:1]


def _bd(x):
    shape = (2 * CHUNK, x.shape[1])
    r = lax.broadcasted_iota(jnp.int32, shape, 0)
    c = lax.broadcasted_iota(jnp.int32, shape, 1)
    return jnp.where((r < CHUNK) == (c < x.shape[1] // 2), jnp.concatenate([x, x], axis=0), 0.0)


def _neumann_inverse_pairs(ns):
    i = lax.broadcasted_iota(jnp.int32, (CHUNK, 2 * CHUNK), 0)
    j = lax.broadcasted_iota(jnp.int32, (CHUNK, 2 * CHUNK), 1) % CHUNK
    eye = jnp.where(i == j, 1.0, 0.0)
    rs = list(ns)
    lhs = [_split2(n) for n in ns]
    rhs = [_split2(_bd(n)) for n in ns]
    span = 2
    while span < CHUNK:
        mul = _mm3s if span <= PRECISE_SPAN else (lambda a, b: _dot(a[0], b[0]))
        ps = [mul(a, b) for a, b in zip(lhs, rhs)]
        rhs = [_split2(_bd(p)) for p in ps]
        rs = [r + p + mul(_split2(r), b) for r, p, b in zip(rs, ps, rhs)]
        span *= 2
        if span < CHUNK:
            lhs = [_split2(p) for p in ps]
    return [eye + r for r in rs]


def _rev_chunk(n, ctx_chunks, n_chunks):
    return jnp.where(n < ctx_chunks, ctx_chunks - 1 - n, n_chunks - 1 + ctx_chunks - n)


def _dn_chunk_kernel(q_ref, k_ref, v_ref, small_ref, u_ref, w_ref, qd_ref, kdt_ref, attn_ref, gl_ref):
    nh2, n_pairs, pw, dk = 2 * DN_HEADS, DN_HEADS // 2, 2 * DN_DK, DN_DK
    first_c = lax.broadcasted_iota(jnp.int32, (CHUNK, 2 * CHUNK), 1) < CHUNK
    first_f = lax.broadcasted_iota(jnp.int32, (CHUNK, pw), 1) < dk

    def cols(x, c, first):
        return jnp.where(first[:x.shape[0]], x[:, c:c + 1], x[:, c + 1:c + 2])

    work = []
    for cc in range(CHUNKS_PER_STEP):
        rows = slice(cc * CHUNK, (cc + 1) * CHUNK)
        sm = small_ref[0, rows]
        q, k, v = q_ref[0, rows] * (DN_DK ** -0.5), k_ref[0, rows], v_ref[0, rows]
        grams = [_mm_nt(jnp.concatenate([k[:, j * pw:(j + 1) * pw], q[:, j * pw:(j + 1) * pw]], axis=0),
                        _bd(k[:, j * pw:(j + 1) * pw])) for j in range(n_pairs)]
        for d in range(2):
            incl, strict = _chunk_masks(d, 2 * CHUNK)
            gc = _mm_exact_lhs(jnp.where(_chunk_masks(d)[0], 1.0, 0.0), sm)
            gc_t = gc.T
            gtot = _last_row(gc, d)
            gl_ref[0, d, cc] = jnp.exp(gtot)
            for j in range(n_pairs):
                c = DN_HEADS * d + 2 * j
                gcr = jnp.concatenate([gc_t[c:c + 1, :], gc_t[c + 1:c + 2, :]], axis=1)
                decay = jnp.exp(jnp.where(incl, cols(gc, c, first_c) - gcr, -1e30))
                lower = jnp.where(strict, (cols(sm, nh2 + c, first_c) * grams[j][:CHUNK]) * decay, 0.0)
                work.append((cc, rows, d, j, c, sm, q, k, v, gc, gtot, decay, grams[j][CHUNK:], -lower))
    t_invs = _neumann_inverse_pairs([item[-1] for item in work])
    for (cc, rows, d, j, c, sm, q, k, v, gc, gtot, decay, qk, _), t_inv in zip(work, t_invs):
        sl = slice(j * pw, (j + 1) * pw)
        beta, gcc, gt = cols(sm, nh2 + c, first_f), cols(gc, c, first_f), cols(gtot, c, first_f)
        egc = jnp.exp(gcc)
        kp, qp = k[:, sl], q[:, sl]
        vb, ke = v[:, sl] * beta, (kp * beta) * egc
        rhs = jnp.concatenate([vb[:, :dk], ke[:, :dk], vb[:, dk:], ke[:, dk:]], axis=1)
        sol = _mm2(t_inv, _bd(rhs))
        u_ref[0, d, rows, sl] = jnp.concatenate([sol[:, :dk], sol[:, 2 * dk:3 * dk]], axis=1).astype(BF16)
        w_ref[0, d, rows, sl] = jnp.concatenate([sol[:, dk:2 * dk], sol[:, 3 * dk:]], axis=1).astype(BF16)
        qd_ref[0, d, rows, sl] = (qp * egc).astype(BF16)
        k_tail = kp * jnp.exp(gt - gcc)
        kdt_ref[0, d, cc, j * dk:(j + 1) * dk, :] = jnp.concatenate([k_tail[:, :dk].T, k_tail[:, dk:].T], axis=1).astype(BF16)
        attn_ref[0, d, rows, 2 * j * CHUNK:2 * (j + 1) * CHUNK] = (qk * decay).astype(BF16)


def _dn_chunk_call(q, k, v, small):
    n_batch, t_len, wd = q.shape
    nc = t_len // CHUNK
    cps = CHUNKS_PER_STEP
    ispec = lambda n: pl.BlockSpec((1, cps * CHUNK, n), lambda b, c: (b, c, 0))
    ospec = lambda n: pl.BlockSpec((1, 2, cps * CHUNK, n), lambda b, c: (b, 0, c, 0))
    shp = lambda n, dt: jax.ShapeDtypeStruct((n_batch, 2, t_len, n), dt)
    return pl.pallas_call(
        _dn_chunk_kernel,
        out_shape=[shp(wd, BF16), shp(wd, BF16), shp(wd, BF16),
                   jax.ShapeDtypeStruct((n_batch, 2, nc, wd // 2, 2 * CHUNK), BF16), shp(DN_HEADS * CHUNK, BF16),
                   jax.ShapeDtypeStruct((n_batch, 2, nc, 1, LANES), F32)],
        grid=(n_batch, nc // cps),
        in_specs=[ispec(wd), ispec(wd), ispec(wd), ispec(LANES)],
        out_specs=[ospec(wd), ospec(wd), ospec(wd),
                   pl.BlockSpec((1, 2, cps, wd // 2, 2 * CHUNK), lambda b, c: (b, 0, c, 0, 0)), ospec(DN_HEADS * CHUNK),
                   pl.BlockSpec((1, 2, cps, 1, LANES), lambda b, c: (b, 0, c, 0, 0))],
        compiler_params=_cparams("parallel", "parallel"),
        name="deltanet_chunk_prep",
    )(q, k, v, small)


def _dn_scan_kernel(*refs):
    ins, (of_ref, ob_ref, s_ref) = refs[:12], refs[12:]

    @pl.when(pl.program_id(1) == 0)
    def _():
        s_ref[...] = jnp.zeros_like(s_ref)

    dk = DN_DK
    chains = [(b, d, j) for b in range(SCAN_BATCH) for d in range(2) for j in range(DN_HEADS // 2)]
    outs = (of_ref, ob_ref)
    stage1 = []
    for b, d, j in chains:
        u_ref, w_ref, qd_ref = ins[6 * d:6 * d + 3]
        s = s_ref[b, d, j]
        sb = s.astype(BF16)
        halves = [(slice((2 * j + i) * dk, (2 * j + i + 1) * dk), slice(i * dk, (i + 1) * dk)) for i in range(2)]
        v_new = jnp.concatenate([u_ref[b, 0, :, sl] - _dot(w_ref[b, 0, :, sl], sb[:, hl]) for sl, hl in halves], axis=1)
        inter = jnp.concatenate([_dot(qd_ref[b, 0, :, sl], sb[:, hl]) for sl, hl in halves], axis=1)
        stage1.append((s, v_new, inter))
    for (b, d, j), (s, v_new, inter) in zip(chains, stage1):
        kdt_ref, attn_ref, gl_ref = ins[6 * d + 3:6 * d + 6]
        v_bd = _bd(v_new).astype(BF16)
        outs[d][b, :, 2 * j * dk:2 * (j + 1) * dk] = inter + _dot(attn_ref[b, 0, :, 2 * j * CHUNK:2 * (j + 1) * CHUNK], v_bd)
        c = DN_HEADS * d + 2 * j
        decayed = jnp.concatenate([s[:, i * dk:(i + 1) * dk] * gl_ref[b, 0, 0, :, c + i:c + i + 1] for i in range(2)], axis=1)
        s_ref[b, d, j] = decayed + _dot(kdt_ref[b, 0, 0, j * dk:(j + 1) * dk, :], v_bd)


def _dn_scan_call(u, w, qd, kdt, attn, gl, *, ctx_chunks):
    n_batch, _, t_len, wd = u.shape
    nc = t_len // CHUNK
    in_specs, args = [], []
    for d in range(2):
        chunk = (lambda n: n) if d == 0 else functools.partial(_rev_chunk, ctx_chunks=ctx_chunks, n_chunks=nc)
        per_token = lambda a: pl.BlockSpec((SCAN_BATCH, 1, CHUNK, a.shape[-1]),
                                           lambda b, n, d=d, chunk=chunk: (b, d, chunk(n), 0))
        per_chunk = lambda a: pl.BlockSpec((SCAN_BATCH, 1, 1) + a.shape[3:],
                                           lambda b, n, d=d, chunk=chunk: (b, d, chunk(n), 0, 0))
        in_specs += [per_token(u), per_token(w), per_token(qd), per_chunk(kdt), per_token(attn), per_chunk(gl)]
        args += [u, w, qd, kdt, attn, gl]
    return pl.pallas_call(
        _dn_scan_kernel,
        out_shape=[jax.ShapeDtypeStruct((n_batch, t_len, wd), F32)] * 2,
        grid=(n_batch // SCAN_BATCH, nc),
        in_specs=in_specs,
        out_specs=[pl.BlockSpec((SCAN_BATCH, CHUNK, wd), lambda b, n: (b, n, 0)),
                   pl.BlockSpec((SCAN_BATCH, CHUNK, wd), lambda b, n: (b, _rev_chunk(n, ctx_chunks, nc), 0))],
        scratch_shapes=[pltpu.VMEM((SCAN_BATCH, 2, DN_HEADS // 2, DN_DK, 2 * DN_DK), F32)],
        compiler_params=_cparams("parallel", "arbitrary"),
        name="deltanet_scan",
    )(*args)


def _rw_chunk_kernel(r_ref, v_ref, kk_ref, lw_ref, kd_ref, kka_ref,
                     ut_ref, wt_ref, rt_ref, arb_ref, kbt_ref, y0_ref, pct_ref, vb_ref):
    pw = 2 * RW_HS
    work, n_list = [], []
    vb_ref[0] = v_ref[0].astype(BF16)
    for cc in range(CHUNKS_PER_STEP):
        rows = slice(cc * CHUNK, (cc + 1) * CHUNK)
        r, v, kk = r_ref[0, rows], v_ref[0, rows], kk_ref[0, rows]
        for d in range(2):
            dsl = slice(d * RW_W, (d + 1) * RW_W)
            lw, kd, kka = lw_ref[0, rows, dsl], kd_ref[0, rows, dsl], kka_ref[0, rows, dsl]
            incl, strict = _chunk_masks(d, 2 * CHUNK)
            cl = _mm_exact_lhs(jnp.where(_chunk_masks(d)[0], 1.0, 0.0), lw)
            tot = _last_row(cl, d)
            p_inv, p_tail = jnp.exp(-cl), jnp.exp(tot - cl)
            at = -kk * jnp.exp(cl - lw)
            rt = r * jnp.exp(cl)
            kh, bh = kd * p_inv, kka * p_inv
            rt_ref[0, d, rows] = rt.astype(BF16)
            pct_ref[0, d, cc] = jnp.broadcast_to(jnp.exp(tot), (8, RW_W)).T
            kbt_ref[0, d, cc] = jnp.concatenate([(kd * p_tail).T, (kka * p_tail).T], axis=1).astype(BF16)
            for j in range(RW_HEADS // 2):
                sl = slice(j * pw, (j + 1) * pw)
                aa = _mm_nt(jnp.concatenate([at[:, sl], rt[:, sl]], axis=0),
                            jnp.concatenate([_bd(bh[:, sl]), _bd(kh[:, sl])], axis=0))
                n_list.append(jnp.where(strict, aa[:CHUNK, :pw], 0.0))
                work.append((rows, d, sl, at[:, sl], v[:, sl], jnp.where(strict, aa[:CHUNK, pw:], 0.0),
                             jnp.where(incl, aa[CHUNK:, :pw], 0.0), jnp.where(incl, aa[CHUNK:, pw:], 0.0)))
    t_invs = _neumann_inverse_pairs(n_list)
    for (rows, d, sl, at_p, v_p, a_ak, a_rb, a_rk), t_inv in zip(work, t_invs):
        v_bd = _bd(v_p).astype(BF16)
        sol = _mm2(t_inv, jnp.concatenate([_bd(at_p), _bd(_dot(a_ak.astype(BF16), v_bd))], axis=1))
        wt_ref[0, d, rows, sl] = sol[:, :pw].astype(BF16)
        ut_ref[0, d, rows, sl] = sol[:, pw:].astype(BF16)
        arb_ref[0, d, rows, sl] = a_rb.astype(BF16)
        y0_ref[0, d, rows, sl] = _dot(a_rk.astype(BF16), v_bd).astype(BF16)


def _rw_chunk_call(r, v, kk, lw, kd, kka):
    n_batch, t_len, wd = r.shape
    nc = t_len // CHUNK
    cps = CHUNKS_PER_STEP
    ispec = lambda n: pl.BlockSpec((1, cps * CHUNK, n), lambda b, c: (b, c, 0))
    ospec = pl.BlockSpec((1, 2, cps * CHUNK, wd), lambda b, c: (b, 0, c, 0))
    shp = jax.ShapeDtypeStruct((n_batch, 2, t_len, wd), BF16)
    return pl.pallas_call(
        _rw_chunk_kernel,
        out_shape=[shp] * 4 + [jax.ShapeDtypeStruct((n_batch, 2, nc, wd, 2 * CHUNK), BF16), shp,
                               jax.ShapeDtypeStruct((n_batch, 2, nc, wd, 8), F32),
                               jax.ShapeDtypeStruct((n_batch, t_len, wd), BF16)],
        grid=(n_batch, nc // cps),
        in_specs=[ispec(wd), ispec(wd), ispec(wd), ispec(2 * wd), ispec(2 * wd), ispec(2 * wd)],
        out_specs=[ospec] * 4 + [pl.BlockSpec((1, 2, cps, wd, 2 * CHUNK), lambda b, c: (b, 0, c, 0, 0)), ospec,
                                 pl.BlockSpec((1, 2, cps, wd, 8), lambda b, c: (b, 0, c, 0, 0)), ispec(wd)],
        compiler_params=_cparams("parallel", "parallel"),
        name="rwkv_chunk_prep",
    )(r, v, kk, lw, kd, kka)


def _rw_scan_kernel(*refs):
    ins, (yf_ref, yb_ref, s_ref) = refs[:16], refs[16:]

    @pl.when(pl.program_id(1) == 0)
    def _():
        s_ref[...] = jnp.zeros_like(s_ref)

    pw = 2 * RW_HS
    r_i = lax.broadcasted_iota(jnp.int32, (pw, pw), 0)
    c_i = lax.broadcasted_iota(jnp.int32, (pw, pw), 1)
    same_head = (r_i < RW_HS) == (c_i < RW_HS)
    chains = [(b, d, j) for b in range(SCAN_BATCH) for d in range(2) for j in range(RW_HEADS // 2)]
    outs = (yf_ref, yb_ref)
    stage1 = []
    for b, d, j in chains:
        ut_ref, wt_ref, rt_ref, _, _, y0_ref = ins[8 * d:8 * d + 6]
        sl = slice(j * pw, (j + 1) * pw)
        s = s_ref[b, d, j]
        sb = s.astype(BF16)
        u = ut_ref[b, 0, :, sl] + _dot(wt_ref[b, 0, :, sl], sb)
        stage1.append((s, u, y0_ref[b, 0, :, sl] + _dot(rt_ref[b, 0, :, sl], sb)))
    for (b, d, j), (s, u, y_inter) in zip(chains, stage1):
        arb_ref, kbt_ref, _, pct_ref, v_ref = ins[8 * d + 3:8 * d + 8]
        sl = slice(j * pw, (j + 1) * pw)
        outs[d][b, :, sl] = y_inter + _dot(arb_ref[b, 0, :, sl], _bd(u).astype(BF16))
        grow = _dot(kbt_ref[b, 0, 0, sl, :], jnp.concatenate([v_ref[b, :, sl], u.astype(BF16)], axis=0))
        s_ref[b, d, j] = s * pct_ref[b, 0, 0, sl, 0:1] + jnp.where(same_head, grow, 0.0)


def _rw_scan_call(ut, wt, rt, arb, kbt, y0, pct, v, *, ctx_chunks):
    n_batch, _, t_len, wd = ut.shape
    nc = t_len // CHUNK
    in_specs, args = [], []
    for d in range(2):
        chunk = (lambda n: n) if d == 0 else functools.partial(_rev_chunk, ctx_chunks=ctx_chunks, n_chunks=nc)
        per_chunk = lambda a: pl.BlockSpec((SCAN_BATCH, 1, 1) + a.shape[3:],
                                           lambda b, n, d=d, chunk=chunk: (b, d, chunk(n), 0, 0))
        per_token = pl.BlockSpec((SCAN_BATCH, 1, CHUNK, wd), lambda b, n, d=d, chunk=chunk: (b, d, chunk(n), 0))
        in_specs += [per_token] * 4 + [per_chunk(kbt), per_token, per_chunk(pct),
                                       pl.BlockSpec((SCAN_BATCH, CHUNK, wd), lambda b, n, chunk=chunk: (b, chunk(n), 0))]
        args += [ut, wt, rt, arb, kbt, y0, pct, v]
    return pl.pallas_call(
        _rw_scan_kernel,
        out_shape=[jax.ShapeDtypeStruct((n_batch, t_len, wd), F32)] * 2,
        grid=(n_batch // SCAN_BATCH, nc),
        in_specs=in_specs,
        out_specs=[pl.BlockSpec((SCAN_BATCH, CHUNK, wd), lambda b, n: (b, n, 0)),
                   pl.BlockSpec((SCAN_BATCH, CHUNK, wd), lambda b, n: (b, _rev_chunk(n, ctx_chunks, nc), 0))],
        scratch_shapes=[pltpu.VMEM((SCAN_BATCH, 2, RW_HEADS // 2, 2 * RW_HS, 2 * RW_HS), F32)],
        compiler_params=_cparams("parallel", "arbitrary"),
        name="rwkv_scan",
    )(*args)


def _ev_out_kernel(h_ref, mod_ref, of_ref, ob_ref, dgate_ref, yf_ref, yb_ref, gate7_ref, bonus_ref,
                   dnorm_ref, gnw_ref, gnb_ref, ones_ref, wout_ref, o_ref):
    o = of_ref[0] + ob_ref[0]
    dgate = dgate_ref[0]
    parts = []
    for h in range(DN_HEADS):
        sl = slice(h * DN_DK, (h + 1) * DN_DK)
        parts.append(_rms_rows(o[:, sl]) * dnorm_ref[:, sl] * _silu(dgate[:, sl]))
    o_dn = jnp.concatenate(parts, axis=1)
    y = yf_ref[0] + yb_ref[0]
    inv_n = 1.0 / RW_HS
    mu = _head_sums(y, ones_ref) * inv_n
    yc = y - mu
    var = _head_sums(yc * yc, ones_ref) * inv_n
    yn = yc * lax.rsqrt(var + RW_GN_EPS) * gnw_ref[...] + gnb_ref[...]
    o_rw = (yn + bonus_ref[0]) * gate7_ref[0]
    wd = DN_HEADS * DN_DK
    proj = _dot(o_dn.astype(BF16), wout_ref[0:wd, :]) + _dot(o_rw.astype(BF16), wout_ref[wd:, :])
    o_ref[0] = h_ref[0] + mod_ref[0, 5:6] * proj


def _ev_out_call(h, mod, o_f, o_b, dgate, y_f, y_b, gate7, bonus, dnorm, gnw, gnb, ones, w_out, *, ctx_tiles):
    n_batch, t_len, d = h.shape
    nt = t_len // TILE
    tile = lambda n: pl.BlockSpec((1, TILE, n), lambda b, t: (b, t, 0))
    params = [dnorm.reshape(1, -1), gnw.reshape(1, -1), gnb.reshape(1, -1), ones, w_out]
    streams = [o_f, o_b, dgate, y_f, y_b, gate7, bonus]
    return pl.pallas_call(
        _ev_out_kernel,
        out_shape=jax.ShapeDtypeStruct(h.shape, F32),
        grid=(n_batch, nt),
        in_specs=[tile(d), pl.BlockSpec((1, N_MOD, d), _mod_row_map(n_batch, ctx_tiles, 0))]
        + [tile(s.shape[-1]) for s in streams] + [_resident(p.shape) for p in params],
        out_specs=tile(d),
        compiler_params=_cparams("parallel", "parallel"),
        name="even_mix_out",
    )(h, mod, *streams, *params)


def _rope_layout(width, rot, starts):
    angle = np.full((width,), -1, np.int64)
    first = np.zeros((1, width), np.float32)
    q = rot // 4
    for start in starts:
        for blk in range(2):
            for idx in range(q):
                l1 = start + blk * 2 * q + idx
                angle[l1] = angle[l1 + q] = blk * q + idx
                first[0, l1] = 1.0
    return angle, jnp.asarray(first)


def _rope_tables(n_ctx, n_lat, rot, angle, first):
    rows = n_lat // GRID_W
    row = jnp.repeat(jnp.arange(rows), GRID_W).astype(F32)
    col = jnp.tile(jnp.arange(GRID_W), rows).astype(F32)
    axis_dim = rot // 2
    inv = ROPE_THETA ** (-jnp.arange(0, axis_dim, 2, dtype=F32) / axis_dim)
    ang = jnp.concatenate([row[:, None] * inv, col[:, None] * inv], axis=-1)
    on = jnp.asarray(angle >= 0)
    idx = np.maximum(angle, 0)
    cos = jnp.where(on, jnp.cos(ang)[:, idx], 1.0)
    sin = jnp.where(on, jnp.sin(ang)[:, idx], 0.0) * (1.0 - 2.0 * first)
    width = angle.shape[0]
    return (jnp.concatenate([jnp.ones((n_ctx, width), F32), cos], axis=0),
            jnp.concatenate([jnp.zeros((n_ctx, width), F32), sin], axis=0))


def _rope(x, first_ref, cos_ref, sin_ref, quarter):
    width = x.shape[1]
    partner = jnp.where(first_ref[...] > 0.5, pltpu.roll(x, width - quarter, 1), pltpu.roll(x, quarter, 1))
    return x * cos_ref[...] + partner * sin_ref[...]


def _od_prep_kernel(h_ref, mod_ref, gain_ref, w_ref, qn_ref, kn_ref, mqn_ref, wuq_ref, mkvn_ref, wukv_ref,
                    ones_ref, fq_ref, fk_ref, fm_ref, fr_ref,
                    cq_ref, sq_ref, ck_ref, sk_ref, cm_ref, sm_ref, cr_ref, sr_ref,
                    qg_ref, qm_ref, kgt_ref, vg_ref, kmt_ref, vm_ref):
    xn = _modulate(h_ref[0], gain_ref[...], mod_ref[0, 3:4], mod_ref[0, 4:5]).astype(BF16)
    p = _dot(xn, w_ref[...])
    nq, nkv = GQ_HEADS * GQ_HD, GQ_KV_HEADS * GQ_HD
    o = 0
    q, o = p[:, o:o + nq], o + nq
    k, o = p[:, o:o + nkv], o + nkv
    v, o = p[:, o:o + nkv], o + nkv
    n_cq, n_ckv = mqn_ref.shape[1], mkvn_ref.shape[1]
    cq, o = p[:, o:o + n_cq], o + n_cq
    ckv, o = p[:, o:o + n_ckv], o + n_ckv
    kr = p[:, o:o + LANES]
    inv_hd = 1.0 / GQ_HD
    q = q * lax.rsqrt(_head_sums(q * q, ones_ref) * inv_hd + NORM_EPS) * qn_ref[...]
    k = k * lax.rsqrt(_head_sums(k * k, ones_ref) * inv_hd + NORM_EPS) * kn_ref[...]
    qm = _dot((_rms_rows(cq) * mqn_ref[...]).astype(BF16), wuq_ref[...])
    kvm = _dot((_rms_rows(ckv) * mkvn_ref[...]).astype(BF16), wukv_ref[...])
    q = _rope(q, fq_ref, cq_ref, sq_ref, GQ_HD // 4) * (GQ_HD ** -0.5 * LOG2_E)
    k = _rope(k, fk_ref, ck_ref, sk_ref, GQ_HD // 4)
    qm = _rope(qm, fm_ref, cm_ref, sm_ref, ML_ROPE // 4) * ((ML_NOPE + ML_ROPE) ** -0.5 * LOG2_E)
    kr = _rope(kr, fr_ref, cr_ref, sr_ref, ML_ROPE // 4)
    qg_ref[0] = q.astype(BF16)
    qm_ref[0] = qm.astype(BF16)
    vg_ref[0] = v.astype(BF16)
    n_nope = ML_HEADS * ML_NOPE
    vm_ref[0] = kvm[:, n_nope:].astype(BF16)
    kgt_ref[0] = k.T.astype(BF16)
    knt = kvm[:, :n_nope].T.astype(BF16)
    krt = kr.T[:ML_ROPE].astype(BF16)
    dk = ML_NOPE + ML_ROPE
    for h in range(ML_HEADS):
        kmt_ref[0, h * dk:h * dk + ML_NOPE, :] = knt[h * ML_NOPE:(h + 1) * ML_NOPE]
        kmt_ref[0, h * dk + ML_NOPE:(h + 1) * dk, :] = krt


def _od_prep_call(h, mod, gain, params, tables, *, ctx_tiles):
    n_batch, t_len, d = h.shape
    nt = t_len // TILE
    nq, nkv = GQ_HEADS * GQ_HD, GQ_KV_HEADS * GQ_HD
    dk = ML_NOPE + ML_ROPE
    tile = lambda n: pl.BlockSpec((1, TILE, n), lambda b, t: (b, t, 0))
    tile_t = lambda n: pl.BlockSpec((1, n, TILE), lambda b, t: (b, 0, t))
    tab = lambda a: pl.BlockSpec((TILE, a.shape[1]), lambda b, t: (t, 0))
    shp = lambda *s: jax.ShapeDtypeStruct((n_batch,) + s, BF16)
    return pl.pallas_call(
        _od_prep_kernel,
        out_shape=[shp(t_len, nq), shp(t_len, ML_HEADS * dk), shp(nkv, t_len), shp(t_len, nkv),
                   shp(ML_HEADS * dk, t_len), shp(t_len, ML_HEADS * ML_V)],
        grid=(n_batch, nt),
        in_specs=[tile(d), pl.BlockSpec((1, N_MOD, d), _mod_row_map(n_batch, ctx_tiles, 0)), _resident((1, d))]
        + [_resident(p.shape) for p in params] + [tab(a) for a in tables],
        out_specs=[tile(nq), tile(ML_HEADS * dk), tile_t(nkv), tile(nkv), tile_t(ML_HEADS * dk), tile(ML_HEADS * ML_V)],
        compiler_params=_cparams("parallel", "parallel"),
        name="attn_prep",
    )(h, mod, gain.reshape(1, d), *params, *tables)


def _odd_layer_weights(od_w_in, gq_q_norm, gq_k_norm, ml_q_norm, ml_w_uq, ml_kv_norm, ml_w_ukv):
    d = od_w_in.shape[0]
    w = jnp.concatenate([od_w_in, jnp.zeros((d, LANES - ML_ROPE), F32)], axis=1)
    ukv = ml_w_ukv.reshape(ml_w_ukv.shape[0], ML_HEADS, ML_NOPE + ML_V)
    ukv = jnp.concatenate([ukv[:, :, :ML_NOPE].reshape(-1, ML_HEADS * ML_NOPE),
                           ukv[:, :, ML_NOPE:].reshape(-1, ML_HEADS * ML_V)], axis=1)
    dk = ML_NOPE + ML_ROPE
    layouts = [(GQ_HD,) + _rope_layout(GQ_HEADS * GQ_HD, GQ_HD, [h * GQ_HD for h in range(GQ_HEADS)]),
               (GQ_HD,) + _rope_layout(GQ_KV_HEADS * GQ_HD, GQ_HD, [h * GQ_HD for h in range(GQ_KV_HEADS)]),
               (ML_ROPE,) + _rope_layout(ML_HEADS * dk, ML_ROPE, [h * dk + ML_NOPE for h in range(ML_HEADS)]),
               (ML_ROPE,) + _rope_layout(LANES, ML_ROPE, [0])]
    params = [w.astype(BF16), jnp.tile(gq_q_norm, GQ_HEADS).reshape(1, -1), jnp.tile(gq_k_norm, GQ_KV_HEADS).reshape(1, -1),
              ml_q_norm.reshape(1, -1), ml_w_uq.astype(BF16), ml_kv_norm.reshape(1, -1), ukv.astype(BF16),
              _block_ones(LANES, GQ_HD)] + [first for _, _, first in layouts]
    return params, layouts


def _softmax_pv(s, v):
    m = jnp.max(s, axis=-1, keepdims=True)
    p = jnp.exp2(s - m)
    return _dot(p.astype(BF16), v) / jnp.sum(p, axis=-1, keepdims=True)


def _attn_kernel(h_ref, mod_ref, qg_ref, qm_ref, kgt_ref, vg_ref, kmt_ref, vm_ref, wout_ref, o_ref):
    group = GQ_HEADS // GQ_KV_HEADS
    dk = ML_NOPE + ML_ROPE

    def logits(h):
        if h < GQ_HEADS:
            g = h // group
            return _dot(qg_ref[0, :, h * GQ_HD:(h + 1) * GQ_HD], kgt_ref[0, g * GQ_HD:(g + 1) * GQ_HD, :])
        h -= GQ_HEADS
        return _dot(qm_ref[0, :, h * dk:(h + 1) * dk], kmt_ref[0, h * dk:(h + 1) * dk, :])

    def values(h):
        if h < GQ_HEADS:
            return vg_ref[0], (h // group) * GQ_HD
        pair = ((h - GQ_HEADS) * ML_V) // LANES * LANES
        return vm_ref[0, :, pair:pair + LANES], (h - GQ_HEADS) * ML_V - pair

    n_heads = GQ_HEADS + ML_HEADS
    parts, s = [], logits(0)
    for h in range(n_heads):
        s_next = logits(h + 1) if h + 1 < n_heads else None
        v, off = values(h)
        parts.append(_softmax_pv(s, v)[:, off:off + ML_V])
        s = s_next
    ol = jnp.concatenate(parts, axis=1).astype(BF16)
    o_ref[0] = h_ref[0] + mod_ref[0, 5:6] * _dot(ol, wout_ref[...])


def _attn_call(h, mod, qg, qm, kgt, vg, kmt, vm, w_out, *, ctx_tiles):
    n_batch, t_len, d = h.shape
    nt = t_len // TILE - ctx_tiles
    qtile = lambda n: pl.BlockSpec((1, TILE, n), lambda b, t: (b, t + ctx_tiles, 0))
    whole = lambda a: pl.BlockSpec((1,) + a.shape[1:], lambda b, t: (b, 0, 0))
    return pl.pallas_call(
        _attn_kernel,
        out_shape=jax.ShapeDtypeStruct((n_batch, nt * TILE, d), F32),
        grid=(n_batch, nt),
        in_specs=[qtile(d), pl.BlockSpec((1, N_MOD, d), lambda b, t: (b, 0, 0)), qtile(qg.shape[-1]), qtile(qm.shape[-1]),
                  whole(kgt), whole(vg), whole(kmt), whole(vm), _resident(w_out.shape)],
        out_specs=pl.BlockSpec((1, TILE, d), lambda b, t: (b, t, 0)),
        compiler_params=_cparams("parallel", "parallel"),
        name="attention_out",
    )(h, mod, qg, qm, kgt, vg, kmt, vm, w_out)


def _block_ones(n, blk):
    i = np.arange(n) // blk
    return jnp.asarray(i[:, None] == i[None, :], BF16)


def _even_layer_weights(ev_w_in, dn_conv, dn_a_log, dn_dt_bias, rw_mu, rw_w0, rw_w2, rw_a0, rw_a2, rw_g2,
                        rw_kk, rw_ka, rw_rk):
    d = ev_w_in.shape[0]
    n_dn = 4 * DN_HEADS * DN_DK
    nh2 = 2 * DN_HEADS
    slab0 = n_dn + 2 * nh2
    zeros = lambda n: jnp.zeros((d, n), F32)
    w_dn = jnp.concatenate([ev_w_in[:, :n_dn], ev_w_in[:, n_dn:slab0], zeros(LANES - 2 * nh2)], axis=1)
    slab = ev_w_in[:, slab0:]
    o = 3 * RW_W
    lora = 2 * RW_W_LORA
    gpad = 2 * LANES - RW_G_LORA
    w_rw = jnp.concatenate([slab[:, :o + 2 * lora + RW_G_LORA], zeros(gpad)], axis=1)
    mu = jnp.concatenate([rw_mu, jnp.zeros((gpad,), F32)]).reshape(1, -1)
    pad_lanes = lambda v: jnp.zeros((1, LANES), F32).at[0, :v.size].set(v.reshape(-1))

    def dir_blocks(m):
        z = jnp.zeros_like(m[0])
        return jnp.concatenate([jnp.concatenate([m[0], z], axis=1), jnp.concatenate([z, m[1]], axis=1)], axis=0)

    g2 = jnp.concatenate([rw_g2, jnp.zeros((gpad, RW_W), F32)], axis=0)
    return dict(
        w_dn=w_dn.astype(BF16), conv=dn_conv, alog=pad_lanes(dn_a_log), dtb=pad_lanes(dn_dt_bias),
        w_rw=w_rw.astype(BF16), mu=mu, w2=dir_blocks(rw_w2), w0=rw_w0.reshape(1, -1), a2=dir_blocks(rw_a2),
        a0=rw_a0.reshape(1, -1), g2=g2, kkw=rw_kk.reshape(1, -1), kaw=rw_ka.reshape(1, -1),
        rk=rw_rk.reshape(1, -1), ones=_block_ones(LANES, RW_HS))


def kernel(x, c, ctx, c_ctx, mod_w, mod_b, norm_ffn1, norm_mix, norm_ffn2, ffn1_w1, ffn1_w3, ffn1_w2, ffn2_w1, ffn2_w3, ffn2_w2, ev_w_in, ev_w_out, dn_conv, dn_a_log, dn_dt_bias, dn_norm, rw_mu, rw_w0, rw_w2, rw_a0, rw_a2, rw_g2, rw_kk, rw_ka, rw_rk, rw_gn_w, rw_gn_b, od_w_in, od_w_out, gq_q_norm, gq_k_norm, ml_q_norm, ml_w_uq, ml_kv_norm, ml_w_ukv, final_norm):
    n_batch, n_lat, d = x.shape
    n_ctx = ctx.shape[1]
    depth = mod_w.shape[0]
    assert n_ctx % TILE == 0 and n_lat % TILE == 0 and n_lat % GRID_W == 0 and n_batch % SCAN_BATCH == 0
    assert depth % 2 == 0 and depth // 2 == od_w_in.shape[0] == 1, "supported stack: [recurrent, attention]"
    ctx_tiles, ctx_chunks = n_ctx // TILE, n_ctx // CHUNK
    bf = lambda a: a.astype(BF16)

    mod = _all_mod(c, c_ctx, mod_w, mod_b)
    h = (ctx, x)
    for i in range(depth):
        j = i // 2
        last = i == depth - 1
        h = _ffn_call(h, mod[i], norm_ffn1[i], bf(ffn1_w1[i]), bf(ffn1_w3[i]), bf(ffn1_w2[i]), j0=0, ctx_tiles=ctx_tiles)
        if i % 2 == 0:
            w = _even_layer_weights(ev_w_in[j], dn_conv[j], dn_a_log[j], dn_dt_bias[j], rw_mu[j], rw_w0[j], rw_w2[j],
                                    rw_a0[j], rw_a2[j], rw_g2[j], rw_kk[j], rw_ka[j], rw_rk[j])
            q, k, v, small, dgate = _dn_prep_call(h, mod[i], norm_mix[i], w["w_dn"], w["conv"], w["alog"], w["dtb"],
                                                  ctx_tiles=ctx_tiles)
            o_f, o_b = _dn_scan_call(*_dn_chunk_call(q, k, v, small), ctx_chunks=ctx_chunks)
            r, v7, kk, lw, kd, kka, gate7, bonus = _rw_prep_call(
                h, mod[i], norm_mix[i], w["w_rw"], w["mu"], w["w2"], w["w0"], w["a2"], w["a0"], w["g2"], w["kkw"],
                w["kaw"], w["rk"], w["ones"], ctx_tiles=ctx_tiles)
            y_f, y_b = _rw_scan_call(*_rw_chunk_call(r, v7, kk, lw, kd, kka), ctx_chunks=ctx_chunks)
            h = _ev_out_call(h, mod[i], o_f, o_b, dgate, y_f, y_b, gate7, bonus, jnp.tile(dn_norm[j], DN_HEADS),
                             rw_gn_w[j], rw_gn_b[j], w["ones"], bf(ev_w_out[j]), ctx_tiles=ctx_tiles)
            h = _ffn_call(h, mod[i], norm_ffn2[i], bf(ffn2_w1[i]), bf(ffn2_w3[i]), bf(ffn2_w2[i]), j0=6,
                          ctx_tiles=ctx_tiles)
        else:
            params, layouts = _odd_layer_weights(od_w_in[j], gq_q_norm[j], gq_k_norm[j], ml_q_norm[j], ml_w_uq[j],
                                                 ml_kv_norm[j], ml_w_ukv[j])
            tables = [t for rot, angle, first in layouts for t in _rope_tables(n_ctx, n_lat, rot, angle, first)]
            qg, qm, kgt, vg, kmt, vm = _od_prep_call(h, mod[i], norm_mix[i], params, tables, ctx_tiles=ctx_tiles)
            hl = _attn_call(h, mod[i], qg, qm, kgt, vg, kmt, vm, bf(od_w_out[j]), ctx_tiles=ctx_tiles)
            assert last
            h = _ffn_call(hl, mod[i], norm_ffn2[i], bf(ffn2_w1[i]), bf(ffn2_w3[i]), bf(ffn2_w2[i]), j0=6, ctx_tiles=0,
                          final_gain=final_norm)
    return h
```

```python
import functools

import jax
import jax.numpy as jnp
import numpy as np
from jax import lax
from jax.experimental import pallas as pl
from jax.experimental.pallas import tpu as pltpu

F32 = jnp.float32
BF16 = jnp.bfloat16

NORM_EPS = 1e-6
ROPE_THETA = 10000.0
GRID_W = 64
N_MOD = 9

DN_HEADS = 4
DN_DK = 128
DN_CONV = 5
RW_HEADS = 8
RW_HS = 64
RW_W = RW_HEADS * RW_HS
RW_W_LORA = 64
RW_A_LORA = 64
RW_G_LORA = 160
RW_GN_EPS = 64e-5
GQ_HEADS = 8
GQ_KV_HEADS = 2
GQ_HD = 64
ML_HEADS = 8
ML_NOPE = 64
ML_ROPE = 32
ML_V = 64

TILE = 256
CHUNK = 64
HALO = 8
LANES = 128
MXU_DIM = 256
VMEM_LIMIT = 56 * 1024 * 1024
SCAN_BATCH = 4
CHUNKS_PER_STEP = 4
PRECISE_SPAN = 0
LOG2_E = 1.4426950408889634


def _cparams(*sem):
    return pltpu.CompilerParams(dimension_semantics=sem, vmem_limit_bytes=VMEM_LIMIT)


def _resident(shape):
    nd = len(shape)
    return pl.BlockSpec(shape, lambda *_: (0,) * nd, pipeline_mode=pl.Buffered(1))


def _mm(a, b):
    return jnp.dot(a.astype(BF16), b.astype(BF16), preferred_element_type=F32)


def _mm_nt(a, b):
    return lax.dot_general(a.astype(BF16), b.astype(BF16), (((1,), (1,)), ((), ())),
                           preferred_element_type=F32)


def _mm_tn(a, b):
    return lax.dot_general(a.astype(BF16), b.astype(BF16), (((0,), (0,)), ((), ())),
                           preferred_element_type=F32)


def _split2(x):
    hi = x.astype(BF16)
    lo = (x - hi.astype(F32)).astype(BF16)
    return hi, lo


def _split3(x):
    hi = x.astype(BF16)
    r = x - hi.astype(F32)
    mid = r.astype(BF16)
    lo = (r - mid.astype(F32)).astype(BF16)
    return hi, mid, lo


def _dot(a, b):
    return jnp.dot(a, b, preferred_element_type=F32)


def _mm3s(asp, bsp):
    (ah, al), (bh, bl) = asp, bsp
    return _dot(ah, bh) + (_dot(ah, bl) + _dot(al, bh))


def _mm3(a, b):
    return _mm3s(_split2(a), _split2(b))


def _mm2(a, b):
    ah, al = _split2(a)
    bb = b.astype(BF16)
    return _dot(ah, bb) + _dot(al, bb)


def _mm_exact_lhs(a01, b):
    a = a01.astype(BF16)
    hi, mid, lo = _split3(b)
    return _dot(a, hi) + (_dot(a, mid) + _dot(a, lo))


def _rms_rows(x):
    return x * lax.rsqrt(jnp.mean(x * x, axis=-1, keepdims=True) + NORM_EPS)


def _modulate(x, gain, shift, scale):
    return (_rms_rows(x) * gain) * (1.0 + scale) + shift


def _silu(x):
    return x * jax.nn.sigmoid(x)


def _softplus(x):
    return jnp.maximum(x, 0.0) + jnp.log1p(jnp.exp(-jnp.abs(x)))


def _mod_kernel(c_ref, w_ref, b_ref, o_ref):
    s = _silu(c_ref[...])
    o_ref[0] = _mm3(s, w_ref[0]) + b_ref[0]


def _mod_call(cc, mod_w, mod_b):
    n_layers, d, n = mod_w.shape
    r = cc.shape[0]
    tn = n // 8
    return pl.pallas_call(
        _mod_kernel,
        out_shape=jax.ShapeDtypeStruct((n_layers, r, n), F32),
        grid=(n_layers, n // tn),
        in_specs=[pl.BlockSpec((r, d), lambda l, j: (0, 0)),
                  pl.BlockSpec((1, d, tn), lambda l, j: (l, 0, j)),
                  pl.BlockSpec((1, 1, tn), lambda l, j: (l, 0, j))],
        out_specs=pl.BlockSpec((1, r, tn), lambda l, j: (l, 0, j)),
        compiler_params=_cparams("parallel", "parallel"),
        name="adaln_mod",
    )(cc, mod_w, mod_b.reshape(n_layers, 1, n))


def _all_mod(c, c_ctx, mod_w, mod_b):
    n_batch, d = c.shape
    rows = -(-(n_batch + 1) // 8) * 8
    cc = jnp.zeros((rows, d), F32).at[:n_batch].set(c).at[n_batch].set(c_ctx)
    return _mod_call(cc, mod_w, mod_b).reshape(mod_w.shape[0], rows, N_MOD, d)


def _mod_row_map(n_batch, ctx_tiles, t_off):
    def index_map(b, t):
        return (jnp.where(t + t_off < ctx_tiles, n_batch, b), 0, 0)
    return index_map


def _ffn_kernel(*refs, j0, final, split_tiles, sub, ctx_tiles, t_off):
    if split_tiles:
        ctx_ref, lat_ref, *refs = refs
        xs = [jnp.where(pl.program_id(1) < split_tiles, ctx_ref[0], lat_ref[0])]
    else:
        h_ref, *refs = refs
        xs = [h_ref[0, i * TILE:(i + 1) * TILE] for i in range(sub)]
    mod_ctx_ref, mod_ref, gain_ref, w1_ref, w3_ref, w2_ref, *rest = refs
    o_ref = rest[-1]
    f = w1_ref.shape[1]
    cut = -(-(f // MXU_DIM) // 2) * MXU_DIM
    fcs = [slice(0, cut), slice(cut, f)] if 0 < cut < f else [slice(0, f)]
    for i, x in enumerate(xs):
        is_ctx = (pl.program_id(1) * sub + i + t_off) < ctx_tiles
        row = lambda j: jnp.where(is_ctx, mod_ctx_ref[0, j:j + 1], mod_ref[0, j:j + 1])
        xn = _modulate(x, gain_ref[...], row(j0), row(j0 + 1)).astype(BF16)
        ups = [(_dot(xn, w1_ref[:, fc]), _dot(xn, w3_ref[:, fc])) for fc in fcs]
        y = None
        for (a, b), fc in zip(ups, fcs):
            part = _dot((_silu(a) * b).astype(BF16), w2_ref[fc, :])
            y = part if y is None else y + part
        y = x + (0.5 * row(j0 + 2)) * y
        if final:
            y = _rms_rows(y) * rest[0][...]
        o_ref[0, i * TILE:(i + 1) * TILE] = y


def _ffn_call(h, mod, gain, w1, w3, w2, *, j0, ctx_tiles, t_off=0, final_gain=None):
    split = isinstance(h, tuple)
    if split:
        ctx, lat = h
        n_batch, _, d = lat.shape
        t_len = ctx.shape[1] + lat.shape[1]
        last_ctx = ctx_tiles - 1
        sub = 1
        streams = [ctx, lat]
        stream_specs = [pl.BlockSpec((1, TILE, d), lambda b, t: (b, jnp.minimum(t, last_ctx), 0)),
                        pl.BlockSpec((1, TILE, d), lambda b, t: (b, jnp.maximum(t - ctx_tiles, 0), 0))]
    else:
        n_batch, t_len, d = h.shape
        nt_all = t_len // TILE - t_off
        sub = next(s for s in (3, 2, 1) if nt_all % s == 0 and t_off % s == 0)
        streams = [h]
        stream_specs = [pl.BlockSpec((1, sub * TILE, d), lambda b, t: (b, t + t_off // sub, 0))]
    f = w1.shape[1]
    nt = t_len // TILE - t_off
    final = final_gain is not None
    in_specs = stream_specs + [pl.BlockSpec((1, N_MOD, d), lambda b, t: (n_batch, 0, 0)),
                               pl.BlockSpec((1, N_MOD, d), lambda b, t: (b, 0, 0)),
                               _resident((1, d)), _resident((d, f)), _resident((d, f)), _resident((f, d))]
    args = streams + [mod, mod, gain.reshape(1, d), w1, w3, w2]
    if final:
        in_specs.append(_resident((1, d)))
        args.append(final_gain.reshape(1, d))
    return pl.pallas_call(
        functools.partial(_ffn_kernel, j0=j0, final=final, split_tiles=ctx_tiles if split else 0, sub=sub,
                          ctx_tiles=ctx_tiles, t_off=t_off),
        out_shape=jax.ShapeDtypeStruct((n_batch, nt * TILE, d), F32),
        grid=(n_batch, nt // sub),
        in_specs=in_specs,
        out_specs=pl.BlockSpec((1, sub * TILE, d), lambda b, t: (b, t, 0)),
        compiler_params=_cparams("parallel", "parallel"),
        name="macaron_ffn",
    )(*args)


def _halo_specs(d, ctx_tiles, n_tiles):
    per = TILE // HALO
    last = n_tiles * per - 1
    return [pl.BlockSpec((1, HALO, d), lambda b, t: (b, jnp.maximum(t * per - 1, 0), 0)),
            pl.BlockSpec((1, TILE, d), lambda b, t: (b, t, 0)),
            pl.BlockSpec((1, HALO, d), lambda b, t: (b, jnp.minimum((t + 1) * per, last), 0))]


def _project_with_halo(prev_ref, cur_ref, next_ref, mod_ref, gain_ref, w_ref, pe_ref, *, ctx_tiles, n_tiles):
    t = pl.program_id(1)
    xe = jnp.concatenate([prev_ref[0], cur_ref[0], next_ref[0]], axis=0)
    xn = _modulate(xe, gain_ref[...], mod_ref[0, 3:4], mod_ref[0, 4:5]).astype(BF16)
    p = _dot(xn, w_ref[...])
    row = lax.broadcasted_iota(jnp.int32, (TILE + 2 * HALO, 1), 0)
    prev_ok = jnp.logical_and(t > 0, t != ctx_tiles)
    next_ok = jnp.logical_and(t + 1 < n_tiles, t + 1 != ctx_tiles)
    keep = jnp.logical_and(jnp.logical_or(row >= HALO, prev_ok),
                           jnp.logical_or(row < HALO + TILE, next_ok))
    pe_ref[...] = jnp.where(keep, p, 0.0)


def _dn_prep_kernel(prev_ref, cur_ref, next_ref, mod_ref, gain_ref, w_ref, conv_ref, alog_ref, dtb_ref,
                    q_ref, k_ref, v_ref, small_ref, gate_ref, pe_ref, *, ctx_tiles, n_tiles):
    _project_with_halo(prev_ref, cur_ref, next_ref, mod_ref, gain_ref, w_ref, pe_ref,
                       ctx_tiles=ctx_tiles, n_tiles=n_tiles)
    nqkv = 3 * DN_HEADS * DN_DK
    half = DN_CONV // 2
    acc = None
    for j in range(DN_CONV):
        term = conv_ref[j:j + 1, :] * pe_ref[pl.ds(HALO - half + j, TILE), 0:nqkv]
        acc = term if acc is None else acc + term
    qkv = _silu(acc)
    w = DN_HEADS * DN_DK
    for idx, ref in ((0, q_ref), (1, k_ref)):
        for h in range(DN_HEADS):
            seg = qkv[:, idx * w + h * DN_DK: idx * w + (h + 1) * DN_DK]
            ref[0, :, h * DN_DK:(h + 1) * DN_DK] = seg * lax.rsqrt(jnp.sum(seg * seg, axis=-1, keepdims=True) + 1e-6)
    v_ref[0] = qkv[:, 2 * w:3 * w]
    gate_ref[0] = pe_ref[pl.ds(HALO, TILE), nqkv:nqkv + w]
    ab = pe_ref[pl.ds(HALO, TILE), nqkv + w:nqkv + w + LANES]
    g = -jnp.exp(alog_ref[...]) * _softplus(ab + dtb_ref[...])
    lane = lax.broadcasted_iota(jnp.int32, ab.shape, 1)
    nh2 = 2 * DN_HEADS
    small_ref[0] = jnp.where(lane < nh2, g, jnp.where(lane < 2 * nh2, jax.nn.sigmoid(ab), 0.0))


def _dn_prep_call(h, mod, gain, w, conv, alog, dtb, *, ctx_tiles):
    n_batch, t_len, d = h.shape
    nt = t_len // TILE
    wd = DN_HEADS * DN_DK
    out = lambda n, dt=F32: jax.ShapeDtypeStruct((n_batch, t_len, n), dt)
    ospec = lambda n: pl.BlockSpec((1, TILE, n), lambda b, t: (b, t, 0))
    return pl.pallas_call(
        functools.partial(_dn_prep_kernel, ctx_tiles=ctx_tiles, n_tiles=nt),
        out_shape=[out(wd), out(wd), out(wd), out(LANES), out(wd)],
        grid=(n_batch, nt),
        in_specs=_halo_specs(d, ctx_tiles, nt) + [
            pl.BlockSpec((1, N_MOD, d), _mod_row_map(n_batch, ctx_tiles, 0)),
            _resident((1, d)), _resident(w.shape), _resident(conv.shape),
            _resident((1, LANES)), _resident((1, LANES))],
        out_specs=[ospec(wd), ospec(wd), ospec(wd), ospec(LANES), ospec(wd)],
        scratch_shapes=[pltpu.VMEM((TILE + 2 * HALO, w.shape[1]), F32)],
        compiler_params=_cparams("parallel", "parallel"),
        name="deltanet_prep",
    )(h, h, h, mod, gain.reshape(1, d), w, conv, alog, dtb)


def _head_sums(x, ones_ref):
    ones = ones_ref[...]
    out = []
    for g in range(x.shape[1] // LANES):
        hi, mid, lo = _split3(x[:, g * LANES:(g + 1) * LANES])
        out.append(_dot(hi, ones) + (_dot(mid, ones) + _dot(lo, ones)))
    return out[0] if len(out) == 1 else jnp.concatenate(out, axis=1)


def _rw_prep_kernel(prev_ref, cur_ref, next_ref, mod_ref, gain_ref, w_ref, mu_ref, w2_ref, w0_ref, a2_ref,
                    a0_ref, g2_ref, kkw_ref, kaw_ref, rk_ref, ones_ref,
                    r_ref, v_ref, kk_ref, lw_ref, kd_ref, kka_ref, gate_ref, bonus_ref, pe_ref,
                    *, ctx_tiles, n_tiles):
    _project_with_halo(prev_ref, cur_ref, next_ref, mod_ref, gain_ref, w_ref, pe_ref,
                       ctx_tiles=ctx_tiles, n_tiles=n_tiles)
    z = pe_ref[pl.ds(HALO, TILE), :]
    zs = 0.5 * (pe_ref[pl.ds(HALO - 1, TILE), :] + pe_ref[pl.ds(HALO + 1, TILE), :])
    s = z + mu_ref[...] * (zs - z)
    r, k7, v7 = s[:, 0:RW_W], s[:, RW_W:2 * RW_W], s[:, 2 * RW_W:3 * RW_W]
    o = 3 * RW_W
    wd, ad, gd = s[:, o:o + LANES], s[:, o + LANES:o + 2 * LANES], s[:, o + 2 * LANES:o + 4 * LANES]
    w_logit = _mm3(jnp.tanh(wd), w2_ref[...]) + w0_ref[...]
    lw = -float(np.exp(-0.5)) * jax.nn.sigmoid(w_logit)
    a = jax.nn.sigmoid(_mm3(ad, a2_ref[...]) + a0_ref[...])
    gate_ref[0] = _mm3(jax.nn.sigmoid(gd), g2_ref[...])
    kx = k7 * kkw_ref[...]
    kk = kx * lax.rsqrt(_head_sums(kx * kx, ones_ref) + 1e-6)
    r_ref[0], v_ref[0], kk_ref[0], lw_ref[0] = r, v7, kk, lw
    kd_sum = None
    for d in range(2):
        a_d = a[:, d * RW_W:(d + 1) * RW_W]
        kd = k7 * (1.0 + (a_d - 1.0) * kaw_ref[...])
        kd_ref[0, :, d * RW_W:(d + 1) * RW_W] = kd
        kka_ref[0, :, d * RW_W:(d + 1) * RW_W] = kk * a_d
        kd_sum = kd if kd_sum is None else kd_sum + kd
    bonus_ref[0] = _head_sums((r * rk_ref[...]) * kd_sum, ones_ref) * v7


def _rw_prep_call(h, mod, gain, w, mu, w2, w0, a2, a0, g2, kkw, kaw, rk, ones, *, ctx_tiles):
    n_batch, t_len, d = h.shape
    nt = t_len // TILE
    out = lambda n: jax.ShapeDtypeStruct((n_batch, t_len, n), F32)
    ospec = lambda n: pl.BlockSpec((1, TILE, n), lambda b, t: (b, t, 0))
    widths = [RW_W, RW_W, RW_W, 2 * RW_W, 2 * RW_W, 2 * RW_W, RW_W, RW_W]
    params = [gain.reshape(1, d), w, mu, w2, w0, a2, a0, g2, kkw, kaw, rk, ones]
    return pl.pallas_call(
        functools.partial(_rw_prep_kernel, ctx_tiles=ctx_tiles, n_tiles=nt),
        out_shape=[out(n) for n in widths],
        grid=(n_batch, nt),
        in_specs=_halo_specs(d, ctx_tiles, nt) + [pl.BlockSpec((1, N_MOD, d), _mod_row_map(n_batch, ctx_tiles, 0))]
        + [_resident(p.shape) for p in params],
        out_specs=[ospec(n) for n in widths],
        scratch_shapes=[pltpu.VMEM((TILE + 2 * HALO, w.shape[1]), F32)],
        compiler_params=_cparams("parallel", "parallel"),
        name="rwkv_prep",
    )(h, h, h, mod, *params)


def _chunk_masks(direction, width=CHUNK):
    i = lax.broadcasted_iota(jnp.int32, (CHUNK, width), 0)
    j = lax.broadcasted_iota(jnp.int32, (CHUNK, width), 1) % CHUNK
    return (i >= j, i > j) if direction == 0 else (i <= j, i < j)


def _last_row(x, direction):
    return x[CHUNK - 1:CHUNK] if direction == 0 else x[0:1]


def _bd(x):
    shape = (2 * CHUNK, x.shape[1])
    r = lax.broadcasted_iota(jnp.int32, shape, 0)
    c = lax.broadcasted_iota(jnp.int32, shape, 1)
    return jnp.where((r < CHUNK) == (c < x.shape[1] // 2), jnp.concatenate([x, x], axis=0), 0.0)


def _neumann_inverse_pairs(ns):
    i = lax.broadcasted_iota(jnp.int32, (CHUNK, 2 * CHUNK), 0)
    j = lax.broadcasted_iota(jnp.int32, (CHUNK, 2 * CHUNK), 1) % CHUNK
    eye = jnp.where(i == j, 1.0, 0.0)
    rs = list(ns)
    lhs = [_split2(n) for n in ns]
    rhs = [_split2(_bd(n)) for n in ns]
    span = 2
    while span < CHUNK:
        mul = _mm3s if span <= PRECISE_SPAN else (lambda a, b: _dot(a[0], b[0]))
        ps = [mul(a, b) for a, b in zip(lhs, rhs)]
        rhs = [_split2(_bd(p)) for p in ps]
        rs = [r + p + mul(_split2(r), b) for r, p, b in zip(rs, ps, rhs)]
        span *= 2
        if span < CHUNK:
            lhs = [_split2(p) for p in ps]
    return [eye + r for r in rs]


def _rev_chunk(n, ctx_chunks, n_chunks):
    return jnp.where(n < ctx_chunks, ctx_chunks - 1 - n, n_chunks - 1 + ctx_chunks - n)


def _dn_chunk_kernel(q_ref, k_ref, v_ref, small_ref, u_ref, w_ref, qd_ref, kdt_ref, attn_ref, gl_ref):
    nh2, n_pairs, pw, dk = 2 * DN_HEADS, DN_HEADS // 2, 2 * DN_DK, DN_DK
    first_c = lax.broadcasted_iota(jnp.int32, (CHUNK, 2 * CHUNK), 1) < CHUNK
    first_f = lax.broadcasted_iota(jnp.int32, (CHUNK, pw), 1) < dk

    def cols(x, c, first):
        return jnp.where(first[:x.shape[0]], x[:, c:c + 1], x[:, c + 1:c + 2])

    work = []
    for cc in range(CHUNKS_PER_STEP):
        rows = slice(cc * CHUNK, (cc + 1) * CHUNK)
        sm = small_ref[0, rows]
        q, k, v = q_ref[0, rows] * (DN_DK ** -0.5), k_ref[0, rows], v_ref[0, rows]
        grams = [_mm_nt(jnp.concatenate([k[:, j * pw:(j + 1) * pw], q[:, j * pw:(j + 1) * pw]], axis=0),
                        _bd(k[:, j * pw:(j + 1) * pw])) for j in range(n_pairs)]
        for d in range(2):
            incl, strict = _chunk_masks(d, 2 * CHUNK)
            gc = _mm_exact_lhs(jnp.where(_chunk_masks(d)[0], 1.0, 0.0), sm)
            gc_t = gc.T
            gtot = _last_row(gc, d)
            gl_ref[0, d, cc] = jnp.exp(gtot)
            for j in range(n_pairs):
                c = DN_HEADS * d + 2 * j
                gcr = jnp.concatenate([gc_t[c:c + 1, :], gc_t[c + 1:c + 2, :]], axis=1)
                decay = jnp.exp(jnp.where(incl, cols(gc, c, first_c) - gcr, -1e30))
                lower = jnp.where(strict, (cols(sm, nh2 + c, first_c) * grams[j][:CHUNK]) * decay, 0.0)
                work.append((cc, rows, d, j, c, sm, q, k, v, gc, gtot, decay, grams[j][CHUNK:], -lower))
    t_invs = _neumann_inverse_pairs([item[-1] for item in work])
    for (cc, rows, d, j, c, sm, q, k, v, gc, gtot, decay, qk, _), t_inv in zip(work, t_invs):
        sl = slice(j * pw, (j + 1) * pw)
        beta, gcc, gt = cols(sm, nh2 + c, first_f), cols(gc, c, first_f), cols(gtot, c, first_f)
        egc = jnp.exp(gcc)
        kp, qp = k[:, sl], q[:, sl]
        vb, ke = v[:, sl] * beta, (kp * beta) * egc
        rhs = jnp.concatenate([vb[:, :dk], ke[:, :dk], vb[:, dk:], ke[:, dk:]], axis=1)
        sol = _mm(t_inv, _bd(rhs))
        u_ref[0, d, rows, sl] = jnp.concatenate([sol[:, :dk], sol[:, 2 * dk:3 * dk]], axis=1).astype(BF16)
        w_ref[0, d, rows, sl] = jnp.concatenate([sol[:, dk:2 * dk], sol[:, 3 * dk:]], axis=1).astype(BF16)
        qd_ref[0, d, rows, sl] = (qp * egc).astype(BF16)
        k_tail = kp * jnp.exp(gt - gcc)
        kdt_ref[0, d, cc, j * dk:(j + 1) * dk, :] = jnp.concatenate([k_tail[:, :dk].T, k_tail[:, dk:].T], axis=1).astype(BF16)
        attn_ref[0, d, rows, 2 * j * CHUNK:2 * (j + 1) * CHUNK] = (qk * decay).astype(BF16)


def _dn_chunk_call(q, k, v, small):
    n_batch, t_len, wd = q.shape
    nc = t_len // CHUNK
    cps = CHUNKS_PER_STEP
    ispec = lambda n: pl.BlockSpec((1, cps * CHUNK, n), lambda b, c: (b, c, 0))
    ospec = lambda n: pl.BlockSpec((1, 2, cps * CHUNK, n), lambda b, c: (b, 0, c, 0))
    shp = lambda n, dt: jax.ShapeDtypeStruct((n_batch, 2, t_len, n), dt)
    return pl.pallas_call(
        _dn_chunk_kernel,
        out_shape=[shp(wd, BF16), shp(wd, BF16), shp(wd, BF16),
                   jax.ShapeDtypeStruct((n_batch, 2, nc, wd // 2, 2 * CHUNK), BF16), shp(DN_HEADS * CHUNK, BF16),
                   jax.ShapeDtypeStruct((n_batch, 2, nc, 1, LANES), F32)],
        grid=(n_batch, nc // cps),
        in_specs=[ispec(wd), ispec(wd), ispec(wd), ispec(LANES)],
        out_specs=[ospec(wd), ospec(wd), ospec(wd),
                   pl.BlockSpec((1, 2, cps, wd // 2, 2 * CHUNK), lambda b, c: (b, 0, c, 0, 0)), ospec(DN_HEADS * CHUNK),
                   pl.BlockSpec((1, 2, cps, 1, LANES), lambda b, c: (b, 0, c, 0, 0))],
        compiler_params=_cparams("parallel", "parallel"),
        name="deltanet_chunk_prep",
    )(q, k, v, small)


def _dn_scan_kernel(*refs):
    ins, (of_ref, ob_ref, s_ref) = refs[:12], refs[12:]

    @pl.when(pl.program_id(1) == 0)
    def _():
        s_ref[...] = jnp.zeros_like(s_ref)

    dk = DN_DK
    chains = [(b, d, j) for b in range(SCAN_BATCH) for d in range(2) for j in range(DN_HEADS // 2)]
    outs = (of_ref, ob_ref)
    stage1 = []
    for b, d, j in chains:
        u_ref, w_ref, qd_ref = ins[6 * d:6 * d + 3]
        s = s_ref[b, d, j]
        sb = s.astype(BF16)
        halves = [(slice((2 * j + i) * dk, (2 * j + i + 1) * dk), slice(i * dk, (i + 1) * dk)) for i in range(2)]
        v_new = jnp.concatenate([u_ref[b, 0, :, sl] - _dot(w_ref[b, 0, :, sl], sb[:, hl]) for sl, hl in halves], axis=1)
        inter = jnp.concatenate([_dot(qd_ref[b, 0, :, sl], sb[:, hl]) for sl, hl in halves], axis=1)
        stage1.append((s, v_new, inter))
    for (b, d, j), (s, v_new, inter) in zip(chains, stage1):
        kdt_ref, attn_ref, gl_ref = ins[6 * d + 3:6 * d + 6]
        v_bd = _bd(v_new).astype(BF16)
        outs[d][b, :, 2 * j * dk:2 * (j + 1) * dk] = inter + _dot(attn_ref[b, 0, :, 2 * j * CHUNK:2 * (j + 1) * CHUNK], v_bd)
        c = DN_HEADS * d + 2 * j
        decayed = jnp.concatenate([s[:, i * dk:(i + 1) * dk] * gl_ref[b, 0, 0, :, c + i:c + i + 1] for i in range(2)], axis=1)
        s_ref[b, d, j] = decayed + _dot(kdt_ref[b, 0, 0, j * dk:(j + 1) * dk, :], v_bd)


def _dn_scan_call(u, w, qd, kdt, attn, gl, *, ctx_chunks):
    n_batch, _, t_len, wd = u.shape
    nc = t_len // CHUNK
    in_specs, args = [], []
    for d in range(2):
        chunk = (lambda n: n) if d == 0 else functools.partial(_rev_chunk, ctx_chunks=ctx_chunks, n_chunks=nc)
        per_token = lambda a: pl.BlockSpec((SCAN_BATCH, 1, CHUNK, a.shape[-1]),
                                           lambda b, n, d=d, chunk=chunk: (b, d, chunk(n), 0))
        per_chunk = lambda a: pl.BlockSpec((SCAN_BATCH, 1, 1) + a.shape[3:],
                                           lambda b, n, d=d, chunk=chunk: (b, d, chunk(n), 0, 0))
        in_specs += [per_token(u), per_token(w), per_token(qd), per_chunk(kdt), per_token(attn), per_chunk(gl)]
        args += [u, w, qd, kdt, attn, gl]
    return pl.pallas_call(
        _dn_scan_kernel,
        out_shape=[jax.ShapeDtypeStruct((n_batch, t_len, wd), F32)] * 2,
        grid=(n_batch // SCAN_BATCH, nc),
        in_specs=in_specs,
        out_specs=[pl.BlockSpec((SCAN_BATCH, CHUNK, wd), lambda b, n: (b, n, 0)),
                   pl.BlockSpec((SCAN_BATCH, CHUNK, wd), lambda b, n: (b, _rev_chunk(n, ctx_chunks, nc), 0))],
        scratch_shapes=[pltpu.VMEM((SCAN_BATCH, 2, DN_HEADS // 2, DN_DK, 2 * DN_DK), F32)],
        compiler_params=_cparams("parallel", "arbitrary"),
        name="deltanet_scan",
    )(*args)


def _rw_chunk_kernel(r_ref, v_ref, kk_ref, lw_ref, kd_ref, kka_ref,
                     ut_ref, wt_ref, rt_ref, arb_ref, kbt_ref, y0_ref, pc_ref, vb_ref):
    pw = 2 * RW_HS
    work, n_list = [], []
    vb_ref[0] = v_ref[0].astype(BF16)
    for cc in range(CHUNKS_PER_STEP):
        rows = slice(cc * CHUNK, (cc + 1) * CHUNK)
        r, v, kk = r_ref[0, rows], v_ref[0, rows], kk_ref[0, rows]
        for d in range(2):
            dsl = slice(d * RW_W, (d + 1) * RW_W)
            lw, kd, kka = lw_ref[0, rows, dsl], kd_ref[0, rows, dsl], kka_ref[0, rows, dsl]
            incl, strict = _chunk_masks(d, 2 * CHUNK)
            cl = _mm_exact_lhs(jnp.where(_chunk_masks(d)[0], 1.0, 0.0), lw)
            tot = _last_row(cl, d)
            p_inv, p_tail = jnp.exp(-cl), jnp.exp(tot - cl)
            at = -kk * jnp.exp(cl - lw)
            rt = r * jnp.exp(cl)
            kh, bh = kd * p_inv, kka * p_inv
            rt_ref[0, d, rows] = rt.astype(BF16)
            pc_ref[0, d, cc] = jnp.broadcast_to(jnp.exp(tot), (8, RW_W))
            kbt_ref[0, d, cc] = jnp.concatenate([(kd * p_tail).T, (kka * p_tail).T], axis=1).astype(BF16)
            for j in range(RW_HEADS // 2):
                sl = slice(j * pw, (j + 1) * pw)
                aa = _mm_nt(jnp.concatenate([at[:, sl], rt[:, sl]], axis=0),
                            jnp.concatenate([_bd(bh[:, sl]), _bd(kh[:, sl])], axis=0))
                n_list.append(jnp.where(strict, aa[:CHUNK, :pw], 0.0))
                work.append((rows, d, sl, at[:, sl], v[:, sl], jnp.where(strict, aa[:CHUNK, pw:], 0.0),
                             jnp.where(incl, aa[CHUNK:, :pw], 0.0), jnp.where(incl, aa[CHUNK:, pw:], 0.0)))
    t_invs = _neumann_inverse_pairs(n_list)
    for (rows, d, sl, at_p, v_p, a_ak, a_rb, a_rk), t_inv in zip(work, t_invs):
        v_bd = _bd(v_p).astype(BF16)
        sol = _mm2(t_inv, jnp.concatenate([_bd(at_p), _bd(_dot(a_ak.astype(BF16), v_bd))], axis=1))
        wt_ref[0, d, rows, sl] = sol[:, :pw].astype(BF16)
        ut_ref[0, d, rows, sl] = sol[:, pw:].astype(BF16)
        arb_ref[0, d, rows, sl] = a_rb.astype(BF16)
        y0_ref[0, d, rows, sl] = _dot(a_rk.astype(BF16), v_bd).astype(BF16)


def _rw_chunk_call(r, v, kk, lw, kd, kka):
    n_batch, t_len, wd = r.shape
    nc = t_len // CHUNK
    cps = CHUNKS_PER_STEP
    ispec = lambda n: pl.BlockSpec((1, cps * CHUNK, n), lambda b, c: (b, c, 0))
    ospec = pl.BlockSpec((1, 2, cps * CHUNK, wd), lambda b, c: (b, 0, c, 0))
    shp = jax.ShapeDtypeStruct((n_batch, 2, t_len, wd), BF16)
    return pl.pallas_call(
        _rw_chunk_kernel,
        out_shape=[shp] * 4 + [jax.ShapeDtypeStruct((n_batch, 2, nc, wd, 2 * CHUNK), BF16), shp,
                               jax.ShapeDtypeStruct((n_batch, 2, nc, 8, wd), F32),
                               jax.ShapeDtypeStruct((n_batch, t_len, wd), BF16)],
        grid=(n_batch, nc // cps),
        in_specs=[ispec(wd), ispec(wd), ispec(wd), ispec(2 * wd), ispec(2 * wd), ispec(2 * wd)],
        out_specs=[ospec] * 4 + [pl.BlockSpec((1, 2, cps, wd, 2 * CHUNK), lambda b, c: (b, 0, c, 0, 0)), ospec,
                                 pl.BlockSpec((1, 2, cps, 8, wd), lambda b, c: (b, 0, c, 0, 0)), ispec(wd)],
        compiler_params=_cparams("parallel", "parallel"),
        name="rwkv_chunk_prep",
    )(r, v, kk, lw, kd, kka)


def _rw_scan_kernel(*refs):
    ins, (yf_ref, yb_ref, s_ref) = refs[:16], refs[16:]

    @pl.when(pl.program_id(1) == 0)
    def _():
        s_ref[...] = jnp.zeros_like(s_ref)

    pw = 2 * RW_HS
    r_i = lax.broadcasted_iota(jnp.int32, (pw, pw), 0)
    c_i = lax.broadcasted_iota(jnp.int32, (pw, pw), 1)
    same_head = (r_i < RW_HS) == (c_i < RW_HS)
    chains = [(b, d, j) for b in range(SCAN_BATCH) for d in range(2) for j in range(RW_HEADS // 2)]
    outs = (yf_ref, yb_ref)
    decay_cols = {(b, d): ins[8 * d + 6][b, 0, 0].T for b in range(SCAN_BATCH) for d in range(2)}
    stage1 = []
    for b, d, j in chains:
        ut_ref, wt_ref, rt_ref, _, _, y0_ref = ins[8 * d:8 * d + 6]
        sl = slice(j * pw, (j + 1) * pw)
        s = s_ref[b, d, j]
        sb = s.astype(BF16)
        u = ut_ref[b, 0, :, sl] + _dot(wt_ref[b, 0, :, sl], sb)
        stage1.append((s, u, y0_ref[b, 0, :, sl] + _dot(rt_ref[b, 0, :, sl], sb)))
    for (b, d, j), (s, u, y_inter) in zip(chains, stage1):
        arb_ref, kbt_ref, _, _, v_ref = ins[8 * d + 3:8 * d + 8]
        sl = slice(j * pw, (j + 1) * pw)
        outs[d][b, :, sl] = y_inter + _dot(arb_ref[b, 0, :, sl], _bd(u).astype(BF16))
        grow = _dot(kbt_ref[b, 0, 0, sl, :], jnp.concatenate([v_ref[b, :, sl], u.astype(BF16)], axis=0))
        s_ref[b, d, j] = s * decay_cols[b, d][sl, 0:1] + jnp.where(same_head, grow, 0.0)


def _rw_scan_call(ut, wt, rt, arb, kbt, y0, pc, v, *, ctx_chunks):
    n_batch, _, t_len, wd = ut.shape
    nc = t_len // CHUNK
    in_specs, args = [], []
    for d in range(2):
        chunk = (lambda n: n) if d == 0 else functools.partial(_rev_chunk, ctx_chunks=ctx_chunks, n_chunks=nc)
        per_chunk = lambda a: pl.BlockSpec((SCAN_BATCH, 1, 1) + a.shape[3:],
                                           lambda b, n, d=d, chunk=chunk: (b, d, chunk(n), 0, 0))
        per_token = pl.BlockSpec((SCAN_BATCH, 1, CHUNK, wd), lambda b, n, d=d, chunk=chunk: (b, d, chunk(n), 0))
        in_specs += [per_token] * 4 + [per_chunk(kbt), per_token, per_chunk(pc),
                                       pl.BlockSpec((SCAN_BATCH, CHUNK, wd), lambda b, n, chunk=chunk: (b, chunk(n), 0))]
        args += [ut, wt, rt, arb, kbt, y0, pc, v]
    return pl.pallas_call(
        _rw_scan_kernel,
        out_shape=[jax.ShapeDtypeStruct((n_batch, t_len, wd), F32)] * 2,
        grid=(n_batch // SCAN_BATCH, nc),
        in_specs=in_specs,
        out_specs=[pl.BlockSpec((SCAN_BATCH, CHUNK, wd), lambda b, n: (b, n, 0)),
                   pl.BlockSpec((SCAN_BATCH, CHUNK, wd), lambda b, n: (b, _rev_chunk(n, ctx_chunks, nc), 0))],
        scratch_shapes=[pltpu.VMEM((SCAN_BATCH, 2, RW_HEADS // 2, 2 * RW_HS, 2 * RW_HS), F32)],
        compiler_params=_cparams("parallel", "arbitrary"),
        name="rwkv_scan",
    )(*args)


def _ev_out_kernel(h_ref, mod_ref, of_ref, ob_ref, dgate_ref, yf_ref, yb_ref, gate7_ref, bonus_ref,
                   dnorm_ref, gnw_ref, gnb_ref, ones_ref, wout_ref, o_ref):
    o = of_ref[0] + ob_ref[0]
    dgate = dgate_ref[0]
    parts = []
    for h in range(DN_HEADS):
        sl = slice(h * DN_DK, (h + 1) * DN_DK)
        parts.append(_rms_rows(o[:, sl]) * dnorm_ref[:, sl] * _silu(dgate[:, sl]))
    o_dn = jnp.concatenate(parts, axis=1)
    y = yf_ref[0] + yb_ref[0]
    inv_n = 1.0 / RW_HS
    mu = _head_sums(y, ones_ref) * inv_n
    yc = y - mu
    var = _head_sums(yc * yc, ones_ref) * inv_n
    yn = yc * lax.rsqrt(var + RW_GN_EPS) * gnw_ref[...] + gnb_ref[...]
    o_rw = (yn + bonus_ref[0]) * gate7_ref[0]
    wd = DN_HEADS * DN_DK
    proj = _dot(o_dn.astype(BF16), wout_ref[0:wd, :]) + _dot(o_rw.astype(BF16), wout_ref[wd:, :])
    o_ref[0] = h_ref[0] + mod_ref[0, 5:6] * proj


def _ev_out_call(h, mod, o_f, o_b, dgate, y_f, y_b, gate7, bonus, dnorm, gnw, gnb, ones, w_out, *, ctx_tiles):
    n_batch, t_len, d = h.shape
    nt = t_len // TILE
    tile = lambda n: pl.BlockSpec((1, TILE, n), lambda b, t: (b, t, 0))
    params = [dnorm.reshape(1, -1), gnw.reshape(1, -1), gnb.reshape(1, -1), ones, w_out]
    streams = [o_f, o_b, dgate, y_f, y_b, gate7, bonus]
    return pl.pallas_call(
        _ev_out_kernel,
        out_shape=jax.ShapeDtypeStruct(h.shape, F32),
        grid=(n_batch, nt),
        in_specs=[tile(d), pl.BlockSpec((1, N_MOD, d), _mod_row_map(n_batch, ctx_tiles, 0))]
        + [tile(s.shape[-1]) for s in streams] + [_resident(p.shape) for p in params],
        out_specs=tile(d),
        compiler_params=_cparams("parallel", "parallel"),
        name="even_mix_out",
    )(h, mod, *streams, *params)


def _rope_layout(width, rot, starts):
    angle = np.full((width,), -1, np.int64)
    first = np.zeros((1, width), np.float32)
    q = rot // 4
    for start in starts:
        for blk in range(2):
            for idx in range(q):
                l1 = start + blk * 2 * q + idx
                angle[l1] = angle[l1 + q] = blk * q + idx
                first[0, l1] = 1.0
    return angle, jnp.asarray(first)


def _rope_tables(n_ctx, n_lat, rot, angle, first):
    rows = n_lat // GRID_W
    row = jnp.repeat(jnp.arange(rows), GRID_W).astype(F32)
    col = jnp.tile(jnp.arange(GRID_W), rows).astype(F32)
    axis_dim = rot // 2
    inv = ROPE_THETA ** (-jnp.arange(0, axis_dim, 2, dtype=F32) / axis_dim)
    ang = jnp.concatenate([row[:, None] * inv, col[:, None] * inv], axis=-1)
    on = jnp.asarray(angle >= 0)
    idx = np.maximum(angle, 0)
    cos = jnp.where(on, jnp.cos(ang)[:, idx], 1.0)
    sin = jnp.where(on, jnp.sin(ang)[:, idx], 0.0) * (1.0 - 2.0 * first)
    width = angle.shape[0]
    return (jnp.concatenate([jnp.ones((n_ctx, width), F32), cos], axis=0),
            jnp.concatenate([jnp.zeros((n_ctx, width), F32), sin], axis=0))


def _rope(x, first_ref, cos_ref, sin_ref, quarter):
    width = x.shape[1]
    partner = jnp.where(first_ref[...] > 0.5, pltpu.roll(x, width - quarter, 1), pltpu.roll(x, quarter, 1))
    return x * cos_ref[...] + partner * sin_ref[...]


def _od_prep_kernel(h_ref, mod_ref, gain_ref, w_ref, qn_ref, kn_ref, mqn_ref, wuq_ref, mkvn_ref, wukv_ref,
                    ones_ref, fq_ref, fk_ref, fm_ref, fr_ref,
                    cq_ref, sq_ref, ck_ref, sk_ref, cm_ref, sm_ref, cr_ref, sr_ref,
                    qg_ref, qm_ref, kgt_ref, vg_ref, kmt_ref, vm_ref):
    xn = _modulate(h_ref[0], gain_ref[...], mod_ref[0, 3:4], mod_ref[0, 4:5]).astype(BF16)
    p = _dot(xn, w_ref[...])
    nq, nkv = GQ_HEADS * GQ_HD, GQ_KV_HEADS * GQ_HD
    o = 0
    q, o = p[:, o:o + nq], o + nq
    k, o = p[:, o:o + nkv], o + nkv
    v, o = p[:, o:o + 2 * nkv], o + 2 * nkv
    n_cq, n_ckv = mqn_ref.shape[1], mkvn_ref.shape[1]
    cq, o = p[:, o:o + n_cq], o + n_cq
    ckv, o = p[:, o:o + n_ckv], o + n_ckv
    kr = p[:, o:o + LANES]
    inv_hd = 1.0 / GQ_HD
    q = q * lax.rsqrt(_head_sums(q * q, ones_ref) * inv_hd + NORM_EPS) * qn_ref[...]
    k = k * lax.rsqrt(_head_sums(k * k, ones_ref) * inv_hd + NORM_EPS) * kn_ref[...]
    qm = _dot((_rms_rows(cq) * mqn_ref[...]).astype(BF16), wuq_ref[...])
    kvm = _dot((_rms_rows(ckv) * mkvn_ref[...]).astype(BF16), wukv_ref[...])
    q = _rope(q, fq_ref, cq_ref, sq_ref, GQ_HD // 4) * (GQ_HD ** -0.5 * LOG2_E)
    k = _rope(k, fk_ref, ck_ref, sk_ref, GQ_HD // 4)
    qm = _rope(qm, fm_ref, cm_ref, sm_ref, ML_ROPE // 4) * ((ML_NOPE + ML_ROPE) ** -0.5 * LOG2_E)
    kr = _rope(kr, fr_ref, cr_ref, sr_ref, ML_ROPE // 4)
    qg_ref[0] = q.astype(BF16)
    qm_ref[0] = qm.astype(BF16)
    n_nope = ML_HEADS * ML_NOPE
    for ref, val in ((vg_ref, v), (vm_ref, kvm[:, n_nope:])):
        upper = lax.broadcasted_iota(jnp.int32, (1, val.shape[1]), 1) % LANES >= LANES // 2
        ref[0] = (val + jnp.where(upper, 1.0, 0.0)).astype(BF16)
    kgt_ref[0] = k.T.astype(BF16)
    knt = kvm[:, :n_nope].T.astype(BF16)
    krt = kr.T[:ML_ROPE].astype(BF16)
    dk = ML_NOPE + ML_ROPE
    for h in range(ML_HEADS):
        kmt_ref[0, h * dk:h * dk + ML_NOPE, :] = knt[h * ML_NOPE:(h + 1) * ML_NOPE]
        kmt_ref[0, h * dk + ML_NOPE:(h + 1) * dk, :] = krt


def _od_prep_call(h, mod, gain, params, tables, *, ctx_tiles):
    n_batch, t_len, d = h.shape
    nt = t_len // TILE
    nq, nkv = GQ_HEADS * GQ_HD, GQ_KV_HEADS * GQ_HD
    dk = ML_NOPE + ML_ROPE
    tile = lambda n: pl.BlockSpec((1, TILE, n), lambda b, t: (b, t, 0))
    tile_t = lambda n: pl.BlockSpec((1, n, TILE), lambda b, t: (b, 0, t))
    tab = lambda a: pl.BlockSpec((TILE, a.shape[1]), lambda b, t: (t, 0))
    shp = lambda *s: jax.ShapeDtypeStruct((n_batch,) + s, BF16)
    return pl.pallas_call(
        _od_prep_kernel,
        out_shape=[shp(t_len, nq), shp(t_len, ML_HEADS * dk), shp(nkv, t_len), shp(t_len, GQ_KV_HEADS * LANES),
                   shp(ML_HEADS * dk, t_len), shp(t_len, ML_HEADS * LANES)],
        grid=(n_batch, nt),
        in_specs=[tile(d), pl.BlockSpec((1, N_MOD, d), _mod_row_map(n_batch, ctx_tiles, 0)), _resident((1, d))]
        + [_resident(p.shape) for p in params] + [tab(a) for a in tables],
        out_specs=[tile(nq), tile(ML_HEADS * dk), tile_t(nkv), tile(GQ_KV_HEADS * LANES), tile_t(ML_HEADS * dk),
                   tile(ML_HEADS * LANES)],
        compiler_params=_cparams("parallel", "parallel"),
        name="attn_prep",
    )(h, mod, gain.reshape(1, d), *params, *tables)


def _odd_layer_weights(od_w_in, gq_q_norm, gq_k_norm, ml_q_norm, ml_w_uq, ml_kv_norm, ml_w_ukv):
    d = od_w_in.shape[0]
    nq, nkv = GQ_HEADS * GQ_HD, GQ_KV_HEADS * GQ_HD

    def slabs(cols, heads, width):
        cols = cols.reshape(cols.shape[0], heads, width)
        return jnp.concatenate([cols, jnp.zeros(cols.shape[:2] + (LANES - width,), F32)], axis=2).reshape(cols.shape[0], -1)

    w = jnp.concatenate([od_w_in[:, :nq + nkv], slabs(od_w_in[:, nq + nkv:nq + 2 * nkv], GQ_KV_HEADS, GQ_HD),
                         od_w_in[:, nq + 2 * nkv:], jnp.zeros((d, LANES - ML_ROPE), F32)], axis=1)
    ukv = ml_w_ukv.reshape(ml_w_ukv.shape[0], ML_HEADS, ML_NOPE + ML_V)
    ukv = jnp.concatenate([ukv[:, :, :ML_NOPE].reshape(-1, ML_HEADS * ML_NOPE),
                           slabs(ukv[:, :, ML_NOPE:].reshape(-1, ML_HEADS * ML_V), ML_HEADS, ML_V)], axis=1)
    dk = ML_NOPE + ML_ROPE
    layouts = [(GQ_HD,) + _rope_layout(GQ_HEADS * GQ_HD, GQ_HD, [h * GQ_HD for h in range(GQ_HEADS)]),
               (GQ_HD,) + _rope_layout(GQ_KV_HEADS * GQ_HD, GQ_HD, [h * GQ_HD for h in range(GQ_KV_HEADS)]),
               (ML_ROPE,) + _rope_layout(ML_HEADS * dk, ML_ROPE, [h * dk + ML_NOPE for h in range(ML_HEADS)]),
               (ML_ROPE,) + _rope_layout(LANES, ML_ROPE, [0])]
    params = [w.astype(BF16), jnp.tile(gq_q_norm, GQ_HEADS).reshape(1, -1), jnp.tile(gq_k_norm, GQ_KV_HEADS).reshape(1, -1),
              ml_q_norm.reshape(1, -1), ml_w_uq.astype(BF16), ml_kv_norm.reshape(1, -1), ukv.astype(BF16),
              _block_ones(LANES, GQ_HD)] + [first for _, _, first in layouts]
    return params, layouts


def _softmax_pv(s, v_slab, width):
    m = jnp.max(s, axis=-1, keepdims=True)
    p = jnp.exp2((s - m).astype(BF16))
    pv = _dot(p, v_slab)
    return pv[:, :width] / pv[:, width:width + 1]


def _attn_kernel(h_ref, mod_ref, qg_ref, qm_ref, kgt_ref, vg_ref, kmt_ref, vm_ref, wout_ref, o_ref):
    group = GQ_HEADS // GQ_KV_HEADS
    dk = ML_NOPE + ML_ROPE

    def logits(h):
        if h < GQ_HEADS:
            g = h // group
            return _dot(qg_ref[0, :, h * GQ_HD:(h + 1) * GQ_HD], kgt_ref[0, g * GQ_HD:(g + 1) * GQ_HD, :])
        h -= GQ_HEADS
        return _dot(qm_ref[0, :, h * dk:(h + 1) * dk], kmt_ref[0, h * dk:(h + 1) * dk, :])

    def values(h):
        if h < GQ_HEADS:
            g = h // group
            return vg_ref[0, :, g * LANES:(g + 1) * LANES], GQ_HD
        h -= GQ_HEADS
        return vm_ref[0, :, h * LANES:(h + 1) * LANES], ML_V

    n_heads = GQ_HEADS + ML_HEADS
    parts, s = [], logits(0)
    for h in range(n_heads):
        s_next = logits(h + 1) if h + 1 < n_heads else None
        parts.append(_softmax_pv(s, *values(h)))
        s = s_next
    ol = jnp.concatenate(parts, axis=1).astype(BF16)
    o_ref[0] = h_ref[0] + mod_ref[0, 5:6] * _dot(ol, wout_ref[...])


def _attn_call(h, mod, qg, qm, kgt, vg, kmt, vm, w_out, *, ctx_tiles):
    n_batch, t_len, d = h.shape
    nt = t_len // TILE - ctx_tiles
    qtile = lambda n: pl.BlockSpec((1, TILE, n), lambda b, t: (b, t + ctx_tiles, 0))
    whole = lambda a: pl.BlockSpec((1,) + a.shape[1:], lambda b, t: (b, 0, 0))
    return pl.pallas_call(
        _attn_kernel,
        out_shape=jax.ShapeDtypeStruct((n_batch, nt * TILE, d), F32),
        grid=(n_batch, nt),
        in_specs=[qtile(d), pl.BlockSpec((1, N_MOD, d), lambda b, t: (b, 0, 0)), qtile(qg.shape[-1]), qtile(qm.shape[-1]),
                  whole(kgt), whole(vg), whole(kmt), whole(vm), _resident(w_out.shape)],
        out_specs=pl.BlockSpec((1, TILE, d), lambda b, t: (b, t, 0)),
        compiler_params=_cparams("parallel", "parallel"),
        name="attention_out",
    )(h, mod, qg, qm, kgt, vg, kmt, vm, w_out)


def _block_ones(n, blk):
    i = np.arange(n) // blk
    return jnp.asarray(i[:, None] == i[None, :], BF16)


def _even_layer_weights(ev_w_in, dn_conv, dn_a_log, dn_dt_bias, rw_mu, rw_w0, rw_w2, rw_a0, rw_a2, rw_g2,
                        rw_kk, rw_ka, rw_rk):
    d = ev_w_in.shape[0]
    n_dn = 4 * DN_HEADS * DN_DK
    nh2 = 2 * DN_HEADS
    slab0 = n_dn + 2 * nh2
    zeros = lambda n: jnp.zeros((d, n), F32)
    w_dn = jnp.concatenate([ev_w_in[:, :n_dn], ev_w_in[:, n_dn:slab0], zeros(LANES - 2 * nh2)], axis=1)
    slab = ev_w_in[:, slab0:]
    o = 3 * RW_W
    lora = 2 * RW_W_LORA
    gpad = 2 * LANES - RW_G_LORA
    w_rw = jnp.concatenate([slab[:, :o + 2 * lora + RW_G_LORA], zeros(gpad)], axis=1)
    mu = jnp.concatenate([rw_mu, jnp.zeros((gpad,), F32)]).reshape(1, -1)
    pad_lanes = lambda v: jnp.zeros((1, LANES), F32).at[0, :v.size].set(v.reshape(-1))

    def dir_blocks(m):
        z = jnp.zeros_like(m[0])
        return jnp.concatenate([jnp.concatenate([m[0], z], axis=1), jnp.concatenate([z, m[1]], axis=1)], axis=0)

    g2 = jnp.concatenate([rw_g2, jnp.zeros((gpad, RW_W), F32)], axis=0)
    return dict(
        w_dn=w_dn.astype(BF16), conv=dn_conv, alog=pad_lanes(dn_a_log), dtb=pad_lanes(dn_dt_bias),
        w_rw=w_rw.astype(BF16), mu=mu, w2=dir_blocks(rw_w2), w0=rw_w0.reshape(1, -1), a2=dir_blocks(rw_a2),
        a0=rw_a0.reshape(1, -1), g2=g2, kkw=rw_kk.reshape(1, -1), kaw=rw_ka.reshape(1, -1),
        rk=rw_rk.reshape(1, -1), ones=_block_ones(LANES, RW_HS))


def kernel(x, c, ctx, c_ctx, mod_w, mod_b, norm_ffn1, norm_mix, norm_ffn2, ffn1_w1, ffn1_w3, ffn1_w2, ffn2_w1, ffn2_w3, ffn2_w2, ev_w_in, ev_w_out, dn_conv, dn_a_log, dn_dt_bias, dn_norm, rw_mu, rw_w0, rw_w2, rw_a0, rw_a2, rw_g2, rw_kk, rw_ka, rw_rk, rw_gn_w, rw_gn_b, od_w_in, od_w_out, gq_q_norm, gq_k_norm, ml_q_norm, ml_w_uq, ml_kv_norm, ml_w_ukv, final_norm):
    n_batch, n_lat, d = x.shape
    n_ctx = ctx.shape[1]
    depth = mod_w.shape[0]
    assert n_ctx % TILE == 0 and n_lat % TILE == 0 and n_lat % GRID_W == 0 and n_batch % SCAN_BATCH == 0
    assert depth % 2 == 0 and depth // 2 == od_w_in.shape[0] == 1, "supported stack: [recurrent, attention]"
    ctx_tiles, ctx_chunks = n_ctx // TILE, n_ctx // CHUNK
    bf = lambda a: a.astype(BF16)

    mod = _all_mod(c, c_ctx, mod_w, mod_b)
    h = (ctx, x)
    for i in range(depth):
        j = i // 2
        last = i == depth - 1
        h = _ffn_call(h, mod[i], norm_ffn1[i], bf(ffn1_w1[i]), bf(ffn1_w3[i]), bf(ffn1_w2[i]), j0=0, ctx_tiles=ctx_tiles)
        if i % 2 == 0:
            w = _even_layer_weights(ev_w_in[j], dn_conv[j], dn_a_log[j], dn_dt_bias[j], rw_mu[j], rw_w0[j], rw_w2[j],
                                    rw_a0[j], rw_a2[j], rw_g2[j], rw_kk[j], rw_ka[j], rw_rk[j])
            q, k, v, small, dgate = _dn_prep_call(h, mod[i], norm_mix[i], w["w_dn"], w["conv"], w["alog"], w["dtb"],
                                                  ctx_tiles=ctx_tiles)
            o_f, o_b = _dn_scan_call(*_dn_chunk_call(q, k, v, small), ctx_chunks=ctx_chunks)
            r, v7, kk, lw, kd, kka, gate7, bonus = _rw_prep_call(
                h, mod[i], norm_mix[i], w["w_rw"], w["mu"], w["w2"], w["w0"], w["a2"], w["a0"], w["g2"], w["kkw"],
                w["kaw"], w["rk"], w["ones"], ctx_tiles=ctx_tiles)
            y_f, y_b = _rw_scan_call(*_rw_chunk_call(r, v7, kk, lw, kd, kka), ctx_chunks=ctx_chunks)
            h = _ev_out_call(h, mod[i], o_f, o_b, dgate, y_f, y_b, gate7, bonus, jnp.tile(dn_norm[j], DN_HEADS),
                             rw_gn_w[j], rw_gn_b[j], w["ones"], bf(ev_w_out[j]), ctx_tiles=ctx_tiles)
            h = _ffn_call(h, mod[i], norm_ffn2[i], bf(ffn2_w1[i]), bf(ffn2_w3[i]), bf(ffn2_w2[i]), j0=6,
                          ctx_tiles=ctx_tiles)
        else:
            params, layouts = _odd_layer_weights(od_w_in[j], gq_q_norm[j], gq_k_norm[j], ml_q_norm[j], ml_w_uq[j],
                                                 ml_kv_norm[j], ml_w_ukv[j])
            tables = [t for rot, angle, first in layouts for t in _rope_tables(n_ctx, n_lat, rot, angle, first)]
            qg, qm, kgt, vg, kmt, vm = _od_prep_call(h, mod[i], norm_mix[i], params, tables, ctx_tiles=ctx_tiles)
            hl = _attn_call(h, mod[i], qg, qm, kgt, vg, kmt, vm, bf(od_w_out[j]), ctx_tiles=ctx_tiles)
            assert last
            h = _ffn_call(hl, mod[i], norm_ffn2[i], bf(ffn2_w1[i]), bf(ffn2_w3[i]), bf(ffn2_w2[i]), j0=6, ctx_tiles=0,
                          final_gain=final_norm)
    return h
```

```python
import functools

import jax
import jax.numpy as jnp
import numpy as np
from jax import lax
from jax.experimental import pallas as pl
from jax.experimental.pallas import tpu as pltpu

F32 = jnp.float32
BF16 = jnp.bfloat16

NORM_EPS = 1e-6
ROPE_THETA = 10000.0
GRID_W = 64
N_MOD = 9

DN_HEADS = 4
DN_DK = 128
DN_CONV = 5
RW_HEADS = 8
RW_HS = 64
RW_W = RW_HEADS * RW_HS
RW_W_LORA = 64
RW_A_LORA = 64
RW_G_LORA = 160
RW_GN_EPS = 64e-5
GQ_HEADS = 8
GQ_KV_HEADS = 2
GQ_HD = 64
ML_HEADS = 8
ML_NOPE = 64
ML_ROPE = 32
ML_V = 64

TILE = 256
CHUNK = 64
HALO = 8
LANES = 128
MXU_DIM = 256
VMEM_LIMIT = 56 * 1024 * 1024
SCAN_BATCH = 4
CHUNKS_PER_STEP = 4
PRECISE_SPAN = 0
LOG2_E = 1.4426950408889634


def _cparams(*sem):
    return pltpu.CompilerParams(dimension_semantics=sem, vmem_limit_bytes=VMEM_LIMIT)


def _resident(shape):
    nd = len(shape)
    return pl.BlockSpec(shape, lambda *_: (0,) * nd, pipeline_mode=pl.Buffered(1))


def _mm(a, b):
    return jnp.dot(a.astype(BF16), b.astype(BF16), preferred_element_type=F32)


def _mm_nt(a, b):
    return lax.dot_general(a.astype(BF16), b.astype(BF16), (((1,), (1,)), ((), ())),
                           preferred_element_type=F32)


def _mm_tn(a, b):
    return lax.dot_general(a.astype(BF16), b.astype(BF16), (((0,), (0,)), ((), ())),
                           preferred_element_type=F32)


def _split2(x):
    hi = x.astype(BF16)
    lo = (x - hi.astype(F32)).astype(BF16)
    return hi, lo


def _split3(x):
    hi = x.astype(BF16)
    r = x - hi.astype(F32)
    mid = r.astype(BF16)
    lo = (r - mid.astype(F32)).astype(BF16)
    return hi, mid, lo


def _dot(a, b):
    return jnp.dot(a, b, preferred_element_type=F32)


def _mm3s(asp, bsp):
    (ah, al), (bh, bl) = asp, bsp
    return _dot(ah, bh) + (_dot(ah, bl) + _dot(al, bh))


def _mm3(a, b):
    return _mm3s(_split2(a), _split2(b))


def _mm2(a, b):
    ah, al = _split2(a)
    bb = b.astype(BF16)
    return _dot(ah, bb) + _dot(al, bb)


def _mm_exact_lhs(a01, b):
    a = a01.astype(BF16)
    hi, mid, lo = _split3(b)
    return _dot(a, hi) + (_dot(a, mid) + _dot(a, lo))


def _rms_rows(x):
    return x * lax.rsqrt(jnp.mean(x * x, axis=-1, keepdims=True) + NORM_EPS)


def _modulate(x, gain, shift, scale):
    return (_rms_rows(x) * gain) * (1.0 + scale) + shift


def _silu(x):
    return x * jax.nn.sigmoid(x)


def _softplus(x):
    return jnp.maximum(x, 0.0) + jnp.log1p(jnp.exp(-jnp.abs(x)))


def _mod_kernel(c_ref, w_ref, b_ref, o_ref):
    s = _silu(c_ref[...])
    o_ref[0] = _mm3(s, w_ref[0]) + b_ref[0]


def _mod_call(cc, mod_w, mod_b):
    n_layers, d, n = mod_w.shape
    r = cc.shape[0]
    tn = n // 8
    return pl.pallas_call(
        _mod_kernel,
        out_shape=jax.ShapeDtypeStruct((n_layers, r, n), F32),
        grid=(n_layers, n // tn),
        in_specs=[pl.BlockSpec((r, d), lambda l, j: (0, 0)),
                  pl.BlockSpec((1, d, tn), lambda l, j: (l, 0, j)),
                  pl.BlockSpec((1, 1, tn), lambda l, j: (l, 0, j))],
        out_specs=pl.BlockSpec((1, r, tn), lambda l, j: (l, 0, j)),
        compiler_params=_cparams("parallel", "parallel"),
        name="adaln_mod",
    )(cc, mod_w, mod_b.reshape(n_layers, 1, n))


def _all_mod(c, c_ctx, mod_w, mod_b):
    n_batch, d = c.shape
    rows = -(-(n_batch + 1) // 8) * 8
    cc = jnp.zeros((rows, d), F32).at[:n_batch].set(c).at[n_batch].set(c_ctx)
    return _mod_call(cc, mod_w, mod_b).reshape(mod_w.shape[0], rows, N_MOD, d)


def _mod_row_map(n_batch, ctx_tiles, t_off):
    def index_map(b, t):
        return (jnp.where(t + t_off < ctx_tiles, n_batch, b), 0, 0)
    return index_map


def _ffn_kernel(*refs, j0, final, split_tiles, sub, ctx_tiles, t_off):
    if split_tiles:
        ctx_ref, lat_ref, *refs = refs
        xs = [jnp.where(pl.program_id(1) < split_tiles, ctx_ref[0], lat_ref[0])]
    else:
        h_ref, *refs = refs
        xs = [h_ref[0, i * TILE:(i + 1) * TILE] for i in range(sub)]
    mod_ctx_ref, mod_ref, gain_ref, w1_ref, w3_ref, w2_ref, *rest = refs
    o_ref = rest[-1]
    f = w1_ref.shape[1]
    cut = -(-(f // MXU_DIM) // 2) * MXU_DIM
    fcs = [slice(0, cut), slice(cut, f)] if 0 < cut < f else [slice(0, f)]
    for i, x in enumerate(xs):
        is_ctx = (pl.program_id(1) * sub + i + t_off) < ctx_tiles
        row = lambda j: jnp.where(is_ctx, mod_ctx_ref[0, j:j + 1], mod_ref[0, j:j + 1])
        xn = _modulate(x, gain_ref[...], row(j0), row(j0 + 1)).astype(BF16)
        ups = [(_dot(xn, w1_ref[:, fc]), _dot(xn, w3_ref[:, fc])) for fc in fcs]
        y = None
        for (a, b), fc in zip(ups, fcs):
            part = _dot((_silu(a) * b).astype(BF16), w2_ref[fc, :])
            y = part if y is None else y + part
        y = x + (0.5 * row(j0 + 2)) * y
        if final:
            y = _rms_rows(y) * rest[0][...]
        o_ref[0, i * TILE:(i + 1) * TILE] = y


def _ffn_call(h, mod, gain, w1, w3, w2, *, j0, ctx_tiles, t_off=0, final_gain=None):
    split = isinstance(h, tuple)
    if split:
        ctx, lat = h
        n_batch, _, d = lat.shape
        t_len = ctx.shape[1] + lat.shape[1]
        last_ctx = ctx_tiles - 1
        sub = 1
        streams = [ctx, lat]
        stream_specs = [pl.BlockSpec((1, TILE, d), lambda b, t: (b, jnp.minimum(t, last_ctx), 0)),
                        pl.BlockSpec((1, TILE, d), lambda b, t: (b, jnp.maximum(t - ctx_tiles, 0), 0))]
    else:
        n_batch, t_len, d = h.shape
        nt_all = t_len // TILE - t_off
        sub = next(s for s in (3, 2, 1) if nt_all % s == 0 and t_off % s == 0)
        streams = [h]
        stream_specs = [pl.BlockSpec((1, sub * TILE, d), lambda b, t: (b, t + t_off // sub, 0))]
    f = w1.shape[1]
    nt = t_len // TILE - t_off
    final = final_gain is not None
    in_specs = stream_specs + [pl.BlockSpec((1, N_MOD, d), lambda b, t: (n_batch, 0, 0)),
                               pl.BlockSpec((1, N_MOD, d), lambda b, t: (b, 0, 0)),
                               _resident((1, d)), _resident((d, f)), _resident((d, f)), _resident((f, d))]
    args = streams + [mod, mod, gain.reshape(1, d), w1, w3, w2]
    if final:
        in_specs.append(_resident((1, d)))
        args.append(final_gain.reshape(1, d))
    return pl.pallas_call(
        functools.partial(_ffn_kernel, j0=j0, final=final, split_tiles=ctx_tiles if split else 0, sub=sub,
                          ctx_tiles=ctx_tiles, t_off=t_off),
        out_shape=jax.ShapeDtypeStruct((n_batch, nt * TILE, d), F32),
        grid=(n_batch, nt // sub),
        in_specs=in_specs,
        out_specs=pl.BlockSpec((1, sub * TILE, d), lambda b, t: (b, t, 0)),
        compiler_params=_cparams("parallel", "parallel"),
        name="macaron_ffn",
    )(*args)


def _halo_specs(d, ctx_tiles, n_tiles):
    per = TILE // HALO
    last = n_tiles * per - 1
    return [pl.BlockSpec((1, HALO, d), lambda b, t: (b, jnp.maximum(t * per - 1, 0), 0)),
            pl.BlockSpec((1, TILE, d), lambda b, t: (b, t, 0)),
            pl.BlockSpec((1, HALO, d), lambda b, t: (b, jnp.minimum((t + 1) * per, last), 0))]


def _project_with_halo(prev_ref, cur_ref, next_ref, mod_ref, gain_ref, w_ref, pe_ref, *, ctx_tiles, n_tiles):
    t = pl.program_id(1)
    xe = jnp.concatenate([prev_ref[0], cur_ref[0], next_ref[0]], axis=0)
    xn = _modulate(xe, gain_ref[...], mod_ref[0, 3:4], mod_ref[0, 4:5]).astype(BF16)
    p = _dot(xn, w_ref[...])
    row = lax.broadcasted_iota(jnp.int32, (TILE + 2 * HALO, 1), 0)
    prev_ok = jnp.logical_and(t > 0, t != ctx_tiles)
    next_ok = jnp.logical_and(t + 1 < n_tiles, t + 1 != ctx_tiles)
    keep = jnp.logical_and(jnp.logical_or(row >= HALO, prev_ok),
                           jnp.logical_or(row < HALO + TILE, next_ok))
    pe_ref[...] = jnp.where(keep, p, 0.0)


def _dn_prep_kernel(prev_ref, cur_ref, next_ref, mod_ref, gain_ref, w_ref, conv_ref, alog_ref, dtb_ref,
                    q_ref, k_ref, v_ref, small_ref, gate_ref, pe_ref, *, ctx_tiles, n_tiles):
    _project_with_halo(prev_ref, cur_ref, next_ref, mod_ref, gain_ref, w_ref, pe_ref,
                       ctx_tiles=ctx_tiles, n_tiles=n_tiles)
    nqkv = 3 * DN_HEADS * DN_DK
    half = DN_CONV // 2
    acc = None
    for j in range(DN_CONV):
        term = conv_ref[j:j + 1, :] * pe_ref[pl.ds(HALO - half + j, TILE), 0:nqkv]
        acc = term if acc is None else acc + term
    qkv = _silu(acc)
    w = DN_HEADS * DN_DK
    for idx, ref in ((0, q_ref), (1, k_ref)):
        for h in range(DN_HEADS):
            seg = qkv[:, idx * w + h * DN_DK: idx * w + (h + 1) * DN_DK]
            ref[0, :, h * DN_DK:(h + 1) * DN_DK] = seg * lax.rsqrt(jnp.sum(seg * seg, axis=-1, keepdims=True) + 1e-6)
    v_ref[0] = qkv[:, 2 * w:3 * w]
    gate_ref[0] = pe_ref[pl.ds(HALO, TILE), nqkv:nqkv + w].astype(BF16)
    ab = pe_ref[pl.ds(HALO, TILE), nqkv + w:nqkv + w + LANES]
    g = -jnp.exp(alog_ref[...]) * _softplus(ab + dtb_ref[...])
    lane = lax.broadcasted_iota(jnp.int32, ab.shape, 1)
    nh2 = 2 * DN_HEADS
    small_ref[0] = jnp.where(lane < nh2, g, jnp.where(lane < 2 * nh2, jax.nn.sigmoid(ab), 0.0))


def _dn_prep_call(h, mod, gain, w, conv, alog, dtb, *, ctx_tiles):
    n_batch, t_len, d = h.shape
    nt = t_len // TILE
    wd = DN_HEADS * DN_DK
    out = lambda n, dt=F32: jax.ShapeDtypeStruct((n_batch, t_len, n), dt)
    ospec = lambda n: pl.BlockSpec((1, TILE, n), lambda b, t: (b, t, 0))
    return pl.pallas_call(
        functools.partial(_dn_prep_kernel, ctx_tiles=ctx_tiles, n_tiles=nt),
        out_shape=[out(wd), out(wd), out(wd), out(LANES), out(wd, BF16)],
        grid=(n_batch, nt),
        in_specs=_halo_specs(d, ctx_tiles, nt) + [
            pl.BlockSpec((1, N_MOD, d), _mod_row_map(n_batch, ctx_tiles, 0)),
            _resident((1, d)), _resident(w.shape), _resident(conv.shape),
            _resident((1, LANES)), _resident((1, LANES))],
        out_specs=[ospec(wd), ospec(wd), ospec(wd), ospec(LANES), ospec(wd)],
        scratch_shapes=[pltpu.VMEM((TILE + 2 * HALO, w.shape[1]), F32)],
        compiler_params=_cparams("parallel", "parallel"),
        name="deltanet_prep",
    )(h, h, h, mod, gain.reshape(1, d), w, conv, alog, dtb)


def _head_sums(x, ones_ref):
    ones = ones_ref[...]
    out = []
    for g in range(x.shape[1] // LANES):
        hi, mid, lo = _split3(x[:, g * LANES:(g + 1) * LANES])
        out.append(_dot(hi, ones) + (_dot(mid, ones) + _dot(lo, ones)))
    return out[0] if len(out) == 1 else jnp.concatenate(out, axis=1)


def _rw_prep_kernel(prev_ref, cur_ref, next_ref, mod_ref, gain_ref, w_ref, mu_ref, w2_ref, w0_ref, a2_ref,
                    a0_ref, g2_ref, kkw_ref, kaw_ref, rk_ref, ones_ref,
                    r_ref, v_ref, kk_ref, lw_ref, kd_ref, kka_ref, gate_ref, bonus_ref, pe_ref,
                    *, ctx_tiles, n_tiles):
    _project_with_halo(prev_ref, cur_ref, next_ref, mod_ref, gain_ref, w_ref, pe_ref,
                       ctx_tiles=ctx_tiles, n_tiles=n_tiles)
    z = pe_ref[pl.ds(HALO, TILE), :]
    zs = 0.5 * (pe_ref[pl.ds(HALO - 1, TILE), :] + pe_ref[pl.ds(HALO + 1, TILE), :])
    s = z + mu_ref[...] * (zs - z)
    r, k7, v7 = s[:, 0:RW_W], s[:, RW_W:2 * RW_W], s[:, 2 * RW_W:3 * RW_W]
    o = 3 * RW_W
    wd, ad, gd = s[:, o:o + LANES], s[:, o + LANES:o + 2 * LANES], s[:, o + 2 * LANES:o + 4 * LANES]
    w_logit = _mm3(jnp.tanh(wd), w2_ref[...]) + w0_ref[...]
    lw = -float(np.exp(-0.5)) * jax.nn.sigmoid(w_logit)
    a = jax.nn.sigmoid(_mm3(ad, a2_ref[...]) + a0_ref[...])
    gate_ref[0] = _mm3(jax.nn.sigmoid(gd), g2_ref[...]).astype(BF16)
    kx = k7 * kkw_ref[...]
    kk = kx * lax.rsqrt(_head_sums(kx * kx, ones_ref) + 1e-6)
    r_ref[0], v_ref[0], kk_ref[0], lw_ref[0] = r, v7, kk, lw
    kd_sum = None
    for d in range(2):
        a_d = a[:, d * RW_W:(d + 1) * RW_W]
        kd = k7 * (1.0 + (a_d - 1.0) * kaw_ref[...])
        kd_ref[0, :, d * RW_W:(d + 1) * RW_W] = kd
        kka_ref[0, :, d * RW_W:(d + 1) * RW_W] = kk * a_d
        kd_sum = kd if kd_sum is None else kd_sum + kd
    bonus_ref[0] = (_head_sums((r * rk_ref[...]) * kd_sum, ones_ref) * v7).astype(BF16)


def _rw_prep_call(h, mod, gain, w, mu, w2, w0, a2, a0, g2, kkw, kaw, rk, ones, *, ctx_tiles):
    n_batch, t_len, d = h.shape
    nt = t_len // TILE
    out = lambda n, dt: jax.ShapeDtypeStruct((n_batch, t_len, n), dt)
    ospec = lambda n: pl.BlockSpec((1, TILE, n), lambda b, t: (b, t, 0))
    widths = [RW_W, RW_W, RW_W, 2 * RW_W, 2 * RW_W, 2 * RW_W, RW_W, RW_W]
    dtypes = [F32] * 6 + [BF16] * 2
    params = [gain.reshape(1, d), w, mu, w2, w0, a2, a0, g2, kkw, kaw, rk, ones]
    return pl.pallas_call(
        functools.partial(_rw_prep_kernel, ctx_tiles=ctx_tiles, n_tiles=nt),
        out_shape=[out(n, dt) for n, dt in zip(widths, dtypes)],
        grid=(n_batch, nt),
        in_specs=_halo_specs(d, ctx_tiles, nt) + [pl.BlockSpec((1, N_MOD, d), _mod_row_map(n_batch, ctx_tiles, 0))]
        + [_resident(p.shape) for p in params],
        out_specs=[ospec(n) for n in widths],
        scratch_shapes=[pltpu.VMEM((TILE + 2 * HALO, w.shape[1]), F32)],
        compiler_params=_cparams("parallel", "parallel"),
        name="rwkv_prep",
    )(h, h, h, mod, *params)


def _chunk_masks(direction, width=CHUNK):
    i = lax.broadcasted_iota(jnp.int32, (CHUNK, width), 0)
    j = lax.broadcasted_iota(jnp.int32, (CHUNK, width), 1) % CHUNK
    return (i >= j, i > j) if direction == 0 else (i <= j, i < j)


def _last_row(x, direction):
    return x[CHUNK - 1:CHUNK] if direction == 0 else x[0:1]


def _bd(x):
    shape = (2 * CHUNK, x.shape[1])
    r = lax.broadcasted_iota(jnp.int32, shape, 0)
    c = lax.broadcasted_iota(jnp.int32, shape, 1)
    return jnp.where((r < CHUNK) == (c < x.shape[1] // 2), jnp.concatenate([x, x], axis=0), 0.0)


def _neumann_inverse_pairs(ns):
    i = lax.broadcasted_iota(jnp.int32, (CHUNK, 2 * CHUNK), 0)
    j = lax.broadcasted_iota(jnp.int32, (CHUNK, 2 * CHUNK), 1) % CHUNK
    eye = jnp.where(i == j, 1.0, 0.0)
    rs = list(ns)
    lhs = [_split2(n) for n in ns]
    rhs = [_split2(_bd(n)) for n in ns]
    span = 2
    while span < CHUNK:
        mul = _mm3s if span <= PRECISE_SPAN else (lambda a, b: _dot(a[0], b[0]))
        ps = [mul(a, b) for a, b in zip(lhs, rhs)]
        rhs = [_split2(_bd(p)) for p in ps]
        rs = [r + p + mul(_split2(r), b) for r, p, b in zip(rs, ps, rhs)]
        span *= 2
        if span < CHUNK:
            lhs = [_split2(p) for p in ps]
    return [eye + r for r in rs]


def _rev_chunk(n, ctx_chunks, n_chunks):
    return jnp.where(n < ctx_chunks, ctx_chunks - 1 - n, n_chunks - 1 + ctx_chunks - n)


def _dn_chunk_kernel(q_ref, k_ref, v_ref, small_ref, u_ref, w_ref, qd_ref, kdt_ref, attn_ref, gl_ref):
    nh2, n_pairs, pw, dk = 2 * DN_HEADS, DN_HEADS // 2, 2 * DN_DK, DN_DK
    first_c = lax.broadcasted_iota(jnp.int32, (CHUNK, 2 * CHUNK), 1) < CHUNK
    first_f = lax.broadcasted_iota(jnp.int32, (CHUNK, pw), 1) < dk

    def cols(x, c, first):
        return jnp.where(first[:x.shape[0]], x[:, c:c + 1], x[:, c + 1:c + 2])

    work = []
    for cc in range(CHUNKS_PER_STEP):
        rows = slice(cc * CHUNK, (cc + 1) * CHUNK)
        sm = small_ref[0, rows]
        q, k, v = q_ref[0, rows] * (DN_DK ** -0.5), k_ref[0, rows], v_ref[0, rows]
        grams = [_mm_nt(jnp.concatenate([k[:, j * pw:(j + 1) * pw], q[:, j * pw:(j + 1) * pw]], axis=0),
                        _bd(k[:, j * pw:(j + 1) * pw])) for j in range(n_pairs)]
        for d in range(2):
            incl, strict = _chunk_masks(d, 2 * CHUNK)
            gc = _mm_exact_lhs(jnp.where(_chunk_masks(d)[0], 1.0, 0.0), sm)
            gc_t = gc.T
            gtot = _last_row(gc, d)
            gl_ref[0, d, cc] = jnp.exp(gtot)
            for j in range(n_pairs):
                c = DN_HEADS * d + 2 * j
                gcr = jnp.concatenate([gc_t[c:c + 1, :], gc_t[c + 1:c + 2, :]], axis=1)
                decay = jnp.exp(jnp.where(incl, cols(gc, c, first_c) - gcr, -1e30))
                lower = jnp.where(strict, (cols(sm, nh2 + c, first_c) * grams[j][:CHUNK]) * decay, 0.0)
                work.append((cc, rows, d, j, c, sm, q, k, v, gc, gtot, decay, grams[j][CHUNK:], -lower))
    t_invs = _neumann_inverse_pairs([item[-1] for item in work])
    for (cc, rows, d, j, c, sm, q, k, v, gc, gtot, decay, qk, _), t_inv in zip(work, t_invs):
        sl = slice(j * pw, (j + 1) * pw)
        beta, gcc, gt = cols(sm, nh2 + c, first_f), cols(gc, c, first_f), cols(gtot, c, first_f)
        egc = jnp.exp(gcc)
        kp, qp = k[:, sl], q[:, sl]
        vb, ke = v[:, sl] * beta, (kp * beta) * egc
        rhs = jnp.concatenate([vb[:, :dk], ke[:, :dk], vb[:, dk:], ke[:, dk:]], axis=1)
        sol = _mm(t_inv, _bd(rhs))
        u_ref[0, d, rows, sl] = jnp.concatenate([sol[:, :dk], sol[:, 2 * dk:3 * dk]], axis=1).astype(BF16)
        w_ref[0, d, rows, sl] = jnp.concatenate([sol[:, dk:2 * dk], sol[:, 3 * dk:]], axis=1).astype(BF16)
        qd_ref[0, d, rows, sl] = (qp * egc).astype(BF16)
        k_tail = kp * jnp.exp(gt - gcc)
        kdt_ref[0, d, cc, j * dk:(j + 1) * dk, :] = jnp.concatenate([k_tail[:, :dk].T, k_tail[:, dk:].T], axis=1).astype(BF16)
        attn_ref[0, d, rows, 2 * j * CHUNK:2 * (j + 1) * CHUNK] = (qk * decay).astype(BF16)


def _dn_chunk_call(q, k, v, small):
    n_batch, t_len, wd = q.shape
    nc = t_len // CHUNK
    cps = CHUNKS_PER_STEP
    ispec = lambda n: pl.BlockSpec((1, cps * CHUNK, n), lambda b, c: (b, c, 0))
    ospec = lambda n: pl.BlockSpec((1, 2, cps * CHUNK, n), lambda b, c: (b, 0, c, 0))
    shp = lambda n, dt: jax.ShapeDtypeStruct((n_batch, 2, t_len, n), dt)
    return pl.pallas_call(
        _dn_chunk_kernel,
        out_shape=[shp(wd, BF16), shp(wd, BF16), shp(wd, BF16),
                   jax.ShapeDtypeStruct((n_batch, 2, nc, wd // 2, 2 * CHUNK), BF16), shp(DN_HEADS * CHUNK, BF16),
                   jax.ShapeDtypeStruct((n_batch, 2, nc, 1, LANES), F32)],
        grid=(n_batch, nc // cps),
        in_specs=[ispec(wd), ispec(wd), ispec(wd), ispec(LANES)],
        out_specs=[ospec(wd), ospec(wd), ospec(wd),
                   pl.BlockSpec((1, 2, cps, wd // 2, 2 * CHUNK), lambda b, c: (b, 0, c, 0, 0)), ospec(DN_HEADS * CHUNK),
                   pl.BlockSpec((1, 2, cps, 1, LANES), lambda b, c: (b, 0, c, 0, 0))],
        compiler_params=_cparams("parallel", "parallel"),
        name="deltanet_chunk_prep",
    )(q, k, v, small)


def _dn_scan_kernel(*refs):
    ins, (of_ref, ob_ref, s_ref) = refs[:12], refs[12:]

    @pl.when(pl.program_id(1) == 0)
    def _():
        s_ref[...] = jnp.zeros_like(s_ref)

    dk = DN_DK
    chains = [(b, d, j) for b in range(SCAN_BATCH) for d in range(2) for j in range(DN_HEADS // 2)]
    outs = (of_ref, ob_ref)
    stage1 = []
    for b, d, j in chains:
        u_ref, w_ref, qd_ref = ins[6 * d:6 * d + 3]
        s = s_ref[b, d, j]
        sb = s.astype(BF16)
        halves = [(slice((2 * j + i) * dk, (2 * j + i + 1) * dk), slice(i * dk, (i + 1) * dk)) for i in range(2)]
        v_new = jnp.concatenate([u_ref[b, 0, :, sl] - _dot(w_ref[b, 0, :, sl], sb[:, hl]) for sl, hl in halves], axis=1)
        inter = jnp.concatenate([_dot(qd_ref[b, 0, :, sl], sb[:, hl]) for sl, hl in halves], axis=1)
        stage1.append((s, v_new, inter))
    for (b, d, j), (s, v_new, inter) in zip(chains, stage1):
        kdt_ref, attn_ref, gl_ref = ins[6 * d + 3:6 * d + 6]
        v_bd = _bd(v_new).astype(BF16)
        intra = _dot(attn_ref[b, 0, :, 2 * j * CHUNK:2 * (j + 1) * CHUNK], v_bd)
        outs[d][b, :, 2 * j * dk:2 * (j + 1) * dk] = (inter + intra).astype(BF16)
        c = DN_HEADS * d + 2 * j
        decayed = jnp.concatenate([s[:, i * dk:(i + 1) * dk] * gl_ref[b, 0, 0, :, c + i:c + i + 1] for i in range(2)], axis=1)
        s_ref[b, d, j] = decayed + _dot(kdt_ref[b, 0, 0, j * dk:(j + 1) * dk, :], v_bd)


def _dn_scan_call(u, w, qd, kdt, attn, gl, *, ctx_chunks):
    n_batch, _, t_len, wd = u.shape
    nc = t_len // CHUNK
    in_specs, args = [], []
    for d in range(2):
        chunk = (lambda n: n) if d == 0 else functools.partial(_rev_chunk, ctx_chunks=ctx_chunks, n_chunks=nc)
        per_token = lambda a: pl.BlockSpec((SCAN_BATCH, 1, CHUNK, a.shape[-1]),
                                           lambda b, n, d=d, chunk=chunk: (b, d, chunk(n), 0))
        per_chunk = lambda a: pl.BlockSpec((SCAN_BATCH, 1, 1) + a.shape[3:],
                                           lambda b, n, d=d, chunk=chunk: (b, d, chunk(n), 0, 0))
        in_specs += [per_token(u), per_token(w), per_token(qd), per_chunk(kdt), per_token(attn), per_chunk(gl)]
        args += [u, w, qd, kdt, attn, gl]
    return pl.pallas_call(
        _dn_scan_kernel,
        out_shape=[jax.ShapeDtypeStruct((n_batch, t_len, wd), BF16)] * 2,
        grid=(n_batch // SCAN_BATCH, nc),
        in_specs=in_specs,
        out_specs=[pl.BlockSpec((SCAN_BATCH, CHUNK, wd), lambda b, n: (b, n, 0)),
                   pl.BlockSpec((SCAN_BATCH, CHUNK, wd), lambda b, n: (b, _rev_chunk(n, ctx_chunks, nc), 0))],
        scratch_shapes=[pltpu.VMEM((SCAN_BATCH, 2, DN_HEADS // 2, DN_DK, 2 * DN_DK), F32)],
        compiler_params=_cparams("parallel", "arbitrary"),
        name="deltanet_scan",
    )(*args)


def _rw_chunk_kernel(r_ref, v_ref, kk_ref, lw_ref, kd_ref, kka_ref,
                     ut_ref, wt_ref, rt_ref, arb_ref, kbt_ref, y0_ref, pc_ref, vb_ref):
    pw = 2 * RW_HS
    vb_ref[0] = v_ref[0].astype(BF16)
    prepared = []
    for cc in range(CHUNKS_PER_STEP):
        rows = slice(cc * CHUNK, (cc + 1) * CHUNK)
        r, kk = r_ref[0, rows], kk_ref[0, rows]
        for d in range(2):
            dsl = slice(d * RW_W, (d + 1) * RW_W)
            lw, kd, kka = lw_ref[0, rows, dsl], kd_ref[0, rows, dsl], kka_ref[0, rows, dsl]
            cl = _mm_exact_lhs(jnp.where(_chunk_masks(d)[0], 1.0, 0.0), lw)
            tot = _last_row(cl, d)
            p_inv, p_tail = jnp.exp(-cl), jnp.exp(tot - cl)
            at = -kk * jnp.exp(cl - lw)
            rt = r * jnp.exp(cl)
            rt_ref[0, d, rows] = rt.astype(BF16)
            pc_ref[0, d, cc] = jnp.broadcast_to(jnp.exp(tot), (8, RW_W))
            kbt_ref[0, d, cc] = jnp.concatenate([(kd * p_tail).T, (kka * p_tail).T], axis=1).astype(BF16)
            prepared.append((rows, d, at, rt, kd * p_inv, kka * p_inv))
    work, n_list = [], []
    for rows, d, at, rt, kh, bh in prepared:
        incl, strict = _chunk_masks(d, 2 * CHUNK)
        for j in range(RW_HEADS // 2):
            sl = slice(j * pw, (j + 1) * pw)
            aa = _mm_nt(jnp.concatenate([at[:, sl], rt[:, sl]], axis=0),
                        jnp.concatenate([_bd(bh[:, sl]), _bd(kh[:, sl])], axis=0))
            n_list.append(jnp.where(strict, aa[:CHUNK, :pw], 0.0))
            work.append((rows, d, sl, at[:, sl], jnp.where(strict, aa[:CHUNK, pw:], 0.0),
                         jnp.where(incl, aa[CHUNK:, :pw], 0.0), jnp.where(incl, aa[CHUNK:, pw:], 0.0)))
    t_invs = _neumann_inverse_pairs(n_list)
    rhs = []
    for rows, d, sl, at_p, a_ak, a_rb, a_rk in work:
        v_bd = _bd(v_ref[0, rows, sl]).astype(BF16)
        arb_ref[0, d, rows, sl] = a_rb.astype(BF16)
        y0_ref[0, d, rows, sl] = _dot(a_rk.astype(BF16), v_bd).astype(BF16)
        rhs.append(jnp.concatenate([_bd(at_p), _bd(_dot(a_ak.astype(BF16), v_bd))], axis=1))
    for (rows, d, sl, *_), t_inv, x in zip(work, t_invs, rhs):
        sol = _mm(t_inv, x)
        wt_ref[0, d, rows, sl] = sol[:, :pw].astype(BF16)
        ut_ref[0, d, rows, sl] = sol[:, pw:].astype(BF16)


def _rw_chunk_call(r, v, kk, lw, kd, kka):
    n_batch, t_len, wd = r.shape
    nc = t_len // CHUNK
    cps = CHUNKS_PER_STEP
    ispec = lambda n: pl.BlockSpec((1, cps * CHUNK, n), lambda b, c: (b, c, 0))
    ospec = pl.BlockSpec((1, 2, cps * CHUNK, wd), lambda b, c: (b, 0, c, 0))
    shp = jax.ShapeDtypeStruct((n_batch, 2, t_len, wd), BF16)
    return pl.pallas_call(
        _rw_chunk_kernel,
        out_shape=[shp] * 4 + [jax.ShapeDtypeStruct((n_batch, 2, nc, wd, 2 * CHUNK), BF16), shp,
                               jax.ShapeDtypeStruct((n_batch, 2, nc, 8, wd), F32),
                               jax.ShapeDtypeStruct((n_batch, t_len, wd), BF16)],
        grid=(n_batch, nc // cps),
        in_specs=[ispec(wd), ispec(wd), ispec(wd), ispec(2 * wd), ispec(2 * wd), ispec(2 * wd)],
        out_specs=[ospec] * 4 + [pl.BlockSpec((1, 2, cps, wd, 2 * CHUNK), lambda b, c: (b, 0, c, 0, 0)), ospec,
                                 pl.BlockSpec((1, 2, cps, 8, wd), lambda b, c: (b, 0, c, 0, 0)), ispec(wd)],
        compiler_params=_cparams("parallel", "parallel"),
        name="rwkv_chunk_prep",
    )(r, v, kk, lw, kd, kka)


def _rw_scan_kernel(*refs):
    ins, (yf_ref, yb_ref, s_ref) = refs[:16], refs[16:]

    @pl.when(pl.program_id(1) == 0)
    def _():
        s_ref[...] = jnp.zeros_like(s_ref)

    pw = 2 * RW_HS
    r_i = lax.broadcasted_iota(jnp.int32, (pw, pw), 0)
    c_i = lax.broadcasted_iota(jnp.int32, (pw, pw), 1)
    same_head = (r_i < RW_HS) == (c_i < RW_HS)
    chains = [(b, d, j) for b in range(SCAN_BATCH) for d in range(2) for j in range(RW_HEADS // 2)]
    outs = (yf_ref, yb_ref)
    decay_cols = {(b, d): ins[8 * d + 6][b, 0, 0].T for b in range(SCAN_BATCH) for d in range(2)}
    stage1 = []
    for b, d, j in chains:
        ut_ref, wt_ref, rt_ref, _, _, y0_ref = ins[8 * d:8 * d + 6]
        sl = slice(j * pw, (j + 1) * pw)
        s = s_ref[b, d, j]
        sb = s.astype(BF16)
        u = ut_ref[b, 0, :, sl] + _dot(wt_ref[b, 0, :, sl], sb)
        stage1.append((s, u, y0_ref[b, 0, :, sl] + _dot(rt_ref[b, 0, :, sl], sb)))
    for (b, d, j), (s, u, y_inter) in zip(chains, stage1):
        arb_ref, kbt_ref, _, _, v_ref = ins[8 * d + 3:8 * d + 8]
        sl = slice(j * pw, (j + 1) * pw)
        outs[d][b, :, sl] = (y_inter + _dot(arb_ref[b, 0, :, sl], _bd(u).astype(BF16))).astype(BF16)
        grow = _dot(kbt_ref[b, 0, 0, sl, :], jnp.concatenate([v_ref[b, :, sl], u.astype(BF16)], axis=0))
        s_ref[b, d, j] = s * decay_cols[b, d][sl, 0:1] + jnp.where(same_head, grow, 0.0)


def _rw_scan_call(ut, wt, rt, arb, kbt, y0, pc, v, *, ctx_chunks):
    n_batch, _, t_len, wd = ut.shape
    nc = t_len // CHUNK
    in_specs, args = [], []
    for d in range(2):
        chunk = (lambda n: n) if d == 0 else functools.partial(_rev_chunk, ctx_chunks=ctx_chunks, n_chunks=nc)
        per_chunk = lambda a: pl.BlockSpec((SCAN_BATCH, 1, 1) + a.shape[3:],
                                           lambda b, n, d=d, chunk=chunk: (b, d, chunk(n), 0, 0))
        per_token = pl.BlockSpec((SCAN_BATCH, 1, CHUNK, wd), lambda b, n, d=d, chunk=chunk: (b, d, chunk(n), 0))
        in_specs += [per_token] * 4 + [per_chunk(kbt), per_token, per_chunk(pc),
                                       pl.BlockSpec((SCAN_BATCH, CHUNK, wd), lambda b, n, chunk=chunk: (b, chunk(n), 0))]
        args += [ut, wt, rt, arb, kbt, y0, pc, v]
    return pl.pallas_call(
        _rw_scan_kernel,
        out_shape=[jax.ShapeDtypeStruct((n_batch, t_len, wd), BF16)] * 2,
        grid=(n_batch // SCAN_BATCH, nc),
        in_specs=in_specs,
        out_specs=[pl.BlockSpec((SCAN_BATCH, CHUNK, wd), lambda b, n: (b, n, 0)),
                   pl.BlockSpec((SCAN_BATCH, CHUNK, wd), lambda b, n: (b, _rev_chunk(n, ctx_chunks, nc), 0))],
        scratch_shapes=[pltpu.VMEM((SCAN_BATCH, 2, RW_HEADS // 2, 2 * RW_HS, 2 * RW_HS), F32)],
        compiler_params=_cparams("parallel", "arbitrary"),
        name="rwkv_scan",
    )(*args)


def _ev_out_kernel(h_ref, mod_ref, of_ref, ob_ref, dgate_ref, yf_ref, yb_ref, gate7_ref, bonus_ref,
                   dnorm_ref, gnw_ref, gnb_ref, ones_ref, wout_ref, o_ref):
    f32 = lambda ref: ref[0].astype(F32)
    o = f32(of_ref) + f32(ob_ref)
    dgate = f32(dgate_ref)
    parts = []
    for h in range(DN_HEADS):
        sl = slice(h * DN_DK, (h + 1) * DN_DK)
        parts.append(_rms_rows(o[:, sl]) * dnorm_ref[:, sl] * _silu(dgate[:, sl]))
    o_dn = jnp.concatenate(parts, axis=1)
    y = f32(yf_ref) + f32(yb_ref)
    inv_n = 1.0 / RW_HS
    mu = _head_sums(y, ones_ref) * inv_n
    yc = y - mu
    var = _head_sums(yc * yc, ones_ref) * inv_n
    yn = yc * lax.rsqrt(var + RW_GN_EPS) * gnw_ref[...] + gnb_ref[...]
    o_rw = (yn + f32(bonus_ref)) * f32(gate7_ref)
    wd = DN_HEADS * DN_DK
    proj = _dot(o_dn.astype(BF16), wout_ref[0:wd, :]) + _dot(o_rw.astype(BF16), wout_ref[wd:, :])
    o_ref[0] = h_ref[0] + mod_ref[0, 5:6] * proj


def _ev_out_call(h, mod, o_f, o_b, dgate, y_f, y_b, gate7, bonus, dnorm, gnw, gnb, ones, w_out, *, ctx_tiles):
    n_batch, t_len, d = h.shape
    nt = t_len // TILE
    tile = lambda n: pl.BlockSpec((1, TILE, n), lambda b, t: (b, t, 0))
    params = [dnorm.reshape(1, -1), gnw.reshape(1, -1), gnb.reshape(1, -1), ones, w_out]
    streams = [o_f, o_b, dgate, y_f, y_b, gate7, bonus]
    return pl.pallas_call(
        _ev_out_kernel,
        out_shape=jax.ShapeDtypeStruct(h.shape, F32),
        grid=(n_batch, nt),
        in_specs=[tile(d), pl.BlockSpec((1, N_MOD, d), _mod_row_map(n_batch, ctx_tiles, 0))]
        + [tile(s.shape[-1]) for s in streams] + [_resident(p.shape) for p in params],
        out_specs=tile(d),
        compiler_params=_cparams("parallel", "parallel"),
        name="even_mix_out",
    )(h, mod, *streams, *params)


def _rope_layout(width, rot, starts):
    angle = np.full((width,), -1, np.int64)
    first = np.zeros((1, width), np.float32)
    q = rot // 4
    for start in starts:
        for blk in range(2):
            for idx in range(q):
                l1 = start + blk * 2 * q + idx
                angle[l1] = angle[l1 + q] = blk * q + idx
                first[0, l1] = 1.0
    return angle, jnp.asarray(first)


def _rope_tables(n_ctx, n_lat, rot, angle, first):
    rows = n_lat // GRID_W
    row = jnp.repeat(jnp.arange(rows), GRID_W).astype(F32)
    col = jnp.tile(jnp.arange(GRID_W), rows).astype(F32)
    axis_dim = rot // 2
    inv = ROPE_THETA ** (-jnp.arange(0, axis_dim, 2, dtype=F32) / axis_dim)
    ang = jnp.concatenate([row[:, None] * inv, col[:, None] * inv], axis=-1)
    on = jnp.asarray(angle >= 0)
    idx = np.maximum(angle, 0)
    cos = jnp.where(on, jnp.cos(ang)[:, idx], 1.0)
    sin = jnp.where(on, jnp.sin(ang)[:, idx], 0.0) * (1.0 - 2.0 * first)
    width = angle.shape[0]
    return (jnp.concatenate([jnp.ones((n_ctx, width), F32), cos], axis=0),
            jnp.concatenate([jnp.zeros((n_ctx, width), F32), sin], axis=0))


def _rope(x, first_ref, cos_ref, sin_ref, quarter):
    width = x.shape[1]
    partner = jnp.where(first_ref[...] > 0.5, pltpu.roll(x, width - quarter, 1), pltpu.roll(x, quarter, 1))
    return x * cos_ref[...] + partner * sin_ref[...]


def _od_prep_kernel(h_ref, mod_ref, gain_ref, w_ref, qn_ref, kn_ref, mqn_ref, wuq_ref, mkvn_ref, wukv_ref,
                    ones_ref, fq_ref, fk_ref, fm_ref, fr_ref,
                    cq_ref, sq_ref, ck_ref, sk_ref, cm_ref, sm_ref, cr_ref, sr_ref,
                    qg_ref, qm_ref, kgt_ref, vg_ref, kmt_ref, vm_ref):
    xn = _modulate(h_ref[0], gain_ref[...], mod_ref[0, 3:4], mod_ref[0, 4:5]).astype(BF16)
    p = _dot(xn, w_ref[...])
    nq, nkv = GQ_HEADS * GQ_HD, GQ_KV_HEADS * GQ_HD
    o = 0
    q, o = p[:, o:o + nq], o + nq
    k, o = p[:, o:o + nkv], o + nkv
    v, o = p[:, o:o + 2 * nkv], o + 2 * nkv
    n_cq, n_ckv = mqn_ref.shape[1], mkvn_ref.shape[1]
    cq, o = p[:, o:o + n_cq], o + n_cq
    ckv, o = p[:, o:o + n_ckv], o + n_ckv
    kr = p[:, o:o + LANES]
    inv_hd = 1.0 / GQ_HD
    q = q * lax.rsqrt(_head_sums(q * q, ones_ref) * inv_hd + NORM_EPS) * qn_ref[...]
    k = k * lax.rsqrt(_head_sums(k * k, ones_ref) * inv_hd + NORM_EPS) * kn_ref[...]
    qm = _dot((_rms_rows(cq) * mqn_ref[...]).astype(BF16), wuq_ref[...])
    kvm = _dot((_rms_rows(ckv) * mkvn_ref[...]).astype(BF16), wukv_ref[...])
    q = _rope(q, fq_ref, cq_ref, sq_ref, GQ_HD // 4) * (GQ_HD ** -0.5 * LOG2_E)
    k = _rope(k, fk_ref, ck_ref, sk_ref, GQ_HD // 4)
    qm = _rope(qm, fm_ref, cm_ref, sm_ref, ML_ROPE // 4) * ((ML_NOPE + ML_ROPE) ** -0.5 * LOG2_E)
    kr = _rope(kr, fr_ref, cr_ref, sr_ref, ML_ROPE // 4)
    qg_ref[0] = q.astype(BF16)
    qm_ref[0] = qm.astype(BF16)
    n_nope = ML_HEADS * ML_NOPE
    for ref, val in ((vg_ref, v), (vm_ref, kvm[:, n_nope:])):
        upper = lax.broadcasted_iota(jnp.int32, (1, val.shape[1]), 1) % LANES >= LANES // 2
        ref[0] = (val + jnp.where(upper, 1.0, 0.0)).astype(BF16)
    kgt_ref[0] = k.T.astype(BF16)
    knt = kvm[:, :n_nope].T.astype(BF16)
    krt = kr.T[:ML_ROPE].astype(BF16)
    dk = ML_NOPE + ML_ROPE
    for h in range(ML_HEADS):
        kmt_ref[0, h * dk:h * dk + ML_NOPE, :] = knt[h * ML_NOPE:(h + 1) * ML_NOPE]
        kmt_ref[0, h * dk + ML_NOPE:(h + 1) * dk, :] = krt


def _od_prep_call(h, mod, gain, params, tables, *, ctx_tiles):
    n_batch, t_len, d = h.shape
    nt = t_len // TILE
    nq, nkv = GQ_HEADS * GQ_HD, GQ_KV_HEADS * GQ_HD
    dk = ML_NOPE + ML_ROPE
    tile = lambda n: pl.BlockSpec((1, TILE, n), lambda b, t: (b, t, 0))
    tile_t = lambda n: pl.BlockSpec((1, n, TILE), lambda b, t: (b, 0, t))
    tab = lambda a: pl.BlockSpec((TILE, a.shape[1]), lambda b, t: (t, 0))
    shp = lambda *s: jax.ShapeDtypeStruct((n_batch,) + s, BF16)
    return pl.pallas_call(
        _od_prep_kernel,
        out_shape=[shp(t_len, nq), shp(t_len, ML_HEADS * dk), shp(nkv, t_len), shp(t_len, GQ_KV_HEADS * LANES),
                   shp(ML_HEADS * dk, t_len), shp(t_len, ML_HEADS * LANES)],
        grid=(n_batch, nt),
        in_specs=[tile(d), pl.BlockSpec((1, N_MOD, d), _mod_row_map(n_batch, ctx_tiles, 0)), _resident((1, d))]
        + [_resident(p.shape) for p in params] + [tab(a) for a in tables],
        out_specs=[tile(nq), tile(ML_HEADS * dk), tile_t(nkv), tile(GQ_KV_HEADS * LANES), tile_t(ML_HEADS * dk),
                   tile(ML_HEADS * LANES)],
        compiler_params=_cparams("parallel", "parallel"),
        name="attn_prep",
    )(h, mod, gain.reshape(1, d), *params, *tables)


def _odd_layer_weights(od_w_in, gq_q_norm, gq_k_norm, ml_q_norm, ml_w_uq, ml_kv_norm, ml_w_ukv):
    d = od_w_in.shape[0]
    nq, nkv = GQ_HEADS * GQ_HD, GQ_KV_HEADS * GQ_HD

    def slabs(cols, heads, width):
        cols = cols.reshape(cols.shape[0], heads, width)
        return jnp.concatenate([cols, jnp.zeros(cols.shape[:2] + (LANES - width,), F32)], axis=2).reshape(cols.shape[0], -1)

    w = jnp.concatenate([od_w_in[:, :nq + nkv], slabs(od_w_in[:, nq + nkv:nq + 2 * nkv], GQ_KV_HEADS, GQ_HD),
                         od_w_in[:, nq + 2 * nkv:], jnp.zeros((d, LANES - ML_ROPE), F32)], axis=1)
    ukv = ml_w_ukv.reshape(ml_w_ukv.shape[0], ML_HEADS, ML_NOPE + ML_V)
    ukv = jnp.concatenate([ukv[:, :, :ML_NOPE].reshape(-1, ML_HEADS * ML_NOPE),
                           slabs(ukv[:, :, ML_NOPE:].reshape(-1, ML_HEADS * ML_V), ML_HEADS, ML_V)], axis=1)
    dk = ML_NOPE + ML_ROPE
    layouts = [(GQ_HD,) + _rope_layout(GQ_HEADS * GQ_HD, GQ_HD, [h * GQ_HD for h in range(GQ_HEADS)]),
               (GQ_HD,) + _rope_layout(GQ_KV_HEADS * GQ_HD, GQ_HD, [h * GQ_HD for h in range(GQ_KV_HEADS)]),
               (ML_ROPE,) + _rope_layout(ML_HEADS * dk, ML_ROPE, [h * dk + ML_NOPE for h in range(ML_HEADS)]),
               (ML_ROPE,) + _rope_layout(LANES, ML_ROPE, [0])]
    params = [w.astype(BF16), jnp.tile(gq_q_norm, GQ_HEADS).reshape(1, -1), jnp.tile(gq_k_norm, GQ_KV_HEADS).reshape(1, -1),
              ml_q_norm.reshape(1, -1), ml_w_uq.astype(BF16), ml_kv_norm.reshape(1, -1), ukv.astype(BF16),
              _block_ones(LANES, GQ_HD)] + [first for _, _, first in layouts]
    return params, layouts


def _softmax_pv(s, v_slab, width):
    m = jnp.max(s, axis=-1, keepdims=True)
    p = jnp.exp2((s - m).astype(BF16))
    pv = _dot(p, v_slab)
    return pv[:, :width] / pv[:, width:width + 1]


def _attn_kernel(h_ref, mod_ref, qg_ref, qm_ref, kgt_ref, vg_ref, kmt_ref, vm_ref, wout_ref, o_ref):
    group = GQ_HEADS // GQ_KV_HEADS
    dk = ML_NOPE + ML_ROPE

    def logits(h):
        if h < GQ_HEADS:
            g = h // group
            return _dot(qg_ref[0, :, h * GQ_HD:(h + 1) * GQ_HD], kgt_ref[0, g * GQ_HD:(g + 1) * GQ_HD, :])
        h -= GQ_HEADS
        return _dot(qm_ref[0, :, h * dk:(h + 1) * dk], kmt_ref[0, h * dk:(h + 1) * dk, :])

    def values(h):
        if h < GQ_HEADS:
            g = h // group
            return vg_ref[0, :, g * LANES:(g + 1) * LANES], GQ_HD
        h -= GQ_HEADS
        return vm_ref[0, :, h * LANES:(h + 1) * LANES], ML_V

    n_heads = GQ_HEADS + ML_HEADS
    parts, s = [], logits(0)
    for h in range(n_heads):
        s_next = logits(h + 1) if h + 1 < n_heads else None
        parts.append(_softmax_pv(s, *values(h)))
        s = s_next
    ol = jnp.concatenate(parts, axis=1).astype(BF16)
    o_ref[0] = h_ref[0] + mod_ref[0, 5:6] * _dot(ol, wout_ref[...])


def _attn_call(h, mod, qg, qm, kgt, vg, kmt, vm, w_out, *, ctx_tiles):
    n_batch, t_len, d = h.shape
    nt = t_len // TILE - ctx_tiles
    qtile = lambda n: pl.BlockSpec((1, TILE, n), lambda b, t: (b, t + ctx_tiles, 0))
    whole = lambda a: pl.BlockSpec((1,) + a.shape[1:], lambda b, t: (b, 0, 0))
    return pl.pallas_call(
        _attn_kernel,
        out_shape=jax.ShapeDtypeStruct((n_batch, nt * TILE, d), F32),
        grid=(n_batch, nt),
        in_specs=[qtile(d), pl.BlockSpec((1, N_MOD, d), lambda b, t: (b, 0, 0)), qtile(qg.shape[-1]), qtile(qm.shape[-1]),
                  whole(kgt), whole(vg), whole(kmt), whole(vm), _resident(w_out.shape)],
        out_specs=pl.BlockSpec((1, TILE, d), lambda b, t: (b, t, 0)),
        compiler_params=_cparams("parallel", "parallel"),
        name="attention_out",
    )(h, mod, qg, qm, kgt, vg, kmt, vm, w_out)


def _block_ones(n, blk):
    i = np.arange(n) // blk
    return jnp.asarray(i[:, None] == i[None, :], BF16)


def _even_layer_weights(ev_w_in, dn_conv, dn_a_log, dn_dt_bias, rw_mu, rw_w0, rw_w2, rw_a0, rw_a2, rw_g2,
                        rw_kk, rw_ka, rw_rk):
    d = ev_w_in.shape[0]
    n_dn = 4 * DN_HEADS * DN_DK
    nh2 = 2 * DN_HEADS
    slab0 = n_dn + 2 * nh2
    zeros = lambda n: jnp.zeros((d, n), F32)
    w_dn = jnp.concatenate([ev_w_in[:, :n_dn], ev_w_in[:, n_dn:slab0], zeros(LANES - 2 * nh2)], axis=1)
    slab = ev_w_in[:, slab0:]
    o = 3 * RW_W
    lora = 2 * RW_W_LORA
    gpad = 2 * LANES - RW_G_LORA
    w_rw = jnp.concatenate([slab[:, :o + 2 * lora + RW_G_LORA], zeros(gpad)], axis=1)
    mu = jnp.concatenate([rw_mu, jnp.zeros((gpad,), F32)]).reshape(1, -1)
    pad_lanes = lambda v: jnp.zeros((1, LANES), F32).at[0, :v.size].set(v.reshape(-1))

    def dir_blocks(m):
        z = jnp.zeros_like(m[0])
        return jnp.concatenate([jnp.concatenate([m[0], z], axis=1), jnp.concatenate([z, m[1]], axis=1)], axis=0)

    g2 = jnp.concatenate([rw_g2, jnp.zeros((gpad, RW_W), F32)], axis=0)
    return dict(
        w_dn=w_dn.astype(BF16), conv=dn_conv, alog=pad_lanes(dn_a_log), dtb=pad_lanes(dn_dt_bias),
        w_rw=w_rw.astype(BF16), mu=mu, w2=dir_blocks(rw_w2), w0=rw_w0.reshape(1, -1), a2=dir_blocks(rw_a2),
        a0=rw_a0.reshape(1, -1), g2=g2, kkw=rw_kk.reshape(1, -1), kaw=rw_ka.reshape(1, -1),
        rk=rw_rk.reshape(1, -1), ones=_block_ones(LANES, RW_HS))


def kernel(x, c, ctx, c_ctx, mod_w, mod_b, norm_ffn1, norm_mix, norm_ffn2, ffn1_w1, ffn1_w3, ffn1_w2, ffn2_w1, ffn2_w3, ffn2_w2, ev_w_in, ev_w_out, dn_conv, dn_a_log, dn_dt_bias, dn_norm, rw_mu, rw_w0, rw_w2, rw_a0, rw_a2, rw_g2, rw_kk, rw_ka, rw_rk, rw_gn_w, rw_gn_b, od_w_in, od_w_out, gq_q_norm, gq_k_norm, ml_q_norm, ml_w_uq, ml_kv_norm, ml_w_ukv, final_norm):
    n_batch, n_lat, d = x.shape
    n_ctx = ctx.shape[1]
    depth = mod_w.shape[0]
    assert n_ctx % TILE == 0 and n_lat % TILE == 0 and n_lat % GRID_W == 0 and n_batch % SCAN_BATCH == 0
    assert depth % 2 == 0 and depth // 2 == od_w_in.shape[0] == 1, "supported stack: [recurrent, attention]"
    ctx_tiles, ctx_chunks = n_ctx // TILE, n_ctx // CHUNK
    bf = lambda a: a.astype(BF16)

    mod = _all_mod(c, c_ctx, mod_w, mod_b)
    h = (ctx, x)
    for i in range(depth):
        j = i // 2
        last = i == depth - 1
        h = _ffn_call(h, mod[i], norm_ffn1[i], bf(ffn1_w1[i]), bf(ffn1_w3[i]), bf(ffn1_w2[i]), j0=0, ctx_tiles=ctx_tiles)
        if i % 2 == 0:
            w = _even_layer_weights(ev_w_in[j], dn_conv[j], dn_a_log[j], dn_dt_bias[j], rw_mu[j], rw_w0[j], rw_w2[j],
                                    rw_a0[j], rw_a2[j], rw_g2[j], rw_kk[j], rw_ka[j], rw_rk[j])
            q, k, v, small, dgate = _dn_prep_call(h, mod[i], norm_mix[i], w["w_dn"], w["conv"], w["alog"], w["dtb"],
                                                  ctx_tiles=ctx_tiles)
            o_f, o_b = _dn_scan_call(*_dn_chunk_call(q, k, v, small), ctx_chunks=ctx_chunks)
            r, v7, kk, lw, kd, kka, gate7, bonus = _rw_prep_call(
                h, mod[i], norm_mix[i], w["w_rw"], w["mu"], w["w2"], w["w0"], w["a2"], w["a0"], w["g2"], w["kkw"],
                w["kaw"], w["rk"], w["ones"], ctx_tiles=ctx_tiles)
            y_f, y_b = _rw_scan_call(*_rw_chunk_call(r, v7, kk, lw, kd, kka), ctx_chunks=ctx_chunks)
            h = _ev_out_call(h, mod[i], o_f, o_b, dgate, y_f, y_b, gate7, bonus, jnp.tile(dn_norm[j], DN_HEADS),
                             rw_gn_w[j], rw_gn_b[j], w["ones"], bf(ev_w_out[j]), ctx_tiles=ctx_tiles)
            h = _ffn_call(h, mod[i], norm_ffn2[i], bf(ffn2_w1[i]), bf(ffn2_w3[i]), bf(ffn2_w2[i]), j0=6,
                          ctx_tiles=ctx_tiles)
        else:
            params, layouts = _odd_layer_weights(od_w_in[j], gq_q_norm[j], gq_k_norm[j], ml_q_norm[j], ml_w_uq[j],
                                                 ml_kv_norm[j], ml_w_ukv[j])
            tables = [t for rot, angle, first in layouts for t in _rope_tables(n_ctx, n_lat, rot, angle, first)]
            qg, qm, kgt, vg, kmt, vm = _od_prep_call(h, mod[i], norm_mix[i], params, tables, ctx_tiles=ctx_tiles)
            hl = _attn_call(h, mod[i], qg, qm, kgt, vg, kmt, vm, bf(od_w_out[j]), ctx_tiles=ctx_tiles)
            assert last
            h = _ffn_call(hl, mod[i], norm_ffn2[i], bf(ffn2_w1[i]), bf(ffn2_w3[i]), bf(ffn2_w2[i]), j0=6, ctx_tiles=0,
                          final_gain=final_norm)
    return h
```

```python
import functools

import jax
import jax.numpy as jnp
import numpy as np
from jax import lax
from jax.experimental import pallas as pl
from jax.experimental.pallas import tpu as pltpu

F32 = jnp.float32
BF16 = jnp.bfloat16

NORM_EPS = 1e-6
ROPE_THETA = 10000.0
GRID_W = 64
N_MOD = 9

DN_HEADS = 4
DN_DK = 128
DN_CONV = 5
RW_HEADS = 8
RW_HS = 64
RW_W = RW_HEADS * RW_HS
RW_W_LORA = 64
RW_A_LORA = 64
RW_G_LORA = 160
RW_GN_EPS = 64e-5
GQ_HEADS = 8
GQ_KV_HEADS = 2
GQ_HD = 64
ML_HEADS = 8
ML_NOPE = 64
ML_ROPE = 32
ML_V = 64

TILE = 256
CHUNK = 64
HALO = 8
LANES = 128
MXU_DIM = 256
VMEM_LIMIT = 56 * 1024 * 1024
SCAN_BATCH = 4
CHUNKS_PER_STEP = 4
LOG2_E = 1.4426950408889634


def _cparams(*sem):
    return pltpu.CompilerParams(dimension_semantics=sem, vmem_limit_bytes=VMEM_LIMIT)


def _resident(shape):
    nd = len(shape)
    return pl.BlockSpec(shape, lambda *_: (0,) * nd, pipeline_mode=pl.Buffered(1))


def _mm(a, b):
    return jnp.dot(a.astype(BF16), b.astype(BF16), preferred_element_type=F32)


def _mm_nt(a, b):
    return lax.dot_general(a.astype(BF16), b.astype(BF16), (((1,), (1,)), ((), ())),
                           preferred_element_type=F32)


def _mm_tn(a, b):
    return lax.dot_general(a.astype(BF16), b.astype(BF16), (((0,), (0,)), ((), ())),
                           preferred_element_type=F32)


def _split2(x):
    hi = x.astype(BF16)
    lo = (x - hi.astype(F32)).astype(BF16)
    return hi, lo


def _split3(x):
    hi = x.astype(BF16)
    r = x - hi.astype(F32)
    mid = r.astype(BF16)
    lo = (r - mid.astype(F32)).astype(BF16)
    return hi, mid, lo


def _dot(a, b):
    return jnp.dot(a, b, preferred_element_type=F32)


def _mm3s(asp, bsp):
    (ah, al), (bh, bl) = asp, bsp
    return _dot(ah, bh) + (_dot(ah, bl) + _dot(al, bh))


def _mm3(a, b):
    return _mm3s(_split2(a), _split2(b))


def _mm2(a, b):
    ah, al = _split2(a)
    bb = b.astype(BF16)
    return _dot(ah, bb) + _dot(al, bb)


def _mm_exact_lhs(a01, b):
    a = a01.astype(BF16)
    hi, mid, lo = _split3(b)
    return _dot(a, hi) + (_dot(a, mid) + _dot(a, lo))


def _rms_rows(x):
    return x * lax.rsqrt(jnp.mean(x * x, axis=-1, keepdims=True) + NORM_EPS)


def _modulate(x, gain, shift, scale):
    return (_rms_rows(x) * gain) * (1.0 + scale) + shift


def _silu(x):
    return x * jax.nn.sigmoid(x)


def _softplus(x):
    return jnp.maximum(x, 0.0) + jnp.log1p(jnp.exp(-jnp.abs(x)))


def _mod_kernel(c_ref, w_ref, b_ref, o_ref):
    s = _silu(c_ref[...])
    o_ref[0] = _mm3(s, w_ref[0]) + b_ref[0]


def _mod_call(cc, mod_w, mod_b):
    n_layers, d, n = mod_w.shape
    r = cc.shape[0]
    tn = n // 8
    return pl.pallas_call(
        _mod_kernel,
        out_shape=jax.ShapeDtypeStruct((n_layers, r, n), F32),
        grid=(n_layers, n // tn),
        in_specs=[pl.BlockSpec((r, d), lambda l, j: (0, 0)),
                  pl.BlockSpec((1, d, tn), lambda l, j: (l, 0, j)),
                  pl.BlockSpec((1, 1, tn), lambda l, j: (l, 0, j))],
        out_specs=pl.BlockSpec((1, r, tn), lambda l, j: (l, 0, j)),
        compiler_params=_cparams("parallel", "parallel"),
        name="adaln_mod",
    )(cc, mod_w, mod_b.reshape(n_layers, 1, n))


def _all_mod(c, c_ctx, mod_w, mod_b):
    n_batch, d = c.shape
    rows = -(-(n_batch + 1) // 8) * 8
    cc = jnp.zeros((rows, d), F32).at[:n_batch].set(c).at[n_batch].set(c_ctx)
    return _mod_call(cc, mod_w, mod_b).reshape(mod_w.shape[0], rows, N_MOD, d)


def _mod_row_map(n_batch, ctx_tiles, t_off):
    def index_map(b, t):
        return (jnp.where(t + t_off < ctx_tiles, n_batch, b), 0, 0)
    return index_map


def _ffn_kernel(*refs, j0, final, split_tiles, sub, ctx_tiles, t_off):
    if split_tiles:
        ctx_ref, lat_ref, *refs = refs
        xs = [jnp.where(pl.program_id(1) < split_tiles, ctx_ref[0], lat_ref[0])]
    else:
        h_ref, *refs = refs
        xs = [h_ref[0, i * TILE:(i + 1) * TILE] for i in range(sub)]
    mod_ctx_ref, mod_ref, gain_ref, w1_ref, w3_ref, w2_ref, *rest = refs
    o_ref = rest[-1]
    f = w1_ref.shape[1]
    cut = -(-(f // MXU_DIM) // 2) * MXU_DIM
    fcs = [slice(0, cut), slice(cut, f)] if 0 < cut < f else [slice(0, f)]
    for i, x in enumerate(xs):
        is_ctx = (pl.program_id(1) * sub + i + t_off) < ctx_tiles
        row = lambda j: jnp.where(is_ctx, mod_ctx_ref[0, j:j + 1], mod_ref[0, j:j + 1])
        xn = _modulate(x, gain_ref[...], row(j0), row(j0 + 1)).astype(BF16)
        ups = [(_dot(xn, w1_ref[:, fc]), _dot(xn, w3_ref[:, fc])) for fc in fcs]
        y = None
        for (a, b), fc in zip(ups, fcs):
            part = _dot((_silu(a) * b).astype(BF16), w2_ref[fc, :])
            y = part if y is None else y + part
        y = x + (0.5 * row(j0 + 2)) * y
        if final:
            y = _rms_rows(y) * rest[0][...]
        o_ref[0, i * TILE:(i + 1) * TILE] = y


def _ffn_call(h, mod, gain, w1, w3, w2, *, j0, ctx_tiles, t_off=0, final_gain=None):
    split = isinstance(h, tuple)
    if split:
        ctx, lat = h
        n_batch, _, d = lat.shape
        t_len = ctx.shape[1] + lat.shape[1]
        last_ctx = ctx_tiles - 1
        sub = 1
        streams = [ctx, lat]
        stream_specs = [pl.BlockSpec((1, TILE, d), lambda b, t: (b, jnp.minimum(t, last_ctx), 0)),
                        pl.BlockSpec((1, TILE, d), lambda b, t: (b, jnp.maximum(t - ctx_tiles, 0), 0))]
    else:
        n_batch, t_len, d = h.shape
        nt_all = t_len // TILE - t_off
        sub = next(s for s in (3, 2, 1) if nt_all % s == 0 and t_off % s == 0)
        streams = [h]
        stream_specs = [pl.BlockSpec((1, sub * TILE, d), lambda b, t: (b, t + t_off // sub, 0))]
    f = w1.shape[1]
    nt = t_len // TILE - t_off
    final = final_gain is not None
    in_specs = stream_specs + [pl.BlockSpec((1, N_MOD, d), lambda b, t: (n_batch, 0, 0)),
                               pl.BlockSpec((1, N_MOD, d), lambda b, t: (b, 0, 0)),
                               _resident((1, d)), _resident((d, f)), _resident((d, f)), _resident((f, d))]
    args = streams + [mod, mod, gain.reshape(1, d), w1, w3, w2]
    if final:
        in_specs.append(_resident((1, d)))
        args.append(final_gain.reshape(1, d))
    return pl.pallas_call(
        functools.partial(_ffn_kernel, j0=j0, final=final, split_tiles=ctx_tiles if split else 0, sub=sub,
                          ctx_tiles=ctx_tiles, t_off=t_off),
        out_shape=jax.ShapeDtypeStruct((n_batch, nt * TILE, d), F32),
        grid=(n_batch, nt // sub),
        in_specs=in_specs,
        out_specs=pl.BlockSpec((1, sub * TILE, d), lambda b, t: (b, t, 0)),
        compiler_params=_cparams("parallel", "parallel"),
        name="macaron_ffn",
    )(*args)


def _halo_specs(d, ctx_tiles, n_tiles):
    per = TILE // HALO
    last = n_tiles * per - 1
    return [pl.BlockSpec((1, HALO, d), lambda b, t: (b, jnp.maximum(t * per - 1, 0), 0)),
            pl.BlockSpec((1, TILE, d), lambda b, t: (b, t, 0)),
            pl.BlockSpec((1, HALO, d), lambda b, t: (b, jnp.minimum((t + 1) * per, last), 0))]


def _project_with_halo(prev_ref, cur_ref, next_ref, mod_ref, gain_ref, w_ref, pe_ref, *, ctx_tiles, n_tiles):
    t = pl.program_id(1)
    xe = jnp.concatenate([prev_ref[0], cur_ref[0], next_ref[0]], axis=0)
    xn = _modulate(xe, gain_ref[...], mod_ref[0, 3:4], mod_ref[0, 4:5]).astype(BF16)
    p = _dot(xn, w_ref[...])
    row = lax.broadcasted_iota(jnp.int32, (TILE + 2 * HALO, 1), 0)
    prev_ok = jnp.logical_and(t > 0, t != ctx_tiles)
    next_ok = jnp.logical_and(t + 1 < n_tiles, t + 1 != ctx_tiles)
    keep = jnp.logical_and(jnp.logical_or(row >= HALO, prev_ok),
                           jnp.logical_or(row < HALO + TILE, next_ok))
    pe_ref[...] = jnp.where(keep, p, 0.0)


def _dn_prep_kernel(prev_ref, cur_ref, next_ref, mod_ref, gain_ref, w_ref, conv_ref, alog_ref, dtb_ref,
                    q_ref, k_ref, v_ref, small_ref, gate_ref, pe_ref, *, ctx_tiles, n_tiles):
    _project_with_halo(prev_ref, cur_ref, next_ref, mod_ref, gain_ref, w_ref, pe_ref,
                       ctx_tiles=ctx_tiles, n_tiles=n_tiles)
    nqkv = 3 * DN_HEADS * DN_DK
    half = DN_CONV // 2
    acc = None
    for j in range(DN_CONV):
        term = conv_ref[j:j + 1, :] * pe_ref[pl.ds(HALO - half + j, TILE), 0:nqkv]
        acc = term if acc is None else acc + term
    qkv = _silu(acc)
    w = DN_HEADS * DN_DK
    for idx, ref in ((0, q_ref), (1, k_ref)):
        for h in range(DN_HEADS):
            seg = qkv[:, idx * w + h * DN_DK: idx * w + (h + 1) * DN_DK]
            ref[0, :, h * DN_DK:(h + 1) * DN_DK] = seg * lax.rsqrt(jnp.sum(seg * seg, axis=-1, keepdims=True) + 1e-6)
    v_ref[0] = qkv[:, 2 * w:3 * w]
    gate_ref[0] = pe_ref[pl.ds(HALO, TILE), nqkv:nqkv + w].astype(BF16)
    ab = pe_ref[pl.ds(HALO, TILE), nqkv + w:nqkv + w + LANES]
    g = -jnp.exp(alog_ref[...]) * _softplus(ab + dtb_ref[...])
    lane = lax.broadcasted_iota(jnp.int32, ab.shape, 1)
    nh2 = 2 * DN_HEADS
    small_ref[0] = jnp.where(lane < nh2, g, jnp.where(lane < 2 * nh2, jax.nn.sigmoid(ab), 0.0))


def _dn_prep_call(h, mod, gain, w, conv, alog, dtb, *, ctx_tiles):
    n_batch, t_len, d = h.shape
    nt = t_len // TILE
    wd = DN_HEADS * DN_DK
    out = lambda n, dt=F32: jax.ShapeDtypeStruct((n_batch, t_len, n), dt)
    ospec = lambda n: pl.BlockSpec((1, TILE, n), lambda b, t: (b, t, 0))
    return pl.pallas_call(
        functools.partial(_dn_prep_kernel, ctx_tiles=ctx_tiles, n_tiles=nt),
        out_shape=[out(wd), out(wd), out(wd), out(LANES), out(wd, BF16)],
        grid=(n_batch, nt),
        in_specs=_halo_specs(d, ctx_tiles, nt) + [
            pl.BlockSpec((1, N_MOD, d), _mod_row_map(n_batch, ctx_tiles, 0)),
            _resident((1, d)), _resident(w.shape), _resident(conv.shape),
            _resident((1, LANES)), _resident((1, LANES))],
        out_specs=[ospec(wd), ospec(wd), ospec(wd), ospec(LANES), ospec(wd)],
        scratch_shapes=[pltpu.VMEM((TILE + 2 * HALO, w.shape[1]), F32)],
        compiler_params=_cparams("parallel", "parallel"),
        name="deltanet_prep",
    )(h, h, h, mod, gain.reshape(1, d), w, conv, alog, dtb)


def _head_sums(x, ones_ref):
    ones = ones_ref[...]
    out = []
    for g in range(x.shape[1] // LANES):
        hi, mid, lo = _split3(x[:, g * LANES:(g + 1) * LANES])
        out.append(_dot(hi, ones) + (_dot(mid, ones) + _dot(lo, ones)))
    return out[0] if len(out) == 1 else jnp.concatenate(out, axis=1)


def _rw_prep_kernel(prev_ref, cur_ref, next_ref, mod_ref, gain_ref, w_ref, mu_ref, w2_ref, w0_ref, a2_ref,
                    a0_ref, g2_ref, kkw_ref, kaw_ref, rk_ref, ones_ref,
                    r_ref, v_ref, kk_ref, lw_ref, kd_ref, kka_ref, gate_ref, bonus_ref, pe_ref,
                    *, ctx_tiles, n_tiles):
    _project_with_halo(prev_ref, cur_ref, next_ref, mod_ref, gain_ref, w_ref, pe_ref,
                       ctx_tiles=ctx_tiles, n_tiles=n_tiles)
    z = pe_ref[pl.ds(HALO, TILE), :]
    zs = 0.5 * (pe_ref[pl.ds(HALO - 1, TILE), :] + pe_ref[pl.ds(HALO + 1, TILE), :])
    s = z + mu_ref[...] * (zs - z)
    r, k7, v7 = s[:, 0:RW_W], s[:, RW_W:2 * RW_W], s[:, 2 * RW_W:3 * RW_W]
    o = 3 * RW_W
    wd, ad, gd = s[:, o:o + LANES], s[:, o + LANES:o + 2 * LANES], s[:, o + 2 * LANES:o + 4 * LANES]
    w_logit = _mm3(jnp.tanh(wd), w2_ref[...]) + w0_ref[...]
    lw = -float(np.exp(-0.5)) * jax.nn.sigmoid(w_logit)
    a = jax.nn.sigmoid(_mm3(ad, a2_ref[...]) + a0_ref[...])
    gate_ref[0] = _mm3(jax.nn.sigmoid(gd), g2_ref[...]).astype(BF16)
    kx = k7 * kkw_ref[...]
    kk = kx * lax.rsqrt(_head_sums(kx * kx, ones_ref) + 1e-6)
    r_ref[0], v_ref[0], kk_ref[0], lw_ref[0] = r, v7, kk, lw
    kd_sum = None
    for d in range(2):
        a_d = a[:, d * RW_W:(d + 1) * RW_W]
        kd = k7 * (1.0 + (a_d - 1.0) * kaw_ref[...])
        kd_ref[0, :, d * RW_W:(d + 1) * RW_W] = kd
        kka_ref[0, :, d * RW_W:(d + 1) * RW_W] = kk * a_d
        kd_sum = kd if kd_sum is None else kd_sum + kd
    bonus_ref[0] = (_head_sums((r * rk_ref[...]) * kd_sum, ones_ref) * v7).astype(BF16)


def _rw_prep_call(h, mod, gain, w, mu, w2, w0, a2, a0, g2, kkw, kaw, rk, ones, *, ctx_tiles):
    n_batch, t_len, d = h.shape
    nt = t_len // TILE
    out = lambda n, dt: jax.ShapeDtypeStruct((n_batch, t_len, n), dt)
    ospec = lambda n: pl.BlockSpec((1, TILE, n), lambda b, t: (b, t, 0))
    widths = [RW_W, RW_W, RW_W, 2 * RW_W, 2 * RW_W, 2 * RW_W, RW_W, RW_W]
    dtypes = [F32] * 6 + [BF16] * 2
    params = [gain.reshape(1, d), w, mu, w2, w0, a2, a0, g2, kkw, kaw, rk, ones]
    return pl.pallas_call(
        functools.partial(_rw_prep_kernel, ctx_tiles=ctx_tiles, n_tiles=nt),
        out_shape=[out(n, dt) for n, dt in zip(widths, dtypes)],
        grid=(n_batch, nt),
        in_specs=_halo_specs(d, ctx_tiles, nt) + [pl.BlockSpec((1, N_MOD, d), _mod_row_map(n_batch, ctx_tiles, 0))]
        + [_resident(p.shape) for p in params],
        out_specs=[ospec(n) for n in widths],
        scratch_shapes=[pltpu.VMEM((TILE + 2 * HALO, w.shape[1]), F32)],
        compiler_params=_cparams("parallel", "parallel"),
        name="rwkv_prep",
    )(h, h, h, mod, *params)


def _chunk_masks(direction, width=CHUNK):
    i = lax.broadcasted_iota(jnp.int32, (CHUNK, width), 0)
    j = lax.broadcasted_iota(jnp.int32, (CHUNK, width), 1) % CHUNK
    return (i >= j, i > j) if direction == 0 else (i <= j, i < j)


def _last_row(x, direction):
    return x[CHUNK - 1:CHUNK] if direction == 0 else x[0:1]


def _bd(x):
    shape = (2 * CHUNK, x.shape[1])
    r = lax.broadcasted_iota(jnp.int32, shape, 0)
    c = lax.broadcasted_iota(jnp.int32, shape, 1)
    return jnp.where((r < CHUNK) == (c < x.shape[1] // 2), jnp.concatenate([x, x], axis=0), 0.0)


def _neumann_inverse_pairs(ns, refine):
    i = lax.broadcasted_iota(jnp.int32, (CHUNK, 2 * CHUNK), 0)
    j = lax.broadcasted_iota(jnp.int32, (CHUNK, 2 * CHUNK), 1) % CHUNK
    eye = jnp.where(i == j, 1.0, 0.0)
    one_pass = lambda a, b: _dot(a.astype(BF16), b.astype(BF16))
    rs, ps = list(ns), list(ns)
    span = 2
    while span < CHUNK:
        ps = [one_pass(p, _bd(p)) for p in ps]
        rs = [r + p + one_pass(r, _bd(p)) for r, p in zip(rs, ps)]
        span *= 2
    if not refine:
        return [eye + r for r in rs]
    res = [_mm3(n, _bd(eye + r)) - r for n, r in zip(ns, rs)]
    return [eye + (r + (e + one_pass(r, _bd(e)))) for r, e in zip(rs, res)]


def _rev_chunk(n, ctx_chunks, n_chunks):
    return jnp.where(n < ctx_chunks, ctx_chunks - 1 - n, n_chunks - 1 + ctx_chunks - n)


def _dn_chunk_kernel(q_ref, k_ref, v_ref, small_ref, u_ref, w_ref, qd_ref, kdt_ref, attn_ref, gl_ref):
    nh2, n_pairs, pw, dk = 2 * DN_HEADS, DN_HEADS // 2, 2 * DN_DK, DN_DK
    first_c = lax.broadcasted_iota(jnp.int32, (CHUNK, 2 * CHUNK), 1) < CHUNK
    first_f = lax.broadcasted_iota(jnp.int32, (CHUNK, pw), 1) < dk

    def cols(x, c, first):
        return jnp.where(first[:x.shape[0]], x[:, c:c + 1], x[:, c + 1:c + 2])

    work = []
    for cc in range(CHUNKS_PER_STEP):
        rows = slice(cc * CHUNK, (cc + 1) * CHUNK)
        sm = small_ref[0, rows]
        q, k, v = q_ref[0, rows] * (DN_DK ** -0.5), k_ref[0, rows], v_ref[0, rows]
        grams = [_mm_nt(jnp.concatenate([k[:, j * pw:(j + 1) * pw], q[:, j * pw:(j + 1) * pw]], axis=0),
                        _bd(k[:, j * pw:(j + 1) * pw])) for j in range(n_pairs)]
        for d in range(2):
            incl, strict = _chunk_masks(d, 2 * CHUNK)
            gc = _mm_exact_lhs(jnp.where(_chunk_masks(d)[0], 1.0, 0.0), sm)
            gc_t = gc.T
            gtot = _last_row(gc, d)
            gl_ref[0, d, cc] = jnp.exp(gtot)
            for j in range(n_pairs):
                c = DN_HEADS * d + 2 * j
                gcr = jnp.concatenate([gc_t[c:c + 1, :], gc_t[c + 1:c + 2, :]], axis=1)
                decay = jnp.exp(jnp.where(incl, cols(gc, c, first_c) - gcr, -1e30))
                lower = jnp.where(strict, (cols(sm, nh2 + c, first_c) * grams[j][:CHUNK]) * decay, 0.0)
                work.append((cc, rows, d, j, c, sm, q, k, v, gc, gtot, decay, grams[j][CHUNK:], -lower))
    t_invs = _neumann_inverse_pairs([item[-1] for item in work], refine=True)
    for (cc, rows, d, j, c, sm, q, k, v, gc, gtot, decay, qk, _), t_inv in zip(work, t_invs):
        sl = slice(j * pw, (j + 1) * pw)
        beta, gcc, gt = cols(sm, nh2 + c, first_f), cols(gc, c, first_f), cols(gtot, c, first_f)
        egc = jnp.exp(gcc)
        kp, qp = k[:, sl], q[:, sl]
        vb, ke = v[:, sl] * beta, (kp * beta) * egc
        rhs = jnp.concatenate([vb[:, :dk], ke[:, :dk], vb[:, dk:], ke[:, dk:]], axis=1)
        sol = _mm(t_inv, _bd(rhs))
        u_ref[0, d, rows, sl] = jnp.concatenate([sol[:, :dk], sol[:, 2 * dk:3 * dk]], axis=1).astype(BF16)
        w_ref[0, d, rows, sl] = jnp.concatenate([sol[:, dk:2 * dk], sol[:, 3 * dk:]], axis=1).astype(BF16)
        qd_ref[0, d, rows, sl] = (qp * egc).astype(BF16)
        k_tail = kp * jnp.exp(gt - gcc)
        kdt_ref[0, d, cc, j * dk:(j + 1) * dk, :] = jnp.concatenate([k_tail[:, :dk].T, k_tail[:, dk:].T], axis=1).astype(BF16)
        attn_ref[0, d, rows, 2 * j * CHUNK:2 * (j + 1) * CHUNK] = (qk * decay).astype(BF16)


def _dn_chunk_call(q, k, v, small):
    n_batch, t_len, wd = q.shape
    nc = t_len // CHUNK
    cps = CHUNKS_PER_STEP
    ispec = lambda n: pl.BlockSpec((1, cps * CHUNK, n), lambda b, c: (b, c, 0))
    ospec = lambda n: pl.BlockSpec((1, 2, cps * CHUNK, n), lambda b, c: (b, 0, c, 0))
    shp = lambda n, dt: jax.ShapeDtypeStruct((n_batch, 2, t_len, n), dt)
    return pl.pallas_call(
        _dn_chunk_kernel,
        out_shape=[shp(wd, BF16), shp(wd, BF16), shp(wd, BF16),
                   jax.ShapeDtypeStruct((n_batch, 2, nc, wd // 2, 2 * CHUNK), BF16), shp(DN_HEADS * CHUNK, BF16),
                   jax.ShapeDtypeStruct((n_batch, 2, nc, 1, LANES), F32)],
        grid=(n_batch, nc // cps),
        in_specs=[ispec(wd), ispec(wd), ispec(wd), ispec(LANES)],
        out_specs=[ospec(wd), ospec(wd), ospec(wd),
                   pl.BlockSpec((1, 2, cps, wd // 2, 2 * CHUNK), lambda b, c: (b, 0, c, 0, 0)), ospec(DN_HEADS * CHUNK),
                   pl.BlockSpec((1, 2, cps, 1, LANES), lambda b, c: (b, 0, c, 0, 0))],
        compiler_params=_cparams("parallel", "parallel"),
        name="deltanet_chunk_prep",
    )(q, k, v, small)


def _dn_scan_kernel(*refs):
    ins, (of_ref, ob_ref, s_ref) = refs[:12], refs[12:]

    @pl.when(pl.program_id(1) == 0)
    def _():
        s_ref[...] = jnp.zeros_like(s_ref)

    dk = DN_DK
    chains = [(b, d, j) for b in range(SCAN_BATCH) for d in range(2) for j in range(DN_HEADS // 2)]
    outs = (of_ref, ob_ref)
    stage1 = []
    for b, d, j in chains:
        u_ref, w_ref, qd_ref = ins[6 * d:6 * d + 3]
        s = s_ref[b, d, j]
        sb = s.astype(BF16)
        halves = [(slice((2 * j + i) * dk, (2 * j + i + 1) * dk), slice(i * dk, (i + 1) * dk)) for i in range(2)]
        v_new = jnp.concatenate([u_ref[b, 0, :, sl] - _dot(w_ref[b, 0, :, sl], sb[:, hl]) for sl, hl in halves], axis=1)
        inter = jnp.concatenate([_dot(qd_ref[b, 0, :, sl], sb[:, hl]) for sl, hl in halves], axis=1)
        stage1.append((s, v_new, inter))
    for (b, d, j), (s, v_new, inter) in zip(chains, stage1):
        kdt_ref, attn_ref, gl_ref = ins[6 * d + 3:6 * d + 6]
        v_bd = _bd(v_new).astype(BF16)
        intra = _dot(attn_ref[b, 0, :, 2 * j * CHUNK:2 * (j + 1) * CHUNK], v_bd)
        outs[d][b, :, 2 * j * dk:2 * (j + 1) * dk] = (inter + intra).astype(BF16)
        c = DN_HEADS * d + 2 * j
        decayed = jnp.concatenate([s[:, i * dk:(i + 1) * dk] * gl_ref[b, 0, 0, :, c + i:c + i + 1] for i in range(2)], axis=1)
        s_ref[b, d, j] = decayed + _dot(kdt_ref[b, 0, 0, j * dk:(j + 1) * dk, :], v_bd)


def _dn_scan_call(u, w, qd, kdt, attn, gl, *, ctx_chunks):
    n_batch, _, t_len, wd = u.shape
    nc = t_len // CHUNK
    in_specs, args = [], []
    for d in range(2):
        chunk = (lambda n: n) if d == 0 else functools.partial(_rev_chunk, ctx_chunks=ctx_chunks, n_chunks=nc)
        per_token = lambda a: pl.BlockSpec((SCAN_BATCH, 1, CHUNK, a.shape[-1]),
                                           lambda b, n, d=d, chunk=chunk: (b, d, chunk(n), 0))
        per_chunk = lambda a: pl.BlockSpec((SCAN_BATCH, 1, 1) + a.shape[3:],
                                           lambda b, n, d=d, chunk=chunk: (b, d, chunk(n), 0, 0))
        in_specs += [per_token(u), per_token(w), per_token(qd), per_chunk(kdt), per_token(attn), per_chunk(gl)]
        args += [u, w, qd, kdt, attn, gl]
    return pl.pallas_call(
        _dn_scan_kernel,
        out_shape=[jax.ShapeDtypeStruct((n_batch, t_len, wd), BF16)] * 2,
        grid=(n_batch // SCAN_BATCH, nc),
        in_specs=in_specs,
        out_specs=[pl.BlockSpec((SCAN_BATCH, CHUNK, wd), lambda b, n: (b, n, 0)),
                   pl.BlockSpec((SCAN_BATCH, CHUNK, wd), lambda b, n: (b, _rev_chunk(n, ctx_chunks, nc), 0))],
        scratch_shapes=[pltpu.VMEM((SCAN_BATCH, 2, DN_HEADS // 2, DN_DK, 2 * DN_DK), F32)],
        compiler_params=_cparams("parallel", "arbitrary"),
        name="deltanet_scan",
    )(*args)


def _rw_chunk_kernel(r_ref, v_ref, kk_ref, lw_ref, kd_ref, kka_ref,
                     ut_ref, wt_ref, rt_ref, arb_ref, kbt_ref, y0_ref, pc_ref, vb_ref):
    pw = 2 * RW_HS
    vb_ref[0] = v_ref[0].astype(BF16)
    prepared = []
    for cc in range(CHUNKS_PER_STEP):
        rows = slice(cc * CHUNK, (cc + 1) * CHUNK)
        r, kk = r_ref[0, rows], kk_ref[0, rows]
        for d in range(2):
            dsl = slice(d * RW_W, (d + 1) * RW_W)
            lw, kd, kka = lw_ref[0, rows, dsl], kd_ref[0, rows, dsl], kka_ref[0, rows, dsl]
            cl = _mm_exact_lhs(jnp.where(_chunk_masks(d)[0], 1.0, 0.0), lw)
            tot = _last_row(cl, d)
            p_inv, p_tail = jnp.exp(-cl), jnp.exp(tot - cl)
            at = -kk * jnp.exp(cl - lw)
            rt = r * jnp.exp(cl)
            rt_ref[0, d, rows] = rt.astype(BF16)
            pc_ref[0, d, cc] = jnp.broadcast_to(jnp.exp(tot), (8, RW_W))
            kbt_ref[0, d, cc] = jnp.concatenate([(kd * p_tail).T, (kka * p_tail).T], axis=1).astype(BF16)
            prepared.append((rows, d, at, rt, kd * p_inv, kka * p_inv))
    work, n_list = [], []
    for rows, d, at, rt, kh, bh in prepared:
        incl, strict = _chunk_masks(d, 2 * CHUNK)
        for j in range(RW_HEADS // 2):
            sl = slice(j * pw, (j + 1) * pw)
            aa = _mm_nt(jnp.concatenate([at[:, sl], rt[:, sl]], axis=0),
                        jnp.concatenate([_bd(bh[:, sl]), _bd(kh[:, sl])], axis=0))
            n_list.append(jnp.where(strict, aa[:CHUNK, :pw], 0.0))
            work.append((rows, d, sl, at[:, sl], jnp.where(strict, aa[:CHUNK, pw:], 0.0),
                         jnp.where(incl, aa[CHUNK:, :pw], 0.0), jnp.where(incl, aa[CHUNK:, pw:], 0.0)))
    t_invs = _neumann_inverse_pairs(n_list, refine=False)
    rhs = []
    for rows, d, sl, at_p, a_ak, a_rb, a_rk in work:
        v_bd = _bd(v_ref[0, rows, sl]).astype(BF16)
        arb_ref[0, d, rows, sl] = a_rb.astype(BF16)
        y0_ref[0, d, rows, sl] = _dot(a_rk.astype(BF16), v_bd).astype(BF16)
        rhs.append(jnp.concatenate([_bd(at_p), _bd(_dot(a_ak.astype(BF16), v_bd))], axis=1))
    for (rows, d, sl, *_), t_inv, x in zip(work, t_invs, rhs):
        sol = _mm(t_inv, x)
        wt_ref[0, d, rows, sl] = sol[:, :pw].astype(BF16)
        ut_ref[0, d, rows, sl] = sol[:, pw:].astype(BF16)


def _rw_chunk_call(r, v, kk, lw, kd, kka):
    n_batch, t_len, wd = r.shape
    nc = t_len // CHUNK
    cps = CHUNKS_PER_STEP
    ispec = lambda n: pl.BlockSpec((1, cps * CHUNK, n), lambda b, c: (b, c, 0))
    ospec = pl.BlockSpec((1, 2, cps * CHUNK, wd), lambda b, c: (b, 0, c, 0))
    shp = jax.ShapeDtypeStruct((n_batch, 2, t_len, wd), BF16)
    return pl.pallas_call(
        _rw_chunk_kernel,
        out_shape=[shp] * 4 + [jax.ShapeDtypeStruct((n_batch, 2, nc, wd, 2 * CHUNK), BF16), shp,
                               jax.ShapeDtypeStruct((n_batch, 2, nc, 8, wd), F32),
                               jax.ShapeDtypeStruct((n_batch, t_len, wd), BF16)],
        grid=(n_batch, nc // cps),
        in_specs=[ispec(wd), ispec(wd), ispec(wd), ispec(2 * wd), ispec(2 * wd), ispec(2 * wd)],
        out_specs=[ospec] * 4 + [pl.BlockSpec((1, 2, cps, wd, 2 * CHUNK), lambda b, c: (b, 0, c, 0, 0)), ospec,
                                 pl.BlockSpec((1, 2, cps, 8, wd), lambda b, c: (b, 0, c, 0, 0)), ispec(wd)],
        compiler_params=_cparams("parallel", "parallel"),
        name="rwkv_chunk_prep",
    )(r, v, kk, lw, kd, kka)


def _rw_scan_kernel(*refs):
    ins, (yf_ref, yb_ref, s_ref) = refs[:16], refs[16:]

    @pl.when(pl.program_id(1) == 0)
    def _():
        s_ref[...] = jnp.zeros_like(s_ref)

    pw = 2 * RW_HS
    r_i = lax.broadcasted_iota(jnp.int32, (pw, pw), 0)
    c_i = lax.broadcasted_iota(jnp.int32, (pw, pw), 1)
    same_head = (r_i < RW_HS) == (c_i < RW_HS)
    chains = [(b, d, j) for b in range(SCAN_BATCH) for d in range(2) for j in range(RW_HEADS // 2)]
    outs = (yf_ref, yb_ref)
    decay_cols = {(b, d): ins[8 * d + 6][b, 0, 0].T for b in range(SCAN_BATCH) for d in range(2)}
    stage1 = []
    for b, d, j in chains:
        ut_ref, wt_ref, rt_ref, _, _, y0_ref = ins[8 * d:8 * d + 6]
        sl = slice(j * pw, (j + 1) * pw)
        s = s_ref[b, d, j]
        sb = s.astype(BF16)
        u = ut_ref[b, 0, :, sl] + _dot(wt_ref[b, 0, :, sl], sb)
        stage1.append((s, u, y0_ref[b, 0, :, sl] + _dot(rt_ref[b, 0, :, sl], sb)))
    for (b, d, j), (s, u, y_inter) in zip(chains, stage1):
        arb_ref, kbt_ref, _, _, v_ref = ins[8 * d + 3:8 * d + 8]
        sl = slice(j * pw, (j + 1) * pw)
        outs[d][b, :, sl] = (y_inter + _dot(arb_ref[b, 0, :, sl], _bd(u).astype(BF16))).astype(BF16)
        grow = _dot(kbt_ref[b, 0, 0, sl, :], jnp.concatenate([v_ref[b, :, sl], u.astype(BF16)], axis=0))
        s_ref[b, d, j] = s * decay_cols[b, d][sl, 0:1] + jnp.where(same_head, grow, 0.0)


def _rw_scan_call(ut, wt, rt, arb, kbt, y0, pc, v, *, ctx_chunks):
    n_batch, _, t_len, wd = ut.shape
    nc = t_len // CHUNK
    in_specs, args = [], []
    for d in range(2):
        chunk = (lambda n: n) if d == 0 else functools.partial(_rev_chunk, ctx_chunks=ctx_chunks, n_chunks=nc)
        per_chunk = lambda a: pl.BlockSpec((SCAN_BATCH, 1, 1) + a.shape[3:],
                                           lambda b, n, d=d, chunk=chunk: (b, d, chunk(n), 0, 0))
        per_token = pl.BlockSpec((SCAN_BATCH, 1, CHUNK, wd), lambda b, n, d=d, chunk=chunk: (b, d, chunk(n), 0))
        in_specs += [per_token] * 4 + [per_chunk(kbt), per_token, per_chunk(pc),
                                       pl.BlockSpec((SCAN_BATCH, CHUNK, wd), lambda b, n, chunk=chunk: (b, chunk(n), 0))]
        args += [ut, wt, rt, arb, kbt, y0, pc, v]
    return pl.pallas_call(
        _rw_scan_kernel,
        out_shape=[jax.ShapeDtypeStruct((n_batch, t_len, wd), BF16)] * 2,
        grid=(n_batch // SCAN_BATCH, nc),
        in_specs=in_specs,
        out_specs=[pl.BlockSpec((SCAN_BATCH, CHUNK, wd), lambda b, n: (b, n, 0)),
                   pl.BlockSpec((SCAN_BATCH, CHUNK, wd), lambda b, n: (b, _rev_chunk(n, ctx_chunks, nc), 0))],
        scratch_shapes=[pltpu.VMEM((SCAN_BATCH, 2, RW_HEADS // 2, 2 * RW_HS, 2 * RW_HS), F32)],
        compiler_params=_cparams("parallel", "arbitrary"),
        name="rwkv_scan",
    )(*args)


def _ev_out_kernel(h_ref, mod_ref, of_ref, ob_ref, dgate_ref, yf_ref, yb_ref, gate7_ref, bonus_ref,
                   dnorm_ref, gnw_ref, gnb_ref, ones_ref, wout_ref, o_ref):
    f32 = lambda ref: ref[0].astype(F32)
    o = f32(of_ref) + f32(ob_ref)
    dgate = f32(dgate_ref)
    parts = []
    for h in range(DN_HEADS):
        sl = slice(h * DN_DK, (h + 1) * DN_DK)
        parts.append(_rms_rows(o[:, sl]) * dnorm_ref[:, sl] * _silu(dgate[:, sl]))
    o_dn = jnp.concatenate(parts, axis=1)
    y = f32(yf_ref) + f32(yb_ref)
    inv_n = 1.0 / RW_HS
    mu = _head_sums(y, ones_ref) * inv_n
    yc = y - mu
    var = _head_sums(yc * yc, ones_ref) * inv_n
    yn = yc * lax.rsqrt(var + RW_GN_EPS) * gnw_ref[...] + gnb_ref[...]
    o_rw = (yn + f32(bonus_ref)) * f32(gate7_ref)
    wd = DN_HEADS * DN_DK
    proj = _dot(o_dn.astype(BF16), wout_ref[0:wd, :]) + _dot(o_rw.astype(BF16), wout_ref[wd:, :])
    o_ref[0] = h_ref[0] + mod_ref[0, 5:6] * proj


def _ev_out_call(h, mod, o_f, o_b, dgate, y_f, y_b, gate7, bonus, dnorm, gnw, gnb, ones, w_out, *, ctx_tiles):
    n_batch, t_len, d = h.shape
    nt = t_len // TILE
    tile = lambda n: pl.BlockSpec((1, TILE, n), lambda b, t: (b, t, 0))
    params = [dnorm.reshape(1, -1), gnw.reshape(1, -1), gnb.reshape(1, -1), ones, w_out]
    streams = [o_f, o_b, dgate, y_f, y_b, gate7, bonus]
    return pl.pallas_call(
        _ev_out_kernel,
        out_shape=jax.ShapeDtypeStruct(h.shape, F32),
        grid=(n_batch, nt),
        in_specs=[tile(d), pl.BlockSpec((1, N_MOD, d), _mod_row_map(n_batch, ctx_tiles, 0))]
        + [tile(s.shape[-1]) for s in streams] + [_resident(p.shape) for p in params],
        out_specs=tile(d),
        compiler_params=_cparams("parallel", "parallel"),
        name="even_mix_out",
    )(h, mod, *streams, *params)


def _rope_layout(width, rot, starts):
    angle = np.full((width,), -1, np.int64)
    first = np.zeros((1, width), np.float32)
    q = rot // 4
    for start in starts:
        for blk in range(2):
            for idx in range(q):
                l1 = start + blk * 2 * q + idx
                angle[l1] = angle[l1 + q] = blk * q + idx
                first[0, l1] = 1.0
    return angle, jnp.asarray(first)


def _rope_tables(n_ctx, n_lat, rot, angle, first):
    rows = n_lat // GRID_W
    row = jnp.repeat(jnp.arange(rows), GRID_W).astype(F32)
    col = jnp.tile(jnp.arange(GRID_W), rows).astype(F32)
    axis_dim = rot // 2
    inv = ROPE_THETA ** (-jnp.arange(0, axis_dim, 2, dtype=F32) / axis_dim)
    ang = jnp.concatenate([row[:, None] * inv, col[:, None] * inv], axis=-1)
    on = jnp.asarray(angle >= 0)
    idx = np.maximum(angle, 0)
    cos = jnp.where(on, jnp.cos(ang)[:, idx], 1.0)
    sin = jnp.where(on, jnp.sin(ang)[:, idx], 0.0) * (1.0 - 2.0 * first)
    width = angle.shape[0]
    return (jnp.concatenate([jnp.ones((n_ctx, width), F32), cos], axis=0),
            jnp.concatenate([jnp.zeros((n_ctx, width), F32), sin], axis=0))


def _rope(x, first_ref, cos_ref, sin_ref, quarter):
    width = x.shape[1]
    partner = jnp.where(first_ref[...] > 0.5, pltpu.roll(x, width - quarter, 1), pltpu.roll(x, quarter, 1))
    return x * cos_ref[...] + partner * sin_ref[...]


def _od_prep_kernel(h_ref, mod_ref, gain_ref, w_ref, qn_ref, kn_ref, mqn_ref, wuq_ref, mkvn_ref, wukv_ref,
                    ones_ref, fq_ref, fk_ref, fm_ref, fr_ref,
                    cq_ref, sq_ref, ck_ref, sk_ref, cm_ref, sm_ref, cr_ref, sr_ref,
                    qg_ref, qm_ref, kgt_ref, vg_ref, kmt_ref, vm_ref):
    xn = _modulate(h_ref[0], gain_ref[...], mod_ref[0, 3:4], mod_ref[0, 4:5]).astype(BF16)
    p = _dot(xn, w_ref[...])
    nq, nkv = GQ_HEADS * GQ_HD, GQ_KV_HEADS * GQ_HD
    o = 0
    q, o = p[:, o:o + nq], o + nq
    k, o = p[:, o:o + nkv], o + nkv
    v, o = p[:, o:o + 2 * nkv], o + 2 * nkv
    n_cq, n_ckv = mqn_ref.shape[1], mkvn_ref.shape[1]
    cq, o = p[:, o:o + n_cq], o + n_cq
    ckv, o = p[:, o:o + n_ckv], o + n_ckv
    kr = p[:, o:o + LANES]
    inv_hd = 1.0 / GQ_HD
    q = q * lax.rsqrt(_head_sums(q * q, ones_ref) * inv_hd + NORM_EPS) * qn_ref[...]
    k = k * lax.rsqrt(_head_sums(k * k, ones_ref) * inv_hd + NORM_EPS) * kn_ref[...]
    qm = _dot((_rms_rows(cq) * mqn_ref[...]).astype(BF16), wuq_ref[...])
    kvm = _dot((_rms_rows(ckv) * mkvn_ref[...]).astype(BF16), wukv_ref[...])
    q = _rope(q, fq_ref, cq_ref, sq_ref, GQ_HD // 4) * (GQ_HD ** -0.5 * LOG2_E)
    k = _rope(k, fk_ref, ck_ref, sk_ref, GQ_HD // 4)
    qm = _rope(qm, fm_ref, cm_ref, sm_ref, ML_ROPE // 4) * ((ML_NOPE + ML_ROPE) ** -0.5 * LOG2_E)
    kr = _rope(kr, fr_ref, cr_ref, sr_ref, ML_ROPE // 4)
    qg_ref[0] = q.astype(BF16)
    qm_ref[0] = qm.astype(BF16)
    n_nope = ML_HEADS * ML_NOPE
    for ref, val in ((vg_ref, v), (vm_ref, kvm[:, n_nope:])):
        upper = lax.broadcasted_iota(jnp.int32, (1, val.shape[1]), 1) % LANES >= LANES // 2
        ref[0] = (val + jnp.where(upper, 1.0, 0.0)).astype(BF16)
    kgt_ref[0] = k.T.astype(BF16)
    knt = kvm[:, :n_nope].T.astype(BF16)
    krt = kr.T[:ML_ROPE].astype(BF16)
    dk = ML_NOPE + ML_ROPE
    for h in range(ML_HEADS):
        kmt_ref[0, h * dk:h * dk + ML_NOPE, :] = knt[h * ML_NOPE:(h + 1) * ML_NOPE]
        kmt_ref[0, h * dk + ML_NOPE:(h + 1) * dk, :] = krt


def _od_prep_call(h, mod, gain, params, tables, *, ctx_tiles):
    n_batch, t_len, d = h.shape
    nt = t_len // TILE
    nq, nkv = GQ_HEADS * GQ_HD, GQ_KV_HEADS * GQ_HD
    dk = ML_NOPE + ML_ROPE
    tile = lambda n: pl.BlockSpec((1, TILE, n), lambda b, t: (b, t, 0))
    tile_t = lambda n: pl.BlockSpec((1, n, TILE), lambda b, t: (b, 0, t))
    tab = lambda a: pl.BlockSpec((TILE, a.shape[1]), lambda b, t: (t, 0))
    shp = lambda *s: jax.ShapeDtypeStruct((n_batch,) + s, BF16)
    return pl.pallas_call(
        _od_prep_kernel,
        out_shape=[shp(t_len, nq), shp(t_len, ML_HEADS * dk), shp(nkv, t_len), shp(t_len, GQ_KV_HEADS * LANES),
                   shp(ML_HEADS * dk, t_len), shp(t_len, ML_HEADS * LANES)],
        grid=(n_batch, nt),
        in_specs=[tile(d), pl.BlockSpec((1, N_MOD, d), _mod_row_map(n_batch, ctx_tiles, 0)), _resident((1, d))]
        + [_resident(p.shape) for p in params] + [tab(a) for a in tables],
        out_specs=[tile(nq), tile(ML_HEADS * dk), tile_t(nkv), tile(GQ_KV_HEADS * LANES), tile_t(ML_HEADS * dk),
                   tile(ML_HEADS * LANES)],
        compiler_params=_cparams("parallel", "parallel"),
        name="attn_prep",
    )(h, mod, gain.reshape(1, d), *params, *tables)


def _odd_layer_weights(od_w_in, gq_q_norm, gq_k_norm, ml_q_norm, ml_w_uq, ml_kv_norm, ml_w_ukv):
    d = od_w_in.shape[0]
    nq, nkv = GQ_HEADS * GQ_HD, GQ_KV_HEADS * GQ_HD

    def slabs(cols, heads, width):
        cols = cols.reshape(cols.shape[0], heads, width)
        return jnp.concatenate([cols, jnp.zeros(cols.shape[:2] + (LANES - width,), F32)], axis=2).reshape(cols.shape[0], -1)

    w = jnp.concatenate([od_w_in[:, :nq + nkv], slabs(od_w_in[:, nq + nkv:nq + 2 * nkv], GQ_KV_HEADS, GQ_HD),
                         od_w_in[:, nq + 2 * nkv:], jnp.zeros((d, LANES - ML_ROPE), F32)], axis=1)
    ukv = ml_w_ukv.reshape(ml_w_ukv.shape[0], ML_HEADS, ML_NOPE + ML_V)
    ukv = jnp.concatenate([ukv[:, :, :ML_NOPE].reshape(-1, ML_HEADS * ML_NOPE),
                           slabs(ukv[:, :, ML_NOPE:].reshape(-1, ML_HEADS * ML_V), ML_HEADS, ML_V)], axis=1)
    dk = ML_NOPE + ML_ROPE
    layouts = [(GQ_HD,) + _rope_layout(GQ_HEADS * GQ_HD, GQ_HD, [h * GQ_HD for h in range(GQ_HEADS)]),
               (GQ_HD,) + _rope_layout(GQ_KV_HEADS * GQ_HD, GQ_HD, [h * GQ_HD for h in range(GQ_KV_HEADS)]),
               (ML_ROPE,) + _rope_layout(ML_HEADS * dk, ML_ROPE, [h * dk + ML_NOPE for h in range(ML_HEADS)]),
               (ML_ROPE,) + _rope_layout(LANES, ML_ROPE, [0])]
    params = [w.astype(BF16), jnp.tile(gq_q_norm, GQ_HEADS).reshape(1, -1), jnp.tile(gq_k_norm, GQ_KV_HEADS).reshape(1, -1),
              ml_q_norm.reshape(1, -1), ml_w_uq.astype(BF16), ml_kv_norm.reshape(1, -1), ukv.astype(BF16),
              _block_ones(LANES, GQ_HD)] + [first for _, _, first in layouts]
    return params, layouts


def _softmax_pv(s, v_slab, width):
    m = jnp.max(s, axis=-1, keepdims=True)
    p = jnp.exp2((s - m).astype(BF16))
    pv = _dot(p, v_slab)
    return pv[:, :width] / pv[:, width:width + 1]


def _attn_kernel(h_ref, mod_ref, qg_ref, qm_ref, kgt_ref, vg_ref, kmt_ref, vm_ref, wout_ref, o_ref):
    group = GQ_HEADS // GQ_KV_HEADS
    dk = ML_NOPE + ML_ROPE

    def logits(h):
        if h < GQ_HEADS:
            g = h // group
            return _dot(qg_ref[0, :, h * GQ_HD:(h + 1) * GQ_HD], kgt_ref[0, g * GQ_HD:(g + 1) * GQ_HD, :])
        h -= GQ_HEADS
        return _dot(qm_ref[0, :, h * dk:(h + 1) * dk], kmt_ref[0, h * dk:(h + 1) * dk, :])

    def values(h):
        if h < GQ_HEADS:
            g = h // group
            return vg_ref[0, :, g * LANES:(g + 1) * LANES], GQ_HD
        h -= GQ_HEADS
        return vm_ref[0, :, h * LANES:(h + 1) * LANES], ML_V

    n_heads = GQ_HEADS + ML_HEADS
    parts, s = [], logits(0)
    for h in range(n_heads):
        s_next = logits(h + 1) if h + 1 < n_heads else None
        parts.append(_softmax_pv(s, *values(h)))
        s = s_next
    ol = jnp.concatenate(parts, axis=1).astype(BF16)
    o_ref[0] = h_ref[0] + mod_ref[0, 5:6] * _dot(ol, wout_ref[...])


def _attn_call(h, mod, qg, qm, kgt, vg, kmt, vm, w_out, *, ctx_tiles):
    n_batch, t_len, d = h.shape
    nt = t_len // TILE - ctx_tiles
    qtile = lambda n: pl.BlockSpec((1, TILE, n), lambda b, t: (b, t + ctx_tiles, 0))
    whole = lambda a: pl.BlockSpec((1,) + a.shape[1:], lambda b, t: (b, 0, 0))
    return pl.pallas_call(
        _attn_kernel,
        out_shape=jax.ShapeDtypeStruct((n_batch, nt * TILE, d), F32),
        grid=(n_batch, nt),
        in_specs=[qtile(d), pl.BlockSpec((1, N_MOD, d), lambda b, t: (b, 0, 0)), qtile(qg.shape[-1]), qtile(qm.shape[-1]),
                  whole(kgt), whole(vg), whole(kmt), whole(vm), _resident(w_out.shape)],
        out_specs=pl.BlockSpec((1, TILE, d), lambda b, t: (b, t, 0)),
        compiler_params=_cparams("parallel", "parallel"),
        name="attention_out",
    )(h, mod, qg, qm, kgt, vg, kmt, vm, w_out)


def _block_ones(n, blk):
    i = np.arange(n) // blk
    return jnp.asarray(i[:, None] == i[None, :], BF16)


def _even_layer_weights(ev_w_in, dn_conv, dn_a_log, dn_dt_bias, rw_mu, rw_w0, rw_w2, rw_a0, rw_a2, rw_g2,
                        rw_kk, rw_ka, rw_rk):
    d = ev_w_in.shape[0]
    n_dn = 4 * DN_HEADS * DN_DK
    nh2 = 2 * DN_HEADS
    slab0 = n_dn + 2 * nh2
    zeros = lambda n: jnp.zeros((d, n), F32)
    w_dn = jnp.concatenate([ev_w_in[:, :n_dn], ev_w_in[:, n_dn:slab0], zeros(LANES - 2 * nh2)], axis=1)
    slab = ev_w_in[:, slab0:]
    o = 3 * RW_W
    lora = 2 * RW_W_LORA
    gpad = 2 * LANES - RW_G_LORA
    w_rw = jnp.concatenate([slab[:, :o + 2 * lora + RW_G_LORA], zeros(gpad)], axis=1)
    mu = jnp.concatenate([rw_mu, jnp.zeros((gpad,), F32)]).reshape(1, -1)
    pad_lanes = lambda v: jnp.zeros((1, LANES), F32).at[0, :v.size].set(v.reshape(-1))

    def dir_blocks(m):
        z = jnp.zeros_like(m[0])
        return jnp.concatenate([jnp.concatenate([m[0], z], axis=1), jnp.concatenate([z, m[1]], axis=1)], axis=0)

    g2 = jnp.concatenate([rw_g2, jnp.zeros((gpad, RW_W), F32)], axis=0)
    return dict(
        w_dn=w_dn.astype(BF16), conv=dn_conv, alog=pad_lanes(dn_a_log), dtb=pad_lanes(dn_dt_bias),
        w_rw=w_rw.astype(BF16), mu=mu, w2=dir_blocks(rw_w2), w0=rw_w0.reshape(1, -1), a2=dir_blocks(rw_a2),
        a0=rw_a0.reshape(1, -1), g2=g2, kkw=rw_kk.reshape(1, -1), kaw=rw_ka.reshape(1, -1),
        rk=rw_rk.reshape(1, -1), ones=_block_ones(LANES, RW_HS))


def kernel(x, c, ctx, c_ctx, mod_w, mod_b, norm_ffn1, norm_mix, norm_ffn2, ffn1_w1, ffn1_w3, ffn1_w2, ffn2_w1, ffn2_w3, ffn2_w2, ev_w_in, ev_w_out, dn_conv, dn_a_log, dn_dt_bias, dn_norm, rw_mu, rw_w0, rw_w2, rw_a0, rw_a2, rw_g2, rw_kk, rw_ka, rw_rk, rw_gn_w, rw_gn_b, od_w_in, od_w_out, gq_q_norm, gq_k_norm, ml_q_norm, ml_w_uq, ml_kv_norm, ml_w_ukv, final_norm):
    n_batch, n_lat, d = x.shape
    n_ctx = ctx.shape[1]
    depth = mod_w.shape[0]
    assert n_ctx % TILE == 0 and n_lat % TILE == 0 and n_lat % GRID_W == 0 and n_batch % SCAN_BATCH == 0
    assert depth % 2 == 0 and depth // 2 == od_w_in.shape[0] == 1, "supported stack: [recurrent, attention]"
    ctx_tiles, ctx_chunks = n_ctx // TILE, n_ctx // CHUNK
    bf = lambda a: a.astype(BF16)

    mod = _all_mod(c, c_ctx, mod_w, mod_b)
    h = (ctx, x)
    for i in range(depth):
        j = i // 2
        last = i == depth - 1
        h = _ffn_call(h, mod[i], norm_ffn1[i], bf(ffn1_w1[i]), bf(ffn1_w3[i]), bf(ffn1_w2[i]), j0=0, ctx_tiles=ctx_tiles)
        if i % 2 == 0:
            w = _even_layer_weights(ev_w_in[j], dn_conv[j], dn_a_log[j], dn_dt_bias[j], rw_mu[j], rw_w0[j], rw_w2[j],
                                    rw_a0[j], rw_a2[j], rw_g2[j], rw_kk[j], rw_ka[j], rw_rk[j])
            q, k, v, small, dgate = _dn_prep_call(h, mod[i], norm_mix[i], w["w_dn"], w["conv"], w["alog"], w["dtb"],
                                                  ctx_tiles=ctx_tiles)
            o_f, o_b = _dn_scan_call(*_dn_chunk_call(q, k, v, small), ctx_chunks=ctx_chunks)
            r, v7, kk, lw, kd, kka, gate7, bonus = _rw_prep_call(
                h, mod[i], norm_mix[i], w["w_rw"], w["mu"], w["w2"], w["w0"], w["a2"], w["a0"], w["g2"], w["kkw"],
                w["kaw"], w["rk"], w["ones"], ctx_tiles=ctx_tiles)
            y_f, y_b = _rw_scan_call(*_rw_chunk_call(r, v7, kk, lw, kd, kka), ctx_chunks=ctx_chunks)
            h = _ev_out_call(h, mod[i], o_f, o_b, dgate, y_f, y_b, gate7, bonus, jnp.tile(dn_norm[j], DN_HEADS),
                             rw_gn_w[j], rw_gn_b[j], w["ones"], bf(ev_w_out[j]), ctx_tiles=ctx_tiles)
            h = _ffn_call(h, mod[i], norm_ffn2[i], bf(ffn2_w1[i]), bf(ffn2_w3[i]), bf(ffn2_w2[i]), j0=6,
                          ctx_tiles=ctx_tiles)
        else:
            params, layouts = _odd_layer_weights(od_w_in[j], gq_q_norm[j], gq_k_norm[j], ml_q_norm[j], ml_w_uq[j],
                                                 ml_kv_norm[j], ml_w_ukv[j])
            tables = [t for rot, angle, first in layouts for t in _rope_tables(n_ctx, n_lat, rot, angle, first)]
            qg, qm, kgt, vg, kmt, vm = _od_prep_call(h, mod[i], norm_mix[i], params, tables, ctx_tiles=ctx_tiles)
            hl = _attn_call(h, mod[i], qg, qm, kgt, vg, kmt, vm, bf(od_w_out[j]), ctx_tiles=ctx_tiles)
            assert last
            h = _ffn_call(hl, mod[i], norm_ffn2[i], bf(ffn2_w1[i]), bf(ffn2_w3[i]), bf(ffn2_w2[i]), j0=6, ctx_tiles=0,
                          final_gain=final_norm)
    return h
```

```python
import functools

import jax
import jax.numpy as jnp
import numpy as np
from jax import lax
from jax.experimental import pallas as pl
from jax.experimental.pallas import tpu as pltpu

F32 = jnp.float32
BF16 = jnp.bfloat16

NORM_EPS = 1e-6
ROPE_THETA = 10000.0
GRID_W = 64
N_MOD = 9

DN_HEADS = 4
DN_DK = 128
DN_CONV = 5
RW_HEADS = 8
RW_HS = 64
RW_W = RW_HEADS * RW_HS
RW_W_LORA = 64
RW_A_LORA = 64
RW_G_LORA = 160
RW_GN_EPS = 64e-5
GQ_HEADS = 8
GQ_KV_HEADS = 2
GQ_HD = 64
ML_HEADS = 8
ML_NOPE = 64
ML_ROPE = 32
ML_V = 64

TILE = 256
CHUNK = 64
HALO = 8
LANES = 128
MXU_DIM = 256
VMEM_LIMIT = 56 * 1024 * 1024
SCAN_BATCH = 4
ATT_TILES = 2
CHUNKS_PER_STEP = 4
LOG2_E = 1.4426950408889634


def _cparams(*sem):
    return pltpu.CompilerParams(dimension_semantics=sem, vmem_limit_bytes=VMEM_LIMIT)


def _resident(shape):
    nd = len(shape)
    return pl.BlockSpec(shape, lambda *_: (0,) * nd, pipeline_mode=pl.Buffered(1))


def _mm(a, b):
    return jnp.dot(a.astype(BF16), b.astype(BF16), preferred_element_type=F32)


def _mm_nt(a, b):
    return lax.dot_general(a.astype(BF16), b.astype(BF16), (((1,), (1,)), ((), ())),
                           preferred_element_type=F32)


def _mm_tn(a, b):
    return lax.dot_general(a.astype(BF16), b.astype(BF16), (((0,), (0,)), ((), ())),
                           preferred_element_type=F32)


def _split2(x):
    hi = x.astype(BF16)
    lo = (x - hi.astype(F32)).astype(BF16)
    return hi, lo


def _split3(x):
    hi = x.astype(BF16)
    r = x - hi.astype(F32)
    mid = r.astype(BF16)
    lo = (r - mid.astype(F32)).astype(BF16)
    return hi, mid, lo


def _dot(a, b):
    return jnp.dot(a, b, preferred_element_type=F32)


def _mm3s(asp, bsp):
    (ah, al), (bh, bl) = asp, bsp
    return _dot(ah, bh) + (_dot(ah, bl) + _dot(al, bh))


def _mm3(a, b):
    return _mm3s(_split2(a), _split2(b))


def _mm2(a, b):
    ah, al = _split2(a)
    bb = b.astype(BF16)
    return _dot(ah, bb) + _dot(al, bb)


def _mm_exact_lhs(a01, b):
    a = a01.astype(BF16)
    hi, mid, lo = _split3(b)
    return _dot(a, hi) + (_dot(a, mid) + _dot(a, lo))


def _rms_rows(x):
    return x * lax.rsqrt(jnp.mean(x * x, axis=-1, keepdims=True) + NORM_EPS)


def _modulate(x, gain, shift, scale):
    return (_rms_rows(x) * gain) * (1.0 + scale) + shift


def _silu(x):
    return x * jax.nn.sigmoid(x)


def _softplus(x):
    return jnp.maximum(x, 0.0) + jnp.log1p(jnp.exp(-jnp.abs(x)))


def _mod_kernel(c_ref, w_ref, b_ref, o_ref):
    s = _silu(c_ref[...])
    o_ref[0] = _mm3(s, w_ref[0]) + b_ref[0]


def _mod_call(cc, mod_w, mod_b):
    n_layers, d, n = mod_w.shape
    r = cc.shape[0]
    tn = n // 8
    return pl.pallas_call(
        _mod_kernel,
        out_shape=jax.ShapeDtypeStruct((n_layers, r, n), F32),
        grid=(n_layers, n // tn),
        in_specs=[pl.BlockSpec((r, d), lambda l, j: (0, 0)),
                  pl.BlockSpec((1, d, tn), lambda l, j: (l, 0, j)),
                  pl.BlockSpec((1, 1, tn), lambda l, j: (l, 0, j))],
        out_specs=pl.BlockSpec((1, r, tn), lambda l, j: (l, 0, j)),
        compiler_params=_cparams("parallel", "parallel"),
        name="adaln_mod",
    )(cc, mod_w, mod_b.reshape(n_layers, 1, n))


def _all_mod(c, c_ctx, mod_w, mod_b):
    n_batch, d = c.shape
    rows = -(-(n_batch + 1) // 8) * 8
    cc = jnp.zeros((rows, d), F32).at[:n_batch].set(c).at[n_batch].set(c_ctx)
    return _mod_call(cc, mod_w, mod_b).reshape(mod_w.shape[0], rows, N_MOD, d)


def _mod_row_map(n_batch, ctx_tiles, t_off):
    def index_map(b, t):
        return (jnp.where(t + t_off < ctx_tiles, n_batch, b), 0, 0)
    return index_map


def _ffn_kernel(*refs, j0, final, split_tiles, sub, ctx_tiles, t_off):
    if split_tiles:
        ctx_ref, lat_ref, *refs = refs
        xs = [jnp.where(pl.program_id(1) < split_tiles, ctx_ref[0], lat_ref[0])]
    else:
        h_ref, *refs = refs
        xs = [h_ref[0, i * TILE:(i + 1) * TILE] for i in range(sub)]
    mod_ctx_ref, mod_ref, gain_ref, w1_ref, w3_ref, w2_ref, *rest = refs
    o_ref = rest[-1]
    f = w1_ref.shape[1]
    cut = -(-(f // MXU_DIM) // 2) * MXU_DIM
    fcs = [slice(0, cut), slice(cut, f)] if 0 < cut < f else [slice(0, f)]
    for i, x in enumerate(xs):
        is_ctx = (pl.program_id(1) * sub + i + t_off) < ctx_tiles
        row = lambda j: jnp.where(is_ctx, mod_ctx_ref[0, j:j + 1], mod_ref[0, j:j + 1])
        xn = _modulate(x, gain_ref[...], row(j0), row(j0 + 1)).astype(BF16)
        ups = [(_dot(xn, w1_ref[:, fc]), _dot(xn, w3_ref[:, fc])) for fc in fcs]
        y = None
        for (a, b), fc in zip(ups, fcs):
            part = _dot((_silu(a) * b).astype(BF16), w2_ref[fc, :])
            y = part if y is None else y + part
        y = x + (0.5 * row(j0 + 2)) * y
        if final:
            y = _rms_rows(y) * rest[0][...]
        o_ref[0, i * TILE:(i + 1) * TILE] = y


def _ffn_call(h, mod, gain, w1, w3, w2, *, j0, ctx_tiles, t_off=0, final_gain=None):
    split = isinstance(h, tuple)
    if split:
        ctx, lat = h
        n_batch, _, d = lat.shape
        t_len = ctx.shape[1] + lat.shape[1]
        last_ctx = ctx_tiles - 1
        sub = 1
        streams = [ctx, lat]
        stream_specs = [pl.BlockSpec((1, TILE, d), lambda b, t: (b, jnp.minimum(t, last_ctx), 0)),
                        pl.BlockSpec((1, TILE, d), lambda b, t: (b, jnp.maximum(t - ctx_tiles, 0), 0))]
    else:
        n_batch, t_len, d = h.shape
        nt_all = t_len // TILE - t_off
        sub = next(s for s in (3, 2, 1) if nt_all % s == 0 and t_off % s == 0)
        streams = [h]
        stream_specs = [pl.BlockSpec((1, sub * TILE, d), lambda b, t: (b, t + t_off // sub, 0))]
    f = w1.shape[1]
    nt = t_len // TILE - t_off
    final = final_gain is not None
    in_specs = stream_specs + [pl.BlockSpec((1, N_MOD, d), lambda b, t: (n_batch, 0, 0)),
                               pl.BlockSpec((1, N_MOD, d), lambda b, t: (b, 0, 0)),
                               _resident((1, d)), _resident((d, f)), _resident((d, f)), _resident((f, d))]
    args = streams + [mod, mod, gain.reshape(1, d), w1, w3, w2]
    if final:
        in_specs.append(_resident((1, d)))
        args.append(final_gain.reshape(1, d))
    return pl.pallas_call(
        functools.partial(_ffn_kernel, j0=j0, final=final, split_tiles=ctx_tiles if split else 0, sub=sub,
                          ctx_tiles=ctx_tiles, t_off=t_off),
        out_shape=jax.ShapeDtypeStruct((n_batch, nt * TILE, d), F32),
        grid=(n_batch, nt // sub),
        in_specs=in_specs,
        out_specs=pl.BlockSpec((1, sub * TILE, d), lambda b, t: (b, t, 0)),
        compiler_params=_cparams("parallel", "parallel"),
        name="macaron_ffn",
    )(*args)


def _halo_specs(d, ctx_tiles, n_tiles):
    per = TILE // HALO
    last = n_tiles * per - 1
    return [pl.BlockSpec((1, HALO, d), lambda b, t: (b, jnp.maximum(t * per - 1, 0), 0)),
            pl.BlockSpec((1, TILE, d), lambda b, t: (b, t, 0)),
            pl.BlockSpec((1, HALO, d), lambda b, t: (b, jnp.minimum((t + 1) * per, last), 0))]


def _project_with_halo(prev_ref, cur_ref, next_ref, mod_ref, gain_ref, w_ref, pe_ref, *, ctx_tiles, n_tiles):
    t = pl.program_id(1)
    xe = jnp.concatenate([prev_ref[0], cur_ref[0], next_ref[0]], axis=0)
    xn = _modulate(xe, gain_ref[...], mod_ref[0, 3:4], mod_ref[0, 4:5]).astype(BF16)
    p = _dot(xn, w_ref[...])
    row = lax.broadcasted_iota(jnp.int32, (TILE + 2 * HALO, 1), 0)
    prev_ok = jnp.logical_and(t > 0, t != ctx_tiles)
    next_ok = jnp.logical_and(t + 1 < n_tiles, t + 1 != ctx_tiles)
    keep = jnp.logical_and(jnp.logical_or(row >= HALO, prev_ok),
                           jnp.logical_or(row < HALO + TILE, next_ok))
    pe_ref[...] = jnp.where(keep, p, 0.0)


def _dn_prep_kernel(prev_ref, cur_ref, next_ref, mod_ref, gain_ref, w_ref, conv_ref, alog_ref, dtb_ref,
                    q_ref, k_ref, v_ref, small_ref, gate_ref, pe_ref, *, ctx_tiles, n_tiles):
    _project_with_halo(prev_ref, cur_ref, next_ref, mod_ref, gain_ref, w_ref, pe_ref,
                       ctx_tiles=ctx_tiles, n_tiles=n_tiles)
    nqkv = 3 * DN_HEADS * DN_DK
    half = DN_CONV // 2
    acc = None
    for j in range(DN_CONV):
        term = conv_ref[j:j + 1, :] * pe_ref[pl.ds(HALO - half + j, TILE), 0:nqkv]
        acc = term if acc is None else acc + term
    qkv = _silu(acc)
    w = DN_HEADS * DN_DK
    for idx, ref in ((0, q_ref), (1, k_ref)):
        for h in range(DN_HEADS):
            seg = qkv[:, idx * w + h * DN_DK: idx * w + (h + 1) * DN_DK]
            ref[0, :, h * DN_DK:(h + 1) * DN_DK] = seg * lax.rsqrt(jnp.sum(seg * seg, axis=-1, keepdims=True) + 1e-6)
    v_ref[0] = qkv[:, 2 * w:3 * w]
    gate_ref[0] = pe_ref[pl.ds(HALO, TILE), nqkv:nqkv + w].astype(BF16)
    ab = pe_ref[pl.ds(HALO, TILE), nqkv + w:nqkv + w + LANES]
    g = -jnp.exp(alog_ref[...]) * _softplus(ab + dtb_ref[...])
    lane = lax.broadcasted_iota(jnp.int32, ab.shape, 1)
    nh2 = 2 * DN_HEADS
    small_ref[0] = jnp.where(lane < nh2, g, jnp.where(lane < 2 * nh2, jax.nn.sigmoid(ab), 0.0))


def _dn_prep_call(h, mod, gain, w, conv, alog, dtb, *, ctx_tiles):
    n_batch, t_len, d = h.shape
    nt = t_len // TILE
    wd = DN_HEADS * DN_DK
    out = lambda n, dt=F32: jax.ShapeDtypeStruct((n_batch, t_len, n), dt)
    ospec = lambda n: pl.BlockSpec((1, TILE, n), lambda b, t: (b, t, 0))
    return pl.pallas_call(
        functools.partial(_dn_prep_kernel, ctx_tiles=ctx_tiles, n_tiles=nt),
        out_shape=[out(wd), out(wd), out(wd), out(LANES), out(wd, BF16)],
        grid=(n_batch, nt),
        in_specs=_halo_specs(d, ctx_tiles, nt) + [
            pl.BlockSpec((1, N_MOD, d), _mod_row_map(n_batch, ctx_tiles, 0)),
            _resident((1, d)), _resident(w.shape), _resident(conv.shape),
            _resident((1, LANES)), _resident((1, LANES))],
        out_specs=[ospec(wd), ospec(wd), ospec(wd), ospec(LANES), ospec(wd)],
        scratch_shapes=[pltpu.VMEM((TILE + 2 * HALO, w.shape[1]), F32)],
        compiler_params=_cparams("parallel", "parallel"),
        name="deltanet_prep",
    )(h, h, h, mod, gain.reshape(1, d), w, conv, alog, dtb)


def _head_sums(x, ones_ref):
    ones = ones_ref[...]
    out = []
    for g in range(x.shape[1] // LANES):
        hi, lo = _split2(x[:, g * LANES:(g + 1) * LANES])
        out.append(_dot(hi, ones) + _dot(lo, ones))
    return out[0] if len(out) == 1 else jnp.concatenate(out, axis=1)


def _rw_prep_kernel(prev_ref, cur_ref, next_ref, mod_ref, gain_ref, w_ref, mu_ref, w2_ref, w0_ref, a2_ref,
                    a0_ref, g2_ref, kkw_ref, kaw_ref, rk_ref, ones_ref,
                    r_ref, v_ref, kk_ref, lw_ref, kd_ref, kka_ref, gate_ref, bonus_ref, pe_ref,
                    *, ctx_tiles, n_tiles):
    _project_with_halo(prev_ref, cur_ref, next_ref, mod_ref, gain_ref, w_ref, pe_ref,
                       ctx_tiles=ctx_tiles, n_tiles=n_tiles)
    z = pe_ref[pl.ds(HALO, TILE), :]
    zs = 0.5 * (pe_ref[pl.ds(HALO - 1, TILE), :] + pe_ref[pl.ds(HALO + 1, TILE), :])
    s = z + mu_ref[...] * (zs - z)
    r, k7, v7 = s[:, 0:RW_W], s[:, RW_W:2 * RW_W], s[:, 2 * RW_W:3 * RW_W]
    o = 3 * RW_W
    wd, ad, gd = s[:, o:o + LANES], s[:, o + LANES:o + 2 * LANES], s[:, o + 2 * LANES:o + 4 * LANES]
    w_logit = _mm3(jnp.tanh(wd), w2_ref[...]) + w0_ref[...]
    lw = -float(np.exp(-0.5)) * jax.nn.sigmoid(w_logit)
    a = jax.nn.sigmoid(_mm3(ad, a2_ref[...]) + a0_ref[...])
    gate_ref[0] = _mm3(jax.nn.sigmoid(gd), g2_ref[...]).astype(BF16)
    kx = k7 * kkw_ref[...]
    kk = kx * lax.rsqrt(_head_sums(kx * kx, ones_ref) + 1e-6)
    r_ref[0], v_ref[0], kk_ref[0], lw_ref[0] = r, v7, kk, lw
    kd_sum = None
    for d in range(2):
        a_d = a[:, d * RW_W:(d + 1) * RW_W]
        kd = k7 * (1.0 + (a_d - 1.0) * kaw_ref[...])
        kd_ref[0, :, d * RW_W:(d + 1) * RW_W] = kd
        kka_ref[0, :, d * RW_W:(d + 1) * RW_W] = kk * a_d
        kd_sum = kd if kd_sum is None else kd_sum + kd
    bonus_ref[0] = (_head_sums((r * rk_ref[...]) * kd_sum, ones_ref) * v7).astype(BF16)


def _rw_prep_call(h, mod, gain, w, mu, w2, w0, a2, a0, g2, kkw, kaw, rk, ones, *, ctx_tiles):
    n_batch, t_len, d = h.shape
    nt = t_len // TILE
    out = lambda n, dt: jax.ShapeDtypeStruct((n_batch, t_len, n), dt)
    ospec = lambda n: pl.BlockSpec((1, TILE, n), lambda b, t: (b, t, 0))
    widths = [RW_W, RW_W, RW_W, 2 * RW_W, 2 * RW_W, 2 * RW_W, RW_W, RW_W]
    dtypes = [F32] * 6 + [BF16] * 2
    params = [gain.reshape(1, d), w, mu, w2, w0, a2, a0, g2, kkw, kaw, rk, ones]
    return pl.pallas_call(
        functools.partial(_rw_prep_kernel, ctx_tiles=ctx_tiles, n_tiles=nt),
        out_shape=[out(n, dt) for n, dt in zip(widths, dtypes)],
        grid=(n_batch, nt),
        in_specs=_halo_specs(d, ctx_tiles, nt) + [pl.BlockSpec((1, N_MOD, d), _mod_row_map(n_batch, ctx_tiles, 0))]
        + [_resident(p.shape) for p in params],
        out_specs=[ospec(n) for n in widths],
        scratch_shapes=[pltpu.VMEM((TILE + 2 * HALO, w.shape[1]), F32)],
        compiler_params=_cparams("parallel", "parallel"),
        name="rwkv_prep",
    )(h, h, h, mod, *params)


def _chunk_masks(direction, width=CHUNK):
    i = lax.broadcasted_iota(jnp.int32, (CHUNK, width), 0)
    j = lax.broadcasted_iota(jnp.int32, (CHUNK, width), 1) % CHUNK
    return (i >= j, i > j) if direction == 0 else (i <= j, i < j)


def _last_row(x, direction):
    return x[CHUNK - 1:CHUNK] if direction == 0 else x[0:1]


def _bd(x):
    shape = (2 * CHUNK, x.shape[1])
    r = lax.broadcasted_iota(jnp.int32, shape, 0)
    c = lax.broadcasted_iota(jnp.int32, shape, 1)
    return jnp.where((r < CHUNK) == (c < x.shape[1] // 2), jnp.concatenate([x, x], axis=0), 0.0)


def _neumann_inverse_pairs(ns, refine):
    i = lax.broadcasted_iota(jnp.int32, (CHUNK, 2 * CHUNK), 0)
    j = lax.broadcasted_iota(jnp.int32, (CHUNK, 2 * CHUNK), 1) % CHUNK
    eye = jnp.where(i == j, 1.0, 0.0)
    one_pass = lambda a, b: _dot(a.astype(BF16), b.astype(BF16))
    rs, ps = list(ns), list(ns)
    span = 2
    while span < CHUNK:
        ps = [one_pass(p, _bd(p)) for p in ps]
        rs = [r + p + one_pass(r, _bd(p)) for r, p in zip(rs, ps)]
        span *= 2
    if not refine:
        return [eye + r for r in rs]
    res = [_mm3(n, _bd(eye + r)) - r for n, r in zip(ns, rs)]
    return [eye + (r + (e + one_pass(r, _bd(e)))) for r, e in zip(rs, res)]


def _rev_chunk(n, ctx_chunks, n_chunks):
    return jnp.where(n < ctx_chunks, ctx_chunks - 1 - n, n_chunks - 1 + ctx_chunks - n)


def _dn_chunk_kernel(q_ref, k_ref, v_ref, small_ref, u_ref, w_ref, qd_ref, kdt_ref, attn_ref, gl_ref):
    nh2, n_pairs, pw, dk = 2 * DN_HEADS, DN_HEADS // 2, 2 * DN_DK, DN_DK
    first_c = lax.broadcasted_iota(jnp.int32, (CHUNK, 2 * CHUNK), 1) < CHUNK
    first_f = lax.broadcasted_iota(jnp.int32, (CHUNK, pw), 1) < dk

    def cols(x, c, first):
        return jnp.where(first[:x.shape[0]], x[:, c:c + 1], x[:, c + 1:c + 2])

    loaded = []
    for cc in range(CHUNKS_PER_STEP):
        rows = slice(cc * CHUNK, (cc + 1) * CHUNK)
        sm = small_ref[0, rows]
        q, k, v = q_ref[0, rows] * (DN_DK ** -0.5), k_ref[0, rows], v_ref[0, rows]
        grams = [_mm_nt(jnp.concatenate([k[:, j * pw:(j + 1) * pw], q[:, j * pw:(j + 1) * pw]], axis=0),
                        _bd(k[:, j * pw:(j + 1) * pw])) for j in range(n_pairs)]
        gcs = [_mm_exact_lhs(jnp.where(_chunk_masks(d)[0], 1.0, 0.0), sm) for d in range(2)]
        loaded.append((cc, rows, sm, q, k, v, grams, gcs))
    work = []
    for cc, rows, sm, q, k, v, grams, gcs in loaded:
        for d in range(2):
            incl, strict = _chunk_masks(d, 2 * CHUNK)
            gc = gcs[d]
            gc_t = gc.T
            gtot = _last_row(gc, d)
            gl_ref[0, d, cc] = jnp.exp(gtot)
            for j in range(n_pairs):
                c = DN_HEADS * d + 2 * j
                gcr = jnp.concatenate([gc_t[c:c + 1, :], gc_t[c + 1:c + 2, :]], axis=1)
                decay = jnp.exp(jnp.where(incl, cols(gc, c, first_c) - gcr, -1e30))
                lower = jnp.where(strict, (cols(sm, nh2 + c, first_c) * grams[j][:CHUNK]) * decay, 0.0)
                work.append((cc, rows, d, j, c, sm, q, k, v, gc, gtot, decay, grams[j][CHUNK:], -lower))
    rhs_list = []
    for cc, rows, d, j, c, sm, q, k, v, gc, gtot, decay, qk, _ in work:
        sl = slice(j * pw, (j + 1) * pw)
        beta, gcc, gt = cols(sm, nh2 + c, first_f), cols(gc, c, first_f), cols(gtot, c, first_f)
        egc = jnp.exp(gcc)
        kp, qp = k[:, sl], q[:, sl]
        vb, ke = v[:, sl] * beta, (kp * beta) * egc
        rhs_list.append(_bd(jnp.concatenate([vb[:, :dk], ke[:, :dk], vb[:, dk:], ke[:, dk:]], axis=1)))
        qd_ref[0, d, rows, sl] = (qp * egc).astype(BF16)
        k_tail = kp * jnp.exp(gt - gcc)
        kdt_ref[0, d, cc, j * dk:(j + 1) * dk, :] = jnp.concatenate([k_tail[:, :dk].T, k_tail[:, dk:].T], axis=1).astype(BF16)
        attn_ref[0, d, rows, 2 * j * CHUNK:2 * (j + 1) * CHUNK] = (qk * decay).astype(BF16)
    t_invs = _neumann_inverse_pairs([item[-1] for item in work], refine=True)
    for (cc, rows, d, j, *_), t_inv, rhs in zip(work, t_invs, rhs_list):
        sl = slice(j * pw, (j + 1) * pw)
        sol = _mm(t_inv, rhs)
        u_ref[0, d, rows, sl] = jnp.concatenate([sol[:, :dk], sol[:, 2 * dk:3 * dk]], axis=1).astype(BF16)
        w_ref[0, d, rows, sl] = jnp.concatenate([sol[:, dk:2 * dk], sol[:, 3 * dk:]], axis=1).astype(BF16)


def _dn_chunk_call(q, k, v, small):
    n_batch, t_len, wd = q.shape
    nc = t_len // CHUNK
    cps = CHUNKS_PER_STEP
    ispec = lambda n: pl.BlockSpec((1, cps * CHUNK, n), lambda b, c: (b, c, 0))
    ospec = lambda n: pl.BlockSpec((1, 2, cps * CHUNK, n), lambda b, c: (b, 0, c, 0))
    shp = lambda n, dt: jax.ShapeDtypeStruct((n_batch, 2, t_len, n), dt)
    return pl.pallas_call(
        _dn_chunk_kernel,
        out_shape=[shp(wd, BF16), shp(wd, BF16), shp(wd, BF16),
                   jax.ShapeDtypeStruct((n_batch, 2, nc, wd // 2, 2 * CHUNK), BF16), shp(DN_HEADS * CHUNK, BF16),
                   jax.ShapeDtypeStruct((n_batch, 2, nc, 1, LANES), F32)],
        grid=(n_batch, nc // cps),
        in_specs=[ispec(wd), ispec(wd), ispec(wd), ispec(LANES)],
        out_specs=[ospec(wd), ospec(wd), ospec(wd),
                   pl.BlockSpec((1, 2, cps, wd // 2, 2 * CHUNK), lambda b, c: (b, 0, c, 0, 0)), ospec(DN_HEADS * CHUNK),
                   pl.BlockSpec((1, 2, cps, 1, LANES), lambda b, c: (b, 0, c, 0, 0))],
        compiler_params=_cparams("parallel", "parallel"),
        name="deltanet_chunk_prep",
    )(q, k, v, small)


def _dn_scan_kernel(*refs):
    ins, (of_ref, ob_ref, s_ref) = refs[:12], refs[12:]

    @pl.when(pl.program_id(1) == 0)
    def _():
        s_ref[...] = jnp.zeros_like(s_ref)

    dk = DN_DK
    chains = [(b, d, j) for b in range(SCAN_BATCH) for d in range(2) for j in range(DN_HEADS // 2)]
    outs = (of_ref, ob_ref)
    stage1 = []
    for b, d, j in chains:
        u_ref, w_ref, qd_ref = ins[6 * d:6 * d + 3]
        s = s_ref[b, d, j]
        sb = s.astype(BF16)
        halves = [(slice((2 * j + i) * dk, (2 * j + i + 1) * dk), slice(i * dk, (i + 1) * dk)) for i in range(2)]
        v_new = jnp.concatenate([u_ref[b, 0, :, sl] - _dot(w_ref[b, 0, :, sl], sb[:, hl]) for sl, hl in halves], axis=1)
        inter = jnp.concatenate([_dot(qd_ref[b, 0, :, sl], sb[:, hl]) for sl, hl in halves], axis=1)
        stage1.append((s, v_new, inter))
    for (b, d, j), (s, v_new, inter) in zip(chains, stage1):
        kdt_ref, attn_ref, gl_ref = ins[6 * d + 3:6 * d + 6]
        v_bd = _bd(v_new).astype(BF16)
        intra = _dot(attn_ref[b, 0, :, 2 * j * CHUNK:2 * (j + 1) * CHUNK], v_bd)
        outs[d][b, :, 2 * j * dk:2 * (j + 1) * dk] = (inter + intra).astype(BF16)
        c = DN_HEADS * d + 2 * j
        decayed = jnp.concatenate([s[:, i * dk:(i + 1) * dk] * gl_ref[b, 0, 0, :, c + i:c + i + 1] for i in range(2)], axis=1)
        s_ref[b, d, j] = decayed + _dot(kdt_ref[b, 0, 0, j * dk:(j + 1) * dk, :], v_bd)


def _scan_operands(per_token_arrays, per_chunk_arrays, order, *, ctx_chunks, n_chunks):
    in_specs, args = [], []
    for d in range(2):
        chunk = (lambda n: n) if d == 0 else functools.partial(_rev_chunk, ctx_chunks=ctx_chunks, n_chunks=n_chunks)
        for name in order:
            if name in per_token_arrays:
                a = per_token_arrays[name]
                if a.ndim == 4:
                    spec = pl.BlockSpec((SCAN_BATCH, 1, CHUNK, a.shape[-1]),
                                        lambda b, n, d=d, chunk=chunk: (b, d, chunk(n), 0))
                else:
                    spec = pl.BlockSpec((SCAN_BATCH, CHUNK, a.shape[-1]), lambda b, n, chunk=chunk: (b, chunk(n), 0))
            else:
                a = per_chunk_arrays[name]
                spec = pl.BlockSpec((SCAN_BATCH, 1, 1) + a.shape[3:],
                                    lambda b, n, d=d, chunk=chunk: (b, d, chunk(n), 0, 0))
            in_specs.append(spec)
            args.append(a)
    return in_specs, args


def _scans_kernel(*refs):
    n_dn, n_rw = 12, 16
    of_ref, ob_ref, yf_ref, yb_ref, s_dn_ref, s_rw_ref = refs[n_dn + n_rw:]
    _dn_scan_kernel(*refs[:n_dn], of_ref, ob_ref, s_dn_ref)
    _rw_scan_kernel(*refs[n_dn:n_dn + n_rw], yf_ref, yb_ref, s_rw_ref)


def _scans_call(dn, rw, *, ctx_chunks):
    u, w, qd, kdt, attn, gl = dn
    ut, wt, rt, arb, kbt, y0, pc, v = rw
    n_batch, _, t_len, wd_dn = u.shape
    wd_rw = ut.shape[-1]
    nc = t_len // CHUNK
    dn_specs, dn_args = _scan_operands(dict(u=u, w=w, qd=qd, attn=attn), dict(kdt=kdt, gl=gl),
                                       ("u", "w", "qd", "kdt", "attn", "gl"), ctx_chunks=ctx_chunks, n_chunks=nc)
    rw_specs, rw_args = _scan_operands(dict(ut=ut, wt=wt, rt=rt, arb=arb, y0=y0, v=v), dict(kbt=kbt, pc=pc),
                                       ("ut", "wt", "rt", "arb", "kbt", "y0", "pc", "v"),
                                       ctx_chunks=ctx_chunks, n_chunks=nc)
    fwd = lambda wd: pl.BlockSpec((SCAN_BATCH, CHUNK, wd), lambda b, n: (b, n, 0))
    bwd = lambda wd: pl.BlockSpec((SCAN_BATCH, CHUNK, wd), lambda b, n: (b, _rev_chunk(n, ctx_chunks, nc), 0))
    out = lambda wd: jax.ShapeDtypeStruct((n_batch, t_len, wd), BF16)
    return pl.pallas_call(
        _scans_kernel,
        out_shape=[out(wd_dn), out(wd_dn), out(wd_rw), out(wd_rw)],
        grid=(n_batch // SCAN_BATCH, nc),
        in_specs=dn_specs + rw_specs,
        out_specs=[fwd(wd_dn), bwd(wd_dn), fwd(wd_rw), bwd(wd_rw)],
        scratch_shapes=[pltpu.VMEM((SCAN_BATCH, 2, DN_HEADS // 2, DN_DK, 2 * DN_DK), F32),
                        pltpu.VMEM((SCAN_BATCH, 2, RW_HEADS // 2, 2 * RW_HS, 2 * RW_HS), F32)],
        compiler_params=_cparams("parallel", "arbitrary"),
        name="chunk_scans",
    )(*dn_args, *rw_args)


def _rw_chunk_kernel(r_ref, v_ref, kk_ref, lw_ref, kd_ref, kka_ref,
                     ut_ref, wt_ref, rt_ref, arb_ref, kbt_ref, y0_ref, pc_ref, vb_ref):
    pw = 2 * RW_HS
    vb_ref[0] = v_ref[0].astype(BF16)
    prepared = []
    for cc in range(CHUNKS_PER_STEP):
        rows = slice(cc * CHUNK, (cc + 1) * CHUNK)
        r, kk = r_ref[0, rows], kk_ref[0, rows]
        for d in range(2):
            dsl = slice(d * RW_W, (d + 1) * RW_W)
            lw, kd, kka = lw_ref[0, rows, dsl], kd_ref[0, rows, dsl], kka_ref[0, rows, dsl]
            cl = _mm_exact_lhs(jnp.where(_chunk_masks(d)[0], 1.0, 0.0), lw)
            tot = _last_row(cl, d)
            p_inv, p_tail = jnp.exp(-cl), jnp.exp(tot - cl)
            at = -kk * jnp.exp(cl - lw)
            rt = r * jnp.exp(cl)
            rt_ref[0, d, rows] = rt.astype(BF16)
            pc_ref[0, d, cc] = jnp.broadcast_to(jnp.exp(tot), (8, RW_W))
            kbt_ref[0, d, cc] = jnp.concatenate([(kd * p_tail).T, (kka * p_tail).T], axis=1).astype(BF16)
            prepared.append((rows, d, at, rt, kd * p_inv, kka * p_inv))
    work, n_list = [], []
    for rows, d, at, rt, kh, bh in prepared:
        incl, strict = _chunk_masks(d, 2 * CHUNK)
        for j in range(RW_HEADS // 2):
            sl = slice(j * pw, (j + 1) * pw)
            aa = _mm_nt(jnp.concatenate([at[:, sl], rt[:, sl]], axis=0),
                        jnp.concatenate([_bd(bh[:, sl]), _bd(kh[:, sl])], axis=0))
            n_list.append(jnp.where(strict, aa[:CHUNK, :pw], 0.0))
            work.append((rows, d, sl, at[:, sl], jnp.where(strict, aa[:CHUNK, pw:], 0.0),
                         jnp.where(incl, aa[CHUNK:, :pw], 0.0), jnp.where(incl, aa[CHUNK:, pw:], 0.0)))
    t_invs = _neumann_inverse_pairs(n_list, refine=False)
    rhs = []
    for rows, d, sl, at_p, a_ak, a_rb, a_rk in work:
        v_bd = _bd(v_ref[0, rows, sl]).astype(BF16)
        arb_ref[0, d, rows, sl] = a_rb.astype(BF16)
        y0_ref[0, d, rows, sl] = _dot(a_rk.astype(BF16), v_bd).astype(BF16)
        rhs.append(jnp.concatenate([_bd(at_p), _bd(_dot(a_ak.astype(BF16), v_bd))], axis=1))
    for (rows, d, sl, *_), t_inv, x in zip(work, t_invs, rhs):
        sol = _mm(t_inv, x)
        wt_ref[0, d, rows, sl] = sol[:, :pw].astype(BF16)
        ut_ref[0, d, rows, sl] = sol[:, pw:].astype(BF16)


def _rw_chunk_call(r, v, kk, lw, kd, kka):
    n_batch, t_len, wd = r.shape
    nc = t_len // CHUNK
    cps = CHUNKS_PER_STEP
    ispec = lambda n: pl.BlockSpec((1, cps * CHUNK, n), lambda b, c: (b, c, 0))
    ospec = pl.BlockSpec((1, 2, cps * CHUNK, wd), lambda b, c: (b, 0, c, 0))
    shp = jax.ShapeDtypeStruct((n_batch, 2, t_len, wd), BF16)
    return pl.pallas_call(
        _rw_chunk_kernel,
        out_shape=[shp] * 4 + [jax.ShapeDtypeStruct((n_batch, 2, nc, wd, 2 * CHUNK), BF16), shp,
                               jax.ShapeDtypeStruct((n_batch, 2, nc, 8, wd), F32),
                               jax.ShapeDtypeStruct((n_batch, t_len, wd), BF16)],
        grid=(n_batch, nc // cps),
        in_specs=[ispec(wd), ispec(wd), ispec(wd), ispec(2 * wd), ispec(2 * wd), ispec(2 * wd)],
        out_specs=[ospec] * 4 + [pl.BlockSpec((1, 2, cps, wd, 2 * CHUNK), lambda b, c: (b, 0, c, 0, 0)), ospec,
                                 pl.BlockSpec((1, 2, cps, 8, wd), lambda b, c: (b, 0, c, 0, 0)), ispec(wd)],
        compiler_params=_cparams("parallel", "parallel"),
        name="rwkv_chunk_prep",
    )(r, v, kk, lw, kd, kka)


def _rw_scan_kernel(*refs):
    ins, (yf_ref, yb_ref, s_ref) = refs[:16], refs[16:]

    @pl.when(pl.program_id(1) == 0)
    def _():
        s_ref[...] = jnp.zeros_like(s_ref)

    pw = 2 * RW_HS
    r_i = lax.broadcasted_iota(jnp.int32, (pw, pw), 0)
    c_i = lax.broadcasted_iota(jnp.int32, (pw, pw), 1)
    same_head = (r_i < RW_HS) == (c_i < RW_HS)
    chains = [(b, d, j) for b in range(SCAN_BATCH) for d in range(2) for j in range(RW_HEADS // 2)]
    outs = (yf_ref, yb_ref)
    decay_cols = {(b, d): ins[8 * d + 6][b, 0, 0].T for b in range(SCAN_BATCH) for d in range(2)}
    stage1 = []
    for b, d, j in chains:
        ut_ref, wt_ref, rt_ref, _, _, y0_ref = ins[8 * d:8 * d + 6]
        sl = slice(j * pw, (j + 1) * pw)
        s = s_ref[b, d, j]
        sb = s.astype(BF16)
        u = ut_ref[b, 0, :, sl] + _dot(wt_ref[b, 0, :, sl], sb)
        stage1.append((s, u, y0_ref[b, 0, :, sl] + _dot(rt_ref[b, 0, :, sl], sb)))
    for (b, d, j), (s, u, y_inter) in zip(chains, stage1):
        arb_ref, kbt_ref, _, _, v_ref = ins[8 * d + 3:8 * d + 8]
        sl = slice(j * pw, (j + 1) * pw)
        outs[d][b, :, sl] = (y_inter + _dot(arb_ref[b, 0, :, sl], _bd(u).astype(BF16))).astype(BF16)
        grow = _dot(kbt_ref[b, 0, 0, sl, :], jnp.concatenate([v_ref[b, :, sl], u.astype(BF16)], axis=0))
        s_ref[b, d, j] = s * decay_cols[b, d][sl, 0:1] + jnp.where(same_head, grow, 0.0)


def _ev_out_kernel(h_ref, mod_ref, of_ref, ob_ref, dgate_ref, yf_ref, yb_ref, gate7_ref, bonus_ref,
                   dnorm_ref, gnw_ref, gnb_ref, ones_ref, wout_ref, o_ref):
    f32 = lambda ref: ref[0].astype(F32)
    o = f32(of_ref) + f32(ob_ref)
    dgate = f32(dgate_ref)
    parts = []
    for h in range(DN_HEADS):
        sl = slice(h * DN_DK, (h + 1) * DN_DK)
        parts.append(_rms_rows(o[:, sl]) * dnorm_ref[:, sl] * _silu(dgate[:, sl]))
    o_dn = jnp.concatenate(parts, axis=1)
    y = f32(yf_ref) + f32(yb_ref)
    inv_n = 1.0 / RW_HS
    mu = _head_sums(y, ones_ref) * inv_n
    yc = y - mu
    var = _head_sums(yc * yc, ones_ref) * inv_n
    yn = yc * lax.rsqrt(var + RW_GN_EPS) * gnw_ref[...] + gnb_ref[...]
    o_rw = (yn + f32(bonus_ref)) * f32(gate7_ref)
    wd = DN_HEADS * DN_DK
    proj = _dot(o_dn.astype(BF16), wout_ref[0:wd, :]) + _dot(o_rw.astype(BF16), wout_ref[wd:, :])
    o_ref[0] = h_ref[0] + mod_ref[0, 5:6] * proj


def _ev_out_call(h, mod, o_f, o_b, dgate, y_f, y_b, gate7, bonus, dnorm, gnw, gnb, ones, w_out, *, ctx_tiles):
    n_batch, t_len, d = h.shape
    nt = t_len // TILE
    tile = lambda n: pl.BlockSpec((1, TILE, n), lambda b, t: (b, t, 0))
    params = [dnorm.reshape(1, -1), gnw.reshape(1, -1), gnb.reshape(1, -1), ones, w_out]
    streams = [o_f, o_b, dgate, y_f, y_b, gate7, bonus]
    return pl.pallas_call(
        _ev_out_kernel,
        out_shape=jax.ShapeDtypeStruct(h.shape, F32),
        grid=(n_batch, nt),
        in_specs=[tile(d), pl.BlockSpec((1, N_MOD, d), _mod_row_map(n_batch, ctx_tiles, 0))]
        + [tile(s.shape[-1]) for s in streams] + [_resident(p.shape) for p in params],
        out_specs=tile(d),
        compiler_params=_cparams("parallel", "parallel"),
        name="even_mix_out",
    )(h, mod, *streams, *params)


def _rope_layout(width, rot, starts):
    angle = np.full((width,), -1, np.int64)
    first = np.zeros((1, width), np.float32)
    q = rot // 4
    for start in starts:
        for blk in range(2):
            for idx in range(q):
                l1 = start + blk * 2 * q + idx
                angle[l1] = angle[l1 + q] = blk * q + idx
                first[0, l1] = 1.0
    return angle, jnp.asarray(first)


def _rope_tables(n_ctx, n_lat, rot, angle, first):
    rows = n_lat // GRID_W
    row = jnp.repeat(jnp.arange(rows), GRID_W).astype(F32)
    col = jnp.tile(jnp.arange(GRID_W), rows).astype(F32)
    axis_dim = rot // 2
    inv = ROPE_THETA ** (-jnp.arange(0, axis_dim, 2, dtype=F32) / axis_dim)
    ang = jnp.concatenate([row[:, None] * inv, col[:, None] * inv], axis=-1)
    on = jnp.asarray(angle >= 0)
    idx = np.maximum(angle, 0)
    cos = jnp.where(on, jnp.cos(ang)[:, idx], 1.0)
    sin = jnp.where(on, jnp.sin(ang)[:, idx], 0.0) * (1.0 - 2.0 * first)
    width = angle.shape[0]
    return (jnp.concatenate([jnp.ones((n_ctx, width), F32), cos], axis=0),
            jnp.concatenate([jnp.zeros((n_ctx, width), F32), sin], axis=0))


def _rope(x, first_ref, cos_ref, sin_ref, quarter):
    width = x.shape[1]
    partner = jnp.where(first_ref[...] > 0.5, pltpu.roll(x, width - quarter, 1), pltpu.roll(x, quarter, 1))
    return x * cos_ref[...] + partner * sin_ref[...]


def _od_prep_kernel(h_ref, mod_ref, gain_ref, w_ref, qn_ref, kn_ref, mqn_ref, wuq_ref, mkvn_ref, wukv_ref,
                    ones_ref, fq_ref, fk_ref, fm_ref, fr_ref,
                    cq_ref, sq_ref, ck_ref, sk_ref, cm_ref, sm_ref, cr_ref, sr_ref,
                    qg_ref, qm_ref, kgt_ref, vg_ref, kmt_ref, vm_ref):
    xn = _modulate(h_ref[0], gain_ref[...], mod_ref[0, 3:4], mod_ref[0, 4:5]).astype(BF16)
    p = _dot(xn, w_ref[...])
    nq, nkv = GQ_HEADS * GQ_HD, GQ_KV_HEADS * GQ_HD
    o = 0
    q, o = p[:, o:o + nq], o + nq
    k, o = p[:, o:o + nkv], o + nkv
    v, o = p[:, o:o + 2 * nkv], o + 2 * nkv
    n_cq, n_ckv = mqn_ref.shape[1], mkvn_ref.shape[1]
    cq, o = p[:, o:o + n_cq], o + n_cq
    ckv, o = p[:, o:o + n_ckv], o + n_ckv
    kr = p[:, o:o + LANES]
    inv_hd = 1.0 / GQ_HD
    q = q * lax.rsqrt(_head_sums(q * q, ones_ref) * inv_hd + NORM_EPS) * qn_ref[...]
    k = k * lax.rsqrt(_head_sums(k * k, ones_ref) * inv_hd + NORM_EPS) * kn_ref[...]
    qm = _dot((_rms_rows(cq) * mqn_ref[...]).astype(BF16), wuq_ref[...])
    kvm = _dot((_rms_rows(ckv) * mkvn_ref[...]).astype(BF16), wukv_ref[...])
    q = _rope(q, fq_ref, cq_ref, sq_ref, GQ_HD // 4) * (GQ_HD ** -0.5 * LOG2_E)
    k = _rope(k, fk_ref, ck_ref, sk_ref, GQ_HD // 4)
    qm = _rope(qm, fm_ref, cm_ref, sm_ref, ML_ROPE // 4) * ((ML_NOPE + ML_ROPE) ** -0.5 * LOG2_E)
    kr = _rope(kr, fr_ref, cr_ref, sr_ref, ML_ROPE // 4)
    qg_ref[0] = q.astype(BF16)
    qm_ref[0] = qm.astype(BF16)
    n_nope = ML_HEADS * ML_NOPE
    for ref, val in ((vg_ref, v), (vm_ref, kvm[:, n_nope:])):
        upper = lax.broadcasted_iota(jnp.int32, (1, val.shape[1]), 1) % LANES >= LANES // 2
        ref[0] = (val + jnp.where(upper, 1.0, 0.0)).astype(BF16)
    kgt_ref[0] = k.T.astype(BF16)
    knt = kvm[:, :n_nope].T.astype(BF16)
    krt = kr.T[:ML_ROPE].astype(BF16)
    dk = ML_NOPE + ML_ROPE
    for h in range(ML_HEADS):
        kmt_ref[0, h * dk:h * dk + ML_NOPE, :] = knt[h * ML_NOPE:(h + 1) * ML_NOPE]
        kmt_ref[0, h * dk + ML_NOPE:(h + 1) * dk, :] = krt


def _od_prep_call(h, mod, gain, params, tables, *, ctx_tiles):
    n_batch, t_len, d = h.shape
    nt = t_len // TILE
    nq, nkv = GQ_HEADS * GQ_HD, GQ_KV_HEADS * GQ_HD
    dk = ML_NOPE + ML_ROPE
    tile = lambda n: pl.BlockSpec((1, TILE, n), lambda b, t: (b, t, 0))
    tile_t = lambda n: pl.BlockSpec((1, n, TILE), lambda b, t: (b, 0, t))
    tab = lambda a: pl.BlockSpec((TILE, a.shape[1]), lambda b, t: (t, 0))
    shp = lambda *s: jax.ShapeDtypeStruct((n_batch,) + s, BF16)
    return pl.pallas_call(
        _od_prep_kernel,
        out_shape=[shp(t_len, nq), shp(t_len, ML_HEADS * dk), shp(nkv, t_len), shp(t_len, GQ_KV_HEADS * LANES),
                   shp(ML_HEADS * dk, t_len), shp(t_len, ML_HEADS * LANES)],
        grid=(n_batch, nt),
        in_specs=[tile(d), pl.BlockSpec((1, N_MOD, d), _mod_row_map(n_batch, ctx_tiles, 0)), _resident((1, d))]
        + [_resident(p.shape) for p in params] + [tab(a) for a in tables],
        out_specs=[tile(nq), tile(ML_HEADS * dk), tile_t(nkv), tile(GQ_KV_HEADS * LANES), tile_t(ML_HEADS * dk),
                   tile(ML_HEADS * LANES)],
        compiler_params=_cparams("parallel", "parallel"),
        name="attn_prep",
    )(h, mod, gain.reshape(1, d), *params, *tables)


def _odd_layer_weights(od_w_in, gq_q_norm, gq_k_norm, ml_q_norm, ml_w_uq, ml_kv_norm, ml_w_ukv):
    d = od_w_in.shape[0]
    nq, nkv = GQ_HEADS * GQ_HD, GQ_KV_HEADS * GQ_HD

    def slabs(cols, heads, width):
        cols = cols.reshape(cols.shape[0], heads, width)
        return jnp.concatenate([cols, jnp.zeros(cols.shape[:2] + (LANES - width,), F32)], axis=2).reshape(cols.shape[0], -1)

    w = jnp.concatenate([od_w_in[:, :nq + nkv], slabs(od_w_in[:, nq + nkv:nq + 2 * nkv], GQ_KV_HEADS, GQ_HD),
                         od_w_in[:, nq + 2 * nkv:], jnp.zeros((d, LANES - ML_ROPE), F32)], axis=1)
    ukv = ml_w_ukv.reshape(ml_w_ukv.shape[0], ML_HEADS, ML_NOPE + ML_V)
    ukv = jnp.concatenate([ukv[:, :, :ML_NOPE].reshape(-1, ML_HEADS * ML_NOPE),
                           slabs(ukv[:, :, ML_NOPE:].reshape(-1, ML_HEADS * ML_V), ML_HEADS, ML_V)], axis=1)
    dk = ML_NOPE + ML_ROPE
    layouts = [(GQ_HD,) + _rope_layout(GQ_HEADS * GQ_HD, GQ_HD, [h * GQ_HD for h in range(GQ_HEADS)]),
               (GQ_HD,) + _rope_layout(GQ_KV_HEADS * GQ_HD, GQ_HD, [h * GQ_HD for h in range(GQ_KV_HEADS)]),
               (ML_ROPE,) + _rope_layout(ML_HEADS * dk, ML_ROPE, [h * dk + ML_NOPE for h in range(ML_HEADS)]),
               (ML_ROPE,) + _rope_layout(LANES, ML_ROPE, [0])]
    params = [w.astype(BF16), jnp.tile(gq_q_norm, GQ_HEADS).reshape(1, -1), jnp.tile(gq_k_norm, GQ_KV_HEADS).reshape(1, -1),
              ml_q_norm.reshape(1, -1), ml_w_uq.astype(BF16), ml_kv_norm.reshape(1, -1), ukv.astype(BF16),
              _block_ones(LANES, GQ_HD)] + [first for _, _, first in layouts]
    return params, layouts


def _softmax_pv(s, v_slab, width):
    m = jnp.max(s, axis=-1, keepdims=True)
    p = jnp.exp2((s - m).astype(BF16))
    pv = _dot(p, v_slab)
    return pv[:, :width] / pv[:, width:width + 1]


def _attn_kernel(*refs):
    n = ATT_TILES
    h_refs, qg_refs, qm_refs = refs[:n], refs[n:2 * n], refs[2 * n:3 * n]
    mod_ref, kgt_ref, vg_ref, kmt_ref, vm_ref, wout_ref, o_ref = refs[3 * n:]
    group = GQ_HEADS // GQ_KV_HEADS
    dk = ML_NOPE + ML_ROPE
    rows = lambda tiles, sl: jnp.concatenate([r[0, :, sl] for r in tiles], axis=0)

    def logits(h):
        if h < GQ_HEADS:
            g = h // group
            return _dot(rows(qg_refs, slice(h * GQ_HD, (h + 1) * GQ_HD)), kgt_ref[0, g * GQ_HD:(g + 1) * GQ_HD, :])
        h -= GQ_HEADS
        return _dot(rows(qm_refs, slice(h * dk, (h + 1) * dk)), kmt_ref[0, h * dk:(h + 1) * dk, :])

    def values(h):
        if h < GQ_HEADS:
            g = h // group
            return vg_ref[0, :, g * LANES:(g + 1) * LANES], GQ_HD
        h -= GQ_HEADS
        return vm_ref[0, :, h * LANES:(h + 1) * LANES], ML_V

    n_heads = GQ_HEADS + ML_HEADS
    parts, s = [], logits(0)
    for h in range(n_heads):
        s_next = logits(h + 1) if h + 1 < n_heads else None
        parts.append(_softmax_pv(s, *values(h)))
        s = s_next
    ol = jnp.concatenate(parts, axis=1).astype(BF16)
    o_ref[0] = rows(h_refs, slice(None)) + mod_ref[0, 5:6] * _dot(ol, wout_ref[...])


def _attn_call(h, mod, qg, qm, kgt, vg, kmt, vm, w_out, *, ctx_tiles):
    n_batch, t_len, d = h.shape
    nt = t_len // TILE - ctx_tiles
    na = ATT_TILES
    qtiles = lambda a: [pl.BlockSpec((1, TILE, a.shape[-1]), lambda b, t, i=i: (b, na * t + i + ctx_tiles, 0))
                        for i in range(na)]
    whole = lambda a: pl.BlockSpec((1,) + a.shape[1:], lambda b, t: (b, 0, 0))
    return pl.pallas_call(
        _attn_kernel,
        out_shape=jax.ShapeDtypeStruct((n_batch, nt * TILE, d), F32),
        grid=(n_batch, nt // na),
        in_specs=qtiles(h) + qtiles(qg) + qtiles(qm) + [pl.BlockSpec((1, N_MOD, d), lambda b, t: (b, 0, 0)),
                                                      whole(kgt), whole(vg), whole(kmt), whole(vm), _resident(w_out.shape)],
        out_specs=pl.BlockSpec((1, na * TILE, d), lambda b, t: (b, t, 0)),
        compiler_params=_cparams("parallel", "parallel"),
        name="attention_out",
    )(*([h] * na + [qg] * na + [qm] * na), mod, kgt, vg, kmt, vm, w_out)


def _block_ones(n, blk):
    i = np.arange(n) // blk
    return jnp.asarray(i[:, None] == i[None, :], BF16)


def _even_layer_weights(ev_w_in, dn_conv, dn_a_log, dn_dt_bias, rw_mu, rw_w0, rw_w2, rw_a0, rw_a2, rw_g2,
                        rw_kk, rw_ka, rw_rk):
    d = ev_w_in.shape[0]
    n_dn = 4 * DN_HEADS * DN_DK
    nh2 = 2 * DN_HEADS
    slab0 = n_dn + 2 * nh2
    zeros = lambda n: jnp.zeros((d, n), F32)
    w_dn = jnp.concatenate([ev_w_in[:, :n_dn], ev_w_in[:, n_dn:slab0], zeros(LANES - 2 * nh2)], axis=1)
    slab = ev_w_in[:, slab0:]
    o = 3 * RW_W
    lora = 2 * RW_W_LORA
    gpad = 2 * LANES - RW_G_LORA
    w_rw = jnp.concatenate([slab[:, :o + 2 * lora + RW_G_LORA], zeros(gpad)], axis=1)
    mu = jnp.concatenate([rw_mu, jnp.zeros((gpad,), F32)]).reshape(1, -1)
    pad_lanes = lambda v: jnp.zeros((1, LANES), F32).at[0, :v.size].set(v.reshape(-1))

    def dir_blocks(m):
        z = jnp.zeros_like(m[0])
        return jnp.concatenate([jnp.concatenate([m[0], z], axis=1), jnp.concatenate([z, m[1]], axis=1)], axis=0)

    g2 = jnp.concatenate([rw_g2, jnp.zeros((gpad, RW_W), F32)], axis=0)
    return dict(
        w_dn=w_dn.astype(BF16), conv=dn_conv, alog=pad_lanes(dn_a_log), dtb=pad_lanes(dn_dt_bias),
        w_rw=w_rw.astype(BF16), mu=mu, w2=dir_blocks(rw_w2), w0=rw_w0.reshape(1, -1), a2=dir_blocks(rw_a2),
        a0=rw_a0.reshape(1, -1), g2=g2, kkw=rw_kk.reshape(1, -1), kaw=rw_ka.reshape(1, -1),
        rk=rw_rk.reshape(1, -1), ones=_block_ones(LANES, RW_HS))


def kernel(x, c, ctx, c_ctx, mod_w, mod_b, norm_ffn1, norm_mix, norm_ffn2, ffn1_w1, ffn1_w3, ffn1_w2, ffn2_w1, ffn2_w3, ffn2_w2, ev_w_in, ev_w_out, dn_conv, dn_a_log, dn_dt_bias, dn_norm, rw_mu, rw_w0, rw_w2, rw_a0, rw_a2, rw_g2, rw_kk, rw_ka, rw_rk, rw_gn_w, rw_gn_b, od_w_in, od_w_out, gq_q_norm, gq_k_norm, ml_q_norm, ml_w_uq, ml_kv_norm, ml_w_ukv, final_norm):
    n_batch, n_lat, d = x.shape
    n_ctx = ctx.shape[1]
    depth = mod_w.shape[0]
    assert n_ctx % TILE == 0 and n_lat % (ATT_TILES * TILE) == 0 and n_lat % GRID_W == 0
    assert n_batch % SCAN_BATCH == 0 and (n_ctx + n_lat) % (CHUNKS_PER_STEP * CHUNK) == 0
    assert depth % 2 == 0 and depth // 2 == od_w_in.shape[0] == 1, "supported stack: [recurrent, attention]"
    ctx_tiles, ctx_chunks = n_ctx // TILE, n_ctx // CHUNK
    bf = lambda a: a.astype(BF16)

    mod = _all_mod(c, c_ctx, mod_w, mod_b)
    h = (ctx, x)
    for i in range(depth):
        j = i // 2
        last = i == depth - 1
        h = _ffn_call(h, mod[i], norm_ffn1[i], bf(ffn1_w1[i]), bf(ffn1_w3[i]), bf(ffn1_w2[i]), j0=0, ctx_tiles=ctx_tiles)
        if i % 2 == 0:
            w = _even_layer_weights(ev_w_in[j], dn_conv[j], dn_a_log[j], dn_dt_bias[j], rw_mu[j], rw_w0[j], rw_w2[j],
                                    rw_a0[j], rw_a2[j], rw_g2[j], rw_kk[j], rw_ka[j], rw_rk[j])
            q, k, v, small, dgate = _dn_prep_call(h, mod[i], norm_mix[i], w["w_dn"], w["conv"], w["alog"], w["dtb"],
                                                  ctx_tiles=ctx_tiles)
            r, v7, kk, lw, kd, kka, gate7, bonus = _rw_prep_call(
                h, mod[i], norm_mix[i], w["w_rw"], w["mu"], w["w2"], w["w0"], w["a2"], w["a0"], w["g2"], w["kkw"],
                w["kaw"], w["rk"], w["ones"], ctx_tiles=ctx_tiles)
            o_f, o_b, y_f, y_b = _scans_call(_dn_chunk_call(q, k, v, small), _rw_chunk_call(r, v7, kk, lw, kd, kka),
                                             ctx_chunks=ctx_chunks)
            h = _ev_out_call(h, mod[i], o_f, o_b, dgate, y_f, y_b, gate7, bonus, jnp.tile(dn_norm[j], DN_HEADS),
                             rw_gn_w[j], rw_gn_b[j], w["ones"], bf(ev_w_out[j]), ctx_tiles=ctx_tiles)
            h = _ffn_call(h, mod[i], norm_ffn2[i], bf(ffn2_w1[i]), bf(ffn2_w3[i]), bf(ffn2_w2[i]), j0=6,
                          ctx_tiles=ctx_tiles)
        else:
            params, layouts = _odd_layer_weights(od_w_in[j], gq_q_norm[j], gq_k_norm[j], ml_q_norm[j], ml_w_uq[j],
                                                 ml_kv_norm[j], ml_w_ukv[j])
            tables = [t for rot, angle, first in layouts for t in _rope_tables(n_ctx, n_lat, rot, angle, first)]
            qg, qm, kgt, vg, kmt, vm = _od_prep_call(h, mod[i], norm_mix[i], params, tables, ctx_tiles=ctx_tiles)
            hl = _attn_call(h, mod[i], qg, qm, kgt, vg, kmt, vm, bf(od_w_out[j]), ctx_tiles=ctx_tiles)
            assert last
            h = _ffn_call(hl, mod[i], norm_ffn2[i], bf(ffn2_w1[i]), bf(ffn2_w3[i]), bf(ffn2_w2[i]), j0=6, ctx_tiles=0,
                          final_gain=final_norm)
    return h
```

```python
import functools

import jax
import jax.numpy as jnp
import numpy as np
from jax import lax
from jax.experimental import pallas as pl
from jax.experimental.pallas import tpu as pltpu

F32 = jnp.float32
BF16 = jnp.bfloat16

NORM_EPS = 1e-6
ROPE_THETA = 10000.0
GRID_W = 64
N_MOD = 9

DN_HEADS = 4
DN_DK = 128
DN_CONV = 5
RW_HEADS = 8
RW_HS = 64
RW_W = RW_HEADS * RW_HS
RW_W_LORA = 64
RW_A_LORA = 64
RW_G_LORA = 160
RW_GN_EPS = 64e-5
GQ_HEADS = 8
GQ_KV_HEADS = 2
GQ_HD = 64
ML_HEADS = 8
ML_NOPE = 64
ML_ROPE = 32
ML_V = 64

TILE = 256
CHUNK = 64
HALO = 8
LANES = 128
MXU_DIM = 256
VMEM_LIMIT = 56 * 1024 * 1024
SCAN_BATCH = 8
ATT_TILES = 2
CHUNKS_PER_STEP = 4
LOG2_E = 1.4426950408889634


def _cparams(*sem):
    return pltpu.CompilerParams(dimension_semantics=sem, vmem_limit_bytes=VMEM_LIMIT)


def _resident(shape):
    nd = len(shape)
    return pl.BlockSpec(shape, lambda *_: (0,) * nd, pipeline_mode=pl.Buffered(1))


def _mm(a, b):
    return jnp.dot(a.astype(BF16), b.astype(BF16), preferred_element_type=F32)


def _mm_nt(a, b):
    return lax.dot_general(a.astype(BF16), b.astype(BF16), (((1,), (1,)), ((), ())),
                           preferred_element_type=F32)


def _mm_tn(a, b):
    return lax.dot_general(a.astype(BF16), b.astype(BF16), (((0,), (0,)), ((), ())),
                           preferred_element_type=F32)


def _split2(x):
    hi = x.astype(BF16)
    lo = (x - hi.astype(F32)).astype(BF16)
    return hi, lo


def _split3(x):
    hi = x.astype(BF16)
    r = x - hi.astype(F32)
    mid = r.astype(BF16)
    lo = (r - mid.astype(F32)).astype(BF16)
    return hi, mid, lo


def _dot(a, b):
    return jnp.dot(a, b, preferred_element_type=F32)


def _mm3s(asp, bsp):
    (ah, al), (bh, bl) = asp, bsp
    return _dot(ah, bh) + (_dot(ah, bl) + _dot(al, bh))


def _mm3(a, b):
    return _mm3s(_split2(a), _split2(b))


def _mm2(a, b):
    ah, al = _split2(a)
    bb = b.astype(BF16)
    return _dot(ah, bb) + _dot(al, bb)


def _mm_exact_lhs(a01, b):
    a = a01.astype(BF16)
    hi, mid, lo = _split3(b)
    return _dot(a, hi) + (_dot(a, mid) + _dot(a, lo))


def _rms_rows(x):
    return x * lax.rsqrt(jnp.mean(x * x, axis=-1, keepdims=True) + NORM_EPS)


def _modulate(x, gain, shift, scale):
    return (_rms_rows(x) * gain) * (1.0 + scale) + shift


def _silu(x):
    return x * jax.nn.sigmoid(x)


def _softplus(x):
    return jnp.maximum(x, 0.0) + jnp.log1p(jnp.exp(-jnp.abs(x)))


def _mod_kernel(c_ref, w_ref, b_ref, o_ref):
    s = _silu(c_ref[...])
    o_ref[0] = _mm3(s, w_ref[0]) + b_ref[0]


def _mod_call(cc, mod_w, mod_b):
    n_layers, d, n = mod_w.shape
    r = cc.shape[0]
    tn = n // 8
    return pl.pallas_call(
        _mod_kernel,
        out_shape=jax.ShapeDtypeStruct((n_layers, r, n), F32),
        grid=(n_layers, n // tn),
        in_specs=[pl.BlockSpec((r, d), lambda l, j: (0, 0)),
                  pl.BlockSpec((1, d, tn), lambda l, j: (l, 0, j)),
                  pl.BlockSpec((1, 1, tn), lambda l, j: (l, 0, j))],
        out_specs=pl.BlockSpec((1, r, tn), lambda l, j: (l, 0, j)),
        compiler_params=_cparams("parallel", "parallel"),
        name="adaln_mod",
    )(cc, mod_w, mod_b.reshape(n_layers, 1, n))


def _all_mod(c, c_ctx, mod_w, mod_b):
    n_batch, d = c.shape
    rows = -(-(n_batch + 1) // 8) * 8
    cc = jnp.zeros((rows, d), F32).at[:n_batch].set(c).at[n_batch].set(c_ctx)
    return _mod_call(cc, mod_w, mod_b).reshape(mod_w.shape[0], rows, N_MOD, d)


def _mod_row_map(n_batch, ctx_tiles, t_off):
    def index_map(b, t):
        return (jnp.where(t + t_off < ctx_tiles, n_batch, b), 0, 0)
    return index_map


def _ffn_kernel(*refs, j0, final, split_tiles, sub, ctx_tiles, t_off):
    if split_tiles:
        ctx_refs, lat_refs, refs = refs[:sub], refs[sub:2 * sub], refs[2 * sub:]
        xs = [jnp.where(pl.program_id(1) * sub + i < split_tiles, c[0], l[0])
              for i, (c, l) in enumerate(zip(ctx_refs, lat_refs))]
    else:
        h_ref, *refs = refs
        xs = [h_ref[0, i * TILE:(i + 1) * TILE] for i in range(sub)]
    mod_ctx_ref, mod_ref, gain_ref, w1_ref, w3_ref, w2_ref, *rest = refs
    o_ref = rest[-1]
    f = w1_ref.shape[1]
    cut = -(-(f // MXU_DIM) // 2) * MXU_DIM
    fcs = [slice(0, cut), slice(cut, f)] if 0 < cut < f else [slice(0, f)]
    for i, x in enumerate(xs):
        is_ctx = (pl.program_id(1) * sub + i + t_off) < ctx_tiles
        row = lambda j: jnp.where(is_ctx, mod_ctx_ref[0, j:j + 1], mod_ref[0, j:j + 1])
        xn = _modulate(x, gain_ref[...], row(j0), row(j0 + 1)).astype(BF16)
        ups = [(_dot(xn, w1_ref[:, fc]), _dot(xn, w3_ref[:, fc])) for fc in fcs]
        y = None
        for (a, b), fc in zip(ups, fcs):
            part = _dot((_silu(a) * b).astype(BF16), w2_ref[fc, :])
            y = part if y is None else y + part
        y = x + (0.5 * row(j0 + 2)) * y
        if final:
            y = _rms_rows(y) * rest[0][...]
        o_ref[0, i * TILE:(i + 1) * TILE] = y


def _ffn_call(h, mod, gain, w1, w3, w2, *, j0, ctx_tiles, t_off=0, final_gain=None):
    split = isinstance(h, tuple)
    if split:
        ctx, lat = h
        n_batch, _, d = lat.shape
        t_len = ctx.shape[1] + lat.shape[1]
        last_ctx = ctx_tiles - 1
        sub = next(s for s in (3, 2, 1) if (t_len // TILE) % s == 0)
        streams = [ctx] * sub + [lat] * sub
        stream_specs = ([pl.BlockSpec((1, TILE, d), lambda b, t, i=i: (b, jnp.minimum(sub * t + i, last_ctx), 0))
                         for i in range(sub)]
                        + [pl.BlockSpec((1, TILE, d), lambda b, t, i=i: (b, jnp.maximum(sub * t + i - ctx_tiles, 0), 0))
                           for i in range(sub)])
    else:
        n_batch, t_len, d = h.shape
        nt_all = t_len // TILE - t_off
        sub = next(s for s in (3, 2, 1) if nt_all % s == 0 and t_off % s == 0)
        streams = [h]
        stream_specs = [pl.BlockSpec((1, sub * TILE, d), lambda b, t: (b, t + t_off // sub, 0))]
    f = w1.shape[1]
    nt = t_len // TILE - t_off
    final = final_gain is not None
    in_specs = stream_specs + [pl.BlockSpec((1, N_MOD, d), lambda b, t: (n_batch, 0, 0)),
                               pl.BlockSpec((1, N_MOD, d), lambda b, t: (b, 0, 0)),
                               _resident((1, d)), _resident((d, f)), _resident((d, f)), _resident((f, d))]
    args = streams + [mod, mod, gain.reshape(1, d), w1, w3, w2]
    if final:
        in_specs.append(_resident((1, d)))
        args.append(final_gain.reshape(1, d))
    return pl.pallas_call(
        functools.partial(_ffn_kernel, j0=j0, final=final, split_tiles=ctx_tiles if split else 0, sub=sub,
                          ctx_tiles=ctx_tiles, t_off=t_off),
        out_shape=jax.ShapeDtypeStruct((n_batch, nt * TILE, d), F32),
        grid=(n_batch, nt // sub),
        in_specs=in_specs,
        out_specs=pl.BlockSpec((1, sub * TILE, d), lambda b, t: (b, t, 0)),
        compiler_params=_cparams("parallel", "parallel"),
        name="macaron_ffn",
    )(*args)


def _halo_specs(d, ctx_tiles, n_tiles):
    per = TILE // HALO
    last = n_tiles * per - 1
    return [pl.BlockSpec((1, HALO, d), lambda b, t: (b, jnp.maximum(t * per - 1, 0), 0)),
            pl.BlockSpec((1, TILE, d), lambda b, t: (b, t, 0)),
            pl.BlockSpec((1, HALO, d), lambda b, t: (b, jnp.minimum((t + 1) * per, last), 0))]


def _project_with_halo(prev_ref, cur_ref, next_ref, mod_ref, gain_ref, w_ref, pe_ref, *, ctx_tiles, n_tiles):
    t = pl.program_id(1)
    xe = jnp.concatenate([prev_ref[0], cur_ref[0], next_ref[0]], axis=0)
    xn = _modulate(xe, gain_ref[...], mod_ref[0, 3:4], mod_ref[0, 4:5]).astype(BF16)
    p = _dot(xn, w_ref[...])
    row = lax.broadcasted_iota(jnp.int32, (TILE + 2 * HALO, 1), 0)
    prev_ok = jnp.logical_and(t > 0, t != ctx_tiles)
    next_ok = jnp.logical_and(t + 1 < n_tiles, t + 1 != ctx_tiles)
    keep = jnp.logical_and(jnp.logical_or(row >= HALO, prev_ok),
                           jnp.logical_or(row < HALO + TILE, next_ok))
    pe_ref[...] = jnp.where(keep, p, 0.0)


def _dn_prep_kernel(prev_ref, cur_ref, next_ref, mod_ref, gain_ref, w_ref, conv_ref, alog_ref, dtb_ref,
                    q_ref, k_ref, v_ref, small_ref, gate_ref, pe_ref, *, ctx_tiles, n_tiles):
    _project_with_halo(prev_ref, cur_ref, next_ref, mod_ref, gain_ref, w_ref, pe_ref,
                       ctx_tiles=ctx_tiles, n_tiles=n_tiles)
    nqkv = 3 * DN_HEADS * DN_DK
    half = DN_CONV // 2
    acc = None
    for j in range(DN_CONV):
        term = conv_ref[j:j + 1, :] * pe_ref[pl.ds(HALO - half + j, TILE), 0:nqkv]
        acc = term if acc is None else acc + term
    qkv = _silu(acc)
    w = DN_HEADS * DN_DK
    for idx, ref in ((0, q_ref), (1, k_ref)):
        for h in range(DN_HEADS):
            seg = qkv[:, idx * w + h * DN_DK: idx * w + (h + 1) * DN_DK]
            ref[0, :, h * DN_DK:(h + 1) * DN_DK] = seg * lax.rsqrt(jnp.sum(seg * seg, axis=-1, keepdims=True) + 1e-6)
    v_ref[0] = qkv[:, 2 * w:3 * w]
    gate_ref[0] = pe_ref[pl.ds(HALO, TILE), nqkv:nqkv + w].astype(BF16)
    ab = pe_ref[pl.ds(HALO, TILE), nqkv + w:nqkv + w + LANES]
    g = -jnp.exp(alog_ref[...]) * _softplus(ab + dtb_ref[...])
    lane = lax.broadcasted_iota(jnp.int32, ab.shape, 1)
    nh2 = 2 * DN_HEADS
    small_ref[0] = jnp.where(lane < nh2, g, jnp.where(lane < 2 * nh2, jax.nn.sigmoid(ab), 0.0))


def _dn_prep_call(h, mod, gain, w, conv, alog, dtb, *, ctx_tiles):
    n_batch, t_len, d = h.shape
    nt = t_len // TILE
    wd = DN_HEADS * DN_DK
    out = lambda n, dt=F32: jax.ShapeDtypeStruct((n_batch, t_len, n), dt)
    ospec = lambda n: pl.BlockSpec((1, TILE, n), lambda b, t: (b, t, 0))
    return pl.pallas_call(
        functools.partial(_dn_prep_kernel, ctx_tiles=ctx_tiles, n_tiles=nt),
        out_shape=[out(wd), out(wd), out(wd), out(LANES), out(wd, BF16)],
        grid=(n_batch, nt),
        in_specs=_halo_specs(d, ctx_tiles, nt) + [
            pl.BlockSpec((1, N_MOD, d), _mod_row_map(n_batch, ctx_tiles, 0)),
            _resident((1, d)), _resident(w.shape), _resident(conv.shape),
            _resident((1, LANES)), _resident((1, LANES))],
        out_specs=[ospec(wd), ospec(wd), ospec(wd), ospec(LANES), ospec(wd)],
        scratch_shapes=[pltpu.VMEM((TILE + 2 * HALO, w.shape[1]), F32)],
        compiler_params=_cparams("parallel", "parallel"),
        name="deltanet_prep",
    )(h, h, h, mod, gain.reshape(1, d), w, conv, alog, dtb)


def _head_sums(x, ones_ref):
    ones = ones_ref[...]
    out = []
    for g in range(x.shape[1] // LANES):
        hi, lo = _split2(x[:, g * LANES:(g + 1) * LANES])
        out.append(_dot(hi, ones) + _dot(lo, ones))
    return out[0] if len(out) == 1 else jnp.concatenate(out, axis=1)


def _rw_prep_kernel(prev_ref, cur_ref, next_ref, mod_ref, gain_ref, w_ref, mu_ref, w2_ref, w0_ref, a2_ref,
                    a0_ref, g2_ref, kkw_ref, kaw_ref, rk_ref, ones_ref,
                    r_ref, v_ref, kk_ref, lw_ref, kd_ref, kka_ref, gate_ref, bonus_ref, pe_ref,
                    *, ctx_tiles, n_tiles):
    _project_with_halo(prev_ref, cur_ref, next_ref, mod_ref, gain_ref, w_ref, pe_ref,
                       ctx_tiles=ctx_tiles, n_tiles=n_tiles)
    z = pe_ref[pl.ds(HALO, TILE), :]
    zs = 0.5 * (pe_ref[pl.ds(HALO - 1, TILE), :] + pe_ref[pl.ds(HALO + 1, TILE), :])
    s = z + mu_ref[...] * (zs - z)
    r, k7, v7 = s[:, 0:RW_W], s[:, RW_W:2 * RW_W], s[:, 2 * RW_W:3 * RW_W]
    o = 3 * RW_W
    wd, ad, gd = s[:, o:o + LANES], s[:, o + LANES:o + 2 * LANES], s[:, o + 2 * LANES:o + 4 * LANES]
    w_logit = _mm3(jnp.tanh(wd), w2_ref[...]) + w0_ref[...]
    lw = -float(np.exp(-0.5)) * jax.nn.sigmoid(w_logit)
    a = jax.nn.sigmoid(_mm3(ad, a2_ref[...]) + a0_ref[...])
    gate_ref[0] = _mm3(jax.nn.sigmoid(gd), g2_ref[...]).astype(BF16)
    kx = k7 * kkw_ref[...]
    kk = kx * lax.rsqrt(_head_sums(kx * kx, ones_ref) + 1e-6)
    r_ref[0], v_ref[0], kk_ref[0], lw_ref[0] = r, v7, kk, lw
    kd_sum = None
    for d in range(2):
        a_d = a[:, d * RW_W:(d + 1) * RW_W]
        kd = k7 * (1.0 + (a_d - 1.0) * kaw_ref[...])
        kd_ref[0, :, d * RW_W:(d + 1) * RW_W] = kd
        kka_ref[0, :, d * RW_W:(d + 1) * RW_W] = kk * a_d
        kd_sum = kd if kd_sum is None else kd_sum + kd
    bonus_ref[0] = (_head_sums((r * rk_ref[...]) * kd_sum, ones_ref) * v7).astype(BF16)


def _rw_prep_call(h, mod, gain, w, mu, w2, w0, a2, a0, g2, kkw, kaw, rk, ones, *, ctx_tiles):
    n_batch, t_len, d = h.shape
    nt = t_len // TILE
    out = lambda n, dt: jax.ShapeDtypeStruct((n_batch, t_len, n), dt)
    ospec = lambda n: pl.BlockSpec((1, TILE, n), lambda b, t: (b, t, 0))
    widths = [RW_W, RW_W, RW_W, 2 * RW_W, 2 * RW_W, 2 * RW_W, RW_W, RW_W]
    dtypes = [F32] * 6 + [BF16] * 2
    params = [gain.reshape(1, d), w, mu, w2, w0, a2, a0, g2, kkw, kaw, rk, ones]
    return pl.pallas_call(
        functools.partial(_rw_prep_kernel, ctx_tiles=ctx_tiles, n_tiles=nt),
        out_shape=[out(n, dt) for n, dt in zip(widths, dtypes)],
        grid=(n_batch, nt),
        in_specs=_halo_specs(d, ctx_tiles, nt) + [pl.BlockSpec((1, N_MOD, d), _mod_row_map(n_batch, ctx_tiles, 0))]
        + [_resident(p.shape) for p in params],
        out_specs=[ospec(n) for n in widths],
        scratch_shapes=[pltpu.VMEM((TILE + 2 * HALO, w.shape[1]), F32)],
        compiler_params=_cparams("parallel", "parallel"),
        name="rwkv_prep",
    )(h, h, h, mod, *params)


def _chunk_masks(direction, width=CHUNK):
    i = lax.broadcasted_iota(jnp.int32, (CHUNK, width), 0)
    j = lax.broadcasted_iota(jnp.int32, (CHUNK, width), 1) % CHUNK
    return (i >= j, i > j) if direction == 0 else (i <= j, i < j)


def _last_row(x, direction):
    return x[CHUNK - 1:CHUNK] if direction == 0 else x[0:1]


def _bd(x):
    shape = (2 * CHUNK, x.shape[1])
    r = lax.broadcasted_iota(jnp.int32, shape, 0)
    c = lax.broadcasted_iota(jnp.int32, shape, 1)
    return jnp.where((r < CHUNK) == (c < x.shape[1] // 2), jnp.concatenate([x, x], axis=0), 0.0)


def _neumann_inverse_pairs(ns, refine):
    i = lax.broadcasted_iota(jnp.int32, (CHUNK, 2 * CHUNK), 0)
    j = lax.broadcasted_iota(jnp.int32, (CHUNK, 2 * CHUNK), 1) % CHUNK
    eye = jnp.where(i == j, 1.0, 0.0)
    one_pass = lambda a, b: _dot(a.astype(BF16), b.astype(BF16))
    rs, ps = list(ns), list(ns)
    span = 2
    while span < CHUNK:
        ps = [one_pass(p, _bd(p)) for p in ps]
        rs = [r + p + one_pass(r, _bd(p)) for r, p in zip(rs, ps)]
        span *= 2
    if not refine:
        return [eye + r for r in rs]
    res = [_mm3(n, _bd(eye + r)) - r for n, r in zip(ns, rs)]
    return [eye + (r + (e + one_pass(r, _bd(e)))) for r, e in zip(rs, res)]


def _rev_chunk(n, ctx_chunks, n_chunks):
    return jnp.where(n < ctx_chunks, ctx_chunks - 1 - n, n_chunks - 1 + ctx_chunks - n)


def _dn_chunk_kernel(q_ref, k_ref, v_ref, small_ref, u_ref, w_ref, qd_ref, kdt_ref, attn_ref, gl_ref):
    nh2, n_pairs, pw, dk = 2 * DN_HEADS, DN_HEADS // 2, 2 * DN_DK, DN_DK
    first_c = lax.broadcasted_iota(jnp.int32, (CHUNK, 2 * CHUNK), 1) < CHUNK
    first_f = lax.broadcasted_iota(jnp.int32, (CHUNK, pw), 1) < dk

    def cols(x, c, first):
        return jnp.where(first[:x.shape[0]], x[:, c:c + 1], x[:, c + 1:c + 2])

    loaded = []
    for cc in range(CHUNKS_PER_STEP):
        rows = slice(cc * CHUNK, (cc + 1) * CHUNK)
        sm = small_ref[0, rows]
        q, k, v = q_ref[0, rows] * (DN_DK ** -0.5), k_ref[0, rows], v_ref[0, rows]
        grams = [_mm_nt(jnp.concatenate([k[:, j * pw:(j + 1) * pw], q[:, j * pw:(j + 1) * pw]], axis=0),
                        _bd(k[:, j * pw:(j + 1) * pw])) for j in range(n_pairs)]
        gcs = [_mm_exact_lhs(jnp.where(_chunk_masks(d)[0], 1.0, 0.0), sm) for d in range(2)]
        loaded.append((cc, rows, sm, q, k, v, grams, gcs))
    work = []
    for cc, rows, sm, q, k, v, grams, gcs in loaded:
        for d in range(2):
            incl, strict = _chunk_masks(d, 2 * CHUNK)
            gc = gcs[d]
            gc_t = gc.T
            gtot = _last_row(gc, d)
            gl_ref[0, d, cc] = jnp.exp(gtot)
            for j in range(n_pairs):
                c = DN_HEADS * d + 2 * j
                gcr = jnp.concatenate([gc_t[c:c + 1, :], gc_t[c + 1:c + 2, :]], axis=1)
                decay = jnp.exp(jnp.where(incl, cols(gc, c, first_c) - gcr, -1e30))
                lower = jnp.where(strict, (cols(sm, nh2 + c, first_c) * grams[j][:CHUNK]) * decay, 0.0)
                work.append((cc, rows, d, j, c, sm, q, k, v, gc, gtot, decay, grams[j][CHUNK:], -lower))
    rhs_list = []
    for cc, rows, d, j, c, sm, q, k, v, gc, gtot, decay, qk, _ in work:
        sl = slice(j * pw, (j + 1) * pw)
        beta, gcc, gt = cols(sm, nh2 + c, first_f), cols(gc, c, first_f), cols(gtot, c, first_f)
        egc = jnp.exp(gcc)
        kp, qp = k[:, sl], q[:, sl]
        vb, ke = v[:, sl] * beta, (kp * beta) * egc
        rhs_list.append(_bd(jnp.concatenate([vb[:, :dk], ke[:, :dk], vb[:, dk:], ke[:, dk:]], axis=1)))
        qd_ref[0, d, rows, sl] = (qp * egc).astype(BF16)
        k_tail = kp * jnp.exp(gt - gcc)
        kdt_ref[0, d, cc, j * dk:(j + 1) * dk, :] = jnp.concatenate([k_tail[:, :dk].T, k_tail[:, dk:].T], axis=1).astype(BF16)
        attn_ref[0, d, rows, 2 * j * CHUNK:2 * (j + 1) * CHUNK] = (qk * decay).astype(BF16)
    t_invs = _neumann_inverse_pairs([item[-1] for item in work], refine=True)
    for (cc, rows, d, j, *_), t_inv, rhs in zip(work, t_invs, rhs_list):
        sl = slice(j * pw, (j + 1) * pw)
        sol = _mm(t_inv, rhs)
        u_ref[0, d, rows, sl] = jnp.concatenate([sol[:, :dk], sol[:, 2 * dk:3 * dk]], axis=1).astype(BF16)
        w_ref[0, d, rows, sl] = jnp.concatenate([sol[:, dk:2 * dk], sol[:, 3 * dk:]], axis=1).astype(BF16)


def _dn_chunk_call(q, k, v, small):
    n_batch, t_len, wd = q.shape
    nc = t_len // CHUNK
    cps = CHUNKS_PER_STEP
    ispec = lambda n: pl.BlockSpec((1, cps * CHUNK, n), lambda b, c: (b, c, 0))
    ospec = lambda n: pl.BlockSpec((1, 2, cps * CHUNK, n), lambda b, c: (b, 0, c, 0))
    shp = lambda n, dt: jax.ShapeDtypeStruct((n_batch, 2, t_len, n), dt)
    return pl.pallas_call(
        _dn_chunk_kernel,
        out_shape=[shp(wd, BF16), shp(wd, BF16), shp(wd, BF16),
                   jax.ShapeDtypeStruct((n_batch, 2, nc, wd // 2, 2 * CHUNK), BF16), shp(DN_HEADS * CHUNK, BF16),
                   jax.ShapeDtypeStruct((n_batch, 2, nc, 1, LANES), F32)],
        grid=(n_batch, nc // cps),
        in_specs=[ispec(wd), ispec(wd), ispec(wd), ispec(LANES)],
        out_specs=[ospec(wd), ospec(wd), ospec(wd),
                   pl.BlockSpec((1, 2, cps, wd // 2, 2 * CHUNK), lambda b, c: (b, 0, c, 0, 0)), ospec(DN_HEADS * CHUNK),
                   pl.BlockSpec((1, 2, cps, 1, LANES), lambda b, c: (b, 0, c, 0, 0))],
        compiler_params=_cparams("parallel", "parallel"),
        name="deltanet_chunk_prep",
    )(q, k, v, small)


def _dn_scan_kernel(*refs):
    ins, (of_ref, ob_ref, s_ref) = refs[:12], refs[12:]

    @pl.when(pl.program_id(1) == 0)
    def _():
        s_ref[...] = jnp.zeros_like(s_ref)

    dk = DN_DK
    chains = [(b, d, j) for b in range(SCAN_BATCH) for d in range(2) for j in range(DN_HEADS // 2)]
    outs = (of_ref, ob_ref)
    stage1 = []
    for b, d, j in chains:
        u_ref, w_ref, qd_ref = ins[6 * d:6 * d + 3]
        s = s_ref[b, d, j]
        sb = s.astype(BF16)
        halves = [(slice((2 * j + i) * dk, (2 * j + i + 1) * dk), slice(i * dk, (i + 1) * dk)) for i in range(2)]
        v_new = jnp.concatenate([u_ref[b, 0, :, sl] - _dot(w_ref[b, 0, :, sl], sb[:, hl]) for sl, hl in halves], axis=1)
        inter = jnp.concatenate([_dot(qd_ref[b, 0, :, sl], sb[:, hl]) for sl, hl in halves], axis=1)
        stage1.append((s, v_new, inter))
    for (b, d, j), (s, v_new, inter) in zip(chains, stage1):
        kdt_ref, attn_ref, gl_ref = ins[6 * d + 3:6 * d + 6]
        v_bd = _bd(v_new).astype(BF16)
        intra = _dot(attn_ref[b, 0, :, 2 * j * CHUNK:2 * (j + 1) * CHUNK], v_bd)
        outs[d][b, :, 2 * j * dk:2 * (j + 1) * dk] = (inter + intra).astype(BF16)
        c = DN_HEADS * d + 2 * j
        decayed = jnp.concatenate([s[:, i * dk:(i + 1) * dk] * gl_ref[b, 0, 0, :, c + i:c + i + 1] for i in range(2)], axis=1)
        s_ref[b, d, j] = decayed + _dot(kdt_ref[b, 0, 0, j * dk:(j + 1) * dk, :], v_bd)


def _scan_operands(per_token_arrays, per_chunk_arrays, order, *, ctx_chunks, n_chunks):
    in_specs, args = [], []
    for d in range(2):
        chunk = (lambda n: n) if d == 0 else functools.partial(_rev_chunk, ctx_chunks=ctx_chunks, n_chunks=n_chunks)
        for name in order:
            if name in per_token_arrays:
                a = per_token_arrays[name]
                if a.ndim == 4:
                    spec = pl.BlockSpec((SCAN_BATCH, 1, CHUNK, a.shape[-1]),
                                        lambda b, n, d=d, chunk=chunk: (b, d, chunk(n), 0))
                else:
                    spec = pl.BlockSpec((SCAN_BATCH, CHUNK, a.shape[-1]), lambda b, n, chunk=chunk: (b, chunk(n), 0))
            else:
                a = per_chunk_arrays[name]
                spec = pl.BlockSpec((SCAN_BATCH, 1, 1) + a.shape[3:],
                                    lambda b, n, d=d, chunk=chunk: (b, d, chunk(n), 0, 0))
            in_specs.append(spec)
            args.append(a)
    return in_specs, args


def _scans_kernel(*refs):
    n_dn, n_rw = 12, 16
    of_ref, ob_ref, yf_ref, yb_ref, s_dn_ref, s_rw_ref = refs[n_dn + n_rw:]
    _dn_scan_kernel(*refs[:n_dn], of_ref, ob_ref, s_dn_ref)
    _rw_scan_kernel(*refs[n_dn:n_dn + n_rw], yf_ref, yb_ref, s_rw_ref)


def _scans_call(dn, rw, *, ctx_chunks):
    u, w, qd, kdt, attn, gl = dn
    ut, wt, rt, arb, kbt, y0, pc, v = rw
    n_batch, _, t_len, wd_dn = u.shape
    wd_rw = ut.shape[-1]
    nc = t_len // CHUNK
    dn_specs, dn_args = _scan_operands(dict(u=u, w=w, qd=qd, attn=attn), dict(kdt=kdt, gl=gl),
                                       ("u", "w", "qd", "kdt", "attn", "gl"), ctx_chunks=ctx_chunks, n_chunks=nc)
    rw_specs, rw_args = _scan_operands(dict(ut=ut, wt=wt, rt=rt, arb=arb, y0=y0, v=v), dict(kbt=kbt, pc=pc),
                                       ("ut", "wt", "rt", "arb", "kbt", "y0", "pc", "v"),
                                       ctx_chunks=ctx_chunks, n_chunks=nc)
    fwd = lambda wd: pl.BlockSpec((SCAN_BATCH, CHUNK, wd), lambda b, n: (b, n, 0))
    bwd = lambda wd: pl.BlockSpec((SCAN_BATCH, CHUNK, wd), lambda b, n: (b, _rev_chunk(n, ctx_chunks, nc), 0))
    out = lambda wd: jax.ShapeDtypeStruct((n_batch, t_len, wd), BF16)
    return pl.pallas_call(
        _scans_kernel,
        out_shape=[out(wd_dn), out(wd_dn), out(wd_rw), out(wd_rw)],
        grid=(n_batch // SCAN_BATCH, nc),
        in_specs=dn_specs + rw_specs,
        out_specs=[fwd(wd_dn), bwd(wd_dn), fwd(wd_rw), bwd(wd_rw)],
        scratch_shapes=[pltpu.VMEM((SCAN_BATCH, 2, DN_HEADS // 2, DN_DK, 2 * DN_DK), F32),
                        pltpu.VMEM((SCAN_BATCH, 2, RW_HEADS // 2, 2 * RW_HS, 2 * RW_HS), F32)],
        compiler_params=_cparams("parallel", "arbitrary"),
        name="chunk_scans",
    )(*dn_args, *rw_args)


def _rw_chunk_kernel(r_ref, v_ref, kk_ref, lw_ref, kd_ref, kka_ref,
                     ut_ref, wt_ref, rt_ref, arb_ref, kbt_ref, y0_ref, pc_ref, vb_ref):
    pw = 2 * RW_HS
    vb_ref[0] = v_ref[0].astype(BF16)
    prepared = []
    for cc in range(CHUNKS_PER_STEP):
        rows = slice(cc * CHUNK, (cc + 1) * CHUNK)
        r, kk = r_ref[0, rows], kk_ref[0, rows]
        for d in range(2):
            dsl = slice(d * RW_W, (d + 1) * RW_W)
            lw, kd, kka = lw_ref[0, rows, dsl], kd_ref[0, rows, dsl], kka_ref[0, rows, dsl]
            cl = _mm_exact_lhs(jnp.where(_chunk_masks(d)[0], 1.0, 0.0), lw)
            tot = _last_row(cl, d)
            p_inv, p_tail = jnp.exp(-cl), jnp.exp(tot - cl)
            at = -kk * jnp.exp(cl - lw)
            rt = r * jnp.exp(cl)
            rt_ref[0, d, rows] = rt.astype(BF16)
            pc_ref[0, d, cc] = jnp.broadcast_to(jnp.exp(tot), (8, RW_W))
            kbt_ref[0, d, cc] = jnp.concatenate([(kd * p_tail).T, (kka * p_tail).T], axis=1).astype(BF16)
            prepared.append((rows, d, at, rt, kd * p_inv, kka * p_inv))
    work, n_list = [], []
    for rows, d, at, rt, kh, bh in prepared:
        incl, strict = _chunk_masks(d, 2 * CHUNK)
        for j in range(RW_HEADS // 2):
            sl = slice(j * pw, (j + 1) * pw)
            aa = _mm_nt(jnp.concatenate([at[:, sl], rt[:, sl]], axis=0),
                        jnp.concatenate([_bd(bh[:, sl]), _bd(kh[:, sl])], axis=0))
            n_list.append(jnp.where(strict, aa[:CHUNK, :pw], 0.0))
            work.append((rows, d, sl, at[:, sl], jnp.where(strict, aa[:CHUNK, pw:], 0.0),
                         jnp.where(incl, aa[CHUNK:, :pw], 0.0), jnp.where(incl, aa[CHUNK:, pw:], 0.0)))
    t_invs = _neumann_inverse_pairs(n_list, refine=False)
    rhs = []
    for rows, d, sl, at_p, a_ak, a_rb, a_rk in work:
        v_bd = _bd(v_ref[0, rows, sl]).astype(BF16)
        arb_ref[0, d, rows, sl] = a_rb.astype(BF16)
        y0_ref[0, d, rows, sl] = _dot(a_rk.astype(BF16), v_bd).astype(BF16)
        rhs.append(jnp.concatenate([_bd(at_p), _bd(_dot(a_ak.astype(BF16), v_bd))], axis=1))
    for (rows, d, sl, *_), t_inv, x in zip(work, t_invs, rhs):
        sol = _mm(t_inv, x)
        wt_ref[0, d, rows, sl] = sol[:, :pw].astype(BF16)
        ut_ref[0, d, rows, sl] = sol[:, pw:].astype(BF16)


def _rw_chunk_call(r, v, kk, lw, kd, kka):
    n_batch, t_len, wd = r.shape
    nc = t_len // CHUNK
    cps = CHUNKS_PER_STEP
    ispec = lambda n: pl.BlockSpec((1, cps * CHUNK, n), lambda b, c: (b, c, 0))
    ospec = pl.BlockSpec((1, 2, cps * CHUNK, wd), lambda b, c: (b, 0, c, 0))
    shp = jax.ShapeDtypeStruct((n_batch, 2, t_len, wd), BF16)
    return pl.pallas_call(
        _rw_chunk_kernel,
        out_shape=[shp] * 4 + [jax.ShapeDtypeStruct((n_batch, 2, nc, wd, 2 * CHUNK), BF16), shp,
                               jax.ShapeDtypeStruct((n_batch, 2, nc, 8, wd), F32),
                               jax.ShapeDtypeStruct((n_batch, t_len, wd), BF16)],
        grid=(n_batch, nc // cps),
        in_specs=[ispec(wd), ispec(wd), ispec(wd), ispec(2 * wd), ispec(2 * wd), ispec(2 * wd)],
        out_specs=[ospec] * 4 + [pl.BlockSpec((1, 2, cps, wd, 2 * CHUNK), lambda b, c: (b, 0, c, 0, 0)), ospec,
                                 pl.BlockSpec((1, 2, cps, 8, wd), lambda b, c: (b, 0, c, 0, 0)), ispec(wd)],
        compiler_params=_cparams("parallel", "parallel"),
        name="rwkv_chunk_prep",
    )(r, v, kk, lw, kd, kka)


def _rw_scan_kernel(*refs):
    ins, (yf_ref, yb_ref, s_ref) = refs[:16], refs[16:]

    @pl.when(pl.program_id(1) == 0)
    def _():
        s_ref[...] = jnp.zeros_like(s_ref)

    pw = 2 * RW_HS
    r_i = lax.broadcasted_iota(jnp.int32, (pw, pw), 0)
    c_i = lax.broadcasted_iota(jnp.int32, (pw, pw), 1)
    same_head = (r_i < RW_HS) == (c_i < RW_HS)
    chains = [(b, d, j) for b in range(SCAN_BATCH) for d in range(2) for j in range(RW_HEADS // 2)]
    outs = (yf_ref, yb_ref)
    decay_cols = {(b, d): ins[8 * d + 6][b, 0, 0].T for b in range(SCAN_BATCH) for d in range(2)}
    stage1 = []
    for b, d, j in chains:
        ut_ref, wt_ref, rt_ref, _, _, y0_ref = ins[8 * d:8 * d + 6]
        sl = slice(j * pw, (j + 1) * pw)
        s = s_ref[b, d, j]
        sb = s.astype(BF16)
        u = ut_ref[b, 0, :, sl] + _dot(wt_ref[b, 0, :, sl], sb)
        stage1.append((s, u, y0_ref[b, 0, :, sl] + _dot(rt_ref[b, 0, :, sl], sb)))
    for (b, d, j), (s, u, y_inter) in zip(chains, stage1):
        arb_ref, kbt_ref, _, _, v_ref = ins[8 * d + 3:8 * d + 8]
        sl = slice(j * pw, (j + 1) * pw)
        outs[d][b, :, sl] = (y_inter + _dot(arb_ref[b, 0, :, sl], _bd(u).astype(BF16))).astype(BF16)
        grow = _dot(kbt_ref[b, 0, 0, sl, :], jnp.concatenate([v_ref[b, :, sl], u.astype(BF16)], axis=0))
        s_ref[b, d, j] = s * decay_cols[b, d][sl, 0:1] + jnp.where(same_head, grow, 0.0)


def _ev_out_kernel(h_ref, mod_ref, of_ref, ob_ref, dgate_ref, yf_ref, yb_ref, gate7_ref, bonus_ref,
                   dnorm_ref, gnw_ref, gnb_ref, ones_ref, wout_ref, o_ref):
    f32 = lambda ref: ref[0].astype(F32)
    o = f32(of_ref) + f32(ob_ref)
    dgate = f32(dgate_ref)
    parts = []
    for h in range(DN_HEADS):
        sl = slice(h * DN_DK, (h + 1) * DN_DK)
        parts.append(_rms_rows(o[:, sl]) * dnorm_ref[:, sl] * _silu(dgate[:, sl]))
    o_dn = jnp.concatenate(parts, axis=1)
    y = f32(yf_ref) + f32(yb_ref)
    inv_n = 1.0 / RW_HS
    mu = _head_sums(y, ones_ref) * inv_n
    yc = y - mu
    var = _head_sums(yc * yc, ones_ref) * inv_n
    yn = yc * lax.rsqrt(var + RW_GN_EPS) * gnw_ref[...] + gnb_ref[...]
    o_rw = (yn + f32(bonus_ref)) * f32(gate7_ref)
    wd = DN_HEADS * DN_DK
    proj = _dot(o_dn.astype(BF16), wout_ref[0:wd, :]) + _dot(o_rw.astype(BF16), wout_ref[wd:, :])
    o_ref[0] = h_ref[0] + mod_ref[0, 5:6] * proj


def _ev_out_call(h, mod, o_f, o_b, dgate, y_f, y_b, gate7, bonus, dnorm, gnw, gnb, ones, w_out, *, ctx_tiles):
    n_batch, t_len, d = h.shape
    nt = t_len // TILE
    tile = lambda n: pl.BlockSpec((1, TILE, n), lambda b, t: (b, t, 0))
    params = [dnorm.reshape(1, -1), gnw.reshape(1, -1), gnb.reshape(1, -1), ones, w_out]
    streams = [o_f, o_b, dgate, y_f, y_b, gate7, bonus]
    return pl.pallas_call(
        _ev_out_kernel,
        out_shape=jax.ShapeDtypeStruct(h.shape, F32),
        grid=(n_batch, nt),
        in_specs=[tile(d), pl.BlockSpec((1, N_MOD, d), _mod_row_map(n_batch, ctx_tiles, 0))]
        + [tile(s.shape[-1]) for s in streams] + [_resident(p.shape) for p in params],
        out_specs=tile(d),
        compiler_params=_cparams("parallel", "parallel"),
        name="even_mix_out",
    )(h, mod, *streams, *params)


def _rope_layout(width, rot, starts):
    angle = np.full((width,), -1, np.int64)
    first = np.zeros((1, width), np.float32)
    q = rot // 4
    for start in starts:
        for blk in range(2):
            for idx in range(q):
                l1 = start + blk * 2 * q + idx
                angle[l1] = angle[l1 + q] = blk * q + idx
                first[0, l1] = 1.0
    return angle, jnp.asarray(first)


def _rope_tables(n_ctx, n_lat, rot, angle, first):
    rows = n_lat // GRID_W
    row = jnp.repeat(jnp.arange(rows), GRID_W).astype(F32)
    col = jnp.tile(jnp.arange(GRID_W), rows).astype(F32)
    axis_dim = rot // 2
    inv = ROPE_THETA ** (-jnp.arange(0, axis_dim, 2, dtype=F32) / axis_dim)
    ang = jnp.concatenate([row[:, None] * inv, col[:, None] * inv], axis=-1)
    on = jnp.asarray(angle >= 0)
    idx = np.maximum(angle, 0)
    cos = jnp.where(on, jnp.cos(ang)[:, idx], 1.0)
    sin = jnp.where(on, jnp.sin(ang)[:, idx], 0.0) * (1.0 - 2.0 * first)
    width = angle.shape[0]
    return (jnp.concatenate([jnp.ones((n_ctx, width), F32), cos], axis=0),
            jnp.concatenate([jnp.zeros((n_ctx, width), F32), sin], axis=0))


def _rope(x, first_ref, cos_ref, sin_ref, quarter):
    width = x.shape[1]
    partner = jnp.where(first_ref[...] > 0.5, pltpu.roll(x, width - quarter, 1), pltpu.roll(x, quarter, 1))
    return x * cos_ref[...] + partner * sin_ref[...]


def _od_prep_kernel(h_ref, mod_ref, gain_ref, w_ref, qn_ref, kn_ref, mqn_ref, wuq_ref, mkvn_ref, wukv_ref,
                    ones_ref, fq_ref, fk_ref, fm_ref, fr_ref,
                    cq_ref, sq_ref, ck_ref, sk_ref, cm_ref, sm_ref, cr_ref, sr_ref,
                    qg_ref, qm_ref, kgt_ref, vg_ref, kmt_ref, vm_ref):
    xn = _modulate(h_ref[0], gain_ref[...], mod_ref[0, 3:4], mod_ref[0, 4:5]).astype(BF16)
    p = _dot(xn, w_ref[...])
    nq, nkv = GQ_HEADS * GQ_HD, GQ_KV_HEADS * GQ_HD
    o = 0
    q, o = p[:, o:o + nq], o + nq
    k, o = p[:, o:o + nkv], o + nkv
    v, o = p[:, o:o + 2 * nkv], o + 2 * nkv
    n_cq, n_ckv = mqn_ref.shape[1], mkvn_ref.shape[1]
    cq, o = p[:, o:o + n_cq], o + n_cq
    ckv, o = p[:, o:o + n_ckv], o + n_ckv
    kr = p[:, o:o + LANES]
    inv_hd = 1.0 / GQ_HD
    q = q * lax.rsqrt(_head_sums(q * q, ones_ref) * inv_hd + NORM_EPS) * qn_ref[...]
    k = k * lax.rsqrt(_head_sums(k * k, ones_ref) * inv_hd + NORM_EPS) * kn_ref[...]
    qm = _dot((_rms_rows(cq) * mqn_ref[...]).astype(BF16), wuq_ref[...])
    kvm = _dot((_rms_rows(ckv) * mkvn_ref[...]).astype(BF16), wukv_ref[...])
    q = _rope(q, fq_ref, cq_ref, sq_ref, GQ_HD // 4) * (GQ_HD ** -0.5 * LOG2_E)
    k = _rope(k, fk_ref, ck_ref, sk_ref, GQ_HD // 4)
    qm = _rope(qm, fm_ref, cm_ref, sm_ref, ML_ROPE // 4) * ((ML_NOPE + ML_ROPE) ** -0.5 * LOG2_E)
    kr = _rope(kr, fr_ref, cr_ref, sr_ref, ML_ROPE // 4)
    qg_ref[0] = q.astype(BF16)
    qm_ref[0] = qm.astype(BF16)
    n_nope = ML_HEADS * ML_NOPE
    for ref, val in ((vg_ref, v), (vm_ref, kvm[:, n_nope:])):
        upper = lax.broadcasted_iota(jnp.int32, (1, val.shape[1]), 1) % LANES >= LANES // 2
        ref[0] = (val + jnp.where(upper, 1.0, 0.0)).astype(BF16)
    kgt_ref[0] = k.T.astype(BF16)
    knt = kvm[:, :n_nope].T.astype(BF16)
    krt = kr.T[:ML_ROPE].astype(BF16)
    dk = ML_NOPE + ML_ROPE
    for h in range(ML_HEADS):
        kmt_ref[0, h * dk:h * dk + ML_NOPE, :] = knt[h * ML_NOPE:(h + 1) * ML_NOPE]
        kmt_ref[0, h * dk + ML_NOPE:(h + 1) * dk, :] = krt


def _od_prep_call(h, mod, gain, params, tables, *, ctx_tiles):
    n_batch, t_len, d = h.shape
    nt = t_len // TILE
    nq, nkv = GQ_HEADS * GQ_HD, GQ_KV_HEADS * GQ_HD
    dk = ML_NOPE + ML_ROPE
    tile = lambda n: pl.BlockSpec((1, TILE, n), lambda b, t: (b, t, 0))
    tile_t = lambda n: pl.BlockSpec((1, n, TILE), lambda b, t: (b, 0, t))
    tab = lambda a: pl.BlockSpec((TILE, a.shape[1]), lambda b, t: (t, 0))
    shp = lambda *s: jax.ShapeDtypeStruct((n_batch,) + s, BF16)
    return pl.pallas_call(
        _od_prep_kernel,
        out_shape=[shp(t_len, nq), shp(t_len, ML_HEADS * dk), shp(nkv, t_len), shp(t_len, GQ_KV_HEADS * LANES),
                   shp(ML_HEADS * dk, t_len), shp(t_len, ML_HEADS * LANES)],
        grid=(n_batch, nt),
        in_specs=[tile(d), pl.BlockSpec((1, N_MOD, d), _mod_row_map(n_batch, ctx_tiles, 0)), _resident((1, d))]
        + [_resident(p.shape) for p in params] + [tab(a) for a in tables],
        out_specs=[tile(nq), tile(ML_HEADS * dk), tile_t(nkv), tile(GQ_KV_HEADS * LANES), tile_t(ML_HEADS * dk),
                   tile(ML_HEADS * LANES)],
        compiler_params=_cparams("parallel", "parallel"),
        name="attn_prep",
    )(h, mod, gain.reshape(1, d), *params, *tables)


def _odd_layer_weights(od_w_in, gq_q_norm, gq_k_norm, ml_q_norm, ml_w_uq, ml_kv_norm, ml_w_ukv):
    d = od_w_in.shape[0]
    nq, nkv = GQ_HEADS * GQ_HD, GQ_KV_HEADS * GQ_HD

    def slabs(cols, heads, width):
        cols = cols.reshape(cols.shape[0], heads, width)
        return jnp.concatenate([cols, jnp.zeros(cols.shape[:2] + (LANES - width,), F32)], axis=2).reshape(cols.shape[0], -1)

    w = jnp.concatenate([od_w_in[:, :nq + nkv], slabs(od_w_in[:, nq + nkv:nq + 2 * nkv], GQ_KV_HEADS, GQ_HD),
                         od_w_in[:, nq + 2 * nkv:], jnp.zeros((d, LANES - ML_ROPE), F32)], axis=1)
    ukv = ml_w_ukv.reshape(ml_w_ukv.shape[0], ML_HEADS, ML_NOPE + ML_V)
    ukv = jnp.concatenate([ukv[:, :, :ML_NOPE].reshape(-1, ML_HEADS * ML_NOPE),
                           slabs(ukv[:, :, ML_NOPE:].reshape(-1, ML_HEADS * ML_V), ML_HEADS, ML_V)], axis=1)
    dk = ML_NOPE + ML_ROPE
    layouts = [(GQ_HD,) + _rope_layout(GQ_HEADS * GQ_HD, GQ_HD, [h * GQ_HD for h in range(GQ_HEADS)]),
               (GQ_HD,) + _rope_layout(GQ_KV_HEADS * GQ_HD, GQ_HD, [h * GQ_HD for h in range(GQ_KV_HEADS)]),
               (ML_ROPE,) + _rope_layout(ML_HEADS * dk, ML_ROPE, [h * dk + ML_NOPE for h in range(ML_HEADS)]),
               (ML_ROPE,) + _rope_layout(LANES, ML_ROPE, [0])]
    params = [w.astype(BF16), jnp.tile(gq_q_norm, GQ_HEADS).reshape(1, -1), jnp.tile(gq_k_norm, GQ_KV_HEADS).reshape(1, -1),
              ml_q_norm.reshape(1, -1), ml_w_uq.astype(BF16), ml_kv_norm.reshape(1, -1), ukv.astype(BF16),
              _block_ones(LANES, GQ_HD)] + [first for _, _, first in layouts]
    return params, layouts


def _softmax_pv(s, v_slab, width):
    m = jnp.max(s, axis=-1, keepdims=True)
    p = jnp.exp2((s - m).astype(BF16))
    pv = _dot(p, v_slab)
    return pv[:, :width] / pv[:, width:width + 1]


def _attn_kernel(*refs):
    n = ATT_TILES
    h_refs, qg_refs, qm_refs = refs[:n], refs[n:2 * n], refs[2 * n:3 * n]
    mod_ref, kgt_ref, vg_ref, kmt_ref, vm_ref, wout_ref, o_ref = refs[3 * n:]
    group = GQ_HEADS // GQ_KV_HEADS
    dk = ML_NOPE + ML_ROPE
    rows = lambda tiles, sl: jnp.concatenate([r[0, :, sl] for r in tiles], axis=0)

    def logits(h):
        if h < GQ_HEADS:
            g = h // group
            return _dot(rows(qg_refs, slice(h * GQ_HD, (h + 1) * GQ_HD)), kgt_ref[0, g * GQ_HD:(g + 1) * GQ_HD, :])
        h -= GQ_HEADS
        return _dot(rows(qm_refs, slice(h * dk, (h + 1) * dk)), kmt_ref[0, h * dk:(h + 1) * dk, :])

    def values(h):
        if h < GQ_HEADS:
            g = h // group
            return vg_ref[0, :, g * LANES:(g + 1) * LANES], GQ_HD
        h -= GQ_HEADS
        return vm_ref[0, :, h * LANES:(h + 1) * LANES], ML_V

    n_heads = GQ_HEADS + ML_HEADS
    parts, s = [], logits(0)
    for h in range(n_heads):
        s_next = logits(h + 1) if h + 1 < n_heads else None
        parts.append(_softmax_pv(s, *values(h)))
        s = s_next
    ol = jnp.concatenate(parts, axis=1).astype(BF16)
    o_ref[0] = rows(h_refs, slice(None)) + mod_ref[0, 5:6] * _dot(ol, wout_ref[...])


def _attn_call(h, mod, qg, qm, kgt, vg, kmt, vm, w_out, *, ctx_tiles):
    n_batch, t_len, d = h.shape
    nt = t_len // TILE - ctx_tiles
    na = ATT_TILES
    qtiles = lambda a: [pl.BlockSpec((1, TILE, a.shape[-1]), lambda b, t, i=i: (b, na * t + i + ctx_tiles, 0))
                        for i in range(na)]
    whole = lambda a: pl.BlockSpec((1,) + a.shape[1:], lambda b, t: (b, 0, 0))
    return pl.pallas_call(
        _attn_kernel,
        out_shape=jax.ShapeDtypeStruct((n_batch, nt * TILE, d), F32),
        grid=(n_batch, nt // na),
        in_specs=qtiles(h) + qtiles(qg) + qtiles(qm) + [pl.BlockSpec((1, N_MOD, d), lambda b, t: (b, 0, 0)),
                                                      whole(kgt), whole(vg), whole(kmt), whole(vm), _resident(w_out.shape)],
        out_specs=pl.BlockSpec((1, na * TILE, d), lambda b, t: (b, t, 0)),
        compiler_params=_cparams("parallel", "parallel"),
        name="attention_out",
    )(*([h] * na + [qg] * na + [qm] * na), mod, kgt, vg, kmt, vm, w_out)


def _block_ones(n, blk):
    i = np.arange(n) // blk
    return jnp.asarray(i[:, None] == i[None, :], BF16)


def _even_layer_weights(ev_w_in, dn_conv, dn_a_log, dn_dt_bias, rw_mu, rw_w0, rw_w2, rw_a0, rw_a2, rw_g2,
                        rw_kk, rw_ka, rw_rk):
    d = ev_w_in.shape[0]
    n_dn = 4 * DN_HEADS * DN_DK
    nh2 = 2 * DN_HEADS
    slab0 = n_dn + 2 * nh2
    zeros = lambda n: jnp.zeros((d, n), F32)
    w_dn = jnp.concatenate([ev_w_in[:, :n_dn], ev_w_in[:, n_dn:slab0], zeros(LANES - 2 * nh2)], axis=1)
    slab = ev_w_in[:, slab0:]
    o = 3 * RW_W
    lora = 2 * RW_W_LORA
    gpad = 2 * LANES - RW_G_LORA
    w_rw = jnp.concatenate([slab[:, :o + 2 * lora + RW_G_LORA], zeros(gpad)], axis=1)
    mu = jnp.concatenate([rw_mu, jnp.zeros((gpad,), F32)]).reshape(1, -1)
    pad_lanes = lambda v: jnp.zeros((1, LANES), F32).at[0, :v.size].set(v.reshape(-1))

    def dir_blocks(m):
        z = jnp.zeros_like(m[0])
        return jnp.concatenate([jnp.concatenate([m[0], z], axis=1), jnp.concatenate([z, m[1]], axis=1)], axis=0)

    g2 = jnp.concatenate([rw_g2, jnp.zeros((gpad, RW_W), F32)], axis=0)
    return dict(
        w_dn=w_dn.astype(BF16), conv=dn_conv, alog=pad_lanes(dn_a_log), dtb=pad_lanes(dn_dt_bias),
        w_rw=w_rw.astype(BF16), mu=mu, w2=dir_blocks(rw_w2), w0=rw_w0.reshape(1, -1), a2=dir_blocks(rw_a2),
        a0=rw_a0.reshape(1, -1), g2=g2, kkw=rw_kk.reshape(1, -1), kaw=rw_ka.reshape(1, -1),
        rk=rw_rk.reshape(1, -1), ones=_block_ones(LANES, RW_HS))


def kernel(x, c, ctx, c_ctx, mod_w, mod_b, norm_ffn1, norm_mix, norm_ffn2, ffn1_w1, ffn1_w3, ffn1_w2, ffn2_w1, ffn2_w3, ffn2_w2, ev_w_in, ev_w_out, dn_conv, dn_a_log, dn_dt_bias, dn_norm, rw_mu, rw_w0, rw_w2, rw_a0, rw_a2, rw_g2, rw_kk, rw_ka, rw_rk, rw_gn_w, rw_gn_b, od_w_in, od_w_out, gq_q_norm, gq_k_norm, ml_q_norm, ml_w_uq, ml_kv_norm, ml_w_ukv, final_norm):
    n_batch, n_lat, d = x.shape
    n_ctx = ctx.shape[1]
    depth = mod_w.shape[0]
    assert n_ctx % TILE == 0 and n_lat % (ATT_TILES * TILE) == 0 and n_lat % GRID_W == 0
    assert n_batch % SCAN_BATCH == 0 and (n_ctx + n_lat) % (CHUNKS_PER_STEP * CHUNK) == 0
    assert depth % 2 == 0 and depth // 2 == od_w_in.shape[0] == 1, "supported stack: [recurrent, attention]"
    ctx_tiles, ctx_chunks = n_ctx // TILE, n_ctx // CHUNK
    bf = lambda a: a.astype(BF16)

    mod = _all_mod(c, c_ctx, mod_w, mod_b)
    h = (ctx, x)
    for i in range(depth):
        j = i // 2
        last = i == depth - 1
        h = _ffn_call(h, mod[i], norm_ffn1[i], bf(ffn1_w1[i]), bf(ffn1_w3[i]), bf(ffn1_w2[i]), j0=0, ctx_tiles=ctx_tiles)
        if i % 2 == 0:
            w = _even_layer_weights(ev_w_in[j], dn_conv[j], dn_a_log[j], dn_dt_bias[j], rw_mu[j], rw_w0[j], rw_w2[j],
                                    rw_a0[j], rw_a2[j], rw_g2[j], rw_kk[j], rw_ka[j], rw_rk[j])
            q, k, v, small, dgate = _dn_prep_call(h, mod[i], norm_mix[i], w["w_dn"], w["conv"], w["alog"], w["dtb"],
                                                  ctx_tiles=ctx_tiles)
            r, v7, kk, lw, kd, kka, gate7, bonus = _rw_prep_call(
                h, mod[i], norm_mix[i], w["w_rw"], w["mu"], w["w2"], w["w0"], w["a2"], w["a0"], w["g2"], w["kkw"],
                w["kaw"], w["rk"], w["ones"], ctx_tiles=ctx_tiles)
            o_f, o_b, y_f, y_b = _scans_call(_dn_chunk_call(q, k, v, small), _rw_chunk_call(r, v7, kk, lw, kd, kka),
                                             ctx_chunks=ctx_chunks)
            h = _ev_out_call(h, mod[i], o_f, o_b, dgate, y_f, y_b, gate7, bonus, jnp.tile(dn_norm[j], DN_HEADS),
                             rw_gn_w[j], rw_gn_b[j], w["ones"], bf(ev_w_out[j]), ctx_tiles=ctx_tiles)
            h = _ffn_call(h, mod[i], norm_ffn2[i], bf(ffn2_w1[i]), bf(ffn2_w3[i]), bf(ffn2_w2[i]), j0=6,
                          ctx_tiles=ctx_tiles)
        else:
            params, layouts = _odd_layer_weights(od_w_in[j], gq_q_norm[j], gq_k_norm[j], ml_q_norm[j], ml_w_uq[j],
                                                 ml_kv_norm[j], ml_w_ukv[j])
            tables = [t for rot, angle, first in layouts for t in _rope_tables(n_ctx, n_lat, rot, angle, first)]
            qg, qm, kgt, vg, kmt, vm = _od_prep_call(h, mod[i], norm_mix[i], params, tables, ctx_tiles=ctx_tiles)
            hl = _attn_call(h, mod[i], qg, qm, kgt, vg, kmt, vm, bf(od_w_out[j]), ctx_tiles=ctx_tiles)
            assert last
            h = _ffn_call(hl, mod[i], norm_ffn2[i], bf(ffn2_w1[i]), bf(ffn2_w3[i]), bf(ffn2_w2[i]), j0=6, ctx_tiles=0,
                          final_gain=final_norm)
    return h
```

```python
import functools

import jax
import jax.numpy as jnp
import numpy as np
from jax import lax
from jax.experimental import pallas as pl
from jax.experimental.pallas import tpu as pltpu

F32 = jnp.float32
BF16 = jnp.bfloat16

NORM_EPS = 1e-6
ROPE_THETA = 10000.0
GRID_W = 64
N_MOD = 9

DN_HEADS = 4
DN_DK = 128
DN_CONV = 5
RW_HEADS = 8
RW_HS = 64
RW_W = RW_HEADS * RW_HS
RW_W_LORA = 64
RW_A_LORA = 64
RW_G_LORA = 160
RW_GN_EPS = 64e-5
GQ_HEADS = 8
GQ_KV_HEADS = 2
GQ_HD = 64
ML_HEADS = 8
ML_NOPE = 64
ML_ROPE = 32
ML_V = 64

TILE = 256
CHUNK = 64
HALO = 8
LANES = 128
MXU_DIM = 256
VMEM_LIMIT = 56 * 1024 * 1024
SCAN_BATCH = 8
ATT_TILES = 2
CHUNKS_PER_STEP = 4
LOG2_E = 1.4426950408889634


def _cparams(*sem):
    return pltpu.CompilerParams(dimension_semantics=sem, vmem_limit_bytes=VMEM_LIMIT)


def _resident(shape):
    nd = len(shape)
    return pl.BlockSpec(shape, lambda *_: (0,) * nd, pipeline_mode=pl.Buffered(1))


def _mm(a, b):
    return jnp.dot(a.astype(BF16), b.astype(BF16), preferred_element_type=F32)


def _mm_nt(a, b):
    return lax.dot_general(a.astype(BF16), b.astype(BF16), (((1,), (1,)), ((), ())),
                           preferred_element_type=F32)


def _split2(x):
    hi = x.astype(BF16)
    lo = (x - hi.astype(F32)).astype(BF16)
    return hi, lo


def _split3(x):
    hi = x.astype(BF16)
    r = x - hi.astype(F32)
    mid = r.astype(BF16)
    lo = (r - mid.astype(F32)).astype(BF16)
    return hi, mid, lo


def _dot(a, b):
    return jnp.dot(a, b, preferred_element_type=F32)


def _mm3s(asp, bsp):
    (ah, al), (bh, bl) = asp, bsp
    return _dot(ah, bh) + (_dot(ah, bl) + _dot(al, bh))


def _mm3(a, b):
    return _mm3s(_split2(a), _split2(b))


def _mm_exact_lhs(a01, b):
    a = a01.astype(BF16)
    hi, mid, lo = _split3(b)
    return _dot(a, hi) + (_dot(a, mid) + _dot(a, lo))


def _rms_rows(x):
    return x * lax.rsqrt(jnp.mean(x * x, axis=-1, keepdims=True) + NORM_EPS)


def _modulate(x, gain, shift, scale):
    return (_rms_rows(x) * gain) * (1.0 + scale) + shift


def _silu(x):
    return x * jax.nn.sigmoid(x)


def _softplus(x):
    return jnp.maximum(x, 0.0) + jnp.log1p(jnp.exp(-jnp.abs(x)))


def _mod_kernel(c_ref, w_ref, b_ref, o_ref):
    s = _silu(c_ref[...])
    o_ref[0] = _mm3(s, w_ref[0]) + b_ref[0]


def _mod_call(cc, mod_w, mod_b):
    n_layers, d, n = mod_w.shape
    r = cc.shape[0]
    tn = n // 8
    return pl.pallas_call(
        _mod_kernel,
        out_shape=jax.ShapeDtypeStruct((n_layers, r, n), F32),
        grid=(n_layers, n // tn),
        in_specs=[pl.BlockSpec((r, d), lambda l, j: (0, 0)),
                  pl.BlockSpec((1, d, tn), lambda l, j: (l, 0, j)),
                  pl.BlockSpec((1, 1, tn), lambda l, j: (l, 0, j))],
        out_specs=pl.BlockSpec((1, r, tn), lambda l, j: (l, 0, j)),
        compiler_params=_cparams("parallel", "parallel"),
        name="adaln_mod",
    )(cc, mod_w, mod_b.reshape(n_layers, 1, n))


def _all_mod(c, c_ctx, mod_w, mod_b):
    n_batch, d = c.shape
    rows = -(-(n_batch + 1) // 8) * 8
    cc = jnp.zeros((rows, d), F32).at[:n_batch].set(c).at[n_batch].set(c_ctx)
    return _mod_call(cc, mod_w, mod_b).reshape(mod_w.shape[0], rows, N_MOD, d)


def _mod_row_map(n_batch, ctx_tiles, t_off):
    def index_map(b, t):
        return (jnp.where(t + t_off < ctx_tiles, n_batch, b), 0, 0)
    return index_map


def _ffn_kernel(*refs, j0, final, split_tiles, sub, ctx_tiles, t_off):
    if split_tiles:
        ctx_refs, lat_refs, refs = refs[:sub], refs[sub:2 * sub], refs[2 * sub:]
        xs = [jnp.where(pl.program_id(1) * sub + i < split_tiles, c[0], l[0])
              for i, (c, l) in enumerate(zip(ctx_refs, lat_refs))]
    else:
        h_ref, *refs = refs
        xs = [h_ref[0, i * TILE:(i + 1) * TILE] for i in range(sub)]
    mod_ctx_ref, mod_ref, gain_ref, w1_ref, w3_ref, w2_ref, *rest = refs
    o_ref = rest[-1]
    f = w1_ref.shape[1]
    cut = -(-(f // MXU_DIM) // 2) * MXU_DIM
    fcs = [slice(0, cut), slice(cut, f)] if 0 < cut < f else [slice(0, f)]
    for i, x in enumerate(xs):
        is_ctx = (pl.program_id(1) * sub + i + t_off) < ctx_tiles
        row = lambda j: jnp.where(is_ctx, mod_ctx_ref[0, j:j + 1], mod_ref[0, j:j + 1])
        xn = _modulate(x, gain_ref[...], row(j0), row(j0 + 1)).astype(BF16)
        ups = [(_dot(xn, w1_ref[:, fc]), _dot(xn, w3_ref[:, fc])) for fc in fcs]
        y = None
        for (a, b), fc in zip(ups, fcs):
            part = _dot((_silu(a) * b).astype(BF16), w2_ref[fc, :])
            y = part if y is None else y + part
        y = x + (0.5 * row(j0 + 2)) * y
        if final:
            y = _rms_rows(y) * rest[0][...]
        o_ref[0, i * TILE:(i + 1) * TILE] = y


def _ffn_call(h, mod, gain, w1, w3, w2, *, j0, ctx_tiles, t_off=0, final_gain=None):
    split = isinstance(h, tuple)
    if split:
        ctx, lat = h
        n_batch, _, d = lat.shape
        t_len = ctx.shape[1] + lat.shape[1]
        last_ctx = ctx_tiles - 1
        sub = next(s for s in (3, 2, 1) if (t_len // TILE) % s == 0)
        streams = [ctx] * sub + [lat] * sub
        stream_specs = ([pl.BlockSpec((1, TILE, d), lambda b, t, i=i: (b, jnp.minimum(sub * t + i, last_ctx), 0))
                         for i in range(sub)]
                        + [pl.BlockSpec((1, TILE, d), lambda b, t, i=i: (b, jnp.maximum(sub * t + i - ctx_tiles, 0), 0))
                           for i in range(sub)])
    else:
        n_batch, t_len, d = h.shape
        nt_all = t_len // TILE - t_off
        sub = next(s for s in (3, 2, 1) if nt_all % s == 0 and t_off % s == 0)
        streams = [h]
        stream_specs = [pl.BlockSpec((1, sub * TILE, d), lambda b, t: (b, t + t_off // sub, 0))]
    f = w1.shape[1]
    nt = t_len // TILE - t_off
    final = final_gain is not None
    in_specs = stream_specs + [pl.BlockSpec((1, N_MOD, d), lambda b, t: (n_batch, 0, 0)),
                               pl.BlockSpec((1, N_MOD, d), lambda b, t: (b, 0, 0)),
                               _resident((1, d)), _resident((d, f)), _resident((d, f)), _resident((f, d))]
    args = streams + [mod, mod, gain.reshape(1, d), w1, w3, w2]
    if final:
        in_specs.append(_resident((1, d)))
        args.append(final_gain.reshape(1, d))
    return pl.pallas_call(
        functools.partial(_ffn_kernel, j0=j0, final=final, split_tiles=ctx_tiles if split else 0, sub=sub,
                          ctx_tiles=ctx_tiles, t_off=t_off),
        out_shape=jax.ShapeDtypeStruct((n_batch, nt * TILE, d), F32),
        grid=(n_batch, nt // sub),
        in_specs=in_specs,
        out_specs=pl.BlockSpec((1, sub * TILE, d), lambda b, t: (b, t, 0)),
        compiler_params=_cparams("parallel", "parallel"),
        name="macaron_ffn",
    )(*args)


def _halo_specs(d, ctx_tiles, n_tiles):
    per = TILE // HALO
    last = n_tiles * per - 1
    return [pl.BlockSpec((1, HALO, d), lambda b, t: (b, jnp.maximum(t * per - 1, 0), 0)),
            pl.BlockSpec((1, TILE, d), lambda b, t: (b, t, 0)),
            pl.BlockSpec((1, HALO, d), lambda b, t: (b, jnp.minimum((t + 1) * per, last), 0))]


def _project_with_halo(prev_ref, cur_ref, next_ref, mod_ref, gain_ref, w_ref, pe_ref, *, ctx_tiles, n_tiles):
    t = pl.program_id(1)
    xe = jnp.concatenate([prev_ref[0], cur_ref[0], next_ref[0]], axis=0)
    xn = _modulate(xe, gain_ref[...], mod_ref[0, 3:4], mod_ref[0, 4:5]).astype(BF16)
    p = _dot(xn, w_ref[...])
    row = lax.broadcasted_iota(jnp.int32, (TILE + 2 * HALO, 1), 0)
    prev_ok = jnp.logical_and(t > 0, t != ctx_tiles)
    next_ok = jnp.logical_and(t + 1 < n_tiles, t + 1 != ctx_tiles)
    keep = jnp.logical_and(jnp.logical_or(row >= HALO, prev_ok),
                           jnp.logical_or(row < HALO + TILE, next_ok))
    pe_ref[...] = jnp.where(keep, p, 0.0)


def _dn_prep_kernel(prev_ref, cur_ref, next_ref, mod_ref, gain_ref, w_ref, conv_ref, alog_ref, dtb_ref,
                    q_ref, k_ref, v_ref, small_ref, gate_ref, pe_ref, *, ctx_tiles, n_tiles):
    _project_with_halo(prev_ref, cur_ref, next_ref, mod_ref, gain_ref, w_ref, pe_ref,
                       ctx_tiles=ctx_tiles, n_tiles=n_tiles)
    nqkv = 3 * DN_HEADS * DN_DK
    half = DN_CONV // 2
    acc = None
    for j in range(DN_CONV):
        term = conv_ref[j:j + 1, :] * pe_ref[pl.ds(HALO - half + j, TILE), 0:nqkv]
        acc = term if acc is None else acc + term
    qkv = _silu(acc)
    w = DN_HEADS * DN_DK
    for idx, ref in ((0, q_ref), (1, k_ref)):
        for h in range(DN_HEADS):
            seg = qkv[:, idx * w + h * DN_DK: idx * w + (h + 1) * DN_DK]
            ref[0, :, h * DN_DK:(h + 1) * DN_DK] = seg * lax.rsqrt(jnp.sum(seg * seg, axis=-1, keepdims=True) + 1e-6)
    v_ref[0] = qkv[:, 2 * w:3 * w]
    gate_ref[0] = pe_ref[pl.ds(HALO, TILE), nqkv:nqkv + w].astype(BF16)
    ab = pe_ref[pl.ds(HALO, TILE), nqkv + w:nqkv + w + LANES]
    g = -jnp.exp(alog_ref[...]) * _softplus(ab + dtb_ref[...])
    lane = lax.broadcasted_iota(jnp.int32, ab.shape, 1)
    nh2 = 2 * DN_HEADS
    small_ref[0] = jnp.where(lane < nh2, g, jnp.where(lane < 2 * nh2, jax.nn.sigmoid(ab), 0.0))


def _dn_prep_call(h, mod, gain, w, conv, alog, dtb, *, ctx_tiles):
    n_batch, t_len, d = h.shape
    nt = t_len // TILE
    wd = DN_HEADS * DN_DK
    out = lambda n, dt=F32: jax.ShapeDtypeStruct((n_batch, t_len, n), dt)
    ospec = lambda n: pl.BlockSpec((1, TILE, n), lambda b, t: (b, t, 0))
    return pl.pallas_call(
        functools.partial(_dn_prep_kernel, ctx_tiles=ctx_tiles, n_tiles=nt),
        out_shape=[out(wd), out(wd), out(wd), out(LANES), out(wd, BF16)],
        grid=(n_batch, nt),
        in_specs=_halo_specs(d, ctx_tiles, nt) + [
            pl.BlockSpec((1, N_MOD, d), _mod_row_map(n_batch, ctx_tiles, 0)),
            _resident((1, d)), _resident(w.shape), _resident(conv.shape),
            _resident((1, LANES)), _resident((1, LANES))],
        out_specs=[ospec(wd), ospec(wd), ospec(wd), ospec(LANES), ospec(wd)],
        scratch_shapes=[pltpu.VMEM((TILE + 2 * HALO, w.shape[1]), F32)],
        compiler_params=_cparams("parallel", "parallel"),
        name="deltanet_prep",
    )(h, h, h, mod, gain.reshape(1, d), w, conv, alog, dtb)


def _head_sums(x, ones_ref):
    ones = ones_ref[...]
    out = []
    for g in range(x.shape[1] // LANES):
        hi, lo = _split2(x[:, g * LANES:(g + 1) * LANES])
        out.append(_dot(hi, ones) + _dot(lo, ones))
    return out[0] if len(out) == 1 else jnp.concatenate(out, axis=1)


def _rw_prep_kernel(prev_ref, cur_ref, next_ref, mod_ref, gain_ref, w_ref, mu_ref, w2_ref, w0_ref, a2_ref,
                    a0_ref, g2_ref, kkw_ref, kaw_ref, rk_ref, ones_ref,
                    r_ref, v_ref, kk_ref, lw_ref, kd_ref, kka_ref, gate_ref, bonus_ref, pe_ref,
                    *, ctx_tiles, n_tiles):
    _project_with_halo(prev_ref, cur_ref, next_ref, mod_ref, gain_ref, w_ref, pe_ref,
                       ctx_tiles=ctx_tiles, n_tiles=n_tiles)
    z = pe_ref[pl.ds(HALO, TILE), :]
    zs = 0.5 * (pe_ref[pl.ds(HALO - 1, TILE), :] + pe_ref[pl.ds(HALO + 1, TILE), :])
    s = z + mu_ref[...] * (zs - z)
    r, k7, v7 = s[:, 0:RW_W], s[:, RW_W:2 * RW_W], s[:, 2 * RW_W:3 * RW_W]
    o = 3 * RW_W
    wd, ad, gd = s[:, o:o + LANES], s[:, o + LANES:o + 2 * LANES], s[:, o + 2 * LANES:o + 4 * LANES]
    w_logit = _mm3(jnp.tanh(wd), w2_ref[...]) + w0_ref[...]
    lw = -float(np.exp(-0.5)) * jax.nn.sigmoid(w_logit)
    a = jax.nn.sigmoid(_mm3(ad, a2_ref[...]) + a0_ref[...])
    gate_ref[0] = _mm3(jax.nn.sigmoid(gd), g2_ref[...]).astype(BF16)
    kx = k7 * kkw_ref[...]
    kk = kx * lax.rsqrt(_head_sums(kx * kx, ones_ref) + 1e-6)
    r_ref[0], v_ref[0], kk_ref[0], lw_ref[0] = r, v7, kk, lw
    kd_sum = None
    for d in range(2):
        a_d = a[:, d * RW_W:(d + 1) * RW_W]
        kd = k7 * (1.0 + (a_d - 1.0) * kaw_ref[...])
        kd_ref[0, :, d * RW_W:(d + 1) * RW_W] = kd
        kka_ref[0, :, d * RW_W:(d + 1) * RW_W] = kk * a_d
        kd_sum = kd if kd_sum is None else kd_sum + kd
    bonus_ref[0] = (_head_sums((r * rk_ref[...]) * kd_sum, ones_ref) * v7).astype(BF16)


def _rw_prep_call(h, mod, gain, w, mu, w2, w0, a2, a0, g2, kkw, kaw, rk, ones, *, ctx_tiles):
    n_batch, t_len, d = h.shape
    nt = t_len // TILE
    out = lambda n, dt: jax.ShapeDtypeStruct((n_batch, t_len, n), dt)
    ospec = lambda n: pl.BlockSpec((1, TILE, n), lambda b, t: (b, t, 0))
    widths = [RW_W, RW_W, RW_W, 2 * RW_W, 2 * RW_W, 2 * RW_W, RW_W, RW_W]
    dtypes = [F32] * 6 + [BF16] * 2
    params = [gain.reshape(1, d), w, mu, w2, w0, a2, a0, g2, kkw, kaw, rk, ones]
    return pl.pallas_call(
        functools.partial(_rw_prep_kernel, ctx_tiles=ctx_tiles, n_tiles=nt),
        out_shape=[out(n, dt) for n, dt in zip(widths, dtypes)],
        grid=(n_batch, nt),
        in_specs=_halo_specs(d, ctx_tiles, nt) + [pl.BlockSpec((1, N_MOD, d), _mod_row_map(n_batch, ctx_tiles, 0))]
        + [_resident(p.shape) for p in params],
        out_specs=[ospec(n) for n in widths],
        scratch_shapes=[pltpu.VMEM((TILE + 2 * HALO, w.shape[1]), F32)],
        compiler_params=_cparams("parallel", "parallel"),
        name="rwkv_prep",
    )(h, h, h, mod, *params)


def _chunk_masks(direction, width=CHUNK):
    i = lax.broadcasted_iota(jnp.int32, (CHUNK, width), 0)
    j = lax.broadcasted_iota(jnp.int32, (CHUNK, width), 1) % CHUNK
    return (i >= j, i > j) if direction == 0 else (i <= j, i < j)


def _last_row(x, direction):
    return x[CHUNK - 1:CHUNK] if direction == 0 else x[0:1]


def _bd(x):
    shape = (2 * CHUNK, x.shape[1])
    r = lax.broadcasted_iota(jnp.int32, shape, 0)
    c = lax.broadcasted_iota(jnp.int32, shape, 1)
    return jnp.where((r < CHUNK) == (c < x.shape[1] // 2), jnp.concatenate([x, x], axis=0), 0.0)


def _neumann_inverse_pairs(ns, refine):
    i = lax.broadcasted_iota(jnp.int32, (CHUNK, 2 * CHUNK), 0)
    j = lax.broadcasted_iota(jnp.int32, (CHUNK, 2 * CHUNK), 1) % CHUNK
    eye = jnp.where(i == j, 1.0, 0.0)
    one_pass = lambda a, b: _dot(a.astype(BF16), b.astype(BF16))
    rs, ps = list(ns), list(ns)
    span = 2
    while span < CHUNK:
        ps = [one_pass(p, _bd(p)) for p in ps]
        rs = [r + p + one_pass(r, _bd(p)) for r, p in zip(rs, ps)]
        span *= 2
    if not refine:
        return [eye + r for r in rs]
    res = [_mm3(n, _bd(eye + r)) - r for n, r in zip(ns, rs)]
    return [eye + (r + (e + one_pass(r, _bd(e)))) for r, e in zip(rs, res)]


def _rev_chunk(n, ctx_chunks, n_chunks):
    return jnp.where(n < ctx_chunks, ctx_chunks - 1 - n, n_chunks - 1 + ctx_chunks - n)


def _dn_chunk_kernel(q_ref, k_ref, v_ref, small_ref, u_ref, w_ref, qd_ref, kdt_ref, attn_ref, gl_ref):
    nh2, n_pairs, pw, dk = 2 * DN_HEADS, DN_HEADS // 2, 2 * DN_DK, DN_DK
    first_c = lax.broadcasted_iota(jnp.int32, (CHUNK, 2 * CHUNK), 1) < CHUNK
    first_f = lax.broadcasted_iota(jnp.int32, (CHUNK, pw), 1) < dk

    def cols(x, c, first):
        return jnp.where(first[:x.shape[0]], x[:, c:c + 1], x[:, c + 1:c + 2])

    loaded = []
    for cc in range(CHUNKS_PER_STEP):
        rows = slice(cc * CHUNK, (cc + 1) * CHUNK)
        sm = small_ref[0, rows]
        q, k, v = q_ref[0, rows] * (DN_DK ** -0.5), k_ref[0, rows], v_ref[0, rows]
        grams = [_mm_nt(jnp.concatenate([k[:, j * pw:(j + 1) * pw], q[:, j * pw:(j + 1) * pw]], axis=0),
                        _bd(k[:, j * pw:(j + 1) * pw])) for j in range(n_pairs)]
        gcs = [_mm_exact_lhs(jnp.where(_chunk_masks(d)[0], 1.0, 0.0), sm) for d in range(2)]
        loaded.append((cc, rows, sm, q, k, v, grams, gcs))
    work = []
    for cc, rows, sm, q, k, v, grams, gcs in loaded:
        for d in range(2):
            incl, strict = _chunk_masks(d, 2 * CHUNK)
            gc = gcs[d]
            gc_t = gc.T
            gtot = _last_row(gc, d)
            gl_ref[0, d, cc] = jnp.exp(gtot)
            for j in range(n_pairs):
                c = DN_HEADS * d + 2 * j
                gcr = jnp.concatenate([gc_t[c:c + 1, :], gc_t[c + 1:c + 2, :]], axis=1)
                decay = jnp.exp(jnp.where(incl, cols(gc, c, first_c) - gcr, -1e30))
                lower = jnp.where(strict, (cols(sm, nh2 + c, first_c) * grams[j][:CHUNK]) * decay, 0.0)
                work.append((cc, rows, d, j, c, sm, q, k, v, gc, gtot, decay, grams[j][CHUNK:], -lower))
    rhs_list = []
    for cc, rows, d, j, c, sm, q, k, v, gc, gtot, decay, qk, _ in work:
        sl = slice(j * pw, (j + 1) * pw)
        beta, gcc, gt = cols(sm, nh2 + c, first_f), cols(gc, c, first_f), cols(gtot, c, first_f)
        egc = jnp.exp(gcc)
        kp, qp = k[:, sl], q[:, sl]
        vb, ke = v[:, sl] * beta, (kp * beta) * egc
        rhs_list.append(_bd(jnp.concatenate([vb[:, :dk], ke[:, :dk], vb[:, dk:], ke[:, dk:]], axis=1)))
        qd_ref[0, d, rows, sl] = (qp * egc).astype(BF16)
        k_tail = kp * jnp.exp(gt - gcc)
        kdt_ref[0, d, cc, j * dk:(j + 1) * dk, :] = jnp.concatenate([k_tail[:, :dk].T, k_tail[:, dk:].T], axis=1).astype(BF16)
        attn_ref[0, d, rows, 2 * j * CHUNK:2 * (j + 1) * CHUNK] = (qk * decay).astype(BF16)
    t_invs = _neumann_inverse_pairs([item[-1] for item in work], refine=True)
    for (cc, rows, d, j, *_), t_inv, rhs in zip(work, t_invs, rhs_list):
        sl = slice(j * pw, (j + 1) * pw)
        sol = _mm(t_inv, rhs)
        u_ref[0, d, rows, sl] = jnp.concatenate([sol[:, :dk], sol[:, 2 * dk:3 * dk]], axis=1).astype(BF16)
        w_ref[0, d, rows, sl] = jnp.concatenate([sol[:, dk:2 * dk], sol[:, 3 * dk:]], axis=1).astype(BF16)


def _dn_chunk_call(q, k, v, small):
    n_batch, t_len, wd = q.shape
    nc = t_len // CHUNK
    cps = CHUNKS_PER_STEP
    ispec = lambda n: pl.BlockSpec((1, cps * CHUNK, n), lambda b, c: (b, c, 0))
    ospec = lambda n: pl.BlockSpec((1, 2, cps * CHUNK, n), lambda b, c: (b, 0, c, 0))
    shp = lambda n, dt: jax.ShapeDtypeStruct((n_batch, 2, t_len, n), dt)
    return pl.pallas_call(
        _dn_chunk_kernel,
        out_shape=[shp(wd, BF16), shp(wd, BF16), shp(wd, BF16),
                   jax.ShapeDtypeStruct((n_batch, 2, nc, wd // 2, 2 * CHUNK), BF16), shp(DN_HEADS * CHUNK, BF16),
                   jax.ShapeDtypeStruct((n_batch, 2, nc, 1, LANES), F32)],
        grid=(n_batch, nc // cps),
        in_specs=[ispec(wd), ispec(wd), ispec(wd), ispec(LANES)],
        out_specs=[ospec(wd), ospec(wd), ospec(wd),
                   pl.BlockSpec((1, 2, cps, wd // 2, 2 * CHUNK), lambda b, c: (b, 0, c, 0, 0)), ospec(DN_HEADS * CHUNK),
                   pl.BlockSpec((1, 2, cps, 1, LANES), lambda b, c: (b, 0, c, 0, 0))],
        compiler_params=_cparams("parallel", "parallel"),
        name="deltanet_chunk_prep",
    )(q, k, v, small)


def _dn_scan_kernel(*refs):
    ins, (of_ref, ob_ref, s_ref) = refs[:12], refs[12:]

    @pl.when(pl.program_id(1) == 0)
    def _():
        s_ref[...] = jnp.zeros_like(s_ref)

    dk = DN_DK
    chains = [(b, d, j) for b in range(SCAN_BATCH) for d in range(2) for j in range(DN_HEADS // 2)]
    outs = (of_ref, ob_ref)
    stage1 = []
    for b, d, j in chains:
        u_ref, w_ref, qd_ref = ins[6 * d:6 * d + 3]
        s = s_ref[b, d, j]
        sb = s.astype(BF16)
        halves = [(slice((2 * j + i) * dk, (2 * j + i + 1) * dk), slice(i * dk, (i + 1) * dk)) for i in range(2)]
        v_new = jnp.concatenate([u_ref[b, 0, :, sl] - _dot(w_ref[b, 0, :, sl], sb[:, hl]) for sl, hl in halves], axis=1)
        inter = jnp.concatenate([_dot(qd_ref[b, 0, :, sl], sb[:, hl]) for sl, hl in halves], axis=1)
        stage1.append((s, v_new, inter))
    for (b, d, j), (s, v_new, inter) in zip(chains, stage1):
        kdt_ref, attn_ref, gl_ref = ins[6 * d + 3:6 * d + 6]
        v_bd = _bd(v_new).astype(BF16)
        intra = _dot(attn_ref[b, 0, :, 2 * j * CHUNK:2 * (j + 1) * CHUNK], v_bd)
        outs[d][b, :, 2 * j * dk:2 * (j + 1) * dk] = (inter + intra).astype(BF16)
        c = DN_HEADS * d + 2 * j
        decayed = jnp.concatenate([s[:, i * dk:(i + 1) * dk] * gl_ref[b, 0, 0, :, c + i:c + i + 1] for i in range(2)], axis=1)
        s_ref[b, d, j] = decayed + _dot(kdt_ref[b, 0, 0, j * dk:(j + 1) * dk, :], v_bd)


def _scan_operands(per_token_arrays, per_chunk_arrays, order, *, ctx_chunks, n_chunks):
    in_specs, args = [], []
    for d in range(2):
        chunk = (lambda n: n) if d == 0 else functools.partial(_rev_chunk, ctx_chunks=ctx_chunks, n_chunks=n_chunks)
        for name in order:
            if name in per_token_arrays:
                a = per_token_arrays[name]
                if a.ndim == 4:
                    spec = pl.BlockSpec((SCAN_BATCH, 1, CHUNK, a.shape[-1]),
                                        lambda b, n, d=d, chunk=chunk: (b, d, chunk(n), 0))
                else:
                    spec = pl.BlockSpec((SCAN_BATCH, CHUNK, a.shape[-1]), lambda b, n, chunk=chunk: (b, chunk(n), 0))
            else:
                a = per_chunk_arrays[name]
                spec = pl.BlockSpec((SCAN_BATCH, 1, 1) + a.shape[3:],
                                    lambda b, n, d=d, chunk=chunk: (b, d, chunk(n), 0, 0))
            in_specs.append(spec)
            args.append(a)
    return in_specs, args


def _scans_kernel(*refs):
    n_dn, n_rw = 12, 16
    of_ref, ob_ref, yf_ref, yb_ref, s_dn_ref, s_rw_ref = refs[n_dn + n_rw:]
    _dn_scan_kernel(*refs[:n_dn], of_ref, ob_ref, s_dn_ref)
    _rw_scan_kernel(*refs[n_dn:n_dn + n_rw], yf_ref, yb_ref, s_rw_ref)


def _scans_call(dn, rw, *, ctx_chunks):
    u, w, qd, kdt, attn, gl = dn
    ut, wt, rt, arb, kbt, y0, pc, v = rw
    n_batch, _, t_len, wd_dn = u.shape
    wd_rw = ut.shape[-1]
    nc = t_len // CHUNK
    dn_specs, dn_args = _scan_operands(dict(u=u, w=w, qd=qd, attn=attn), dict(kdt=kdt, gl=gl),
                                       ("u", "w", "qd", "kdt", "attn", "gl"), ctx_chunks=ctx_chunks, n_chunks=nc)
    rw_specs, rw_args = _scan_operands(dict(ut=ut, wt=wt, rt=rt, arb=arb, y0=y0, v=v), dict(kbt=kbt, pc=pc),
                                       ("ut", "wt", "rt", "arb", "kbt", "y0", "pc", "v"),
                                       ctx_chunks=ctx_chunks, n_chunks=nc)
    fwd = lambda wd: pl.BlockSpec((SCAN_BATCH, CHUNK, wd), lambda b, n: (b, n, 0))
    bwd = lambda wd: pl.BlockSpec((SCAN_BATCH, CHUNK, wd), lambda b, n: (b, _rev_chunk(n, ctx_chunks, nc), 0))
    out = lambda wd: jax.ShapeDtypeStruct((n_batch, t_len, wd), BF16)
    return pl.pallas_call(
        _scans_kernel,
        out_shape=[out(wd_dn), out(wd_dn), out(wd_rw), out(wd_rw)],
        grid=(n_batch // SCAN_BATCH, nc),
        in_specs=dn_specs + rw_specs,
        out_specs=[fwd(wd_dn), bwd(wd_dn), fwd(wd_rw), bwd(wd_rw)],
        scratch_shapes=[pltpu.VMEM((SCAN_BATCH, 2, DN_HEADS // 2, DN_DK, 2 * DN_DK), F32),
                        pltpu.VMEM((SCAN_BATCH, 2, RW_HEADS // 2, 2 * RW_HS, 2 * RW_HS), F32)],
        compiler_params=_cparams("parallel", "arbitrary"),
        name="chunk_scans",
    )(*dn_args, *rw_args)


def _rw_chunk_kernel(r_ref, v_ref, kk_ref, lw_ref, kd_ref, kka_ref,
                     ut_ref, wt_ref, rt_ref, arb_ref, kbt_ref, y0_ref, pc_ref, vb_ref):
    pw = 2 * RW_HS
    vb_ref[0] = v_ref[0].astype(BF16)
    prepared = []
    for cc in range(CHUNKS_PER_STEP):
        rows = slice(cc * CHUNK, (cc + 1) * CHUNK)
        r, kk = r_ref[0, rows], kk_ref[0, rows]
        for d in range(2):
            dsl = slice(d * RW_W, (d + 1) * RW_W)
            lw, kd, kka = lw_ref[0, rows, dsl], kd_ref[0, rows, dsl], kka_ref[0, rows, dsl]
            cl = _mm_exact_lhs(jnp.where(_chunk_masks(d)[0], 1.0, 0.0), lw)
            tot = _last_row(cl, d)
            p_inv, p_tail = jnp.exp(-cl), jnp.exp(tot - cl)
            at = -kk * jnp.exp(cl - lw)
            rt = r * jnp.exp(cl)
            rt_ref[0, d, rows] = rt.astype(BF16)
            pc_ref[0, d, cc] = jnp.broadcast_to(jnp.exp(tot), (8, RW_W))
            kbt_ref[0, d, cc] = jnp.concatenate([(kd * p_tail).T, (kka * p_tail).T], axis=1).astype(BF16)
            prepared.append((rows, d, at, rt, kd * p_inv, kka * p_inv))
    work, n_list = [], []
    for rows, d, at, rt, kh, bh in prepared:
        incl, strict = _chunk_masks(d, 2 * CHUNK)
        for j in range(RW_HEADS // 2):
            sl = slice(j * pw, (j + 1) * pw)
            aa = _mm_nt(jnp.concatenate([at[:, sl], rt[:, sl]], axis=0),
                        jnp.concatenate([_bd(bh[:, sl]), _bd(kh[:, sl])], axis=0))
            n_list.append(jnp.where(strict, aa[:CHUNK, :pw], 0.0))
            work.append((rows, d, sl, at[:, sl], jnp.where(strict, aa[:CHUNK, pw:], 0.0),
                         jnp.where(incl, aa[CHUNK:, :pw], 0.0), jnp.where(incl, aa[CHUNK:, pw:], 0.0)))
    t_invs = _neumann_inverse_pairs(n_list, refine=False)
    rhs = []
    for rows, d, sl, at_p, a_ak, a_rb, a_rk in work:
        v_bd = _bd(v_ref[0, rows, sl]).astype(BF16)
        arb_ref[0, d, rows, sl] = a_rb.astype(BF16)
        y0_ref[0, d, rows, sl] = _dot(a_rk.astype(BF16), v_bd).astype(BF16)
        rhs.append(jnp.concatenate([_bd(at_p), _bd(_dot(a_ak.astype(BF16), v_bd))], axis=1))
    for (rows, d, sl, *_), t_inv, x in zip(work, t_invs, rhs):
        sol = _mm(t_inv, x)
        wt_ref[0, d, rows, sl] = sol[:, :pw].astype(BF16)
        ut_ref[0, d, rows, sl] = sol[:, pw:].astype(BF16)


def _rw_chunk_call(r, v, kk, lw, kd, kka):
    n_batch, t_len, wd = r.shape
    nc = t_len // CHUNK
    cps = CHUNKS_PER_STEP
    ispec = lambda n: pl.BlockSpec((1, cps * CHUNK, n), lambda b, c: (b, c, 0))
    ospec = pl.BlockSpec((1, 2, cps * CHUNK, wd), lambda b, c: (b, 0, c, 0))
    shp = jax.ShapeDtypeStruct((n_batch, 2, t_len, wd), BF16)
    return pl.pallas_call(
        _rw_chunk_kernel,
        out_shape=[shp] * 4 + [jax.ShapeDtypeStruct((n_batch, 2, nc, wd, 2 * CHUNK), BF16), shp,
                               jax.ShapeDtypeStruct((n_batch, 2, nc, 8, wd), F32),
                               jax.ShapeDtypeStruct((n_batch, t_len, wd), BF16)],
        grid=(n_batch, nc // cps),
        in_specs=[ispec(wd), ispec(wd), ispec(wd), ispec(2 * wd), ispec(2 * wd), ispec(2 * wd)],
        out_specs=[ospec] * 4 + [pl.BlockSpec((1, 2, cps, wd, 2 * CHUNK), lambda b, c: (b, 0, c, 0, 0)), ospec,
                                 pl.BlockSpec((1, 2, cps, 8, wd), lambda b, c: (b, 0, c, 0, 0)), ispec(wd)],
        compiler_params=_cparams("parallel", "parallel"),
        name="rwkv_chunk_prep",
    )(r, v, kk, lw, kd, kka)


def _rw_scan_kernel(*refs):
    ins, (yf_ref, yb_ref, s_ref) = refs[:16], refs[16:]

    @pl.when(pl.program_id(1) == 0)
    def _():
        s_ref[...] = jnp.zeros_like(s_ref)

    pw = 2 * RW_HS
    r_i = lax.broadcasted_iota(jnp.int32, (pw, pw), 0)
    c_i = lax.broadcasted_iota(jnp.int32, (pw, pw), 1)
    same_head = (r_i < RW_HS) == (c_i < RW_HS)
    chains = [(b, d, j) for b in range(SCAN_BATCH) for d in range(2) for j in range(RW_HEADS // 2)]
    outs = (yf_ref, yb_ref)
    decay_cols = {(b, d): ins[8 * d + 6][b, 0, 0].T for b in range(SCAN_BATCH) for d in range(2)}
    stage1 = []
    for b, d, j in chains:
        ut_ref, wt_ref, rt_ref, _, _, y0_ref = ins[8 * d:8 * d + 6]
        sl = slice(j * pw, (j + 1) * pw)
        s = s_ref[b, d, j]
        sb = s.astype(BF16)
        u = ut_ref[b, 0, :, sl] + _dot(wt_ref[b, 0, :, sl], sb)
        stage1.append((s, u, y0_ref[b, 0, :, sl] + _dot(rt_ref[b, 0, :, sl], sb)))
    for (b, d, j), (s, u, y_inter) in zip(chains, stage1):
        arb_ref, kbt_ref, _, _, v_ref = ins[8 * d + 3:8 * d + 8]
        sl = slice(j * pw, (j + 1) * pw)
        outs[d][b, :, sl] = (y_inter + _dot(arb_ref[b, 0, :, sl], _bd(u).astype(BF16))).astype(BF16)
        grow = _dot(kbt_ref[b, 0, 0, sl, :], jnp.concatenate([v_ref[b, :, sl], u.astype(BF16)], axis=0))
        s_ref[b, d, j] = s * decay_cols[b, d][sl, 0:1] + jnp.where(same_head, grow, 0.0)


def _ev_out_kernel(h_ref, mod_ref, of_ref, ob_ref, dgate_ref, yf_ref, yb_ref, gate7_ref, bonus_ref,
                   dnorm_ref, gnw_ref, gnb_ref, ones_ref, wout_ref, o_ref):
    f32 = lambda ref: ref[0].astype(F32)
    o = f32(of_ref) + f32(ob_ref)
    dgate = f32(dgate_ref)
    parts = []
    for h in range(DN_HEADS):
        sl = slice(h * DN_DK, (h + 1) * DN_DK)
        parts.append(_rms_rows(o[:, sl]) * dnorm_ref[:, sl] * _silu(dgate[:, sl]))
    o_dn = jnp.concatenate(parts, axis=1)
    y = f32(yf_ref) + f32(yb_ref)
    inv_n = 1.0 / RW_HS
    mu = _head_sums(y, ones_ref) * inv_n
    yc = y - mu
    var = _head_sums(yc * yc, ones_ref) * inv_n
    yn = yc * lax.rsqrt(var + RW_GN_EPS) * gnw_ref[...] + gnb_ref[...]
    o_rw = (yn + f32(bonus_ref)) * f32(gate7_ref)
    wd = DN_HEADS * DN_DK
    proj = _dot(o_dn.astype(BF16), wout_ref[0:wd, :]) + _dot(o_rw.astype(BF16), wout_ref[wd:, :])
    o_ref[0] = h_ref[0] + mod_ref[0, 5:6] * proj


def _ev_out_call(h, mod, o_f, o_b, dgate, y_f, y_b, gate7, bonus, dnorm, gnw, gnb, ones, w_out, *, ctx_tiles):
    n_batch, t_len, d = h.shape
    nt = t_len // TILE
    tile = lambda n: pl.BlockSpec((1, TILE, n), lambda b, t: (b, t, 0))
    params = [dnorm.reshape(1, -1), gnw.reshape(1, -1), gnb.reshape(1, -1), ones, w_out]
    streams = [o_f, o_b, dgate, y_f, y_b, gate7, bonus]
    return pl.pallas_call(
        _ev_out_kernel,
        out_shape=jax.ShapeDtypeStruct(h.shape, F32),
        grid=(n_batch, nt),
        in_specs=[tile(d), pl.BlockSpec((1, N_MOD, d), _mod_row_map(n_batch, ctx_tiles, 0))]
        + [tile(s.shape[-1]) for s in streams] + [_resident(p.shape) for p in params],
        out_specs=tile(d),
        compiler_params=_cparams("parallel", "parallel"),
        name="even_mix_out",
    )(h, mod, *streams, *params)


def _rope_layout(width, rot, starts):
    angle = np.full((width,), -1, np.int64)
    first = np.zeros((1, width), np.float32)
    q = rot // 4
    for start in starts:
        for blk in range(2):
            for idx in range(q):
                l1 = start + blk * 2 * q + idx
                angle[l1] = angle[l1 + q] = blk * q + idx
                first[0, l1] = 1.0
    return angle, jnp.asarray(first)


def _rope_tables(n_ctx, n_lat, rot, angle, first):
    rows = n_lat // GRID_W
    row = jnp.repeat(jnp.arange(rows), GRID_W).astype(F32)
    col = jnp.tile(jnp.arange(GRID_W), rows).astype(F32)
    axis_dim = rot // 2
    inv = ROPE_THETA ** (-jnp.arange(0, axis_dim, 2, dtype=F32) / axis_dim)
    ang = jnp.concatenate([row[:, None] * inv, col[:, None] * inv], axis=-1)
    on = jnp.asarray(angle >= 0)
    idx = np.maximum(angle, 0)
    cos = jnp.where(on, jnp.cos(ang)[:, idx], 1.0)
    sin = jnp.where(on, jnp.sin(ang)[:, idx], 0.0) * (1.0 - 2.0 * first)
    width = angle.shape[0]
    return (jnp.concatenate([jnp.ones((n_ctx, width), F32), cos], axis=0),
            jnp.concatenate([jnp.zeros((n_ctx, width), F32), sin], axis=0))


def _rope(x, first_ref, cos_ref, sin_ref, quarter):
    width = x.shape[1]
    partner = jnp.where(first_ref[...] > 0.5, pltpu.roll(x, width - quarter, 1), pltpu.roll(x, quarter, 1))
    return x * cos_ref[...] + partner * sin_ref[...]


def _od_prep_kernel(h_ref, mod_ref, gain_ref, w_ref, qn_ref, kn_ref, mqn_ref, wuq_ref, mkvn_ref, wukv_ref,
                    ones_ref, fq_ref, fk_ref, fm_ref, fr_ref,
                    cq_ref, sq_ref, ck_ref, sk_ref, cm_ref, sm_ref, cr_ref, sr_ref,
                    qg_ref, qm_ref, kgt_ref, vg_ref, kmt_ref, vm_ref):
    xn = _modulate(h_ref[0], gain_ref[...], mod_ref[0, 3:4], mod_ref[0, 4:5]).astype(BF16)
    p = _dot(xn, w_ref[...])
    nq, nkv = GQ_HEADS * GQ_HD, GQ_KV_HEADS * GQ_HD
    o = 0
    q, o = p[:, o:o + nq], o + nq
    k, o = p[:, o:o + nkv], o + nkv
    v, o = p[:, o:o + 2 * nkv], o + 2 * nkv
    n_cq, n_ckv = mqn_ref.shape[1], mkvn_ref.shape[1]
    cq, o = p[:, o:o + n_cq], o + n_cq
    ckv, o = p[:, o:o + n_ckv], o + n_ckv
    kr = p[:, o:o + LANES]
    inv_hd = 1.0 / GQ_HD
    q = q * lax.rsqrt(_head_sums(q * q, ones_ref) * inv_hd + NORM_EPS) * qn_ref[...]
    k = k * lax.rsqrt(_head_sums(k * k, ones_ref) * inv_hd + NORM_EPS) * kn_ref[...]
    qm = _dot((_rms_rows(cq) * mqn_ref[...]).astype(BF16), wuq_ref[...])
    kvm = _dot((_rms_rows(ckv) * mkvn_ref[...]).astype(BF16), wukv_ref[...])
    q = _rope(q, fq_ref, cq_ref, sq_ref, GQ_HD // 4) * (GQ_HD ** -0.5 * LOG2_E)
    k = _rope(k, fk_ref, ck_ref, sk_ref, GQ_HD // 4)
    qm = _rope(qm, fm_ref, cm_ref, sm_ref, ML_ROPE // 4) * ((ML_NOPE + ML_ROPE) ** -0.5 * LOG2_E)
    kr = _rope(kr, fr_ref, cr_ref, sr_ref, ML_ROPE // 4)
    qg_ref[0] = q.astype(BF16)
    qm_ref[0] = qm.astype(BF16)
    n_nope = ML_HEADS * ML_NOPE
    for ref, val in ((vg_ref, v), (vm_ref, kvm[:, n_nope:])):
        upper = lax.broadcasted_iota(jnp.int32, (1, val.shape[1]), 1) % LANES >= LANES // 2
        ref[0] = (val + jnp.where(upper, 1.0, 0.0)).astype(BF16)
    kgt_ref[0] = k.T.astype(BF16)
    knt = kvm[:, :n_nope].T.astype(BF16)
    krt = kr.T[:ML_ROPE].astype(BF16)
    dk = ML_NOPE + ML_ROPE
    for h in range(ML_HEADS):
        kmt_ref[0, h * dk:h * dk + ML_NOPE, :] = knt[h * ML_NOPE:(h + 1) * ML_NOPE]
        kmt_ref[0, h * dk + ML_NOPE:(h + 1) * dk, :] = krt


def _od_prep_call(h, mod, gain, params, tables, *, ctx_tiles):
    n_batch, t_len, d = h.shape
    nt = t_len // TILE
    nq, nkv = GQ_HEADS * GQ_HD, GQ_KV_HEADS * GQ_HD
    dk = ML_NOPE + ML_ROPE
    tile = lambda n: pl.BlockSpec((1, TILE, n), lambda b, t: (b, t, 0))
    tile_t = lambda n: pl.BlockSpec((1, n, TILE), lambda b, t: (b, 0, t))
    tab = lambda a: pl.BlockSpec((TILE, a.shape[1]), lambda b, t: (t, 0))
    shp = lambda *s: jax.ShapeDtypeStruct((n_batch,) + s, BF16)
    return pl.pallas_call(
        _od_prep_kernel,
        out_shape=[shp(t_len, nq), shp(t_len, ML_HEADS * dk), shp(nkv, t_len), shp(t_len, GQ_KV_HEADS * LANES),
                   shp(ML_HEADS * dk, t_len), shp(t_len, ML_HEADS * LANES)],
        grid=(n_batch, nt),
        in_specs=[tile(d), pl.BlockSpec((1, N_MOD, d), _mod_row_map(n_batch, ctx_tiles, 0)), _resident((1, d))]
        + [_resident(p.shape) for p in params] + [tab(a) for a in tables],
        out_specs=[tile(nq), tile(ML_HEADS * dk), tile_t(nkv), tile(GQ_KV_HEADS * LANES), tile_t(ML_HEADS * dk),
                   tile(ML_HEADS * LANES)],
        compiler_params=_cparams("parallel", "parallel"),
        name="attn_prep",
    )(h, mod, gain.reshape(1, d), *params, *tables)


def _odd_layer_weights(od_w_in, gq_q_norm, gq_k_norm, ml_q_norm, ml_w_uq, ml_kv_norm, ml_w_ukv):
    d = od_w_in.shape[0]
    nq, nkv = GQ_HEADS * GQ_HD, GQ_KV_HEADS * GQ_HD

    def slabs(cols, heads, width):
        cols = cols.reshape(cols.shape[0], heads, width)
        return jnp.concatenate([cols, jnp.zeros(cols.shape[:2] + (LANES - width,), F32)], axis=2).reshape(cols.shape[0], -1)

    w = jnp.concatenate([od_w_in[:, :nq + nkv], slabs(od_w_in[:, nq + nkv:nq + 2 * nkv], GQ_KV_HEADS, GQ_HD),
                         od_w_in[:, nq + 2 * nkv:], jnp.zeros((d, LANES - ML_ROPE), F32)], axis=1)
    ukv = ml_w_ukv.reshape(ml_w_ukv.shape[0], ML_HEADS, ML_NOPE + ML_V)
    ukv = jnp.concatenate([ukv[:, :, :ML_NOPE].reshape(-1, ML_HEADS * ML_NOPE),
                           slabs(ukv[:, :, ML_NOPE:].reshape(-1, ML_HEADS * ML_V), ML_HEADS, ML_V)], axis=1)
    dk = ML_NOPE + ML_ROPE
    layouts = [(GQ_HD,) + _rope_layout(GQ_HEADS * GQ_HD, GQ_HD, [h * GQ_HD for h in range(GQ_HEADS)]),
               (GQ_HD,) + _rope_layout(GQ_KV_HEADS * GQ_HD, GQ_HD, [h * GQ_HD for h in range(GQ_KV_HEADS)]),
               (ML_ROPE,) + _rope_layout(ML_HEADS * dk, ML_ROPE, [h * dk + ML_NOPE for h in range(ML_HEADS)]),
               (ML_ROPE,) + _rope_layout(LANES, ML_ROPE, [0])]
    params = [w.astype(BF16), jnp.tile(gq_q_norm, GQ_HEADS).reshape(1, -1), jnp.tile(gq_k_norm, GQ_KV_HEADS).reshape(1, -1),
              ml_q_norm.reshape(1, -1), ml_w_uq.astype(BF16), ml_kv_norm.reshape(1, -1), ukv.astype(BF16),
              _block_ones(LANES, GQ_HD)] + [first for _, _, first in layouts]
    return params, layouts


def _softmax_pv(s, v_slab, width):
    m = jnp.max(s, axis=-1, keepdims=True)
    p = jnp.exp2((s - m).astype(BF16))
    pv = _dot(p, v_slab)
    return pv[:, :width] / pv[:, width:width + 1]


def _attn_kernel(*refs):
    n = ATT_TILES
    h_refs, qg_refs, qm_refs = refs[:n], refs[n:2 * n], refs[2 * n:3 * n]
    mod_ref, kgt_ref, vg_ref, kmt_ref, vm_ref, wout_ref, o_ref = refs[3 * n:]
    group = GQ_HEADS // GQ_KV_HEADS
    dk = ML_NOPE + ML_ROPE
    rows = lambda tiles, sl: jnp.concatenate([r[0, :, sl] for r in tiles], axis=0)

    def logits(h):
        if h < GQ_HEADS:
            g = h // group
            return _dot(rows(qg_refs, slice(h * GQ_HD, (h + 1) * GQ_HD)), kgt_ref[0, g * GQ_HD:(g + 1) * GQ_HD, :])
        h -= GQ_HEADS
        return _dot(rows(qm_refs, slice(h * dk, (h + 1) * dk)), kmt_ref[0, h * dk:(h + 1) * dk, :])

    def values(h):
        if h < GQ_HEADS:
            g = h // group
            return vg_ref[0, :, g * LANES:(g + 1) * LANES], GQ_HD
        h -= GQ_HEADS
        return vm_ref[0, :, h * LANES:(h + 1) * LANES], ML_V

    n_heads = GQ_HEADS + ML_HEADS
    parts, s = [], logits(0)
    for h in range(n_heads):
        s_next = logits(h + 1) if h + 1 < n_heads else None
        parts.append(_softmax_pv(s, *values(h)))
        s = s_next
    ol = jnp.concatenate(parts, axis=1).astype(BF16)
    o_ref[0] = rows(h_refs, slice(None)) + mod_ref[0, 5:6] * _dot(ol, wout_ref[...])


def _attn_call(h, mod, qg, qm, kgt, vg, kmt, vm, w_out, *, ctx_tiles):
    n_batch, t_len, d = h.shape
    nt = t_len // TILE - ctx_tiles
    na = ATT_TILES
    qtiles = lambda a: [pl.BlockSpec((1, TILE, a.shape[-1]), lambda b, t, i=i: (b, na * t + i + ctx_tiles, 0))
                        for i in range(na)]
    whole = lambda a: pl.BlockSpec((1,) + a.shape[1:], lambda b, t: (b, 0, 0))
    return pl.pallas_call(
        _attn_kernel,
        out_shape=jax.ShapeDtypeStruct((n_batch, nt * TILE, d), F32),
        grid=(n_batch, nt // na),
        in_specs=qtiles(h) + qtiles(qg) + qtiles(qm) + [pl.BlockSpec((1, N_MOD, d), lambda b, t: (b, 0, 0)),
                                                      whole(kgt), whole(vg), whole(kmt), whole(vm), _resident(w_out.shape)],
        out_specs=pl.BlockSpec((1, na * TILE, d), lambda b, t: (b, t, 0)),
        compiler_params=_cparams("parallel", "parallel"),
        name="attention_out",
    )(*([h] * na + [qg] * na + [qm] * na), mod, kgt, vg, kmt, vm, w_out)


def _block_ones(n, blk):
    i = np.arange(n) // blk
    return jnp.asarray(i[:, None] == i[None, :], BF16)


def _even_layer_weights(ev_w_in, dn_conv, dn_a_log, dn_dt_bias, rw_mu, rw_w0, rw_w2, rw_a0, rw_a2, rw_g2,
                        rw_kk, rw_ka, rw_rk):
    d = ev_w_in.shape[0]
    n_dn = 4 * DN_HEADS * DN_DK
    nh2 = 2 * DN_HEADS
    slab0 = n_dn + 2 * nh2
    zeros = lambda n: jnp.zeros((d, n), F32)
    w_dn = jnp.concatenate([ev_w_in[:, :n_dn], ev_w_in[:, n_dn:slab0], zeros(LANES - 2 * nh2)], axis=1)
    slab = ev_w_in[:, slab0:]
    o = 3 * RW_W
    lora = 2 * RW_W_LORA
    gpad = 2 * LANES - RW_G_LORA
    w_rw = jnp.concatenate([slab[:, :o + 2 * lora + RW_G_LORA], zeros(gpad)], axis=1)
    mu = jnp.concatenate([rw_mu, jnp.zeros((gpad,), F32)]).reshape(1, -1)
    pad_lanes = lambda v: jnp.zeros((1, LANES), F32).at[0, :v.size].set(v.reshape(-1))

    def dir_blocks(m):
        z = jnp.zeros_like(m[0])
        return jnp.concatenate([jnp.concatenate([m[0], z], axis=1), jnp.concatenate([z, m[1]], axis=1)], axis=0)

    g2 = jnp.concatenate([rw_g2, jnp.zeros((gpad, RW_W), F32)], axis=0)
    return dict(
        w_dn=w_dn.astype(BF16), conv=dn_conv, alog=pad_lanes(dn_a_log), dtb=pad_lanes(dn_dt_bias),
        w_rw=w_rw.astype(BF16), mu=mu, w2=dir_blocks(rw_w2), w0=rw_w0.reshape(1, -1), a2=dir_blocks(rw_a2),
        a0=rw_a0.reshape(1, -1), g2=g2, kkw=rw_kk.reshape(1, -1), kaw=rw_ka.reshape(1, -1),
        rk=rw_rk.reshape(1, -1), ones=_block_ones(LANES, RW_HS))


def kernel(x, c, ctx, c_ctx, mod_w, mod_b, norm_ffn1, norm_mix, norm_ffn2, ffn1_w1, ffn1_w3, ffn1_w2, ffn2_w1, ffn2_w3, ffn2_w2, ev_w_in, ev_w_out, dn_conv, dn_a_log, dn_dt_bias, dn_norm, rw_mu, rw_w0, rw_w2, rw_a0, rw_a2, rw_g2, rw_kk, rw_ka, rw_rk, rw_gn_w, rw_gn_b, od_w_in, od_w_out, gq_q_norm, gq_k_norm, ml_q_norm, ml_w_uq, ml_kv_norm, ml_w_ukv, final_norm):
    n_batch, n_lat, d = x.shape
    n_ctx = ctx.shape[1]
    depth = mod_w.shape[0]
    assert n_ctx % TILE == 0 and n_lat % (ATT_TILES * TILE) == 0 and n_lat % GRID_W == 0
    assert n_batch % SCAN_BATCH == 0 and (n_ctx + n_lat) % (CHUNKS_PER_STEP * CHUNK) == 0
    assert depth % 2 == 0 and depth // 2 == od_w_in.shape[0] == 1, "supported stack: [recurrent, attention]"
    ctx_tiles, ctx_chunks = n_ctx // TILE, n_ctx // CHUNK
    bf = lambda a: a.astype(BF16)

    mod = _all_mod(c, c_ctx, mod_w, mod_b)
    h = (ctx, x)
    for i in range(depth):
        j = i // 2
        last = i == depth - 1
        h = _ffn_call(h, mod[i], norm_ffn1[i], bf(ffn1_w1[i]), bf(ffn1_w3[i]), bf(ffn1_w2[i]), j0=0, ctx_tiles=ctx_tiles)
        if i % 2 == 0:
            w = _even_layer_weights(ev_w_in[j], dn_conv[j], dn_a_log[j], dn_dt_bias[j], rw_mu[j], rw_w0[j], rw_w2[j],
                                    rw_a0[j], rw_a2[j], rw_g2[j], rw_kk[j], rw_ka[j], rw_rk[j])
            q, k, v, small, dgate = _dn_prep_call(h, mod[i], norm_mix[i], w["w_dn"], w["conv"], w["alog"], w["dtb"],
                                                  ctx_tiles=ctx_tiles)
            r, v7, kk, lw, kd, kka, gate7, bonus = _rw_prep_call(
                h, mod[i], norm_mix[i], w["w_rw"], w["mu"], w["w2"], w["w0"], w["a2"], w["a0"], w["g2"], w["kkw"],
                w["kaw"], w["rk"], w["ones"], ctx_tiles=ctx_tiles)
            o_f, o_b, y_f, y_b = _scans_call(_dn_chunk_call(q, k, v, small), _rw_chunk_call(r, v7, kk, lw, kd, kka),
                                             ctx_chunks=ctx_chunks)
            h = _ev_out_call(h, mod[i], o_f, o_b, dgate, y_f, y_b, gate7, bonus, jnp.tile(dn_norm[j], DN_HEADS),
                             rw_gn_w[j], rw_gn_b[j], w["ones"], bf(ev_w_out[j]), ctx_tiles=ctx_tiles)
            h = _ffn_call(h, mod[i], norm_ffn2[i], bf(ffn2_w1[i]), bf(ffn2_w3[i]), bf(ffn2_w2[i]), j0=6,
                          ctx_tiles=ctx_tiles)
        else:
            params, layouts = _odd_layer_weights(od_w_in[j], gq_q_norm[j], gq_k_norm[j], ml_q_norm[j], ml_w_uq[j],
                                                 ml_kv_norm[j], ml_w_ukv[j])
            tables = [t for rot, angle, first in layouts for t in _rope_tables(n_ctx, n_lat, rot, angle, first)]
            qg, qm, kgt, vg, kmt, vm = _od_prep_call(h, mod[i], norm_mix[i], params, tables, ctx_tiles=ctx_tiles)
            hl = _attn_call(h, mod[i], qg, qm, kgt, vg, kmt, vm, bf(od_w_out[j]), ctx_tiles=ctx_tiles)
            assert last
            h = _ffn_call(hl, mod[i], norm_ffn2[i], bf(ffn2_w1[i]), bf(ffn2_w3[i]), bf(ffn2_w2[i]), j0=6, ctx_tiles=0,
                          final_gain=final_norm)
    return h
```

```python
import functools

import jax
import jax.numpy as jnp
import numpy as np
from jax import lax
from jax.experimental import pallas as pl
from jax.experimental.pallas import tpu as pltpu

F32 = jnp.float32
BF16 = jnp.bfloat16

NORM_EPS = 1e-6
ROPE_THETA = 10000.0
GRID_W = 64
N_MOD = 9

DN_HEADS = 4
DN_DK = 128
DN_CONV = 5
RW_HEADS = 8
RW_HS = 64
RW_W = RW_HEADS * RW_HS
RW_W_LORA = 64
RW_A_LORA = 64
RW_G_LORA = 160
RW_GN_EPS = 64e-5
GQ_HEADS = 8
GQ_KV_HEADS = 2
GQ_HD = 64
ML_HEADS = 8
ML_NOPE = 64
ML_ROPE = 32
ML_V = 64

TILE = 256
CHUNK = 64
HALO = 8
LANES = 128
MXU_DIM = 256
VMEM_LIMIT = 56 * 1024 * 1024
SCAN_BATCH = 8
ATT_TILES = 2
CHUNKS_PER_STEP = 4
LOG2_E = 1.4426950408889634


def _cparams(*sem):
    return pltpu.CompilerParams(dimension_semantics=sem, vmem_limit_bytes=VMEM_LIMIT)


def _resident(shape):
    nd = len(shape)
    return pl.BlockSpec(shape, lambda *_: (0,) * nd, pipeline_mode=pl.Buffered(1))


def _mm(a, b):
    return jnp.dot(a.astype(BF16), b.astype(BF16), preferred_element_type=F32)


def _mm_nt(a, b):
    return lax.dot_general(a.astype(BF16), b.astype(BF16), (((1,), (1,)), ((), ())),
                           preferred_element_type=F32)


def _split2(x):
    hi = x.astype(BF16)
    lo = (x - hi.astype(F32)).astype(BF16)
    return hi, lo


def _split3(x):
    hi = x.astype(BF16)
    r = x - hi.astype(F32)
    mid = r.astype(BF16)
    lo = (r - mid.astype(F32)).astype(BF16)
    return hi, mid, lo


def _dot(a, b):
    return jnp.dot(a, b, preferred_element_type=F32)


def _mm3s(asp, bsp):
    (ah, al), (bh, bl) = asp, bsp
    return _dot(ah, bh) + (_dot(ah, bl) + _dot(al, bh))


def _mm3(a, b):
    return _mm3s(_split2(a), _split2(b))


def _mm_exact_lhs(a01, b):
    a = a01.astype(BF16)
    hi, mid, lo = _split3(b)
    return _dot(a, hi) + (_dot(a, mid) + _dot(a, lo))


def _rms_rows(x):
    return x * lax.rsqrt(jnp.mean(x * x, axis=-1, keepdims=True) + NORM_EPS)


def _modulate(x, gain, shift, scale):
    return (_rms_rows(x) * gain) * (1.0 + scale) + shift


def _silu(x):
    return x * jax.nn.sigmoid(x)


def _softplus(x):
    return jnp.maximum(x, 0.0) + jnp.log1p(jnp.exp(-jnp.abs(x)))


def _mod_kernel(c_ref, w_ref, b_ref, o_ref):
    s = _silu(c_ref[...])
    o_ref[0] = _mm3(s, w_ref[0]) + b_ref[0]


def _mod_call(cc, mod_w, mod_b):
    n_layers, d, n = mod_w.shape
    r = cc.shape[0]
    tn = n // 8
    return pl.pallas_call(
        _mod_kernel,
        out_shape=jax.ShapeDtypeStruct((n_layers, r, n), F32),
        grid=(n_layers, n // tn),
        in_specs=[pl.BlockSpec((r, d), lambda l, j: (0, 0)),
                  pl.BlockSpec((1, d, tn), lambda l, j: (l, 0, j)),
                  pl.BlockSpec((1, 1, tn), lambda l, j: (l, 0, j))],
        out_specs=pl.BlockSpec((1, r, tn), lambda l, j: (l, 0, j)),
        compiler_params=_cparams("parallel", "parallel"),
        name="adaln_mod",
    )(cc, mod_w, mod_b.reshape(n_layers, 1, n))


def _all_mod(c, c_ctx, mod_w, mod_b):
    n_batch, d = c.shape
    rows = -(-(n_batch + 1) // 8) * 8
    cc = jnp.zeros((rows, d), F32).at[:n_batch].set(c).at[n_batch].set(c_ctx)
    return _mod_call(cc, mod_w, mod_b).reshape(mod_w.shape[0], rows, N_MOD, d)


def _mod_row_map(n_batch, ctx_tiles, t_off):
    def index_map(b, t):
        return (jnp.where(t + t_off < ctx_tiles, n_batch, b), 0, 0)
    return index_map


def _ffn_kernel(*refs, j0, final, split_tiles, sub, ctx_tiles, t_off):
    if split_tiles:
        ctx_refs, lat_refs, refs = refs[:sub], refs[sub:2 * sub], refs[2 * sub:]
        xs = [jnp.where(pl.program_id(1) * sub + i < split_tiles, c[0], l[0])
              for i, (c, l) in enumerate(zip(ctx_refs, lat_refs))]
    else:
        h_ref, *refs = refs
        xs = [h_ref[0, i * TILE:(i + 1) * TILE] for i in range(sub)]
    mod_ctx_ref, mod_ref, gain_ref, w1_ref, w3_ref, w2_ref, *rest = refs
    o_ref = rest[-1]
    f = w1_ref.shape[1]
    cut = -(-(f // MXU_DIM) // 2) * MXU_DIM
    fcs = [slice(0, cut), slice(cut, f)] if 0 < cut < f else [slice(0, f)]
    for i, x in enumerate(xs):
        is_ctx = (pl.program_id(1) * sub + i + t_off) < ctx_tiles
        row = lambda j: jnp.where(is_ctx, mod_ctx_ref[0, j:j + 1], mod_ref[0, j:j + 1])
        xn = _modulate(x, gain_ref[...], row(j0), row(j0 + 1)).astype(BF16)
        ups = [(_dot(xn, w1_ref[:, fc]), _dot(xn, w3_ref[:, fc])) for fc in fcs]
        y = None
        for (a, b), fc in zip(ups, fcs):
            part = _dot((_silu(a) * b).astype(BF16), w2_ref[fc, :])
            y = part if y is None else y + part
        y = x + (0.5 * row(j0 + 2)) * y
        if final:
            y = _rms_rows(y) * rest[0][...]
        o_ref[0, i * TILE:(i + 1) * TILE] = y


def _ffn_call(h, mod, gain, w1, w3, w2, *, j0, ctx_tiles, t_off=0, final_gain=None):
    split = isinstance(h, tuple)
    if split:
        ctx, lat = h
        n_batch, _, d = lat.shape
        t_len = ctx.shape[1] + lat.shape[1]
        last_ctx = ctx_tiles - 1
        sub = next(s for s in (3, 2, 1) if (t_len // TILE) % s == 0)
        streams = [ctx] * sub + [lat] * sub
        stream_specs = ([pl.BlockSpec((1, TILE, d), lambda b, t, i=i: (b, jnp.minimum(sub * t + i, last_ctx), 0))
                         for i in range(sub)]
                        + [pl.BlockSpec((1, TILE, d), lambda b, t, i=i: (b, jnp.maximum(sub * t + i - ctx_tiles, 0), 0))
                           for i in range(sub)])
    else:
        n_batch, t_len, d = h.shape
        nt_all = t_len // TILE - t_off
        sub = next(s for s in (3, 2, 1) if nt_all % s == 0 and t_off % s == 0)
        streams = [h]
        stream_specs = [pl.BlockSpec((1, sub * TILE, d), lambda b, t: (b, t + t_off // sub, 0))]
    f = w1.shape[1]
    nt = t_len // TILE - t_off
    final = final_gain is not None
    in_specs = stream_specs + [pl.BlockSpec((1, N_MOD, d), lambda b, t: (n_batch, 0, 0)),
                               pl.BlockSpec((1, N_MOD, d), lambda b, t: (b, 0, 0)),
                               _resident((1, d)), _resident((d, f)), _resident((d, f)), _resident((f, d))]
    args = streams + [mod, mod, gain.reshape(1, d), w1, w3, w2]
    if final:
        in_specs.append(_resident((1, d)))
        args.append(final_gain.reshape(1, d))
    return pl.pallas_call(
        functools.partial(_ffn_kernel, j0=j0, final=final, split_tiles=ctx_tiles if split else 0, sub=sub,
                          ctx_tiles=ctx_tiles, t_off=t_off),
        out_shape=jax.ShapeDtypeStruct((n_batch, nt * TILE, d), F32),
        grid=(n_batch, nt // sub),
        in_specs=in_specs,
        out_specs=pl.BlockSpec((1, sub * TILE, d), lambda b, t: (b, t, 0)),
        compiler_params=_cparams("parallel", "parallel"),
        name="macaron_ffn",
    )(*args)


def _halo_specs(d, ctx_tiles, n_tiles):
    per = TILE // HALO
    last = n_tiles * per - 1
    return [pl.BlockSpec((1, HALO, d), lambda b, t: (b, jnp.maximum(t * per - 1, 0), 0)),
            pl.BlockSpec((1, TILE, d), lambda b, t: (b, t, 0)),
            pl.BlockSpec((1, HALO, d), lambda b, t: (b, jnp.minimum((t + 1) * per, last), 0))]


def _project_with_halo(prev_ref, cur_ref, next_ref, mod_ref, gain_ref, w_ref, pe_ref, *, ctx_tiles, n_tiles):
    t = pl.program_id(1)
    xe = jnp.concatenate([prev_ref[0], cur_ref[0], next_ref[0]], axis=0)
    xn = _modulate(xe, gain_ref[...], mod_ref[0, 3:4], mod_ref[0, 4:5]).astype(BF16)
    p = _dot(xn, w_ref[...])
    row = lax.broadcasted_iota(jnp.int32, (TILE + 2 * HALO, 1), 0)
    prev_ok = jnp.logical_and(t > 0, t != ctx_tiles)
    next_ok = jnp.logical_and(t + 1 < n_tiles, t + 1 != ctx_tiles)
    keep = jnp.logical_and(jnp.logical_or(row >= HALO, prev_ok),
                           jnp.logical_or(row < HALO + TILE, next_ok))
    pe_ref[...] = jnp.where(keep, p, 0.0)


def _dn_prep_kernel(prev_ref, cur_ref, next_ref, mod_ref, gain_ref, w_ref, conv_ref, alog_ref, dtb_ref,
                    q_ref, k_ref, v_ref, small_ref, gate_ref, pe_ref, *, ctx_tiles, n_tiles):
    _project_with_halo(prev_ref, cur_ref, next_ref, mod_ref, gain_ref, w_ref, pe_ref,
                       ctx_tiles=ctx_tiles, n_tiles=n_tiles)
    nqkv = 3 * DN_HEADS * DN_DK
    half = DN_CONV // 2
    acc = None
    for j in range(DN_CONV):
        term = conv_ref[j:j + 1, :] * pe_ref[pl.ds(HALO - half + j, TILE), 0:nqkv]
        acc = term if acc is None else acc + term
    qkv = _silu(acc)
    w = DN_HEADS * DN_DK
    for idx, ref in ((0, q_ref), (1, k_ref)):
        for h in range(DN_HEADS):
            seg = qkv[:, idx * w + h * DN_DK: idx * w + (h + 1) * DN_DK]
            ref[0, :, h * DN_DK:(h + 1) * DN_DK] = seg * lax.rsqrt(jnp.sum(seg * seg, axis=-1, keepdims=True) + 1e-6)
    v_ref[0] = qkv[:, 2 * w:3 * w]
    gate_ref[0] = pe_ref[pl.ds(HALO, TILE), nqkv:nqkv + w].astype(BF16)
    ab = pe_ref[pl.ds(HALO, TILE), nqkv + w:nqkv + w + LANES]
    g = -jnp.exp(alog_ref[...]) * _softplus(ab + dtb_ref[...])
    lane = lax.broadcasted_iota(jnp.int32, ab.shape, 1)
    nh2 = 2 * DN_HEADS
    small_ref[0] = jnp.where(lane < nh2, g, jnp.where(lane < 2 * nh2, jax.nn.sigmoid(ab), 0.0))


def _dn_prep_call(h, mod, gain, w, conv, alog, dtb, *, ctx_tiles):
    n_batch, t_len, d = h.shape
    nt = t_len // TILE
    wd = DN_HEADS * DN_DK
    out = lambda n, dt=F32: jax.ShapeDtypeStruct((n_batch, t_len, n), dt)
    ospec = lambda n: pl.BlockSpec((1, TILE, n), lambda b, t: (b, t, 0))
    return pl.pallas_call(
        functools.partial(_dn_prep_kernel, ctx_tiles=ctx_tiles, n_tiles=nt),
        out_shape=[out(wd), out(wd), out(wd), out(LANES), out(wd, BF16)],
        grid=(n_batch, nt),
        in_specs=_halo_specs(d, ctx_tiles, nt) + [
            pl.BlockSpec((1, N_MOD, d), _mod_row_map(n_batch, ctx_tiles, 0)),
            _resident((1, d)), _resident(w.shape), _resident(conv.shape),
            _resident((1, LANES)), _resident((1, LANES))],
        out_specs=[ospec(wd), ospec(wd), ospec(wd), ospec(LANES), ospec(wd)],
        scratch_shapes=[pltpu.VMEM((TILE + 2 * HALO, w.shape[1]), F32)],
        compiler_params=_cparams("parallel", "parallel"),
        name="deltanet_prep",
    )(h, h, h, mod, gain.reshape(1, d), w, conv, alog, dtb)


def _head_sums(x, ones_ref):
    ones = ones_ref[...]
    out = []
    for g in range(x.shape[1] // LANES):
        hi, lo = _split2(x[:, g * LANES:(g + 1) * LANES])
        out.append(_dot(hi, ones) + _dot(lo, ones))
    return out[0] if len(out) == 1 else jnp.concatenate(out, axis=1)


def _rw_prep_kernel(prev_ref, cur_ref, next_ref, mod_ref, gain_ref, w_ref, mu_ref, w2_ref, w0_ref, a2_ref,
                    a0_ref, g2_ref, kkw_ref, kaw_ref, rk_ref, ones_ref,
                    r_ref, v_ref, kk_ref, lw_ref, kd_ref, kka_ref, gate_ref, bonus_ref, pe_ref,
                    *, ctx_tiles, n_tiles):
    _project_with_halo(prev_ref, cur_ref, next_ref, mod_ref, gain_ref, w_ref, pe_ref,
                       ctx_tiles=ctx_tiles, n_tiles=n_tiles)
    z = pe_ref[pl.ds(HALO, TILE), :]
    zs = 0.5 * (pe_ref[pl.ds(HALO - 1, TILE), :] + pe_ref[pl.ds(HALO + 1, TILE), :])
    s = z + mu_ref[...] * (zs - z)
    r, k7, v7 = s[:, 0:RW_W], s[:, RW_W:2 * RW_W], s[:, 2 * RW_W:3 * RW_W]
    o = 3 * RW_W
    wd, ad, gd = s[:, o:o + LANES], s[:, o + LANES:o + 2 * LANES], s[:, o + 2 * LANES:o + 4 * LANES]
    w_logit = _mm3(jnp.tanh(wd), w2_ref[...]) + w0_ref[...]
    lw = -float(np.exp(-0.5)) * jax.nn.sigmoid(w_logit)
    a = jax.nn.sigmoid(_mm3(ad, a2_ref[...]) + a0_ref[...])
    gate_ref[0] = _mm3(jax.nn.sigmoid(gd), g2_ref[...]).astype(BF16)
    kx = k7 * kkw_ref[...]
    kk = kx * lax.rsqrt(_head_sums(kx * kx, ones_ref) + 1e-6)
    r_ref[0], v_ref[0], kk_ref[0], lw_ref[0] = r, v7, kk, lw
    kd_sum = None
    for d in range(2):
        a_d = a[:, d * RW_W:(d + 1) * RW_W]
        kd = k7 * (1.0 + (a_d - 1.0) * kaw_ref[...])
        kd_ref[0, :, d * RW_W:(d + 1) * RW_W] = kd
        kka_ref[0, :, d * RW_W:(d + 1) * RW_W] = kk * a_d
        kd_sum = kd if kd_sum is None else kd_sum + kd
    bonus_ref[0] = (_head_sums((r * rk_ref[...]) * kd_sum, ones_ref) * v7).astype(BF16)


def _rw_prep_call(h, mod, gain, w, mu, w2, w0, a2, a0, g2, kkw, kaw, rk, ones, *, ctx_tiles):
    n_batch, t_len, d = h.shape
    nt = t_len // TILE
    out = lambda n, dt: jax.ShapeDtypeStruct((n_batch, t_len, n), dt)
    ospec = lambda n: pl.BlockSpec((1, TILE, n), lambda b, t: (b, t, 0))
    widths = [RW_W, RW_W, RW_W, 2 * RW_W, 2 * RW_W, 2 * RW_W, RW_W, RW_W]
    dtypes = [F32] * 6 + [BF16] * 2
    params = [gain.reshape(1, d), w, mu, w2, w0, a2, a0, g2, kkw, kaw, rk, ones]
    return pl.pallas_call(
        functools.partial(_rw_prep_kernel, ctx_tiles=ctx_tiles, n_tiles=nt),
        out_shape=[out(n, dt) for n, dt in zip(widths, dtypes)],
        grid=(n_batch, nt),
        in_specs=_halo_specs(d, ctx_tiles, nt) + [pl.BlockSpec((1, N_MOD, d), _mod_row_map(n_batch, ctx_tiles, 0))]
        + [_resident(p.shape) for p in params],
        out_specs=[ospec(n) for n in widths],
        scratch_shapes=[pltpu.VMEM((TILE + 2 * HALO, w.shape[1]), F32)],
        compiler_params=_cparams("parallel", "parallel"),
        name="rwkv_prep",
    )(h, h, h, mod, *params)


def _ev_prep_kernel(*refs, ctx_tiles, n_tiles):
    shared, dn_params, rw_params = refs[:5], refs[5:9], refs[9:20]
    dn_outs, rw_outs, (pe_dn_ref, pe_rw_ref) = refs[20:25], refs[25:33], refs[33:]
    _dn_prep_kernel(*shared, *dn_params, *dn_outs, pe_dn_ref, ctx_tiles=ctx_tiles, n_tiles=n_tiles)
    _rw_prep_kernel(*shared, *rw_params, *rw_outs, pe_rw_ref, ctx_tiles=ctx_tiles, n_tiles=n_tiles)


def _ev_prep_call(h, mod, gain, w, *, ctx_tiles):
    n_batch, t_len, d = h.shape
    nt = t_len // TILE
    wd = DN_HEADS * DN_DK
    out = lambda n, dt=F32: jax.ShapeDtypeStruct((n_batch, t_len, n), dt)
    ospec = lambda n: pl.BlockSpec((1, TILE, n), lambda b, t: (b, t, 0))
    dn_params = [w["w_dn"], w["conv"], w["alog"], w["dtb"]]
    rw_params = [w["w_rw"], w["mu"], w["w2"], w["w0"], w["a2"], w["a0"], w["g2"], w["kkw"], w["kaw"], w["rk"], w["ones"]]
    widths = [wd, wd, wd, LANES, wd] + [RW_W, RW_W, RW_W, 2 * RW_W, 2 * RW_W, 2 * RW_W, RW_W, RW_W]
    dtypes = [F32] * 4 + [BF16] + [F32] * 6 + [BF16] * 2
    outs = pl.pallas_call(
        functools.partial(_ev_prep_kernel, ctx_tiles=ctx_tiles, n_tiles=nt),
        out_shape=[out(n, dt) for n, dt in zip(widths, dtypes)],
        grid=(n_batch, nt),
        in_specs=_halo_specs(d, ctx_tiles, nt) + [pl.BlockSpec((1, N_MOD, d), _mod_row_map(n_batch, ctx_tiles, 0)),
                                                  _resident((1, d))]
        + [_resident(p.shape) for p in dn_params + rw_params],
        out_specs=[ospec(n) for n in widths],
        scratch_shapes=[pltpu.VMEM((TILE + 2 * HALO, w["w_dn"].shape[1]), F32),
                        pltpu.VMEM((TILE + 2 * HALO, w["w_rw"].shape[1]), F32)],
        compiler_params=_cparams("parallel", "parallel"),
        name="even_mix_prep",
    )(h, h, h, mod, gain.reshape(1, d), *dn_params, *rw_params)
    return outs[:5], outs[5:]


def _chunk_masks(direction, width=CHUNK):
    i = lax.broadcasted_iota(jnp.int32, (CHUNK, width), 0)
    j = lax.broadcasted_iota(jnp.int32, (CHUNK, width), 1) % CHUNK
    return (i >= j, i > j) if direction == 0 else (i <= j, i < j)


def _last_row(x, direction):
    return x[CHUNK - 1:CHUNK] if direction == 0 else x[0:1]


def _bd(x):
    shape = (2 * CHUNK, x.shape[1])
    r = lax.broadcasted_iota(jnp.int32, shape, 0)
    c = lax.broadcasted_iota(jnp.int32, shape, 1)
    return jnp.where((r < CHUNK) == (c < x.shape[1] // 2), jnp.concatenate([x, x], axis=0), 0.0)


def _neumann_inverse_pairs(ns, refine):
    i = lax.broadcasted_iota(jnp.int32, (CHUNK, 2 * CHUNK), 0)
    j = lax.broadcasted_iota(jnp.int32, (CHUNK, 2 * CHUNK), 1) % CHUNK
    eye = jnp.where(i == j, 1.0, 0.0)
    one_pass = lambda a, b: _dot(a.astype(BF16), b.astype(BF16))
    rs, ps = list(ns), list(ns)
    span = 2
    while span < CHUNK:
        ps = [one_pass(p, _bd(p)) for p in ps]
        rs = [r + p + one_pass(r, _bd(p)) for r, p in zip(rs, ps)]
        span *= 2
    if not refine:
        return [eye + r for r in rs]
    res = [_mm3(n, _bd(eye + r)) - r for n, r in zip(ns, rs)]
    return [eye + (r + (e + one_pass(r, _bd(e)))) for r, e in zip(rs, res)]


def _rev_chunk(n, ctx_chunks, n_chunks):
    return jnp.where(n < ctx_chunks, ctx_chunks - 1 - n, n_chunks - 1 + ctx_chunks - n)


def _dn_chunk_kernel(q_ref, k_ref, v_ref, small_ref, u_ref, w_ref, qd_ref, kdt_ref, attn_ref, gl_ref):
    nh2, n_pairs, pw, dk = 2 * DN_HEADS, DN_HEADS // 2, 2 * DN_DK, DN_DK
    first_c = lax.broadcasted_iota(jnp.int32, (CHUNK, 2 * CHUNK), 1) < CHUNK
    first_f = lax.broadcasted_iota(jnp.int32, (CHUNK, pw), 1) < dk

    def cols(x, c, first):
        return jnp.where(first[:x.shape[0]], x[:, c:c + 1], x[:, c + 1:c + 2])

    loaded = []
    for cc in range(CHUNKS_PER_STEP):
        rows = slice(cc * CHUNK, (cc + 1) * CHUNK)
        sm = small_ref[0, rows]
        q, k, v = q_ref[0, rows] * (DN_DK ** -0.5), k_ref[0, rows], v_ref[0, rows]
        grams = [_mm_nt(jnp.concatenate([k[:, j * pw:(j + 1) * pw], q[:, j * pw:(j + 1) * pw]], axis=0),
                        _bd(k[:, j * pw:(j + 1) * pw])) for j in range(n_pairs)]
        gcs = [_mm_exact_lhs(jnp.where(_chunk_masks(d)[0], 1.0, 0.0), sm) for d in range(2)]
        loaded.append((cc, rows, sm, q, k, v, grams, gcs))
    work = []
    for cc, rows, sm, q, k, v, grams, gcs in loaded:
        for d in range(2):
            incl, strict = _chunk_masks(d, 2 * CHUNK)
            gc = gcs[d]
            gc_t = gc.T
            gtot = _last_row(gc, d)
            gl_ref[0, d, cc] = jnp.exp(gtot)
            for j in range(n_pairs):
                c = DN_HEADS * d + 2 * j
                gcr = jnp.concatenate([gc_t[c:c + 1, :], gc_t[c + 1:c + 2, :]], axis=1)
                decay = jnp.exp(jnp.where(incl, cols(gc, c, first_c) - gcr, -1e30))
                lower = jnp.where(strict, (cols(sm, nh2 + c, first_c) * grams[j][:CHUNK]) * decay, 0.0)
                work.append((cc, rows, d, j, c, sm, q, k, v, gc, gtot, decay, grams[j][CHUNK:], -lower))
    rhs_list = []
    for cc, rows, d, j, c, sm, q, k, v, gc, gtot, decay, qk, _ in work:
        sl = slice(j * pw, (j + 1) * pw)
        beta, gcc, gt = cols(sm, nh2 + c, first_f), cols(gc, c, first_f), cols(gtot, c, first_f)
        egc = jnp.exp(gcc)
        kp, qp = k[:, sl], q[:, sl]
        vb, ke = v[:, sl] * beta, (kp * beta) * egc
        rhs_list.append(_bd(jnp.concatenate([vb[:, :dk], ke[:, :dk], vb[:, dk:], ke[:, dk:]], axis=1)))
        qd_ref[0, d, rows, sl] = (qp * egc).astype(BF16)
        k_tail = kp * jnp.exp(gt - gcc)
        kdt_ref[0, d, cc, j * dk:(j + 1) * dk, :] = jnp.concatenate([k_tail[:, :dk].T, k_tail[:, dk:].T], axis=1).astype(BF16)
        attn_ref[0, d, rows, 2 * j * CHUNK:2 * (j + 1) * CHUNK] = (qk * decay).astype(BF16)
    t_invs = _neumann_inverse_pairs([item[-1] for item in work], refine=True)
    for (cc, rows, d, j, *_), t_inv, rhs in zip(work, t_invs, rhs_list):
        sl = slice(j * pw, (j + 1) * pw)
        sol = _mm(t_inv, rhs)
        u_ref[0, d, rows, sl] = jnp.concatenate([sol[:, :dk], sol[:, 2 * dk:3 * dk]], axis=1).astype(BF16)
        w_ref[0, d, rows, sl] = jnp.concatenate([sol[:, dk:2 * dk], sol[:, 3 * dk:]], axis=1).astype(BF16)


def _dn_chunk_call(q, k, v, small):
    n_batch, t_len, wd = q.shape
    nc = t_len // CHUNK
    cps = CHUNKS_PER_STEP
    ispec = lambda n: pl.BlockSpec((1, cps * CHUNK, n), lambda b, c: (b, c, 0))
    ospec = lambda n: pl.BlockSpec((1, 2, cps * CHUNK, n), lambda b, c: (b, 0, c, 0))
    shp = lambda n, dt: jax.ShapeDtypeStruct((n_batch, 2, t_len, n), dt)
    return pl.pallas_call(
        _dn_chunk_kernel,
        out_shape=[shp(wd, BF16), shp(wd, BF16), shp(wd, BF16),
                   jax.ShapeDtypeStruct((n_batch, 2, nc, wd // 2, 2 * CHUNK), BF16), shp(DN_HEADS * CHUNK, BF16),
                   jax.ShapeDtypeStruct((n_batch, 2, nc, 1, LANES), F32)],
        grid=(n_batch, nc // cps),
        in_specs=[ispec(wd), ispec(wd), ispec(wd), ispec(LANES)],
        out_specs=[ospec(wd), ospec(wd), ospec(wd),
                   pl.BlockSpec((1, 2, cps, wd // 2, 2 * CHUNK), lambda b, c: (b, 0, c, 0, 0)), ospec(DN_HEADS * CHUNK),
                   pl.BlockSpec((1, 2, cps, 1, LANES), lambda b, c: (b, 0, c, 0, 0))],
        compiler_params=_cparams("parallel", "parallel"),
        name="deltanet_chunk_prep",
    )(q, k, v, small)


def _dn_scan_kernel(*refs):
    ins, (of_ref, ob_ref, s_ref) = refs[:12], refs[12:]

    @pl.when(pl.program_id(1) == 0)
    def _():
        s_ref[...] = jnp.zeros_like(s_ref)

    dk = DN_DK
    chains = [(b, d, j) for b in range(SCAN_BATCH) for d in range(2) for j in range(DN_HEADS // 2)]
    outs = (of_ref, ob_ref)
    stage1 = []
    for b, d, j in chains:
        u_ref, w_ref, qd_ref = ins[6 * d:6 * d + 3]
        s = s_ref[b, d, j]
        sb = s.astype(BF16)
        halves = [(slice((2 * j + i) * dk, (2 * j + i + 1) * dk), slice(i * dk, (i + 1) * dk)) for i in range(2)]
        v_new = jnp.concatenate([u_ref[b, 0, :, sl] - _dot(w_ref[b, 0, :, sl], sb[:, hl]) for sl, hl in halves], axis=1)
        inter = jnp.concatenate([_dot(qd_ref[b, 0, :, sl], sb[:, hl]) for sl, hl in halves], axis=1)
        stage1.append((s, v_new, inter))
    for (b, d, j), (s, v_new, inter) in zip(chains, stage1):
        kdt_ref, attn_ref, gl_ref = ins[6 * d + 3:6 * d + 6]
        v_bd = _bd(v_new).astype(BF16)
        intra = _dot(attn_ref[b, 0, :, 2 * j * CHUNK:2 * (j + 1) * CHUNK], v_bd)
        outs[d][b, :, 2 * j * dk:2 * (j + 1) * dk] = (inter + intra).astype(BF16)
        c = DN_HEADS * d + 2 * j
        decayed = jnp.concatenate([s[:, i * dk:(i + 1) * dk] * gl_ref[b, 0, 0, :, c + i:c + i + 1] for i in range(2)], axis=1)
        s_ref[b, d, j] = decayed + _dot(kdt_ref[b, 0, 0, j * dk:(j + 1) * dk, :], v_bd)


def _scan_operands(per_token_arrays, per_chunk_arrays, order, *, ctx_chunks, n_chunks):
    in_specs, args = [], []
    for d in range(2):
        chunk = (lambda n: n) if d == 0 else functools.partial(_rev_chunk, ctx_chunks=ctx_chunks, n_chunks=n_chunks)
        for name in order:
            if name in per_token_arrays:
                a = per_token_arrays[name]
                if a.ndim == 4:
                    spec = pl.BlockSpec((SCAN_BATCH, 1, CHUNK, a.shape[-1]),
                                        lambda b, n, d=d, chunk=chunk: (b, d, chunk(n), 0))
                else:
                    spec = pl.BlockSpec((SCAN_BATCH, CHUNK, a.shape[-1]), lambda b, n, chunk=chunk: (b, chunk(n), 0))
            else:
                a = per_chunk_arrays[name]
                spec = pl.BlockSpec((SCAN_BATCH, 1, 1) + a.shape[3:],
                                    lambda b, n, d=d, chunk=chunk: (b, d, chunk(n), 0, 0))
            in_specs.append(spec)
            args.append(a)
    return in_specs, args


def _scans_kernel(*refs):
    n_dn, n_rw = 12, 16
    of_ref, ob_ref, yf_ref, yb_ref, s_dn_ref, s_rw_ref = refs[n_dn + n_rw:]
    _dn_scan_kernel(*refs[:n_dn], of_ref, ob_ref, s_dn_ref)
    _rw_scan_kernel(*refs[n_dn:n_dn + n_rw], yf_ref, yb_ref, s_rw_ref)


def _scans_call(dn, rw, *, ctx_chunks):
    u, w, qd, kdt, attn, gl = dn
    ut, wt, rt, arb, kbt, y0, pc, v = rw
    n_batch, _, t_len, wd_dn = u.shape
    wd_rw = ut.shape[-1]
    nc = t_len // CHUNK
    dn_specs, dn_args = _scan_operands(dict(u=u, w=w, qd=qd, attn=attn), dict(kdt=kdt, gl=gl),
                                       ("u", "w", "qd", "kdt", "attn", "gl"), ctx_chunks=ctx_chunks, n_chunks=nc)
    rw_specs, rw_args = _scan_operands(dict(ut=ut, wt=wt, rt=rt, arb=arb, y0=y0, v=v), dict(kbt=kbt, pc=pc),
                                       ("ut", "wt", "rt", "arb", "kbt", "y0", "pc", "v"),
                                       ctx_chunks=ctx_chunks, n_chunks=nc)
    fwd = lambda wd: pl.BlockSpec((SCAN_BATCH, CHUNK, wd), lambda b, n: (b, n, 0))
    bwd = lambda wd: pl.BlockSpec((SCAN_BATCH, CHUNK, wd), lambda b, n: (b, _rev_chunk(n, ctx_chunks, nc), 0))
    out = lambda wd: jax.ShapeDtypeStruct((n_batch, t_len, wd), BF16)
    return pl.pallas_call(
        _scans_kernel,
        out_shape=[out(wd_dn), out(wd_dn), out(wd_rw), out(wd_rw)],
        grid=(n_batch // SCAN_BATCH, nc),
        in_specs=dn_specs + rw_specs,
        out_specs=[fwd(wd_dn), bwd(wd_dn), fwd(wd_rw), bwd(wd_rw)],
        scratch_shapes=[pltpu.VMEM((SCAN_BATCH, 2, DN_HEADS // 2, DN_DK, 2 * DN_DK), F32),
                        pltpu.VMEM((SCAN_BATCH, 2, RW_HEADS // 2, 2 * RW_HS, 2 * RW_HS), F32)],
        compiler_params=_cparams("parallel", "arbitrary"),
        name="chunk_scans",
    )(*dn_args, *rw_args)


def _rw_chunk_kernel(r_ref, v_ref, kk_ref, lw_ref, kd_ref, kka_ref,
                     ut_ref, wt_ref, rt_ref, arb_ref, kbt_ref, y0_ref, pc_ref, vb_ref):
    pw = 2 * RW_HS
    vb_ref[0] = v_ref[0].astype(BF16)
    prepared = []
    for cc in range(CHUNKS_PER_STEP):
        rows = slice(cc * CHUNK, (cc + 1) * CHUNK)
        r, kk = r_ref[0, rows], kk_ref[0, rows]
        for d in range(2):
            dsl = slice(d * RW_W, (d + 1) * RW_W)
            lw, kd, kka = lw_ref[0, rows, dsl], kd_ref[0, rows, dsl], kka_ref[0, rows, dsl]
            cl = _mm_exact_lhs(jnp.where(_chunk_masks(d)[0], 1.0, 0.0), lw)
            tot = _last_row(cl, d)
            p_inv, p_tail = jnp.exp(-cl), jnp.exp(tot - cl)
            at = -kk * jnp.exp(cl - lw)
            rt = r * jnp.exp(cl)
            rt_ref[0, d, rows] = rt.astype(BF16)
            pc_ref[0, d, cc] = jnp.broadcast_to(jnp.exp(tot), (8, RW_W))
            kbt_ref[0, d, cc] = jnp.concatenate([(kd * p_tail).T, (kka * p_tail).T], axis=1).astype(BF16)
            prepared.append((rows, d, at, rt, kd * p_inv, kka * p_inv))
    work, n_list = [], []
    for rows, d, at, rt, kh, bh in prepared:
        incl, strict = _chunk_masks(d, 2 * CHUNK)
        for j in range(RW_HEADS // 2):
            sl = slice(j * pw, (j + 1) * pw)
            aa = _mm_nt(jnp.concatenate([at[:, sl], rt[:, sl]], axis=0),
                        jnp.concatenate([_bd(bh[:, sl]), _bd(kh[:, sl])], axis=0))
            n_list.append(jnp.where(strict, aa[:CHUNK, :pw], 0.0))
            work.append((rows, d, sl, at[:, sl], jnp.where(strict, aa[:CHUNK, pw:], 0.0),
                         jnp.where(incl, aa[CHUNK:, :pw], 0.0), jnp.where(incl, aa[CHUNK:, pw:], 0.0)))
    t_invs = _neumann_inverse_pairs(n_list, refine=False)
    rhs = []
    for rows, d, sl, at_p, a_ak, a_rb, a_rk in work:
        v_bd = _bd(v_ref[0, rows, sl]).astype(BF16)
        arb_ref[0, d, rows, sl] = a_rb.astype(BF16)
        y0_ref[0, d, rows, sl] = _dot(a_rk.astype(BF16), v_bd).astype(BF16)
        rhs.append(jnp.concatenate([_bd(at_p), _bd(_dot(a_ak.astype(BF16), v_bd))], axis=1))
    for (rows, d, sl, *_), t_inv, x in zip(work, t_invs, rhs):
        sol = _mm(t_inv, x)
        wt_ref[0, d, rows, sl] = sol[:, :pw].astype(BF16)
        ut_ref[0, d, rows, sl] = sol[:, pw:].astype(BF16)


def _rw_chunk_call(r, v, kk, lw, kd, kka):
    n_batch, t_len, wd = r.shape
    nc = t_len // CHUNK
    cps = CHUNKS_PER_STEP
    ispec = lambda n: pl.BlockSpec((1, cps * CHUNK, n), lambda b, c: (b, c, 0))
    ospec = pl.BlockSpec((1, 2, cps * CHUNK, wd), lambda b, c: (b, 0, c, 0))
    shp = jax.ShapeDtypeStruct((n_batch, 2, t_len, wd), BF16)
    return pl.pallas_call(
        _rw_chunk_kernel,
        out_shape=[shp] * 4 + [jax.ShapeDtypeStruct((n_batch, 2, nc, wd, 2 * CHUNK), BF16), shp,
                               jax.ShapeDtypeStruct((n_batch, 2, nc, 8, wd), F32),
                               jax.ShapeDtypeStruct((n_batch, t_len, wd), BF16)],
        grid=(n_batch, nc // cps),
        in_specs=[ispec(wd), ispec(wd), ispec(wd), ispec(2 * wd), ispec(2 * wd), ispec(2 * wd)],
        out_specs=[ospec] * 4 + [pl.BlockSpec((1, 2, cps, wd, 2 * CHUNK), lambda b, c: (b, 0, c, 0, 0)), ospec,
                                 pl.BlockSpec((1, 2, cps, 8, wd), lambda b, c: (b, 0, c, 0, 0)), ispec(wd)],
        compiler_params=_cparams("parallel", "parallel"),
        name="rwkv_chunk_prep",
    )(r, v, kk, lw, kd, kka)


def _rw_scan_kernel(*refs):
    ins, (yf_ref, yb_ref, s_ref) = refs[:16], refs[16:]

    @pl.when(pl.program_id(1) == 0)
    def _():
        s_ref[...] = jnp.zeros_like(s_ref)

    pw = 2 * RW_HS
    r_i = lax.broadcasted_iota(jnp.int32, (pw, pw), 0)
    c_i = lax.broadcasted_iota(jnp.int32, (pw, pw), 1)
    same_head = (r_i < RW_HS) == (c_i < RW_HS)
    chains = [(b, d, j) for b in range(SCAN_BATCH) for d in range(2) for j in range(RW_HEADS // 2)]
    outs = (yf_ref, yb_ref)
    decay_cols = {(b, d): ins[8 * d + 6][b, 0, 0].T for b in range(SCAN_BATCH) for d in range(2)}
    stage1 = []
    for b, d, j in chains:
        ut_ref, wt_ref, rt_ref, _, _, y0_ref = ins[8 * d:8 * d + 6]
        sl = slice(j * pw, (j + 1) * pw)
        s = s_ref[b, d, j]
        sb = s.astype(BF16)
        u = ut_ref[b, 0, :, sl] + _dot(wt_ref[b, 0, :, sl], sb)
        stage1.append((s, u, y0_ref[b, 0, :, sl] + _dot(rt_ref[b, 0, :, sl], sb)))
    for (b, d, j), (s, u, y_inter) in zip(chains, stage1):
        arb_ref, kbt_ref, _, _, v_ref = ins[8 * d + 3:8 * d + 8]
        sl = slice(j * pw, (j + 1) * pw)
        outs[d][b, :, sl] = (y_inter + _dot(arb_ref[b, 0, :, sl], _bd(u).astype(BF16))).astype(BF16)
        grow = _dot(kbt_ref[b, 0, 0, sl, :], jnp.concatenate([v_ref[b, :, sl], u.astype(BF16)], axis=0))
        s_ref[b, d, j] = s * decay_cols[b, d][sl, 0:1] + jnp.where(same_head, grow, 0.0)


def _ev_out_kernel(h_ref, mod_ref, of_ref, ob_ref, dgate_ref, yf_ref, yb_ref, gate7_ref, bonus_ref,
                   dnorm_ref, gnw_ref, gnb_ref, ones_ref, wout_ref, o_ref):
    f32 = lambda ref: ref[0].astype(F32)
    o = f32(of_ref) + f32(ob_ref)
    dgate = f32(dgate_ref)
    parts = []
    for h in range(DN_HEADS):
        sl = slice(h * DN_DK, (h + 1) * DN_DK)
        parts.append(_rms_rows(o[:, sl]) * dnorm_ref[:, sl] * _silu(dgate[:, sl]))
    o_dn = jnp.concatenate(parts, axis=1)
    y = f32(yf_ref) + f32(yb_ref)
    inv_n = 1.0 / RW_HS
    mu = _head_sums(y, ones_ref) * inv_n
    yc = y - mu
    var = _head_sums(yc * yc, ones_ref) * inv_n
    yn = yc * lax.rsqrt(var + RW_GN_EPS) * gnw_ref[...] + gnb_ref[...]
    o_rw = (yn + f32(bonus_ref)) * f32(gate7_ref)
    wd = DN_HEADS * DN_DK
    proj = _dot(o_dn.astype(BF16), wout_ref[0:wd, :]) + _dot(o_rw.astype(BF16), wout_ref[wd:, :])
    o_ref[0] = h_ref[0] + mod_ref[0, 5:6] * proj


def _ev_out_call(h, mod, o_f, o_b, dgate, y_f, y_b, gate7, bonus, dnorm, gnw, gnb, ones, w_out, *, ctx_tiles):
    n_batch, t_len, d = h.shape
    nt = t_len // TILE
    tile = lambda n: pl.BlockSpec((1, TILE, n), lambda b, t: (b, t, 0))
    params = [dnorm.reshape(1, -1), gnw.reshape(1, -1), gnb.reshape(1, -1), ones, w_out]
    streams = [o_f, o_b, dgate, y_f, y_b, gate7, bonus]
    return pl.pallas_call(
        _ev_out_kernel,
        out_shape=jax.ShapeDtypeStruct(h.shape, F32),
        grid=(n_batch, nt),
        in_specs=[tile(d), pl.BlockSpec((1, N_MOD, d), _mod_row_map(n_batch, ctx_tiles, 0))]
        + [tile(s.shape[-1]) for s in streams] + [_resident(p.shape) for p in params],
        out_specs=tile(d),
        compiler_params=_cparams("parallel", "parallel"),
        name="even_mix_out",
    )(h, mod, *streams, *params)


def _rope_layout(width, rot, starts):
    angle = np.full((width,), -1, np.int64)
    first = np.zeros((1, width), np.float32)
    q = rot // 4
    for start in starts:
        for blk in range(2):
            for idx in range(q):
                l1 = start + blk * 2 * q + idx
                angle[l1] = angle[l1 + q] = blk * q + idx
                first[0, l1] = 1.0
    return angle, jnp.asarray(first)


def _rope_tables(n_ctx, n_lat, rot, angle, first):
    rows = n_lat // GRID_W
    row = jnp.repeat(jnp.arange(rows), GRID_W).astype(F32)
    col = jnp.tile(jnp.arange(GRID_W), rows).astype(F32)
    axis_dim = rot // 2
    inv = ROPE_THETA ** (-jnp.arange(0, axis_dim, 2, dtype=F32) / axis_dim)
    ang = jnp.concatenate([row[:, None] * inv, col[:, None] * inv], axis=-1)
    on = jnp.asarray(angle >= 0)
    idx = np.maximum(angle, 0)
    cos = jnp.where(on, jnp.cos(ang)[:, idx], 1.0)
    sin = jnp.where(on, jnp.sin(ang)[:, idx], 0.0) * (1.0 - 2.0 * first)
    width = angle.shape[0]
    return (jnp.concatenate([jnp.ones((n_ctx, width), F32), cos], axis=0),
            jnp.concatenate([jnp.zeros((n_ctx, width), F32), sin], axis=0))


def _rope(x, first_ref, cos_ref, sin_ref, quarter):
    width = x.shape[1]
    partner = jnp.where(first_ref[...] > 0.5, pltpu.roll(x, width - quarter, 1), pltpu.roll(x, quarter, 1))
    return x * cos_ref[...] + partner * sin_ref[...]


def _od_prep_kernel(h_ref, mod_ref, gain_ref, w_ref, qn_ref, kn_ref, mqn_ref, wuq_ref, mkvn_ref, wukv_ref,
                    ones_ref, fq_ref, fk_ref, fm_ref, fr_ref,
                    cq_ref, sq_ref, ck_ref, sk_ref, cm_ref, sm_ref, cr_ref, sr_ref,
                    qg_ref, qm_ref, kgt_ref, vg_ref, kmt_ref, vm_ref):
    xn = _modulate(h_ref[0], gain_ref[...], mod_ref[0, 3:4], mod_ref[0, 4:5]).astype(BF16)
    p = _dot(xn, w_ref[...])
    nq, nkv = GQ_HEADS * GQ_HD, GQ_KV_HEADS * GQ_HD
    o = 0
    q, o = p[:, o:o + nq], o + nq
    k, o = p[:, o:o + nkv], o + nkv
    v, o = p[:, o:o + 2 * nkv], o + 2 * nkv
    n_cq, n_ckv = mqn_ref.shape[1], mkvn_ref.shape[1]
    cq, o = p[:, o:o + n_cq], o + n_cq
    ckv, o = p[:, o:o + n_ckv], o + n_ckv
    kr = p[:, o:o + LANES]
    inv_hd = 1.0 / GQ_HD
    q = q * lax.rsqrt(_head_sums(q * q, ones_ref) * inv_hd + NORM_EPS) * qn_ref[...]
    k = k * lax.rsqrt(_head_sums(k * k, ones_ref) * inv_hd + NORM_EPS) * kn_ref[...]
    qm = _dot((_rms_rows(cq) * mqn_ref[...]).astype(BF16), wuq_ref[...])
    kvm = _dot((_rms_rows(ckv) * mkvn_ref[...]).astype(BF16), wukv_ref[...])
    q = _rope(q, fq_ref, cq_ref, sq_ref, GQ_HD // 4) * (GQ_HD ** -0.5 * LOG2_E)
    k = _rope(k, fk_ref, ck_ref, sk_ref, GQ_HD // 4)
    qm = _rope(qm, fm_ref, cm_ref, sm_ref, ML_ROPE // 4) * ((ML_NOPE + ML_ROPE) ** -0.5 * LOG2_E)
    kr = _rope(kr, fr_ref, cr_ref, sr_ref, ML_ROPE // 4)
    qg_ref[0] = q.astype(BF16)
    qm_ref[0] = qm.astype(BF16)
    n_nope = ML_HEADS * ML_NOPE
    for ref, val in ((vg_ref, v), (vm_ref, kvm[:, n_nope:])):
        upper = lax.broadcasted_iota(jnp.int32, (1, val.shape[1]), 1) % LANES >= LANES // 2
        ref[0] = (val + jnp.where(upper, 1.0, 0.0)).astype(BF16)
    kgt_ref[0] = k.T.astype(BF16)
    knt = kvm[:, :n_nope].T.astype(BF16)
    krt = kr.T[:ML_ROPE].astype(BF16)
    dk = ML_NOPE + ML_ROPE
    for h in range(ML_HEADS):
        kmt_ref[0, h * dk:h * dk + ML_NOPE, :] = knt[h * ML_NOPE:(h + 1) * ML_NOPE]
        kmt_ref[0, h * dk + ML_NOPE:(h + 1) * dk, :] = krt


def _od_prep_call(h, mod, gain, params, tables, *, ctx_tiles):
    n_batch, t_len, d = h.shape
    nt = t_len // TILE
    nq, nkv = GQ_HEADS * GQ_HD, GQ_KV_HEADS * GQ_HD
    dk = ML_NOPE + ML_ROPE
    tile = lambda n: pl.BlockSpec((1, TILE, n), lambda b, t: (b, t, 0))
    tile_t = lambda n: pl.BlockSpec((1, n, TILE), lambda b, t: (b, 0, t))
    tab = lambda a: pl.BlockSpec((TILE, a.shape[1]), lambda b, t: (t, 0))
    shp = lambda *s: jax.ShapeDtypeStruct((n_batch,) + s, BF16)
    return pl.pallas_call(
        _od_prep_kernel,
        out_shape=[shp(t_len, nq), shp(t_len, ML_HEADS * dk), shp(nkv, t_len), shp(t_len, GQ_KV_HEADS * LANES),
                   shp(ML_HEADS * dk, t_len), shp(t_len, ML_HEADS * LANES)],
        grid=(n_batch, nt),
        in_specs=[tile(d), pl.BlockSpec((1, N_MOD, d), _mod_row_map(n_batch, ctx_tiles, 0)), _resident((1, d))]
        + [_resident(p.shape) for p in params] + [tab(a) for a in tables],
        out_specs=[tile(nq), tile(ML_HEADS * dk), tile_t(nkv), tile(GQ_KV_HEADS * LANES), tile_t(ML_HEADS * dk),
                   tile(ML_HEADS * LANES)],
        compiler_params=_cparams("parallel", "parallel"),
        name="attn_prep",
    )(h, mod, gain.reshape(1, d), *params, *tables)


def _odd_layer_weights(od_w_in, gq_q_norm, gq_k_norm, ml_q_norm, ml_w_uq, ml_kv_norm, ml_w_ukv):
    d = od_w_in.shape[0]
    nq, nkv = GQ_HEADS * GQ_HD, GQ_KV_HEADS * GQ_HD

    def slabs(cols, heads, width):
        cols = cols.reshape(cols.shape[0], heads, width)
        return jnp.concatenate([cols, jnp.zeros(cols.shape[:2] + (LANES - width,), F32)], axis=2).reshape(cols.shape[0], -1)

    w = jnp.concatenate([od_w_in[:, :nq + nkv], slabs(od_w_in[:, nq + nkv:nq + 2 * nkv], GQ_KV_HEADS, GQ_HD),
                         od_w_in[:, nq + 2 * nkv:], jnp.zeros((d, LANES - ML_ROPE), F32)], axis=1)
    ukv = ml_w_ukv.reshape(ml_w_ukv.shape[0], ML_HEADS, ML_NOPE + ML_V)
    ukv = jnp.concatenate([ukv[:, :, :ML_NOPE].reshape(-1, ML_HEADS * ML_NOPE),
                           slabs(ukv[:, :, ML_NOPE:].reshape(-1, ML_HEADS * ML_V), ML_HEADS, ML_V)], axis=1)
    dk = ML_NOPE + ML_ROPE
    layouts = [(GQ_HD,) + _rope_layout(GQ_HEADS * GQ_HD, GQ_HD, [h * GQ_HD for h in range(GQ_HEADS)]),
               (GQ_HD,) + _rope_layout(GQ_KV_HEADS * GQ_HD, GQ_HD, [h * GQ_HD for h in range(GQ_KV_HEADS)]),
               (ML_ROPE,) + _rope_layout(ML_HEADS * dk, ML_ROPE, [h * dk + ML_NOPE for h in range(ML_HEADS)]),
               (ML_ROPE,) + _rope_layout(LANES, ML_ROPE, [0])]
    params = [w.astype(BF16), jnp.tile(gq_q_norm, GQ_HEADS).reshape(1, -1), jnp.tile(gq_k_norm, GQ_KV_HEADS).reshape(1, -1),
              ml_q_norm.reshape(1, -1), ml_w_uq.astype(BF16), ml_kv_norm.reshape(1, -1), ukv.astype(BF16),
              _block_ones(LANES, GQ_HD)] + [first for _, _, first in layouts]
    return params, layouts


def _softmax_pv(s, v_slab, width):
    m = jnp.max(s, axis=-1, keepdims=True)
    p = jnp.exp2((s - m).astype(BF16))
    pv = _dot(p, v_slab)
    return pv[:, :width] / pv[:, width:width + 1]


def _attn_kernel(*refs):
    n = ATT_TILES
    h_refs, qg_refs, qm_refs = refs[:n], refs[n:2 * n], refs[2 * n:3 * n]
    mod_ref, kgt_ref, vg_ref, kmt_ref, vm_ref, wout_ref, o_ref = refs[3 * n:]
    group = GQ_HEADS // GQ_KV_HEADS
    dk = ML_NOPE + ML_ROPE
    rows = lambda tiles, sl: jnp.concatenate([r[0, :, sl] for r in tiles], axis=0)

    def logits(h):
        if h < GQ_HEADS:
            g = h // group
            return _dot(rows(qg_refs, slice(h * GQ_HD, (h + 1) * GQ_HD)), kgt_ref[0, g * GQ_HD:(g + 1) * GQ_HD, :])
        h -= GQ_HEADS
        return _dot(rows(qm_refs, slice(h * dk, (h + 1) * dk)), kmt_ref[0, h * dk:(h + 1) * dk, :])

    def values(h):
        if h < GQ_HEADS:
            g = h // group
            return vg_ref[0, :, g * LANES:(g + 1) * LANES], GQ_HD
        h -= GQ_HEADS
        return vm_ref[0, :, h * LANES:(h + 1) * LANES], ML_V

    n_heads = GQ_HEADS + ML_HEADS
    parts, s = [], logits(0)
    for h in range(n_heads):
        s_next = logits(h + 1) if h + 1 < n_heads else None
        parts.append(_softmax_pv(s, *values(h)))
        s = s_next
    ol = jnp.concatenate(parts, axis=1).astype(BF16)
    o_ref[0] = rows(h_refs, slice(None)) + mod_ref[0, 5:6] * _dot(ol, wout_ref[...])


def _attn_call(h, mod, qg, qm, kgt, vg, kmt, vm, w_out, *, ctx_tiles):
    n_batch, t_len, d = h.shape
    nt = t_len // TILE - ctx_tiles
    na = ATT_TILES
    qtiles = lambda a: [pl.BlockSpec((1, TILE, a.shape[-1]), lambda b, t, i=i: (b, na * t + i + ctx_tiles, 0))
                        for i in range(na)]
    whole = lambda a: pl.BlockSpec((1,) + a.shape[1:], lambda b, t: (b, 0, 0))
    return pl.pallas_call(
        _attn_kernel,
        out_shape=jax.ShapeDtypeStruct((n_batch, nt * TILE, d), F32),
        grid=(n_batch, nt // na),
        in_specs=qtiles(h) + qtiles(qg) + qtiles(qm) + [pl.BlockSpec((1, N_MOD, d), lambda b, t: (b, 0, 0)),
                                                      whole(kgt), whole(vg), whole(kmt), whole(vm), _resident(w_out.shape)],
        out_specs=pl.BlockSpec((1, na * TILE, d), lambda b, t: (b, t, 0)),
        compiler_params=_cparams("parallel", "parallel"),
        name="attention_out",
    )(*([h] * na + [qg] * na + [qm] * na), mod, kgt, vg, kmt, vm, w_out)


def _block_ones(n, blk):
    i = np.arange(n) // blk
    return jnp.asarray(i[:, None] == i[None, :], BF16)


def _even_layer_weights(ev_w_in, dn_conv, dn_a_log, dn_dt_bias, rw_mu, rw_w0, rw_w2, rw_a0, rw_a2, rw_g2,
                        rw_kk, rw_ka, rw_rk):
    d = ev_w_in.shape[0]
    n_dn = 4 * DN_HEADS * DN_DK
    nh2 = 2 * DN_HEADS
    slab0 = n_dn + 2 * nh2
    zeros = lambda n: jnp.zeros((d, n), F32)
    w_dn = jnp.concatenate([ev_w_in[:, :n_dn], ev_w_in[:, n_dn:slab0], zeros(LANES - 2 * nh2)], axis=1)
    slab = ev_w_in[:, slab0:]
    o = 3 * RW_W
    lora = 2 * RW_W_LORA
    gpad = 2 * LANES - RW_G_LORA
    w_rw = jnp.concatenate([slab[:, :o + 2 * lora + RW_G_LORA], zeros(gpad)], axis=1)
    mu = jnp.concatenate([rw_mu, jnp.zeros((gpad,), F32)]).reshape(1, -1)
    pad_lanes = lambda v: jnp.zeros((1, LANES), F32).at[0, :v.size].set(v.reshape(-1))

    def dir_blocks(m):
        z = jnp.zeros_like(m[0])
        return jnp.concatenate([jnp.concatenate([m[0], z], axis=1), jnp.concatenate([z, m[1]], axis=1)], axis=0)

    g2 = jnp.concatenate([rw_g2, jnp.zeros((gpad, RW_W), F32)], axis=0)
    return dict(
        w_dn=w_dn.astype(BF16), conv=dn_conv, alog=pad_lanes(dn_a_log), dtb=pad_lanes(dn_dt_bias),
        w_rw=w_rw.astype(BF16), mu=mu, w2=dir_blocks(rw_w2), w0=rw_w0.reshape(1, -1), a2=dir_blocks(rw_a2),
        a0=rw_a0.reshape(1, -1), g2=g2, kkw=rw_kk.reshape(1, -1), kaw=rw_ka.reshape(1, -1),
        rk=rw_rk.reshape(1, -1), ones=_block_ones(LANES, RW_HS))


def kernel(x, c, ctx, c_ctx, mod_w, mod_b, norm_ffn1, norm_mix, norm_ffn2, ffn1_w1, ffn1_w3, ffn1_w2, ffn2_w1, ffn2_w3, ffn2_w2, ev_w_in, ev_w_out, dn_conv, dn_a_log, dn_dt_bias, dn_norm, rw_mu, rw_w0, rw_w2, rw_a0, rw_a2, rw_g2, rw_kk, rw_ka, rw_rk, rw_gn_w, rw_gn_b, od_w_in, od_w_out, gq_q_norm, gq_k_norm, ml_q_norm, ml_w_uq, ml_kv_norm, ml_w_ukv, final_norm):
    n_batch, n_lat, d = x.shape
    n_ctx = ctx.shape[1]
    depth = mod_w.shape[0]
    assert n_ctx % TILE == 0 and n_lat % (ATT_TILES * TILE) == 0 and n_lat % GRID_W == 0
    assert n_batch % SCAN_BATCH == 0 and (n_ctx + n_lat) % (CHUNKS_PER_STEP * CHUNK) == 0
    assert depth % 2 == 0 and depth // 2 == od_w_in.shape[0] == 1, "supported stack: [recurrent, attention]"
    ctx_tiles, ctx_chunks = n_ctx // TILE, n_ctx // CHUNK
    bf = lambda a: a.astype(BF16)

    mod = _all_mod(c, c_ctx, mod_w, mod_b)
    h = (ctx, x)
    for i in range(depth):
        j = i // 2
        last = i == depth - 1
        h = _ffn_call(h, mod[i], norm_ffn1[i], bf(ffn1_w1[i]), bf(ffn1_w3[i]), bf(ffn1_w2[i]), j0=0, ctx_tiles=ctx_tiles)
        if i % 2 == 0:
            w = _even_layer_weights(ev_w_in[j], dn_conv[j], dn_a_log[j], dn_dt_bias[j], rw_mu[j], rw_w0[j], rw_w2[j],
                                    rw_a0[j], rw_a2[j], rw_g2[j], rw_kk[j], rw_ka[j], rw_rk[j])
            (q, k, v, small, dgate), (r, v7, kk, lw, kd, kka, gate7, bonus) = _ev_prep_call(
                h, mod[i], norm_mix[i], w, ctx_tiles=ctx_tiles)
            o_f, o_b, y_f, y_b = _scans_call(_dn_chunk_call(q, k, v, small), _rw_chunk_call(r, v7, kk, lw, kd, kka),
                                             ctx_chunks=ctx_chunks)
            h = _ev_out_call(h, mod[i], o_f, o_b, dgate, y_f, y_b, gate7, bonus, jnp.tile(dn_norm[j], DN_HEADS),
                             rw_gn_w[j], rw_gn_b[j], w["ones"], bf(ev_w_out[j]), ctx_tiles=ctx_tiles)
            h = _ffn_call(h, mod[i], norm_ffn2[i], bf(ffn2_w1[i]), bf(ffn2_w3[i]), bf(ffn2_w2[i]), j0=6,
                          ctx_tiles=ctx_tiles)
        else:
            params, layouts = _odd_layer_weights(od_w_in[j], gq_q_norm[j], gq_k_norm[j], ml_q_norm[j], ml_w_uq[j],
                                                 ml_kv_norm[j], ml_w_ukv[j])
            tables = [t for rot, angle, first in layouts for t in _rope_tables(n_ctx, n_lat, rot, angle, first)]
            qg, qm, kgt, vg, kmt, vm = _od_prep_call(h, mod[i], norm_mix[i], params, tables, ctx_tiles=ctx_tiles)
            hl = _attn_call(h, mod[i], qg, qm, kgt, vg, kmt, vm, bf(od_w_out[j]), ctx_tiles=ctx_tiles)
            assert last
            h = _ffn_call(hl, mod[i], norm_ffn2[i], bf(ffn2_w1[i]), bf(ffn2_w3[i]), bf(ffn2_w2[i]), j0=6, ctx_tiles=0,
                          final_gain=final_norm)
    return h
```
